```python
import jax, jax.numpy as jnp
from jax import lax
import numpy as np

D_MODEL = 1024
BATCH = 4
SEQ = 4096
DEPTH = 2
DEC_BATCH = 32
DEC_SEQ = 1
PAST_LEN = 8192
PAGE_SIZE = 128

HEAD_DIM = 64
N_FOX_HEADS = 8
N_RET_HEADS = 8
FOX_WIDTH = N_FOX_HEADS * HEAD_DIM
RET_WIDTH = N_RET_HEADS * HEAD_DIM
EVEN_IN = 3 * FOX_WIDTH + N_FOX_HEADS + 4 * RET_WIDTH
CONV_WIDTH = 3
D_FF = 2816
Q_BLOCK = 128
RET_CHUNK = 128
RET_ANGLE_BASE = 10000.0
N_EVEN = (DEPTH + 1) // 2
N_ODD = DEPTH // 2
N_MOD = 9
DEEPNORM_ALPHA = (2.0 * DEPTH) ** 0.25
DEEPNORM_BETA = (8.0 * DEPTH) ** -0.25
LN_EPS = 1e-5
GN_EPS = 1e-6

kernel_name = "fox_retnet_shortconv_macaron_deepnorm_adaln_step"


def _layernorm(x, g, b):
    xf = x.astype(jnp.float32)
    mu = jnp.mean(xf, axis=-1, keepdims=True)
    var = jnp.mean(jnp.square(xf - mu), axis=-1, keepdims=True)
    return ((xf - mu) * lax.rsqrt(var + LN_EPS) * g + b).astype(x.dtype)


def _head_norm(o):
    of = o.astype(jnp.float32)
    mu = jnp.mean(of, axis=-1, keepdims=True)
    var = jnp.mean(jnp.square(of - mu), axis=-1, keepdims=True)
    return (of - mu) * lax.rsqrt(var + GN_EPS)


def _swiglu(h, w_in, w_out):
    a, b = jnp.split(h @ w_in, 2, axis=-1)
    return (jax.nn.silu(a) * b) @ w_out


def _rotary(x, pos):
    half = x.shape[-1] // 2
    inv = 1.0 / (RET_ANGLE_BASE ** jnp.linspace(0.0, 1.0, half, dtype=jnp.float32))
    ang = pos.astype(jnp.float32)[:, None] * inv[None, :]
    cos = jnp.cos(ang)[None, :, None, :]
    sin = jnp.sin(ang)[None, :, None, :]
    x1 = x[..., :half].astype(jnp.float32)
    x2 = x[..., half:].astype(jnp.float32)
    return jnp.concatenate([x1 * cos - x2 * sin, x1 * sin + x2 * cos], axis=-1).astype(x.dtype)


def _fox_attend(q, cq, qpos, k, ck, kpos, v):
    s = jnp.einsum('bqhd,bkhd->bhqk', q, k).astype(jnp.float32) * (HEAD_DIM ** -0.5)
    s = s + (jnp.transpose(cq, (0, 2, 1))[:, :, :, None] - jnp.transpose(ck, (0, 2, 1))[:, :, None, :])
    mask = kpos[None, :] <= qpos[:, None]
    s = jnp.where(mask[None, None], s, -jnp.inf)
    p = jax.nn.softmax(s, axis=-1)
    return jnp.einsum('bhqk,bkhd->bqhd', p.astype(v.dtype), v)


def _fox_prompt(q, k, v, logf):
    B, T, H, d = q.shape
    c = jnp.cumsum(logf.astype(jnp.float32), axis=1)
    nb = T // Q_BLOCK
    qb = q.reshape(B, nb, Q_BLOCK, H, d).transpose(1, 0, 2, 3, 4)
    cb = c.reshape(B, nb, Q_BLOCK, H).transpose(1, 0, 2, 3)
    pb = jnp.arange(T, dtype=jnp.int32).reshape(nb, Q_BLOCK)
    kpos = jnp.arange(T, dtype=jnp.int32)
    out = lax.map(lambda a: _fox_attend(a[0], a[1], a[2], k, c, kpos, v), (qb, cb, pb))
    return out.transpose(1, 0, 2, 3, 4).reshape(B, T, H, d)


def _fox_sample(q, k, v, logf, k_past, v_past, logf_past):
    P = k_past.shape[1]
    L = q.shape[1]
    k_all = jnp.concatenate([k_past.astype(k.dtype), k], axis=1)
    v_all = jnp.concatenate([v_past.astype(v.dtype), v], axis=1)
    c_all = jnp.cumsum(jnp.concatenate([logf_past.astype(jnp.float32), logf.astype(jnp.float32)], axis=1), axis=1)
    qpos = P + jnp.arange(L, dtype=jnp.int32)
    kpos = jnp.arange(P + L, dtype=jnp.int32)
    return _fox_attend(q, c_all[:, P:], qpos, k_all, c_all, kpos, v_all)


def _ret_log_decay():
    return jnp.log(1.0 - 2.0 ** (-5.0 - jnp.arange(N_RET_HEADS, dtype=jnp.float32)))


def _ret_chunk(S, q, k, v, lg):
    L = q.shape[1]
    i = jnp.arange(L, dtype=jnp.float32)
    diff = i[:, None] - i[None, :]
    dmask = jnp.where(diff[None] >= 0, jnp.exp(jnp.maximum(diff, 0.0)[None] * lg[:, None, None]), 0.0)
    qf = q.astype(jnp.float32)
    kf = k.astype(jnp.float32)
    vf = v.astype(jnp.float32)
    qk = jnp.einsum('bihd,bjhd->bhij', qf, kf) * dmask[None]
    inner = jnp.einsum('bhij,bjhe->bihe', qk, vf)
    cross = jnp.einsum('bihd,bhde->bihe', qf, S) * jnp.exp((i + 1.0)[:, None] * lg[None, :])[None, :, :, None]
    kd = kf * jnp.exp((L - 1.0 - i)[:, None] * lg[None, :])[None, :, :, None]
    S_new = jnp.exp(L * lg)[None, :, None, None] * S + jnp.einsum('bjhd,bjhe->bhde', kd, vf)
    return S_new, inner + cross


def _ret_scan(q, k, v, S0):
    B, T, H, _ = q.shape
    lg = _ret_log_decay()
    S0 = S0.astype(jnp.float32)
    if T % RET_CHUNK != 0:
        return _ret_chunk(S0, q, k, v, lg)
    n = T // RET_CHUNK

    def to_chunks(a):
        return a.reshape(B, n, RET_CHUNK, H, a.shape[-1]).swapaxes(0, 1)

    S, o = lax.scan(lambda s, a: _ret_chunk(s, a[0], a[1], a[2], lg), S0, (to_chunks(q), to_chunks(k), to_chunks(v)))
    return S, o.swapaxes(0, 1).reshape(B, T, H, -1)


def _even_mixer(h, pos, w_in, b_f, w_out, fox_past, S0):
    B, T, _ = h.shape
    F, R, Hf = FOX_WIDTH, RET_WIDTH, N_FOX_HEADS
    split_at = [F, 2 * F, 3 * F, 3 * F + Hf, 3 * F + Hf + R, 3 * F + Hf + 2 * R, 3 * F + Hf + 3 * R]
    qf, kf, vf, fg, qr, kr, vr, gr = jnp.split(h @ w_in, split_at, axis=-1)

    def heads(a):
        return a.reshape(B, T, -1, HEAD_DIM)

    qf, kf, vf = heads(qf), heads(kf), heads(vf)
    logf = jax.nn.log_sigmoid((fg + b_f).astype(jnp.float32))
    if fox_past is None:
        of = _fox_prompt(qf, kf, vf, logf)
    else:
        of = _fox_sample(qf, kf, vf, logf, fox_past[0], fox_past[1], fox_past[2])
    qr = _rotary(heads(qr), pos)
    kr = _rotary(heads(kr), pos) * (HEAD_DIM ** -0.5)
    S, orr = _ret_scan(qr, kr, heads(vr), S0)
    yr = jax.nn.silu(gr) * _head_norm(orr).reshape(B, T, R).astype(h.dtype)
    y = jnp.concatenate([of.reshape(B, T, F).astype(h.dtype), yr], axis=-1) @ w_out
    return y, kf, vf, logf, S


def _odd_mixer(h, w_in, conv_w, w_out, buf):
    T = h.shape[1]
    b_gate, c_gate, u = jnp.split(h @ w_in, 3, axis=-1)
    u = c_gate * u
    up = jnp.concatenate([buf.astype(u.dtype), u], axis=1)
    z = conv_w[0] * up[:, 0:T]
    for j in range(1, CONV_WIDTH):
        z = z + conv_w[j] * up[:, j:j + T]
    y = (b_gate * z) @ w_out
    return y, up[:, -(CONV_WIDTH - 1):]


def _trunk(x, c, pos, fox_cache, ret_state, conv_state, page_table, w_ada, b_ada, w_ffn_in, w_ffn_out,
           ln_g, ln_b, w_in_even, b_forget, w_out_even, w_in_odd, conv_w, w_out_odd):
    B = x.shape[0]
    ks, vs, lfs, Ss, convs = [], [], [], [], []
    for l in range(DEPTH):
        mods = (jax.nn.silu(c) @ w_ada[l] + b_ada[l]).reshape(B, 1, N_MOD, D_MODEL)
        sh, sc, g = mods[:, :, 0], mods[:, :, 1], mods[:, :, 2]
        f = _swiglu(x * (1.0 + sc) + sh, w_ffn_in[l, 0], w_ffn_out[l, 0])
        x = _layernorm(DEEPNORM_ALPHA * x + 0.5 * g * f, ln_g[l, 0], ln_b[l, 0])
        sh, sc, g = mods[:, :, 3], mods[:, :, 4], mods[:, :, 5]
        h = x * (1.0 + sc) + sh
        if l % 2 == 0:
            e = l // 2
            if fox_cache is None:
                fox_past = None
                s0 = jnp.zeros((B, N_RET_HEADS, HEAD_DIM, HEAD_DIM), jnp.float32)
            else:
                k_pool, v_pool, lf_pool = fox_cache
                fox_past = (k_pool[e][page_table].reshape(B, -1, N_FOX_HEADS, HEAD_DIM),
                            v_pool[e][page_table].reshape(B, -1, N_FOX_HEADS, HEAD_DIM),
                            lf_pool[e][page_table].reshape(B, -1, N_FOX_HEADS))
                s0 = ret_state[e]
            y, kn, vn, lfn, Sn = _even_mixer(h, pos, w_in_even[e], b_forget[e], w_out_even[e], fox_past, s0)
            ks.append(kn)
            vs.append(vn)
            lfs.append(lfn)
            Ss.append(Sn)
        else:
            o = l // 2
            buf = jnp.zeros((B, CONV_WIDTH - 1, D_MODEL), x.dtype) if conv_state is None else conv_state[o]
            y, nbuf = _odd_mixer(h, w_in_odd[o], conv_w[o], w_out_odd[o], buf)
            convs.append(nbuf)
        x = _layernorm(DEEPNORM_ALPHA * x + g * y, ln_g[l, 1], ln_b[l, 1])
        sh, sc, g = mods[:, :, 6], mods[:, :, 7], mods[:, :, 8]
        f = _swiglu(x * (1.0 + sc) + sh, w_ffn_in[l, 1], w_ffn_out[l, 1])
        x = _layernorm(DEEPNORM_ALPHA * x + 0.5 * g * f, ln_g[l, 2], ln_b[l, 2])
    return x, jnp.stack(ks), jnp.stack(vs), jnp.stack(lfs), jnp.stack(Ss), jnp.stack(convs)


def setup_inputs(seed: int = 0) -> dict:
    key = jax.random.key(seed)
    k = jax.random.split(key, 24)

    def nrm(kk, shape, scale):
        return jax.random.normal(kk, shape, jnp.float32) * scale

    n_pages = PAST_LEN // PAGE_SIZE
    used = DEC_BATCH * n_pages
    n_phys = used + max(1, used // 4)
    page_table = jax.random.permutation(k[0], n_phys)[:used].reshape(DEC_BATCH, n_pages).astype(jnp.int32)
    d = D_MODEL
    return {
        "x_prompt": nrm(k[1], (BATCH, SEQ, d), 1.0),
        "x_sample": nrm(k[2], (DEC_BATCH, DEC_SEQ, d), 1.0),
        "cache_k": nrm(k[3], (N_EVEN, n_phys, PAGE_SIZE, N_FOX_HEADS, HEAD_DIM), 1.0),
        "cache_v": nrm(k[4], (N_EVEN, n_phys, PAGE_SIZE, N_FOX_HEADS, HEAD_DIM), 1.0),
        "cache_logf": jax.nn.log_sigmoid(3.0 + nrm(k[5], (N_EVEN, n_phys, PAGE_SIZE, N_FOX_HEADS), 1.0)),
        "state_ret": nrm(k[6], (N_EVEN, DEC_BATCH, N_RET_HEADS, HEAD_DIM, HEAD_DIM), 0.5),
        "state_conv": nrm(k[7], (N_ODD, DEC_BATCH, CONV_WIDTH - 1, d), 1.0),
        "page_table": page_table,
        "c_prompt": nrm(k[8], (BATCH, d), 1.0),
        "c_sample": nrm(k[9], (DEC_BATCH, d), 1.0),
        "w_ada": nrm(k[10], (DEPTH, d, N_MOD * d), 0.5 * d ** -0.5),
        "b_ada": nrm(k[11], (DEPTH, N_MOD * d), 0.01),
        "w_ffn_in": nrm(k[12], (DEPTH, 2, d, 2 * D_FF), d ** -0.5),
        "w_ffn_out": nrm(k[13], (DEPTH, 2, D_FF, d), DEEPNORM_BETA * D_FF ** -0.5),
        "ln_g": 1.0 + nrm(k[14], (DEPTH, 3, d), 0.01),
        "ln_b": nrm(k[15], (DEPTH, 3, d), 0.01),
        "w_in_even": nrm(k[16], (N_EVEN, d, EVEN_IN), d ** -0.5),
        "b_forget": 3.0 + nrm(k[17], (N_EVEN, N_FOX_HEADS), 0.5),
        "w_out_even": nrm(k[18], (N_EVEN, d, d), DEEPNORM_BETA * d ** -0.5),
        "w_in_odd": nrm(k[19], (N_ODD, d, 3 * d), d ** -0.5),
        "conv_w": nrm(k[20], (N_ODD, CONV_WIDTH, d), CONV_WIDTH ** -0.5),
        "w_out_odd": nrm(k[21], (N_ODD, d, d), DEEPNORM_BETA * d ** -0.5),
    }


def reference(x_prompt, x_sample, cache_k, cache_v, cache_logf, state_ret, state_conv, page_table,
              c_prompt, c_sample, w_ada, b_ada, w_ffn_in, w_ffn_out, ln_g, ln_b,
              w_in_even, b_forget, w_out_even, w_in_odd, conv_w, w_out_odd):
    pos_prompt = jnp.arange(x_prompt.shape[1], dtype=jnp.int32)
    pos_sample = PAST_LEN + jnp.arange(x_sample.shape[1], dtype=jnp.int32)
    y_prompt, k_p, v_p, lf_p, ret_p, conv_p = _trunk(
        x_prompt, c_prompt, pos_prompt, None, None, None, None,
        w_ada, b_ada, w_ffn_in, w_ffn_out, ln_g, ln_b,
        w_in_even, b_forget, w_out_even, w_in_odd, conv_w, w_out_odd)
    y_sample, k_s, v_s, lf_s, ret_s, conv_s = _trunk(
        x_sample, c_sample, pos_sample, (cache_k, cache_v, cache_logf), state_ret, state_conv, page_table,
        w_ada, b_ada, w_ffn_in, w_ffn_out, ln_g, ln_b,
        w_in_even, b_forget, w_out_even, w_in_odd, conv_w, w_out_odd)
    return (y_prompt, y_sample, k_p, v_p, lf_p, ret_p, conv_p, k_s, v_s, lf_s, ret_s, conv_s)
```

```python
import functools

import jax
import jax.numpy as jnp
from jax import lax
from jax.experimental import pallas as pl
from jax.experimental.pallas import tpu as pltpu

F32 = jnp.float32
BF16 = jnp.bfloat16

HEAD_DIM = 64
N_HEADS = 8
HEAD_WIDTH = N_HEADS * HEAD_DIM
N_MOD = 9
CONV_WIDTH = 3
RET_ANGLE_BASE = 10000.0
LN_EPS = 1e-5
GN_EPS = 1e-6
QK_SCALE = HEAD_DIM ** -0.5

V7X_LANES = 128
V7X_VMEM_LIMIT_BYTES = 56 * 1024 * 1024

ROW_BLOCK = 512
FF_CHUNK = 256
RET_CHUNK = 256
ADA_COL_BLOCK = 1152
DEC_PAGES_PER_STEP = 8
HEAD_PAIRS = N_HEADS // 2


def _cparams(sem, big=False):
    return pltpu.CompilerParams(dimension_semantics=sem,
                                vmem_limit_bytes=V7X_VMEM_LIMIT_BYTES if big else None)


def _const_spec(arr):
    nd = arr.ndim
    return pl.BlockSpec(arr.shape, lambda *_: (0,) * nd, pipeline_mode=pl.Buffered(1))


def _layernorm(z, g, b):
    mu = jnp.mean(z, axis=-1, keepdims=True)
    d = z - mu
    var = jnp.mean(d * d, axis=-1, keepdims=True)
    return d * lax.rsqrt(var + LN_EPS) * g + b


def _silu(a):
    return a * jax.nn.sigmoid(a)


def _log_sigmoid(z):
    return jnp.minimum(z, 0.0) - jnp.log1p(jnp.exp(-jnp.abs(z)))


def _dot(a, b):
    return jnp.dot(a, b, preferred_element_type=F32)


def _dot_nt(a, b):
    return lax.dot_general(a, b, (((1,), (1,)), ((), ())), preferred_element_type=F32)


def _ada_kernel(c_ref, w_ref, b_ref, o_ref):
    s = _silu(c_ref[...]).astype(BF16)
    o_ref[...] = _dot(s, w_ref[...].astype(BF16)) + b_ref[...]


def _ada_call(c_all, w_ada, b_ada):
    depth, d, nm = w_ada.shape
    rows = c_all.shape[0]
    tn = ADA_COL_BLOCK
    return pl.pallas_call(
        _ada_kernel,
        grid=(depth, nm // tn),
        in_specs=[pl.BlockSpec((rows, d), lambda l, j: (0, 0)),
                  pl.BlockSpec((None, d, tn), lambda l, j: (l, 0, j)),
                  pl.BlockSpec((None, 1, tn), lambda l, j: (l, 0, j))],
        out_specs=pl.BlockSpec((None, rows, tn), lambda l, j: (l, 0, j)),
        out_shape=jax.ShapeDtypeStruct((depth, rows, nm), F32),
        compiler_params=_cparams(("arbitrary", "arbitrary"), big=True),
        name="ada_mods",
    )(c_all, w_ada, b_ada.reshape(depth, 1, nm))


class _Rows:
    def __init__(self, mods, layer, tm, blocks_per_seq, d):
        self.mods, self.layer, self.tm, self.bps, self.d = mods, layer, tm, blocks_per_seq, d

    def mod_spec(self, k):
        l, d = self.layer, self.d
        if self.bps is None:
            return pl.BlockSpec((None, self.tm, d), lambda i: (l, 0, k))
        bps = self.bps
        return pl.BlockSpec((None, None, 1, d), lambda i: (l, i // bps, 0, k))

    def row_spec(self, width):
        return pl.BlockSpec((self.tm, width), lambda i: (i, 0))


def _ffn_kernel(x_ref, sh_ref, sc_ref, g_ref, win_ref, wout_ref, lng_ref, lnb_ref, o_ref, *, ff, alpha):
    x = x_ref[...]
    h = (x * (1.0 + sc_ref[...]) + sh_ref[...]).astype(BF16)
    acc = None
    for c in range(ff // FF_CHUNK):
        lo, hi = c * FF_CHUNK, (c + 1) * FF_CHUNK
        a = _dot(h, win_ref[:, lo:hi])
        b = _dot(h, win_ref[:, ff + lo:ff + hi])
        y = _dot((_silu(a) * b).astype(BF16), wout_ref[lo:hi, :])
        acc = y if acc is None else acc + y
    z = alpha * x + (0.5 * g_ref[...]) * acc
    o_ref[...] = _layernorm(z, lng_ref[...], lnb_ref[...])


def _ffn_call(x, rows, ks, w_in, w_out, ln_g, ln_b, ln_idx, alpha):
    n, d = x.shape
    ff = w_out.shape[0]
    return pl.pallas_call(
        functools.partial(_ffn_kernel, ff=ff, alpha=alpha),
        grid=(n // rows.tm,),
        in_specs=[rows.row_spec(d), rows.mod_spec(ks[0]), rows.mod_spec(ks[1]), rows.mod_spec(ks[2]),
                  _const_spec(w_in), _const_spec(w_out),
                  pl.BlockSpec((None, 1, d), lambda i: (ln_idx, 0, 0)),
                  pl.BlockSpec((None, 1, d), lambda i: (ln_idx, 0, 0))],
        out_specs=rows.row_spec(d),
        out_shape=jax.ShapeDtypeStruct((n, d), F32),
        compiler_params=_cparams(("parallel",), big=True),
        name="ffn",
    )(x, rows.mods, rows.mods, rows.mods, w_in, w_out, ln_g, ln_b)


def _outproj_kernel(x_ref, g_ref, a0_ref, a1_ref, w_ref, lng_ref, lnb_ref, o_ref, *, alpha):
    half = a0_ref.shape[-1]
    y = _dot(a0_ref[...].astype(BF16), w_ref[0:half, :]) + _dot(a1_ref[...].astype(BF16), w_ref[half:, :])
    z = alpha * x_ref[...] + g_ref[...] * y
    o_ref[...] = _layernorm(z, lng_ref[...], lnb_ref[...])


def _outproj_call(x, rows, k_gate, a0, a1, w_out, ln_g, ln_b, ln_idx, alpha):
    n, d = x.shape
    return pl.pallas_call(
        functools.partial(_outproj_kernel, alpha=alpha),
        grid=(n // rows.tm,),
        in_specs=[rows.row_spec(d), rows.mod_spec(k_gate), rows.row_spec(a0.shape[1]), rows.row_spec(a1.shape[1]),
                  _const_spec(w_out),
                  pl.BlockSpec((None, 1, d), lambda i: (ln_idx, 0, 0)),
                  pl.BlockSpec((None, 1, d), lambda i: (ln_idx, 0, 0))],
        out_specs=rows.row_spec(d),
        out_shape=jax.ShapeDtypeStruct((n, d), F32),
        compiler_params=_cparams(("parallel",), big=True),
        name="mixer_outproj",
    )(x, rows.mods, a0, a1, w_out, ln_g, ln_b)


def _rotate_token_major(a, cos, sin_signed):
    width = a.shape[-1]
    lane = lax.broadcasted_iota(jnp.int32, a.shape, 1)
    first_half = (lane & (HEAD_DIM // 2)) == 0
    partner = jnp.where(first_half, pltpu.roll(a, width - HEAD_DIM // 2, 1), pltpu.roll(a, HEAD_DIM // 2, 1))
    return a * cos + partner * sin_signed


def _even_in_kernel(x_ref, sh_ref, sc_ref, wtok_ref, wt_ref, bf_ref, cos_ref, sin_ref, cost_ref, sint_ref,
                    q_o, qr_o, vr_o, sg_o, kt_o, vt_o, ktc_o, vtc_o, krt_o, lft_o):
    w = HEAD_WIDTH
    x = x_ref[...]
    h = (x * (1.0 + sc_ref[...]) + sh_ref[...]).astype(BF16)

    q_o[...] = (_dot(h, wtok_ref[:, 0:w]) * QK_SCALE).astype(BF16)
    qr = _dot(h, wtok_ref[:, w:2 * w])
    qr_o[...] = _rotate_token_major(qr, cos_ref[...], sin_ref[...]).astype(BF16)
    vr_o[...] = _dot(h, wtok_ref[:, 2 * w:3 * w]).astype(BF16)
    sg_o[...] = _silu(_dot(h, wtok_ref[:, 3 * w:4 * w])).astype(BF16)

    kt = _dot_nt(wt_ref[0:w, :], h)
    kt_o[...] = kt
    ktc_o[...] = kt.astype(BF16)
    vt = _dot_nt(wt_ref[w:2 * w, :], h)
    vt_o[...] = vt
    vtc_o[...] = vt.astype(BF16)

    krt = _dot_nt(wt_ref[2 * w:3 * w, :], h)
    cos_t, sin_t = cost_ref[...], sint_ref[...]
    half = HEAD_DIM // 2
    for hh in range(N_HEADS):
        x1 = krt[hh * HEAD_DIM:hh * HEAD_DIM + half, :]
        x2 = krt[hh * HEAD_DIM + half:(hh + 1) * HEAD_DIM, :]
        krt_o[hh * HEAD_DIM:hh * HEAD_DIM + half, :] = (x1 * cos_t - x2 * sin_t) * QK_SCALE
        krt_o[hh * HEAD_DIM + half:(hh + 1) * HEAD_DIM, :] = (x1 * sin_t + x2 * cos_t) * QK_SCALE

    fg = _dot_nt(wt_ref[3 * w:, :], h)[0:N_HEADS, :] + bf_ref[...]
    lft_o[...] = _log_sigmoid(fg)


def _even_in_call(x, rows, ks, w_tok, w_t, bf_col, rot, batch, seq):
    n, d = x.shape
    tm = rows.tm
    bps = seq // tm
    w = HEAD_WIDTH
    cos_tok, sin_tok, cos_t, sin_t = rot
    tok_out = pl.BlockSpec((tm, w), lambda i: (i, 0))
    t_out = pl.BlockSpec((None, w, tm), lambda i: (i // bps, 0, i % bps))
    tc_out = pl.BlockSpec((None, None, w, tm), lambda i: (i // bps, i % bps, 0, 0))
    tok_shape = jax.ShapeDtypeStruct((n, w), BF16)
    return pl.pallas_call(
        _even_in_kernel,
        grid=(n // tm,),
        in_specs=[rows.row_spec(d), rows.mod_spec(ks[0]), rows.mod_spec(ks[1]),
                  _const_spec(w_tok), _const_spec(w_t), _const_spec(bf_col),
                  pl.BlockSpec((tm, w), lambda i: (i % bps, 0)),
                  pl.BlockSpec((tm, w), lambda i: (i % bps, 0)),
                  pl.BlockSpec((HEAD_DIM // 2, tm), lambda i: (0, i % bps)),
                  pl.BlockSpec((HEAD_DIM // 2, tm), lambda i: (0, i % bps))],
        out_specs=[tok_out, tok_out, tok_out, tok_out, t_out, t_out, tc_out, tc_out, tc_out,
                   pl.BlockSpec((None, N_HEADS, tm), lambda i: (i // bps, 0, i % bps))],
        out_shape=[tok_shape, tok_shape, tok_shape, tok_shape,
                   jax.ShapeDtypeStruct((batch, w, seq), F32),
                   jax.ShapeDtypeStruct((batch, w, seq), F32),
                   jax.ShapeDtypeStruct((batch, bps, w, tm), BF16),
                   jax.ShapeDtypeStruct((batch, bps, w, tm), BF16),
                   jax.ShapeDtypeStruct((batch, bps, w, tm), F32),
                   jax.ShapeDtypeStruct((batch, N_HEADS, seq), F32)],
        compiler_params=_cparams(("parallel",), big=True),
        name="even_inproj",
    )(x, rows.mods, rows.mods, w_tok, w_t, bf_col, cos_tok, sin_tok, cos_t, sin_t)


def _cumsum_kernel(x_ref, o_ref):
    x = x_ref[...]
    n = x.shape[-1]
    lane = lax.broadcasted_iota(jnp.int32, x.shape, 1)
    shift = 1
    while shift < n:
        x = x + jnp.where(lane >= shift, pltpu.roll(x, shift, 1), 0.0)
        shift *= 2
    o_ref[...] = x


def _cumsum_call(lft):
    batch, heads, seq = lft.shape
    spec = pl.BlockSpec((None, heads, seq), lambda b: (b, 0, 0))
    return pl.pallas_call(
        _cumsum_kernel, grid=(batch,), in_specs=[spec], out_specs=spec,
        out_shape=jax.ShapeDtypeStruct(lft.shape, F32),
        compiler_params=_cparams(("parallel",)),
        name="logf_cumsum",
    )(lft)


def _fox_kernel(q_ref, kt_ref, vt_ref, c_ref, o_ref):
    tq = q_ref.shape[0]
    tk = kt_ref.shape[-1]
    i = pl.program_id(2)
    q = q_ref[...]
    lane = lax.broadcasted_iota(jnp.int32, q.shape, 1)
    zero = jnp.zeros_like(q)
    q_heads = (jnp.where(lane < HEAD_DIM, q, zero), jnp.where(lane >= HEAD_DIM, q, zero))
    c_start = tuple(c_ref[hd, pl.ds(i, 1), :][:, 0:1] for hd in range(2))
    row = lax.broadcasted_iota(jnp.int32, (tq, tk), 0)
    col = lax.broadcasted_iota(jnp.int32, (tq, tk), 1)

    def step(j, carry, diagonal):
        kt = kt_ref[j]
        vt = vt_ref[j]
        out = []
        for hd in range(2):
            m, l, acc = carry[hd]
            s = _dot(q_heads[hd], kt) + (c_start[hd] - c_ref[hd, pl.ds(j, 1), :])
            if diagonal:
                s = jnp.where(row >= col, s, -jnp.inf)
            m_new = jnp.maximum(m, jnp.max(s, axis=-1, keepdims=True))
            a = jnp.exp(m - m_new)
            p = jnp.exp(s - m_new)
            l = a * l + jnp.sum(p, axis=-1, keepdims=True)
            acc = a * acc + _dot_nt(p.astype(BF16), vt)
            out.append((m_new, l, acc))
        return tuple(out)

    init_head = (jnp.full((tq, 1), -jnp.inf, F32), jnp.zeros((tq, 1), F32), jnp.zeros((tq, 2 * HEAD_DIM), F32))
    carry = lax.fori_loop(0, i, lambda j, c: step(j, c, False), (init_head, init_head))
    carry = step(i, carry, True)
    o0 = carry[0][2] / carry[0][1]
    o1 = carry[1][2] / carry[1][1]
    o_ref[...] = jnp.where(lane < HEAD_DIM, o0, o1).astype(BF16)


def _fox_call(q, ktc, vtc, c4, batch, seq):
    n, w = q.shape
    tq = ktc.shape[-1]
    nq = seq // tq
    pw = 2 * HEAD_DIM
    return pl.pallas_call(
        _fox_kernel,
        grid=(batch, HEAD_PAIRS, nq),
        in_specs=[pl.BlockSpec((tq, pw), lambda b, p, i: (b * nq + i, p)),
                  pl.BlockSpec((None, nq, pw, tq), lambda b, p, i: (b, 0, p, 0)),
                  pl.BlockSpec((None, nq, pw, tq), lambda b, p, i: (b, 0, p, 0)),
                  pl.BlockSpec((None, 2, nq, tq), lambda b, p, i: (b, p, 0, 0))],
        out_specs=pl.BlockSpec((tq, pw), lambda b, p, i: (b * nq + i, p)),
        out_shape=jax.ShapeDtypeStruct((n, w), BF16),
        compiler_params=_cparams(("parallel", "parallel", "arbitrary"), big=True),
        name="fox_prompt",
    )(q, ktc, vtc, c4)


def _ret_kernel(q_ref, kt_ref, v_ref, sg_ref, lgl_ref, lgr_ref, y_ref, s_ref):
    chunk = RET_CHUNK
    pw = 2 * HEAD_DIM
    nkb, _, tb = kt_ref.shape
    lg_lane = lgl_ref[...]
    lg_row = lgr_ref[...]
    lg_a, lg_b = lg_lane[:, 0:1], lg_lane[:, HEAD_DIM:HEAD_DIM + 1]

    ri = lax.broadcasted_iota(jnp.int32, (2 * chunk, chunk), 0)
    cj = lax.broadcasted_iota(jnp.int32, (2 * chunk, chunk), 1)
    first = ri < chunk
    diff = (jnp.where(first, ri, ri - chunk) - cj).astype(F32)
    decay_mask = jnp.where(diff >= 0, jnp.exp(jnp.maximum(diff, 0.0) * jnp.where(first, lg_a, lg_b)), 0.0)
    jj = lax.broadcasted_iota(jnp.int32, (pw, chunk), 1).astype(F32)
    col_decay = jnp.exp((chunk - 1.0 - jj) * lg_row)
    ii = lax.broadcasted_iota(jnp.int32, (chunk, pw), 0).astype(F32)
    row_decay = jnp.exp((ii + 1.0) * lg_lane)
    chunk_decay = jnp.exp(float(chunk) * lg_row)
    r2 = lax.broadcasted_iota(jnp.int32, (pw, pw), 0)
    c2 = lax.broadcasted_iota(jnp.int32, (pw, pw), 1)
    same_head = (r2 < HEAD_DIM) == (c2 < HEAD_DIM)
    seg_avg = jnp.where(same_head, 1.0 / HEAD_DIM, 0.0).astype(BF16)
    lane = lax.broadcasted_iota(jnp.int32, (chunk, pw), 1)

    def seg_mean(a):
        hi = a.astype(BF16)
        lo = (a - hi.astype(F32)).astype(BF16)
        return _dot(hi, seg_avg) + _dot(lo, seg_avg)

    def one_chunk(q, kt, v, sg, state):
        zero = jnp.zeros_like(q)
        q_stack = jnp.concatenate([jnp.where(lane < HEAD_DIM, q, zero), jnp.where(lane >= HEAD_DIM, q, zero)], axis=0)
        qk = _dot(q_stack, kt.astype(BF16)) * decay_mask
        kd = (kt * col_decay).astype(BF16)
        r = _dot(jnp.concatenate([qk.astype(BF16), kd], axis=0), v)
        inner = jnp.where(lane < HEAD_DIM, r[0:chunk], r[chunk:2 * chunk])
        update = jnp.where(same_head, r[2 * chunk:], 0.0)
        cross = _dot(q, state.astype(BF16)) * row_decay
        o = inner + cross
        mu = seg_mean(o)
        d = o - mu
        var = seg_mean(d * d)
        y = sg.astype(F32) * (d * lax.rsqrt(var + GN_EPS))
        return y.astype(BF16), chunk_decay * state + update

    def body(jb, state):
        kt_blk = kt_ref[jb]
        for sub in range(tb // chunk):
            t0 = pl.multiple_of(jb * tb + sub * chunk, chunk)
            y, state = one_chunk(q_ref[pl.ds(t0, chunk), :], kt_blk[:, sub * chunk:(sub + 1) * chunk],
                                 v_ref[pl.ds(t0, chunk), :], sg_ref[pl.ds(t0, chunk), :], state)
            y_ref[pl.ds(t0, chunk), :] = y
        return state

    state = lax.fori_loop(0, nkb, body, jnp.zeros((pw, pw), F32))
    s_ref[0] = state[0:HEAD_DIM, 0:HEAD_DIM]
    s_ref[1] = pltpu.roll(state, HEAD_DIM, 1)[HEAD_DIM:, 0:HEAD_DIM]


def _ret_call(qr, krt, vr, sg, lg_lane, lg_row, batch, seq):
    n, w = qr.shape
    nkb, tb = krt.shape[1], krt.shape[3]
    pw = 2 * HEAD_DIM
    seq_spec = pl.BlockSpec((seq, pw), lambda b, p: (b, p))
    return pl.pallas_call(
        _ret_kernel,
        grid=(batch, HEAD_PAIRS),
        in_specs=[seq_spec,
                  pl.BlockSpec((None, nkb, pw, tb), lambda b, p: (b, 0, p, 0)),
                  seq_spec, seq_spec,
                  pl.BlockSpec((None, 1, pw), lambda b, p: (p, 0, 0)),
                  pl.BlockSpec((None, pw, 1), lambda b, p: (p, 0, 0))],
        out_specs=[seq_spec, pl.BlockSpec((None, 2, HEAD_DIM, HEAD_DIM), lambda b, p: (b, p, 0, 0))],
        out_shape=[jax.ShapeDtypeStruct((n, w), BF16),
                   jax.ShapeDtypeStruct((batch, N_HEADS, HEAD_DIM, HEAD_DIM), F32)],
        compiler_params=_cparams(("parallel", "parallel"), big=True),
        name="ret_prompt",
    )(qr, krt, vr, sg, lg_lane, lg_row)


def _odd_prompt_kernel(x_ref, sh_ref, sc_ref, g_ref, win_ref, cw_ref, wout_ref, lng_ref, lnb_ref,
                       o_ref, st_ref, carry_ref, *, bps, alpha):
    i = pl.program_id(0)
    x = x_ref[...]
    tm, d = x.shape
    h = (x * (1.0 + sc_ref[...]) + sh_ref[...]).astype(BF16)
    b_gate = _dot(h, win_ref[:, 0:d])
    u = _dot(h, win_ref[:, d:2 * d]) * _dot(h, win_ref[:, 2 * d:3 * d])

    @pl.when(i % bps == 0)
    def _():
        carry_ref[...] = jnp.zeros_like(carry_ref)

    prev = carry_ref[...]
    p1, p2 = prev[7:8, :], prev[6:7, :]
    row = lax.broadcasted_iota(jnp.int32, (tm, d), 0)
    u1 = jnp.where(row == 0, p1, pltpu.roll(u, 1, 0))
    u2 = jnp.where(row == 0, p2, jnp.where(row == 1, p1, pltpu.roll(u, 2, 0)))
    cw = cw_ref[...]
    z = cw[0:1, :] * u2 + cw[1:2, :] * u1 + cw[2:3, :] * u
    carry_ref[...] = u[tm - 8:, :]
    st_ref[...] = u[tm - (CONV_WIDTH - 1):, :]
    y = _dot((b_gate * z).astype(BF16), wout_ref[...])
    zz = alpha * x + g_ref[...] * y
    o_ref[...] = _layernorm(zz, lng_ref[...], lnb_ref[...])


def _odd_prompt_call(x, rows, ks, w_in, conv_w, w_out, ln_g, ln_b, ln_idx, alpha, batch, seq):
    n, d = x.shape
    bps = seq // rows.tm
    return pl.pallas_call(
        functools.partial(_odd_prompt_kernel, bps=bps, alpha=alpha),
        grid=(n // rows.tm,),
        in_specs=[rows.row_spec(d), rows.mod_spec(ks[0]), rows.mod_spec(ks[1]), rows.mod_spec(ks[2]),
                  _const_spec(w_in), _const_spec(conv_w), _const_spec(w_out),
                  pl.BlockSpec((None, 1, d), lambda i: (ln_idx, 0, 0)),
                  pl.BlockSpec((None, 1, d), lambda i: (ln_idx, 0, 0))],
        out_specs=[rows.row_spec(d), pl.BlockSpec((None, CONV_WIDTH - 1, d), lambda i: (i // bps, 0, 0))],
        out_shape=[jax.ShapeDtypeStruct((n, d), F32), jax.ShapeDtypeStruct((batch, CONV_WIDTH - 1, d), F32)],
        scratch_shapes=[pltpu.VMEM((8, d), F32)],
        compiler_params=_cparams(("arbitrary",), big=True),
        name="odd_prompt",
    )(x, rows.mods, rows.mods, rows.mods, w_in, conv_w, w_out, ln_g, ln_b)


def _even_in_sample_kernel(x_ref, sh_ref, sc_ref, w_ref, bf_ref, cos_ref, sin_ref,
                           q_o, k_o, v_o, qr_o, kr_o, vr_o, sg_o, lf_o):
    w = HEAD_WIDTH
    h = (x_ref[...] * (1.0 + sc_ref[...]) + sh_ref[...]).astype(BF16)
    cos, sin = cos_ref[...], sin_ref[...]
    q_o[...] = _dot(h, w_ref[:, 0:w]) * QK_SCALE
    k_o[...] = _dot(h, w_ref[:, w:2 * w])
    v_o[...] = _dot(h, w_ref[:, 2 * w:3 * w])
    qr_o[...] = _rotate_token_major(_dot(h, w_ref[:, 3 * w:4 * w]), cos, sin)
    kr_o[...] = _rotate_token_major(_dot(h, w_ref[:, 4 * w:5 * w]), cos, sin) * QK_SCALE
    vr_o[...] = _dot(h, w_ref[:, 5 * w:6 * w])
    sg_o[...] = _silu(_dot(h, w_ref[:, 6 * w:7 * w]))
    lf_o[...] = _log_sigmoid(_dot(h, w_ref[:, 7 * w:]) + bf_ref[...])


def _even_in_sample_call(x, rows, ks, w_all, bf_row, cos, sin):
    n, d = x.shape
    w = HEAD_WIDTH
    full = pl.BlockSpec((n, w), lambda i: (0, 0))
    shp = jax.ShapeDtypeStruct((n, w), F32)
    return pl.pallas_call(
        _even_in_sample_kernel,
        grid=(1,),
        in_specs=[rows.row_spec(d), rows.mod_spec(ks[0]), rows.mod_spec(ks[1]),
                  _const_spec(w_all), _const_spec(bf_row), full, full],
        out_specs=[full] * 7 + [pl.BlockSpec((n, V7X_LANES), lambda i: (0, 0))],
        out_shape=[shp] * 7 + [jax.ShapeDtypeStruct((n, V7X_LANES), F32)],
        compiler_params=_cparams(("arbitrary",), big=True),
        name="even_inproj_sample",
    )(x, rows.mods, rows.mods, w_all, bf_row, cos, sin)


def _decode_fox_kernel(pt_ref, q_ref, kn_ref, vn_ref, lfn_ref, *refs, pages):
    k_refs, v_refs, lf_refs = refs[0:pages], refs[pages:2 * pages], refs[2 * pages:3 * pages]
    o_ref, qb_s, m_s, l_s, run_s, acc_s = refs[3 * pages:]
    g = pl.program_id(1)
    page_len = k_refs[0].shape[-1]
    hd = HEAD_DIM

    @pl.when(g == 0)
    def _():
        q = q_ref[...]
        qb_s[...] = jnp.broadcast_to(q, qb_s.shape)
        lane = lax.broadcasted_iota(jnp.int32, (hd, page_len), 1)
        for h in range(N_HEADS):
            sl = slice(h * hd, (h + 1) * hd)
            m_s[h:h + 1, :] = jnp.sum(q[sl, :] * kn_ref[sl, :], axis=0, keepdims=True)
            acc_s[sl, :] = jnp.where(lane == 0, vn_ref[sl, :], 0.0)
        l_s[...] = jnp.ones_like(l_s)
        run_s[...] = lfn_ref[...]

    lane8 = lax.broadcasted_iota(jnp.int32, (N_HEADS, page_len), 1)
    for r in range(pages):
        kt = k_refs[r][...]
        vt = v_refs[r][...]
        lf = lf_refs[r][...]
        suf = lf
        shift = 1
        while shift < page_len:
            suf = suf + jnp.where(lane8 < page_len - shift, pltpu.roll(suf, page_len - shift, 1), 0.0)
            shift *= 2
        run = run_s[...]
        bias = (suf - lf) + run
        prod = kt * qb_s[...]
        for h in range(N_HEADS):
            sl = slice(h * hd, (h + 1) * hd)
            s = jnp.sum(prod[sl, :], axis=0, keepdims=True) + bias[h:h + 1, :]
            m = m_s[h:h + 1, :]
            m_new = jnp.maximum(m, jnp.max(s, axis=-1, keepdims=True))
            a = jnp.exp(m - m_new)
            p = jnp.exp(s - m_new)
            l_s[h:h + 1, :] = a * l_s[h:h + 1, :] + jnp.sum(p, axis=-1, keepdims=True)
            m_s[h:h + 1, :] = m_new
            acc_s[sl, :] = a * acc_s[sl, :] + p * vt[sl, :]
        run_s[...] = run + suf[:, 0:1]

    @pl.when(g == pl.num_programs(1) - 1)
    def _():
        for h in range(N_HEADS):
            sl = slice(h * hd, (h + 1) * hd)
            o_ref[sl, :] = jnp.sum(acc_s[sl, :], axis=-1, keepdims=True) / l_s[h:h + 1, :]


def _decode_fox_call(page_table, q_col, kn_col, vn_col, lfn_col, kt_pages, vt_pages, lft_pages):
    nseq, npg = page_table.shape
    pages = DEC_PAGES_PER_STEP
    w, page_len = kt_pages.shape[1], kt_pages.shape[2]

    def page_spec(r, rows):
        return pl.BlockSpec((None, rows, page_len),
                            lambda b, g, pt: (pt[b, npg - 1 - (g * pages + r)], 0, 0))

    col = pl.BlockSpec((None, w, 1), lambda b, g, pt: (b, 0, 0))
    grid_spec = pltpu.PrefetchScalarGridSpec(
        num_scalar_prefetch=1,
        grid=(nseq, npg // pages),
        in_specs=[col, col, col, pl.BlockSpec((None, N_HEADS, 1), lambda b, g, pt: (b, 0, 0))]
        + [page_spec(r, w) for r in range(pages)]
        + [page_spec(r, w) for r in range(pages)]
        + [page_spec(r, N_HEADS) for r in range(pages)],
        out_specs=col,
        scratch_shapes=[pltpu.VMEM((w, page_len), F32), pltpu.VMEM((N_HEADS, 1), F32),
                        pltpu.VMEM((N_HEADS, 1), F32), pltpu.VMEM((N_HEADS, 1), F32),
                        pltpu.VMEM((w, page_len), F32)],
    )
    return pl.pallas_call(
        functools.partial(_decode_fox_kernel, pages=pages),
        grid_spec=grid_spec,
        out_shape=jax.ShapeDtypeStruct((nseq, w, 1), F32),
        compiler_params=_cparams(("parallel", "arbitrary"), big=True),
        name="fox_decode",
    )(page_table, q_col, kn_col, vn_col, lfn_col,
      *([kt_pages] * pages), *([vt_pages] * pages), *([lft_pages] * pages))


def _decode_ret_kernel(q_ref, k_ref, v_ref, sg_ref, s_ref, gl_ref, y_ref, so_ref):
    q, k, v = q_ref[...], k_ref[...], v_ref[...]
    state = s_ref[...]
    g = gl_ref[...]
    inner = jnp.sum(q * k, axis=2, keepdims=True) * v
    cross = jnp.sum(q * state, axis=2, keepdims=True) * g
    so_ref[...] = g * state + k * v
    o = inner + cross
    mu = jnp.mean(o, axis=-1, keepdims=True)
    d = o - mu
    var = jnp.mean(d * d, axis=-1, keepdims=True)
    y_ref[...] = sg_ref[...] * (d * lax.rsqrt(var + GN_EPS))


def _decode_ret_call(q_col, k_col, v_row, sg_row, state, decay):
    nseq = state.shape[0]
    nb = 8
    hd = HEAD_DIM
    col = pl.BlockSpec((nb, N_HEADS, hd, 1), lambda i: (i, 0, 0, 0))
    row = pl.BlockSpec((nb, N_HEADS, 1, hd), lambda i: (i, 0, 0, 0))
    st = pl.BlockSpec((nb, N_HEADS, hd, hd), lambda i: (i, 0, 0, 0))
    return pl.pallas_call(
        _decode_ret_kernel,
        grid=(nseq // nb,),
        in_specs=[col, col, row, row, st, pl.BlockSpec((N_HEADS, 1, 1), lambda i: (0, 0, 0))],
        out_specs=[row, st],
        out_shape=[jax.ShapeDtypeStruct((nseq, N_HEADS, 1, hd), F32),
                   jax.ShapeDtypeStruct(state.shape, F32)],
        compiler_params=_cparams(("parallel",), big=True),
        name="ret_decode",
    )(q_col, k_col, v_row, sg_row, state, decay)


def _odd_sample_kernel(x_ref, sh_ref, sc_ref, g_ref, win_ref, cw_ref, b0_ref, b1_ref, wout_ref, lng_ref, lnb_ref,
                       o_ref, u_ref, *, alpha):
    x = x_ref[...]
    d = x.shape[-1]
    h = (x * (1.0 + sc_ref[...]) + sh_ref[...]).astype(BF16)
    b_gate = _dot(h, win_ref[:, 0:d])
    u = _dot(h, win_ref[:, d:2 * d]) * _dot(h, win_ref[:, 2 * d:3 * d])
    cw = cw_ref[...]
    z = cw[0:1, :] * b0_ref[...] + cw[1:2, :] * b1_ref[...] + cw[2:3, :] * u
    u_ref[...] = u
    y = _dot((b_gate * z).astype(BF16), wout_ref[...])
    o_ref[...] = _layernorm(alpha * x + g_ref[...] * y, lng_ref[...], lnb_ref[...])


def _odd_sample_call(x, rows, ks, w_in, conv_w, buf0, buf1, w_out, ln_g, ln_b, ln_idx, alpha):
    n, d = x.shape
    full = pl.BlockSpec((n, d), lambda i: (0, 0))
    return pl.pallas_call(
        functools.partial(_odd_sample_kernel, alpha=alpha),
        grid=(1,),
        in_specs=[rows.row_spec(d), rows.mod_spec(ks[0]), rows.mod_spec(ks[1]), rows.mod_spec(ks[2]),
                  _const_spec(w_in), _const_spec(conv_w), full, full, _const_spec(w_out),
                  pl.BlockSpec((None, 1, d), lambda i: (ln_idx, 0, 0)),
                  pl.BlockSpec((None, 1, d), lambda i: (ln_idx, 0, 0))],
        out_specs=[full, full],
        out_shape=[jax.ShapeDtypeStruct((n, d), F32), jax.ShapeDtypeStruct((n, d), F32)],
        compiler_params=_cparams(("arbitrary",), big=True),
        name="odd_sample",
    )(x, rows.mods, rows.mods, rows.mods, w_in, conv_w, buf0, buf1, w_out, ln_g, ln_b)


def _rotary_tables(pos):
    half = HEAD_DIM // 2
    inv = 1.0 / (RET_ANGLE_BASE ** jnp.linspace(0.0, 1.0, half, dtype=F32))
    ang = pos.astype(F32)[:, None] * inv[None, :]
    return jnp.cos(ang), jnp.sin(ang)


def _token_major_tables(cos, sin):
    cos_h = jnp.concatenate([cos, cos], axis=1)
    sin_h = jnp.concatenate([-sin, sin], axis=1)
    return jnp.tile(cos_h, (1, N_HEADS)), jnp.tile(sin_h, (1, N_HEADS))


def kernel(x_prompt, x_sample, cache_k, cache_v, cache_logf, state_ret, state_conv, page_table, c_prompt, c_sample,
           w_ada, b_ada, w_ffn_in, w_ffn_out, ln_g, ln_b, w_in_even, b_forget, w_out_even, w_in_odd, conv_w,
           w_out_odd):
    batch, seq, d = x_prompt.shape
    nseq = x_sample.shape[0]
    depth = w_ada.shape[0]
    past_len = page_table.shape[1] * cache_k.shape[2]
    alpha = (2.0 * depth) ** 0.25
    w = HEAD_WIDTH
    tm = ROW_BLOCK
    assert seq % tm == 0 and x_sample.shape[1] == 1 and d % V7X_LANES == 0
    assert cache_k.shape[3] == N_HEADS and cache_k.shape[4] == HEAD_DIM

    n_c = batch + nseq
    pad = (-n_c) % 8
    c_all = jnp.concatenate([c_prompt, c_sample, jnp.zeros((pad, d), F32)], axis=0)
    mods = _ada_call(c_all, w_ada, b_ada)
    mods_p = mods[:, :batch].reshape(depth, batch, 1, N_MOD * d)
    mods_s = mods[:, batch:n_c]

    ln_g3 = ln_g.reshape(depth * 3, 1, d)
    ln_b3 = ln_b.reshape(depth * 3, 1, d)
    w_ffn_in_b = w_ffn_in.astype(BF16)
    w_ffn_out_b = w_ffn_out.astype(BF16)

    xp = x_prompt.reshape(batch * seq, d)
    xs = x_sample.reshape(nseq, d)

    cos_p, sin_p = _rotary_tables(jnp.arange(seq, dtype=jnp.int32))
    rot_p = _token_major_tables(cos_p, sin_p) + (cos_p.T, sin_p.T)
    cos_s, sin_s = _rotary_tables(jnp.full((nseq,), past_len, dtype=jnp.int32))
    cos_s, sin_s = _token_major_tables(cos_s, sin_s)

    log_decay = jnp.log(1.0 - 2.0 ** (-5.0 - jnp.arange(N_HEADS, dtype=F32)))
    lg_pairs = jnp.repeat(log_decay.reshape(HEAD_PAIRS, 2), HEAD_DIM, axis=1)
    lg_lane = lg_pairs.reshape(HEAD_PAIRS, 1, 2 * HEAD_DIM)
    lg_row = lg_pairs.reshape(HEAD_PAIRS, 2 * HEAD_DIM, 1)
    step_decay = jnp.exp(log_decay).reshape(N_HEADS, 1, 1)

    outs_p = {"k": [], "v": [], "lf": [], "ret": [], "conv": []}
    outs_s = {"k": [], "v": [], "lf": [], "ret": [], "conv": []}

    for l in range(depth):
        rows_p = _Rows(mods_p, l, tm, seq // tm, d)
        rows_s = _Rows(mods_s, l, nseq, None, d)
        xp = _ffn_call(xp, rows_p, (0, 1, 2), w_ffn_in_b[l, 0], w_ffn_out_b[l, 0], ln_g3, ln_b3, 3 * l, alpha)
        xs = _ffn_call(xs, rows_s, (0, 1, 2), w_ffn_in_b[l, 0], w_ffn_out_b[l, 0], ln_g3, ln_b3, 3 * l, alpha)
        if l % 2 == 0:
            e = l // 2
            wi = w_in_even[e]
            cuts = [0, w, 2 * w, 3 * w, 3 * w + N_HEADS, 4 * w + N_HEADS, 5 * w + N_HEADS, 6 * w + N_HEADS,
                    7 * w + N_HEADS]
            qf_w, kf_w, vf_w, fg_w, qr_w, kr_w, vr_w, gr_w = [wi[:, a:b] for a, b in zip(cuts[:-1], cuts[1:])]
            w_out_b = w_out_even[e].astype(BF16)
            bf = b_forget[e]

            w_tok = jnp.concatenate([qf_w, qr_w, vr_w, gr_w], axis=1).astype(BF16)
            w_t = jnp.concatenate([kf_w, vf_w, kr_w, fg_w, jnp.zeros((d, 8), F32)], axis=1).T.astype(BF16)
            (q, qr, vr, sg, kt, vt, ktc, vtc, krt, lft) = _even_in_call(
                xp, rows_p, (3, 4), w_tok, w_t, bf.reshape(N_HEADS, 1), rot_p, batch, seq)
            c_t = _cumsum_call(lft)
            of = _fox_call(q, ktc, vtc, c_t.reshape(batch, N_HEADS, seq // tm, tm), batch, seq)
            yr, s_new = _ret_call(qr, krt, vr, sg, lg_lane, lg_row, batch, seq)
            xp = _outproj_call(xp, rows_p, 5, of, yr, w_out_b, ln_g3, ln_b3, 3 * l + 1, alpha)
            outs_p["k"].append(jnp.transpose(kt.reshape(batch, N_HEADS, HEAD_DIM, seq), (0, 3, 1, 2)))
            outs_p["v"].append(jnp.transpose(vt.reshape(batch, N_HEADS, HEAD_DIM, seq), (0, 3, 1, 2)))
            outs_p["lf"].append(jnp.transpose(lft, (0, 2, 1)))
            outs_p["ret"].append(s_new)

            w_all = jnp.concatenate([qf_w, kf_w, vf_w, qr_w, kr_w, vr_w, gr_w, fg_w,
                                     jnp.zeros((d, V7X_LANES - N_HEADS), F32)], axis=1).astype(BF16)
            bf_row = jnp.concatenate([bf, jnp.zeros((V7X_LANES - N_HEADS,), F32)]).reshape(1, V7X_LANES)
            (qs, ks_, vs, qrs, krs, vrs, sgs, lfs) = _even_in_sample_call(xs, rows_s, (3, 4), w_all, bf_row, cos_s, sin_s)
            lfs = lfs[:, :N_HEADS]
            n_phys, page_len = cache_k.shape[1], cache_k.shape[2]
            kt_pages = jnp.transpose(cache_k[e], (0, 2, 3, 1)).reshape(n_phys, w, page_len)
            vt_pages = jnp.transpose(cache_v[e], (0, 2, 3, 1)).reshape(n_phys, w, page_len)
            lft_pages = jnp.transpose(cache_logf[e], (0, 2, 1))
            of_s = _decode_fox_call(page_table, qs.reshape(nseq, w, 1), ks_.reshape(nseq, w, 1),
                                    vs.reshape(nseq, w, 1), lfs.reshape(nseq, N_HEADS, 1),
                                    kt_pages, vt_pages, lft_pages).reshape(nseq, w)
            col = (nseq, N_HEADS, HEAD_DIM, 1)
            rw = (nseq, N_HEADS, 1, HEAD_DIM)
            yr_s, s_new_s = _decode_ret_call(qrs.reshape(col), krs.reshape(col), vrs.reshape(rw), sgs.reshape(rw),
                                             state_ret[e], step_decay)
            xs = _outproj_call(xs, rows_s, 5, of_s, yr_s.reshape(nseq, w), w_out_b, ln_g3, ln_b3, 3 * l + 1, alpha)
            outs_s["k"].append(ks_.reshape(nseq, 1, N_HEADS, HEAD_DIM))
            outs_s["v"].append(vs.reshape(nseq, 1, N_HEADS, HEAD_DIM))
            outs_s["lf"].append(lfs.reshape(nseq, 1, N_HEADS))
            outs_s["ret"].append(s_new_s)
        else:
            o = l // 2
            w_in_b = w_in_odd[o].astype(BF16)
            w_out_b = w_out_odd[o].astype(BF16)
            xp, conv_p = _odd_prompt_call(xp, rows_p, (3, 4, 5), w_in_b, conv_w[o], w_out_b, ln_g3, ln_b3,
                                          3 * l + 1, alpha, batch, seq)
            buf = state_conv[o]
            xs, u_s = _odd_sample_call(xs, rows_s, (3, 4, 5), w_in_b, conv_w[o], buf[:, 0], buf[:, 1], w_out_b,
                                       ln_g3, ln_b3, 3 * l + 1, alpha)
            outs_p["conv"].append(conv_p)
            outs_s["conv"].append(jnp.stack([buf[:, 1], u_s], axis=1))
        xp = _ffn_call(xp, rows_p, (6, 7, 8), w_ffn_in_b[l, 1], w_ffn_out_b[l, 1], ln_g3, ln_b3, 3 * l + 2, alpha)
        xs = _ffn_call(xs, rows_s, (6, 7, 8), w_ffn_in_b[l, 1], w_ffn_out_b[l, 1], ln_g3, ln_b3, 3 * l + 2, alpha)

    def stk(lst):
        return jnp.stack(lst)

    return (xp.reshape(batch, seq, d), xs.reshape(nseq, 1, d),
            stk(outs_p["k"]), stk(outs_p["v"]), stk(outs_p["lf"]), stk(outs_p["ret"]), stk(outs_p["conv"]),
            stk(outs_s["k"]), stk(outs_s["v"]), stk(outs_s["lf"]), stk(outs_s["ret"]), stk(outs_s["conv"]))
```

```python
import functools

import jax
import jax.numpy as jnp
from jax import lax
from jax.experimental import pallas as pl
from jax.experimental.pallas import tpu as pltpu

F32 = jnp.float32
BF16 = jnp.bfloat16

HEAD_DIM = 64
N_HEADS = 8
HEAD_WIDTH = N_HEADS * HEAD_DIM
N_MOD = 9
CONV_WIDTH = 3
RET_ANGLE_BASE = 10000.0
LN_EPS = 1e-5
GN_EPS = 1e-6
QK_SCALE = HEAD_DIM ** -0.5

V7X_LANES = 128
V7X_VMEM_LIMIT_BYTES = 56 * 1024 * 1024

ROW_BLOCK = 512
FF_CHUNK = 256
RET_CHUNK = 256
ADA_COL_BLOCK = 1152
DEC_PAGES_PER_STEP = 16
HEAD_PAIRS = N_HEADS // 2


def _cparams(sem, big=False):
    return pltpu.CompilerParams(dimension_semantics=sem,
                                vmem_limit_bytes=V7X_VMEM_LIMIT_BYTES if big else None)


def _const_spec(arr):
    nd = arr.ndim
    return pl.BlockSpec(arr.shape, lambda *_: (0,) * nd, pipeline_mode=pl.Buffered(1))


def _layernorm(z, g, b):
    mu = jnp.mean(z, axis=-1, keepdims=True)
    d = z - mu
    var = jnp.mean(d * d, axis=-1, keepdims=True)
    return d * lax.rsqrt(var + LN_EPS) * g + b


def _silu(a):
    return a * jax.nn.sigmoid(a)


def _log_sigmoid(z):
    return jnp.minimum(z, 0.0) - jnp.log1p(jnp.exp(-jnp.abs(z)))


def _dot(a, b):
    return jnp.dot(a, b, preferred_element_type=F32)


def _dot_nt(a, b):
    return lax.dot_general(a, b, (((1,), (1,)), ((), ())), preferred_element_type=F32)


def _ada_kernel(c_ref, w_ref, b_ref, o_ref):
    s = _silu(c_ref[...]).astype(BF16)
    o_ref[...] = _dot(s, w_ref[...].astype(BF16)) + b_ref[...]


def _ada_call(c_all, w_ada, b_ada):
    depth, d, nm = w_ada.shape
    rows = c_all.shape[0]
    tn = ADA_COL_BLOCK
    return pl.pallas_call(
        _ada_kernel,
        grid=(depth, nm // tn),
        in_specs=[pl.BlockSpec((rows, d), lambda l, j: (0, 0)),
                  pl.BlockSpec((None, d, tn), lambda l, j: (l, 0, j)),
                  pl.BlockSpec((None, 1, tn), lambda l, j: (l, 0, j))],
        out_specs=pl.BlockSpec((None, rows, tn), lambda l, j: (l, 0, j)),
        out_shape=jax.ShapeDtypeStruct((depth, rows, nm), F32),
        compiler_params=_cparams(("arbitrary", "arbitrary"), big=True),
        name="ada_mods",
    )(c_all, w_ada, b_ada.reshape(depth, 1, nm))


class _Rows:
    def __init__(self, mods, layer, tm, blocks_per_seq, d):
        self.mods, self.layer, self.tm, self.bps, self.d = mods, layer, tm, blocks_per_seq, d

    def mod_spec(self, k):
        l, d = self.layer, self.d
        if self.bps is None:
            return pl.BlockSpec((None, self.tm, d), lambda i: (l, 0, k))
        bps = self.bps
        return pl.BlockSpec((None, None, 1, d), lambda i: (l, i // bps, 0, k))

    def row_spec(self, width):
        return pl.BlockSpec((self.tm, width), lambda i: (i, 0))


def _ffn_kernel(x_ref, sh_ref, sc_ref, g_ref, win_ref, wout_ref, lng_ref, lnb_ref, o_ref, *, ff, alpha):
    x = x_ref[...]
    h = (x * (1.0 + sc_ref[...]) + sh_ref[...]).astype(BF16)
    acc = None
    for c in range(ff // FF_CHUNK):
        lo, hi = c * FF_CHUNK, (c + 1) * FF_CHUNK
        a = _dot(h, win_ref[:, lo:hi])
        b = _dot(h, win_ref[:, ff + lo:ff + hi])
        y = _dot((_silu(a) * b).astype(BF16), wout_ref[lo:hi, :])
        acc = y if acc is None else acc + y
    z = alpha * x + (0.5 * g_ref[...]) * acc
    o_ref[...] = _layernorm(z, lng_ref[...], lnb_ref[...])


def _ffn_call(x, rows, ks, w_in, w_out, which, ln_g, ln_b, ln_idx, alpha):
    n, d = x.shape
    ff = w_out.shape[2]
    l, j = which

    def w_spec(arr):
        return pl.BlockSpec((None, None) + arr.shape[2:], lambda i: (l, j, 0, 0), pipeline_mode=pl.Buffered(1))

    return pl.pallas_call(
        functools.partial(_ffn_kernel, ff=ff, alpha=alpha),
        grid=(n // rows.tm,),
        in_specs=[rows.row_spec(d), rows.mod_spec(ks[0]), rows.mod_spec(ks[1]), rows.mod_spec(ks[2]),
                  w_spec(w_in), w_spec(w_out),
                  pl.BlockSpec((None, 1, d), lambda i: (ln_idx, 0, 0)),
                  pl.BlockSpec((None, 1, d), lambda i: (ln_idx, 0, 0))],
        out_specs=rows.row_spec(d),
        out_shape=jax.ShapeDtypeStruct((n, d), F32),
        compiler_params=_cparams(("parallel",), big=True),
        name="ffn",
    )(x, rows.mods, rows.mods, rows.mods, w_in, w_out, ln_g, ln_b)


def _outproj_kernel(x_ref, g_ref, a0_ref, a1_ref, w_ref, lng_ref, lnb_ref, o_ref, *, alpha):
    half = a0_ref.shape[-1]
    y = _dot(a0_ref[...].astype(BF16), w_ref[0:half, :]) + _dot(a1_ref[...].astype(BF16), w_ref[half:, :])
    z = alpha * x_ref[...] + g_ref[...] * y
    o_ref[...] = _layernorm(z, lng_ref[...], lnb_ref[...])


def _outproj_call(x, rows, k_gate, a0, a1, w_out, ln_g, ln_b, ln_idx, alpha):
    n, d = x.shape
    return pl.pallas_call(
        functools.partial(_outproj_kernel, alpha=alpha),
        grid=(n // rows.tm,),
        in_specs=[rows.row_spec(d), rows.mod_spec(k_gate), rows.row_spec(a0.shape[1]), rows.row_spec(a1.shape[1]),
                  _const_spec(w_out),
                  pl.BlockSpec((None, 1, d), lambda i: (ln_idx, 0, 0)),
                  pl.BlockSpec((None, 1, d), lambda i: (ln_idx, 0, 0))],
        out_specs=rows.row_spec(d),
        out_shape=jax.ShapeDtypeStruct((n, d), F32),
        compiler_params=_cparams(("parallel",), big=True),
        name="mixer_outproj",
    )(x, rows.mods, a0, a1, w_out, ln_g, ln_b)


def _rotate_token_major(a, cos, sin_signed):
    width = a.shape[-1]
    lane = lax.broadcasted_iota(jnp.int32, a.shape, 1)
    first_half = (lane & (HEAD_DIM // 2)) == 0
    partner = jnp.where(first_half, pltpu.roll(a, width - HEAD_DIM // 2, 1), pltpu.roll(a, HEAD_DIM // 2, 1))
    return a * cos + partner * sin_signed


def _even_in_kernel(x_ref, sh_ref, sc_ref, wtok_ref, wt_ref, bf_ref, cos_ref, sin_ref, cost_ref, sint_ref,
                    q_o, qr_o, vr_o, sg_o, kt_o, vt_o, ktc_o, vtc_o, krt_o, lft_o):
    w = HEAD_WIDTH
    x = x_ref[...]
    h = (x * (1.0 + sc_ref[...]) + sh_ref[...]).astype(BF16)

    q_o[...] = (_dot(h, wtok_ref[:, 0:w]) * QK_SCALE).astype(BF16)
    qr = _dot(h, wtok_ref[:, w:2 * w])
    qr_o[...] = _rotate_token_major(qr, cos_ref[...], sin_ref[...]).astype(BF16)
    vr_o[...] = _dot(h, wtok_ref[:, 2 * w:3 * w]).astype(BF16)
    sg_o[...] = _silu(_dot(h, wtok_ref[:, 3 * w:4 * w])).astype(BF16)

    kt = _dot_nt(wt_ref[0:w, :], h)
    kt_o[...] = kt
    ktc_o[...] = kt.astype(BF16)
    vt = _dot_nt(wt_ref[w:2 * w, :], h)
    vt_o[...] = vt
    vtc_o[...] = vt.astype(BF16)

    krt = _dot_nt(wt_ref[2 * w:3 * w, :], h)
    cos_t, sin_t = cost_ref[...], sint_ref[...]
    half = HEAD_DIM // 2
    for hh in range(N_HEADS):
        x1 = krt[hh * HEAD_DIM:hh * HEAD_DIM + half, :]
        x2 = krt[hh * HEAD_DIM + half:(hh + 1) * HEAD_DIM, :]
        krt_o[hh * HEAD_DIM:hh * HEAD_DIM + half, :] = (x1 * cos_t - x2 * sin_t) * QK_SCALE
        krt_o[hh * HEAD_DIM + half:(hh + 1) * HEAD_DIM, :] = (x1 * sin_t + x2 * cos_t) * QK_SCALE

    fg = _dot_nt(wt_ref[3 * w:, :], h)[0:N_HEADS, :] + bf_ref[...]
    lft_o[...] = _log_sigmoid(fg)


def _even_in_call(x, rows, ks, w_tok, w_t, bf_col, rot, batch, seq):
    n, d = x.shape
    tm = rows.tm
    bps = seq // tm
    w = HEAD_WIDTH
    cos_tok, sin_tok, cos_t, sin_t = rot
    tok_out = pl.BlockSpec((tm, w), lambda i: (i, 0))
    t_out = pl.BlockSpec((None, w, tm), lambda i: (i // bps, 0, i % bps))
    tc_out = pl.BlockSpec((None, None, w, tm), lambda i: (i // bps, i % bps, 0, 0))
    tok_shape = jax.ShapeDtypeStruct((n, w), BF16)
    return pl.pallas_call(
        _even_in_kernel,
        grid=(n // tm,),
        in_specs=[rows.row_spec(d), rows.mod_spec(ks[0]), rows.mod_spec(ks[1]),
                  _const_spec(w_tok), _const_spec(w_t), _const_spec(bf_col),
                  pl.BlockSpec((tm, w), lambda i: (i % bps, 0)),
                  pl.BlockSpec((tm, w), lambda i: (i % bps, 0)),
                  pl.BlockSpec((HEAD_DIM // 2, tm), lambda i: (0, i % bps)),
                  pl.BlockSpec((HEAD_DIM // 2, tm), lambda i: (0, i % bps))],
        out_specs=[tok_out, tok_out, tok_out, tok_out, t_out, t_out, tc_out, tc_out, tc_out,
                   pl.BlockSpec((None, N_HEADS, tm), lambda i: (i // bps, 0, i % bps))],
        out_shape=[tok_shape, tok_shape, tok_shape, tok_shape,
                   jax.ShapeDtypeStruct((batch, w, seq), F32),
                   jax.ShapeDtypeStruct((batch, w, seq), F32),
                   jax.ShapeDtypeStruct((batch, bps, w, tm), BF16),
                   jax.ShapeDtypeStruct((batch, bps, w, tm), BF16),
                   jax.ShapeDtypeStruct((batch, bps, w, tm), F32),
                   jax.ShapeDtypeStruct((batch, N_HEADS, seq), F32)],
        compiler_params=_cparams(("parallel",), big=True),
        name="even_inproj",
    )(x, rows.mods, rows.mods, w_tok, w_t, bf_col, cos_tok, sin_tok, cos_t, sin_t)


def _cumsum_kernel(x_ref, o_ref):
    x = x_ref[...]
    n = x.shape[-1]
    lane = lax.broadcasted_iota(jnp.int32, x.shape, 1)
    shift = 1
    while shift < n:
        x = x + jnp.where(lane >= shift, pltpu.roll(x, shift, 1), 0.0)
        shift *= 2
    o_ref[...] = x


def _cumsum_call(lft):
    batch, heads, seq = lft.shape
    spec = pl.BlockSpec((None, heads, seq), lambda b: (b, 0, 0))
    return pl.pallas_call(
        _cumsum_kernel, grid=(batch,), in_specs=[spec], out_specs=spec,
        out_shape=jax.ShapeDtypeStruct(lft.shape, F32),
        compiler_params=_cparams(("parallel",)),
        name="logf_cumsum",
    )(lft)


def _fox_kernel(q_ref, kt_ref, vt_ref, c_ref, o_ref):
    tq = q_ref.shape[0]
    tk = kt_ref.shape[-1]
    i = pl.program_id(2)
    q = q_ref[...]
    lane = lax.broadcasted_iota(jnp.int32, q.shape, 1)
    zero = jnp.zeros_like(q)
    q_heads = (jnp.where(lane < HEAD_DIM, q, zero), jnp.where(lane >= HEAD_DIM, q, zero))
    c_start = tuple(c_ref[hd, pl.ds(i, 1), :][:, 0:1] for hd in range(2))
    row = lax.broadcasted_iota(jnp.int32, (tq, tk), 0)
    col = lax.broadcasted_iota(jnp.int32, (tq, tk), 1)

    def step(j, carry, diagonal):
        kt = kt_ref[j]
        vt = vt_ref[j]
        out = []
        for hd in range(2):
            m, l, acc = carry[hd]
            s = _dot(q_heads[hd], kt) + (c_start[hd] - c_ref[hd, pl.ds(j, 1), :])
            if diagonal:
                s = jnp.where(row >= col, s, -jnp.inf)
            m_new = jnp.maximum(m, jnp.max(s, axis=-1, keepdims=True))
            a = jnp.exp(m - m_new)
            p = jnp.exp(s - m_new)
            l = a * l + jnp.sum(p, axis=-1, keepdims=True)
            acc = a * acc + _dot_nt(p.astype(BF16), vt)
            out.append((m_new, l, acc))
        return tuple(out)

    init_head = (jnp.full((tq, 1), -jnp.inf, F32), jnp.zeros((tq, 1), F32), jnp.zeros((tq, 2 * HEAD_DIM), F32))
    carry = lax.fori_loop(0, i, lambda j, c: step(j, c, False), (init_head, init_head))
    carry = step(i, carry, True)
    o0 = carry[0][2] / carry[0][1]
    o1 = carry[1][2] / carry[1][1]
    o_ref[...] = jnp.where(lane < HEAD_DIM, o0, o1).astype(BF16)


def _fox_call(q, ktc, vtc, c4, batch, seq):
    n, w = q.shape
    tq = ktc.shape[-1]
    nq = seq // tq
    pw = 2 * HEAD_DIM
    return pl.pallas_call(
        _fox_kernel,
        grid=(batch, HEAD_PAIRS, nq),
        in_specs=[pl.BlockSpec((tq, pw), lambda b, p, i: (b * nq + i, p)),
                  pl.BlockSpec((None, nq, pw, tq), lambda b, p, i: (b, 0, p, 0)),
                  pl.BlockSpec((None, nq, pw, tq), lambda b, p, i: (b, 0, p, 0)),
                  pl.BlockSpec((None, 2, nq, tq), lambda b, p, i: (b, p, 0, 0))],
        out_specs=pl.BlockSpec((tq, pw), lambda b, p, i: (b * nq + i, p)),
        out_shape=jax.ShapeDtypeStruct((n, w), BF16),
        compiler_params=_cparams(("parallel", "parallel", "arbitrary"), big=True),
        name="fox_prompt",
    )(q, ktc, vtc, c4)


def _ret_kernel(q_ref, kt_ref, v_ref, sg_ref, lgl_ref, lgr_ref, y_ref, s_ref):
    chunk = RET_CHUNK
    pw = 2 * HEAD_DIM
    nkb, _, tb = kt_ref.shape
    lg_lane = lgl_ref[...]
    lg_row = lgr_ref[...]
    lg_a, lg_b = lg_lane[:, 0:1], lg_lane[:, HEAD_DIM:HEAD_DIM + 1]

    ri = lax.broadcasted_iota(jnp.int32, (2 * chunk, chunk), 0)
    cj = lax.broadcasted_iota(jnp.int32, (2 * chunk, chunk), 1)
    first = ri < chunk
    diff = (jnp.where(first, ri, ri - chunk) - cj).astype(F32)
    decay_mask = jnp.where(diff >= 0, jnp.exp(jnp.maximum(diff, 0.0) * jnp.where(first, lg_a, lg_b)), 0.0)
    jj = lax.broadcasted_iota(jnp.int32, (pw, chunk), 1).astype(F32)
    col_decay = jnp.exp((chunk - 1.0 - jj) * lg_row)
    ii = lax.broadcasted_iota(jnp.int32, (chunk, pw), 0).astype(F32)
    row_decay = jnp.exp((ii + 1.0) * lg_lane)
    chunk_decay = jnp.exp(float(chunk) * lg_row)
    r2 = lax.broadcasted_iota(jnp.int32, (pw, pw), 0)
    c2 = lax.broadcasted_iota(jnp.int32, (pw, pw), 1)
    same_head = (r2 < HEAD_DIM) == (c2 < HEAD_DIM)
    seg_avg = jnp.where(same_head, 1.0 / HEAD_DIM, 0.0).astype(BF16)
    lane = lax.broadcasted_iota(jnp.int32, (chunk, pw), 1)

    def seg_mean(a):
        hi = a.astype(BF16)
        lo = (a - hi.astype(F32)).astype(BF16)
        return _dot(hi, seg_avg) + _dot(lo, seg_avg)

    def one_chunk(q, kt, v, sg, state):
        zero = jnp.zeros_like(q)
        q_stack = jnp.concatenate([jnp.where(lane < HEAD_DIM, q, zero), jnp.where(lane >= HEAD_DIM, q, zero)], axis=0)
        qk = _dot(q_stack, kt.astype(BF16)) * decay_mask
        kd = (kt * col_decay).astype(BF16)
        r = _dot(jnp.concatenate([qk.astype(BF16), kd], axis=0), v)
        inner = jnp.where(lane < HEAD_DIM, r[0:chunk], r[chunk:2 * chunk])
        update = jnp.where(same_head, r[2 * chunk:], 0.0)
        cross = _dot(q, state.astype(BF16)) * row_decay
        o = inner + cross
        mu = seg_mean(o)
        d = o - mu
        var = seg_mean(d * d)
        y = sg.astype(F32) * (d * lax.rsqrt(var + GN_EPS))
        return y.astype(BF16), chunk_decay * state + update

    def body(jb, state):
        kt_blk = kt_ref[jb]
        for sub in range(tb // chunk):
            t0 = pl.multiple_of(jb * tb + sub * chunk, chunk)
            y, state = one_chunk(q_ref[pl.ds(t0, chunk), :], kt_blk[:, sub * chunk:(sub + 1) * chunk],
                                 v_ref[pl.ds(t0, chunk), :], sg_ref[pl.ds(t0, chunk), :], state)
            y_ref[pl.ds(t0, chunk), :] = y
        return state

    state = lax.fori_loop(0, nkb, body, jnp.zeros((pw, pw), F32))
    s_ref[0] = state[0:HEAD_DIM, 0:HEAD_DIM]
    s_ref[1] = pltpu.roll(state, HEAD_DIM, 1)[HEAD_DIM:, 0:HEAD_DIM]


def _ret_call(qr, krt, vr, sg, lg_lane, lg_row, batch, seq):
    n, w = qr.shape
    nkb, tb = krt.shape[1], krt.shape[3]
    pw = 2 * HEAD_DIM
    seq_spec = pl.BlockSpec((seq, pw), lambda b, p: (b, p))
    return pl.pallas_call(
        _ret_kernel,
        grid=(batch, HEAD_PAIRS),
        in_specs=[seq_spec,
                  pl.BlockSpec((None, nkb, pw, tb), lambda b, p: (b, 0, p, 0)),
                  seq_spec, seq_spec,
                  pl.BlockSpec((None, 1, pw), lambda b, p: (p, 0, 0)),
                  pl.BlockSpec((None, pw, 1), lambda b, p: (p, 0, 0))],
        out_specs=[seq_spec, pl.BlockSpec((None, 2, HEAD_DIM, HEAD_DIM), lambda b, p: (b, p, 0, 0))],
        out_shape=[jax.ShapeDtypeStruct((n, w), BF16),
                   jax.ShapeDtypeStruct((batch, N_HEADS, HEAD_DIM, HEAD_DIM), F32)],
        compiler_params=_cparams(("parallel", "parallel"), big=True),
        name="ret_prompt",
    )(qr, krt, vr, sg, lg_lane, lg_row)


def _odd_prompt_kernel(x_ref, sh_ref, sc_ref, g_ref, win_ref, cw_ref, wout_ref, lng_ref, lnb_ref,
                       o_ref, st_ref, carry_ref, *, bps, alpha):
    i = pl.program_id(0)
    x = x_ref[...]
    tm, d = x.shape
    h = (x * (1.0 + sc_ref[...]) + sh_ref[...]).astype(BF16)
    b_gate = _dot(h, win_ref[:, 0:d])
    u = _dot(h, win_ref[:, d:2 * d]) * _dot(h, win_ref[:, 2 * d:3 * d])

    @pl.when(i % bps == 0)
    def _():
        carry_ref[...] = jnp.zeros_like(carry_ref)

    prev = carry_ref[...]
    p1, p2 = prev[7:8, :], prev[6:7, :]
    row = lax.broadcasted_iota(jnp.int32, (tm, d), 0)
    u1 = jnp.where(row == 0, p1, pltpu.roll(u, 1, 0))
    u2 = jnp.where(row == 0, p2, jnp.where(row == 1, p1, pltpu.roll(u, 2, 0)))
    cw = cw_ref[...]
    z = cw[0:1, :] * u2 + cw[1:2, :] * u1 + cw[2:3, :] * u
    carry_ref[...] = u[tm - 8:, :]
    st_ref[...] = u[tm - (CONV_WIDTH - 1):, :]
    y = _dot((b_gate * z).astype(BF16), wout_ref[...])
    zz = alpha * x + g_ref[...] * y
    o_ref[...] = _layernorm(zz, lng_ref[...], lnb_ref[...])


def _odd_prompt_call(x, rows, ks, w_in, conv_w, w_out, ln_g, ln_b, ln_idx, alpha, batch, seq):
    n, d = x.shape
    bps = seq // rows.tm
    return pl.pallas_call(
        functools.partial(_odd_prompt_kernel, bps=bps, alpha=alpha),
        grid=(n // rows.tm,),
        in_specs=[rows.row_spec(d), rows.mod_spec(ks[0]), rows.mod_spec(ks[1]), rows.mod_spec(ks[2]),
                  _const_spec(w_in), _const_spec(conv_w), _const_spec(w_out),
                  pl.BlockSpec((None, 1, d), lambda i: (ln_idx, 0, 0)),
                  pl.BlockSpec((None, 1, d), lambda i: (ln_idx, 0, 0))],
        out_specs=[rows.row_spec(d), pl.BlockSpec((None, CONV_WIDTH - 1, d), lambda i: (i // bps, 0, 0))],
        out_shape=[jax.ShapeDtypeStruct((n, d), F32), jax.ShapeDtypeStruct((batch, CONV_WIDTH - 1, d), F32)],
        scratch_shapes=[pltpu.VMEM((8, d), F32)],
        compiler_params=_cparams(("arbitrary",), big=True),
        name="odd_prompt",
    )(x, rows.mods, rows.mods, rows.mods, w_in, conv_w, w_out, ln_g, ln_b)


def _even_in_sample_kernel(x_ref, sh_ref, sc_ref, w_ref, bf_ref, cos_ref, sin_ref,
                           q_o, k_o, v_o, qr_o, kr_o, vr_o, sg_o, lf_o):
    w = HEAD_WIDTH
    h = (x_ref[...] * (1.0 + sc_ref[...]) + sh_ref[...]).astype(BF16)
    cos, sin = cos_ref[...], sin_ref[...]
    q_o[...] = _dot(h, w_ref[:, 0:w]) * QK_SCALE
    k_o[...] = _dot(h, w_ref[:, w:2 * w])
    v_o[...] = _dot(h, w_ref[:, 2 * w:3 * w])
    qr_o[...] = _rotate_token_major(_dot(h, w_ref[:, 3 * w:4 * w]), cos, sin)
    kr_o[...] = _rotate_token_major(_dot(h, w_ref[:, 4 * w:5 * w]), cos, sin) * QK_SCALE
    vr_o[...] = _dot(h, w_ref[:, 5 * w:6 * w])
    sg_o[...] = _silu(_dot(h, w_ref[:, 6 * w:7 * w]))
    lf_o[...] = _log_sigmoid(_dot(h, w_ref[:, 7 * w:]) + bf_ref[...])


def _even_in_sample_call(x, rows, ks, w_all, bf_row, cos, sin):
    n, d = x.shape
    w = HEAD_WIDTH
    full = pl.BlockSpec((n, w), lambda i: (0, 0))
    shp = jax.ShapeDtypeStruct((n, w), F32)
    return pl.pallas_call(
        _even_in_sample_kernel,
        grid=(1,),
        in_specs=[rows.row_spec(d), rows.mod_spec(ks[0]), rows.mod_spec(ks[1]),
                  _const_spec(w_all), _const_spec(bf_row), full, full],
        out_specs=[full] * 7 + [pl.BlockSpec((n, V7X_LANES), lambda i: (0, 0))],
        out_shape=[shp] * 7 + [jax.ShapeDtypeStruct((n, V7X_LANES), F32)],
        compiler_params=_cparams(("arbitrary",), big=True),
        name="even_inproj_sample",
    )(x, rows.mods, rows.mods, w_all, bf_row, cos, sin)


def _decode_fox_kernel(pt_ref, q_ref, kn_ref, vn_ref, lfn_ref, *refs, pages):
    k_refs, v_refs, lf_refs = refs[0:pages], refs[pages:2 * pages], refs[2 * pages:3 * pages]
    o_ref, m_s, l_s, run_s, acc_s = refs[3 * pages:]
    g = pl.program_id(1)
    page_len = k_refs[0].shape[-1]
    w = HEAD_WIDTH
    sub = lax.broadcasted_iota(jnp.int32, (N_HEADS, w), 0)
    lane = lax.broadcasted_iota(jnp.int32, (N_HEADS, w), 1)
    own = (lane >= sub * HEAD_DIM) & (lane < (sub + 1) * HEAD_DIM)
    q_bd = jnp.where(own, jnp.broadcast_to(q_ref[...], (N_HEADS, w)), 0.0)

    @pl.when(g == 0)
    def _():
        m_s[...] = jnp.sum(q_bd * kn_ref[...], axis=-1, keepdims=True)
        l_s[...] = jnp.ones_like(l_s)
        acc_s[...] = jnp.broadcast_to(vn_ref[...], (N_HEADS, w))
        s128 = lax.broadcasted_iota(jnp.int32, (N_HEADS, V7X_LANES), 0)
        l128 = lax.broadcasted_iota(jnp.int32, (N_HEADS, V7X_LANES), 1)
        lfn = jnp.broadcast_to(lfn_ref[...], (N_HEADS, V7X_LANES))
        run_s[...] = jnp.sum(jnp.where(s128 == l128, lfn, 0.0), axis=-1, keepdims=True)

    q_b = q_bd.astype(BF16)
    order = list(range(pages - 1, -1, -1))
    s = jnp.concatenate([_dot(q_b, k_refs[r][...].astype(BF16)) for r in order], axis=1)
    lf = jnp.concatenate([lf_refs[r][...] for r in order], axis=1)
    n = pages * page_len
    lane_n = lax.broadcasted_iota(jnp.int32, (N_HEADS, n), 1)
    suf = lf
    shift = 1
    while shift < n:
        suf = suf + jnp.where(lane_n < n - shift, pltpu.roll(suf, n - shift, 1), 0.0)
        shift *= 2
    run = run_s[...]
    s = s + ((suf - lf) + run)
    m = m_s[...]
    m_new = jnp.maximum(m, jnp.max(s, axis=-1, keepdims=True))
    a = jnp.exp(m - m_new)
    p = jnp.exp(s - m_new)
    l_s[...] = a * l_s[...] + jnp.sum(p, axis=-1, keepdims=True)
    m_s[...] = m_new
    p_b = p.astype(BF16)
    pv = None
    for idx, r in enumerate(order):
        t = _dot_nt(p_b[:, idx * page_len:(idx + 1) * page_len], v_refs[r][...].astype(BF16))
        pv = t if pv is None else pv + t
    acc_s[...] = a * acc_s[...] + pv
    run_s[...] = run + suf[:, 0:1]

    @pl.when(g == pl.num_programs(1) - 1)
    def _():
        o = acc_s[...] / l_s[...]
        o_ref[...] = jnp.sum(jnp.where(own, o, 0.0), axis=0, keepdims=True)


def _decode_fox_call(page_table, q_row, kn_row, vn_row, lfn_row, kt_pages, vt_pages, lft_pages):
    nseq, npg = page_table.shape
    pages = DEC_PAGES_PER_STEP
    w, page_len = kt_pages.shape[1], kt_pages.shape[2]

    def page_spec(r, rows):
        return pl.BlockSpec((None, rows, page_len),
                            lambda b, g, pt: (pt[b, npg - 1 - (g * pages + r)], 0, 0))

    row = pl.BlockSpec((None, 1, w), lambda b, g, pt: (b, 0, 0))
    grid_spec = pltpu.PrefetchScalarGridSpec(
        num_scalar_prefetch=1,
        grid=(nseq, npg // pages),
        in_specs=[row, row, row, pl.BlockSpec((None, 1, V7X_LANES), lambda b, g, pt: (b, 0, 0))]
        + [page_spec(r, w) for r in range(pages)]
        + [page_spec(r, w) for r in range(pages)]
        + [page_spec(r, N_HEADS) for r in range(pages)],
        out_specs=row,
        scratch_shapes=[pltpu.VMEM((N_HEADS, 1), F32), pltpu.VMEM((N_HEADS, 1), F32),
                        pltpu.VMEM((N_HEADS, 1), F32), pltpu.VMEM((N_HEADS, w), F32)],
    )
    return pl.pallas_call(
        functools.partial(_decode_fox_kernel, pages=pages),
        grid_spec=grid_spec,
        out_shape=jax.ShapeDtypeStruct((nseq, 1, w), F32),
        compiler_params=_cparams(("parallel", "arbitrary"), big=True),
        name="fox_decode",
    )(page_table, q_row, kn_row, vn_row, lfn_row,
      *([kt_pages] * pages), *([vt_pages] * pages), *([lft_pages] * pages))


def _decode_ret_kernel(q_ref, k_ref, v_ref, sg_ref, s_ref, gl_ref, y_ref, so_ref):
    q, k, v = q_ref[...], k_ref[...], v_ref[...]
    state = s_ref[...]
    g = gl_ref[...]
    inner = jnp.sum(q * k, axis=2, keepdims=True) * v
    cross = jnp.sum(q * state, axis=2, keepdims=True) * g
    so_ref[...] = g * state + k * v
    o = inner + cross
    mu = jnp.mean(o, axis=-1, keepdims=True)
    d = o - mu
    var = jnp.mean(d * d, axis=-1, keepdims=True)
    y_ref[...] = sg_ref[...] * (d * lax.rsqrt(var + GN_EPS))


def _decode_ret_call(q_col, k_col, v_row, sg_row, state, decay):
    nseq = state.shape[0]
    nb = 8
    hd = HEAD_DIM
    col = pl.BlockSpec((nb, N_HEADS, hd, 1), lambda i: (i, 0, 0, 0))
    row = pl.BlockSpec((nb, N_HEADS, 1, hd), lambda i: (i, 0, 0, 0))
    st = pl.BlockSpec((nb, N_HEADS, hd, hd), lambda i: (i, 0, 0, 0))
    return pl.pallas_call(
        _decode_ret_kernel,
        grid=(nseq // nb,),
        in_specs=[col, col, row, row, st, pl.BlockSpec((N_HEADS, 1, 1), lambda i: (0, 0, 0))],
        out_specs=[row, st],
        out_shape=[jax.ShapeDtypeStruct((nseq, N_HEADS, 1, hd), F32),
                   jax.ShapeDtypeStruct(state.shape, F32)],
        compiler_params=_cparams(("parallel",), big=True),
        name="ret_decode",
    )(q_col, k_col, v_row, sg_row, state, decay)


def _odd_sample_kernel(x_ref, sh_ref, sc_ref, g_ref, win_ref, cw_ref, b0_ref, b1_ref, wout_ref, lng_ref, lnb_ref,
                       o_ref, u_ref, *, alpha):
    x = x_ref[...]
    d = x.shape[-1]
    h = (x * (1.0 + sc_ref[...]) + sh_ref[...]).astype(BF16)
    b_gate = _dot(h, win_ref[:, 0:d])
    u = _dot(h, win_ref[:, d:2 * d]) * _dot(h, win_ref[:, 2 * d:3 * d])
    cw = cw_ref[...]
    z = cw[0:1, :] * b0_ref[...] + cw[1:2, :] * b1_ref[...] + cw[2:3, :] * u
    u_ref[...] = u
    y = _dot((b_gate * z).astype(BF16), wout_ref[...])
    o_ref[...] = _layernorm(alpha * x + g_ref[...] * y, lng_ref[...], lnb_ref[...])


def _odd_sample_call(x, rows, ks, w_in, conv_w, buf0, buf1, w_out, ln_g, ln_b, ln_idx, alpha):
    n, d = x.shape
    full = pl.BlockSpec((n, d), lambda i: (0, 0))
    return pl.pallas_call(
        functools.partial(_odd_sample_kernel, alpha=alpha),
        grid=(1,),
        in_specs=[rows.row_spec(d), rows.mod_spec(ks[0]), rows.mod_spec(ks[1]), rows.mod_spec(ks[2]),
                  _const_spec(w_in), _const_spec(conv_w), full, full, _const_spec(w_out),
                  pl.BlockSpec((None, 1, d), lambda i: (ln_idx, 0, 0)),
                  pl.BlockSpec((None, 1, d), lambda i: (ln_idx, 0, 0))],
        out_specs=[full, full],
        out_shape=[jax.ShapeDtypeStruct((n, d), F32), jax.ShapeDtypeStruct((n, d), F32)],
        compiler_params=_cparams(("arbitrary",), big=True),
        name="odd_sample",
    )(x, rows.mods, rows.mods, rows.mods, w_in, conv_w, buf0, buf1, w_out, ln_g, ln_b)


def _rotary_tables(pos):
    half = HEAD_DIM // 2
    inv = 1.0 / (RET_ANGLE_BASE ** jnp.linspace(0.0, 1.0, half, dtype=F32))
    ang = pos.astype(F32)[:, None] * inv[None, :]
    return jnp.cos(ang), jnp.sin(ang)


def _token_major_tables(cos, sin):
    cos_h = jnp.concatenate([cos, cos], axis=1)
    sin_h = jnp.concatenate([-sin, sin], axis=1)
    return jnp.tile(cos_h, (1, N_HEADS)), jnp.tile(sin_h, (1, N_HEADS))


def kernel(x_prompt, x_sample, cache_k, cache_v, cache_logf, state_ret, state_conv, page_table, c_prompt, c_sample,
           w_ada, b_ada, w_ffn_in, w_ffn_out, ln_g, ln_b, w_in_even, b_forget, w_out_even, w_in_odd, conv_w,
           w_out_odd):
    batch, seq, d = x_prompt.shape
    nseq = x_sample.shape[0]
    depth = w_ada.shape[0]
    past_len = page_table.shape[1] * cache_k.shape[2]
    alpha = (2.0 * depth) ** 0.25
    w = HEAD_WIDTH
    tm = ROW_BLOCK
    assert seq % tm == 0 and x_sample.shape[1] == 1 and d % V7X_LANES == 0
    assert cache_k.shape[3] == N_HEADS and cache_k.shape[4] == HEAD_DIM

    n_c = batch + nseq
    pad = (-n_c) % 8
    c_all = jnp.concatenate([c_prompt, c_sample, jnp.zeros((pad, d), F32)], axis=0)
    mods = _ada_call(c_all, w_ada, b_ada)
    mods_p = mods[:, :batch].reshape(depth, batch, 1, N_MOD * d)
    mods_s = mods[:, batch:n_c]

    ln_g3 = ln_g.reshape(depth * 3, 1, d)
    ln_b3 = ln_b.reshape(depth * 3, 1, d)
    w_ffn_in_b = w_ffn_in.astype(BF16)
    w_ffn_out_b = w_ffn_out.astype(BF16)

    xp = x_prompt.reshape(batch * seq, d)
    xs = x_sample.reshape(nseq, d)

    cos_p, sin_p = _rotary_tables(jnp.arange(seq, dtype=jnp.int32))
    rot_p = _token_major_tables(cos_p, sin_p) + (cos_p.T, sin_p.T)
    cos_s, sin_s = _rotary_tables(jnp.full((nseq,), past_len, dtype=jnp.int32))
    cos_s, sin_s = _token_major_tables(cos_s, sin_s)

    log_decay = jnp.log(1.0 - 2.0 ** (-5.0 - jnp.arange(N_HEADS, dtype=F32)))
    lg_pairs = jnp.repeat(log_decay.reshape(HEAD_PAIRS, 2), HEAD_DIM, axis=1)
    lg_lane = lg_pairs.reshape(HEAD_PAIRS, 1, 2 * HEAD_DIM)
    lg_row = lg_pairs.reshape(HEAD_PAIRS, 2 * HEAD_DIM, 1)
    step_decay = jnp.exp(log_decay).reshape(N_HEADS, 1, 1)

    outs_p = {"k": [], "v": [], "lf": [], "ret": [], "conv": []}
    outs_s = {"k": [], "v": [], "lf": [], "ret": [], "conv": []}

    for l in range(depth):
        rows_p = _Rows(mods_p, l, tm, seq // tm, d)
        rows_s = _Rows(mods_s, l, nseq, None, d)
        xp = _ffn_call(xp, rows_p, (0, 1, 2), w_ffn_in_b, w_ffn_out_b, (l, 0), ln_g3, ln_b3, 3 * l, alpha)
        xs = _ffn_call(xs, rows_s, (0, 1, 2), w_ffn_in_b, w_ffn_out_b, (l, 0), ln_g3, ln_b3, 3 * l, alpha)
        if l % 2 == 0:
            e = l // 2
            wi = w_in_even[e]
            cuts = [0, w, 2 * w, 3 * w, 3 * w + N_HEADS, 4 * w + N_HEADS, 5 * w + N_HEADS, 6 * w + N_HEADS,
                    7 * w + N_HEADS]
            qf_w, kf_w, vf_w, fg_w, qr_w, kr_w, vr_w, gr_w = [wi[:, a:b] for a, b in zip(cuts[:-1], cuts[1:])]
            w_out_b = w_out_even[e].astype(BF16)
            bf = b_forget[e]

            w_tok = jnp.concatenate([qf_w, qr_w, vr_w, gr_w], axis=1).astype(BF16)
            w_t = jnp.concatenate([kf_w, vf_w, kr_w, fg_w, jnp.zeros((d, 8), F32)], axis=1).T.astype(BF16)
            (q, qr, vr, sg, kt, vt, ktc, vtc, krt, lft) = _even_in_call(
                xp, rows_p, (3, 4), w_tok, w_t, bf.reshape(N_HEADS, 1), rot_p, batch, seq)
            c_t = _cumsum_call(lft)
            of = _fox_call(q, ktc, vtc, c_t.reshape(batch, N_HEADS, seq // tm, tm), batch, seq)
            yr, s_new = _ret_call(qr, krt, vr, sg, lg_lane, lg_row, batch, seq)
            xp = _outproj_call(xp, rows_p, 5, of, yr, w_out_b, ln_g3, ln_b3, 3 * l + 1, alpha)
            outs_p["k"].append(jnp.transpose(kt.reshape(batch, N_HEADS, HEAD_DIM, seq), (0, 3, 1, 2)))
            outs_p["v"].append(jnp.transpose(vt.reshape(batch, N_HEADS, HEAD_DIM, seq), (0, 3, 1, 2)))
            outs_p["lf"].append(jnp.transpose(lft, (0, 2, 1)))
            outs_p["ret"].append(s_new)

            w_all = jnp.concatenate([qf_w, kf_w, vf_w, qr_w, kr_w, vr_w, gr_w, fg_w,
                                     jnp.zeros((d, V7X_LANES - N_HEADS), F32)], axis=1).astype(BF16)
            bf_row = jnp.concatenate([bf, jnp.zeros((V7X_LANES - N_HEADS,), F32)]).reshape(1, V7X_LANES)
            (qs, ks_, vs, qrs, krs, vrs, sgs, lfs) = _even_in_sample_call(xs, rows_s, (3, 4), w_all, bf_row, cos_s, sin_s)
            n_phys, page_len = cache_k.shape[1], cache_k.shape[2]
            kt_pages = jnp.transpose(cache_k[e], (0, 2, 3, 1)).reshape(n_phys, w, page_len)
            vt_pages = jnp.transpose(cache_v[e], (0, 2, 3, 1)).reshape(n_phys, w, page_len)
            lft_pages = jnp.transpose(cache_logf[e], (0, 2, 1))
            of_s = _decode_fox_call(page_table, qs.reshape(nseq, 1, w), ks_.reshape(nseq, 1, w),
                                    vs.reshape(nseq, 1, w), lfs.reshape(nseq, 1, V7X_LANES),
                                    kt_pages, vt_pages, lft_pages).reshape(nseq, w)
            lfs = lfs[:, :N_HEADS]
            col = (nseq, N_HEADS, HEAD_DIM, 1)
            rw = (nseq, N_HEADS, 1, HEAD_DIM)
            yr_s, s_new_s = _decode_ret_call(qrs.reshape(col), krs.reshape(col), vrs.reshape(rw), sgs.reshape(rw),
                                             state_ret[e], step_decay)
            xs = _outproj_call(xs, rows_s, 5, of_s, yr_s.reshape(nseq, w), w_out_b, ln_g3, ln_b3, 3 * l + 1, alpha)
            outs_s["k"].append(ks_.reshape(nseq, 1, N_HEADS, HEAD_DIM))
            outs_s["v"].append(vs.reshape(nseq, 1, N_HEADS, HEAD_DIM))
            outs_s["lf"].append(lfs.reshape(nseq, 1, N_HEADS))
            outs_s["ret"].append(s_new_s)
        else:
            o = l // 2
            w_in_b = w_in_odd[o].astype(BF16)
            w_out_b = w_out_odd[o].astype(BF16)
            xp, conv_p = _odd_prompt_call(xp, rows_p, (3, 4, 5), w_in_b, conv_w[o], w_out_b, ln_g3, ln_b3,
                                          3 * l + 1, alpha, batch, seq)
            buf = state_conv[o]
            xs, u_s = _odd_sample_call(xs, rows_s, (3, 4, 5), w_in_b, conv_w[o], buf[:, 0], buf[:, 1], w_out_b,
                                       ln_g3, ln_b3, 3 * l + 1, alpha)
            outs_p["conv"].append(conv_p)
            outs_s["conv"].append(jnp.stack([buf[:, 1], u_s], axis=1))
        xp = _ffn_call(xp, rows_p, (6, 7, 8), w_ffn_in_b, w_ffn_out_b, (l, 1), ln_g3, ln_b3, 3 * l + 2, alpha)
        xs = _ffn_call(xs, rows_s, (6, 7, 8), w_ffn_in_b, w_ffn_out_b, (l, 1), ln_g3, ln_b3, 3 * l + 2, alpha)

    def stk(lst):
        return jnp.stack(lst)

    return (xp.reshape(batch, seq, d), xs.reshape(nseq, 1, d),
            stk(outs_p["k"]), stk(outs_p["v"]), stk(outs_p["lf"]), stk(outs_p["ret"]), stk(outs_p["conv"]),
            stk(outs_s["k"]), stk(outs_s["v"]), stk(outs_s["lf"]), stk(outs_s["ret"]), stk(outs_s["conv"]))
```

```python
import functools

import jax
import jax.numpy as jnp
from jax import lax
from jax.experimental import pallas as pl
from jax.experimental.pallas import tpu as pltpu

F32 = jnp.float32
BF16 = jnp.bfloat16

HEAD_DIM = 64
N_HEADS = 8
HEAD_WIDTH = N_HEADS * HEAD_DIM
N_MOD = 9
CONV_WIDTH = 3
RET_ANGLE_BASE = 10000.0
LN_EPS = 1e-5
GN_EPS = 1e-6
QK_SCALE = HEAD_DIM ** -0.5
LOG2E = 1.4426950408889634

V7X_LANES = 128
V7X_VMEM_LIMIT_BYTES = 56 * 1024 * 1024

ROW_BLOCK = 512
FF_CHUNK = 256
RET_CHUNK = 256
ADA_COL_BLOCK = 1152
HEAD_PAIRS = N_HEADS // 2


def _cparams(sem, big=False):
    return pltpu.CompilerParams(dimension_semantics=sem,
                                vmem_limit_bytes=V7X_VMEM_LIMIT_BYTES if big else None)


def _const_spec(arr):
    nd = arr.ndim
    return pl.BlockSpec(arr.shape, lambda *_: (0,) * nd, pipeline_mode=pl.Buffered(1))


def _layernorm(z, g, b):
    mu = jnp.mean(z, axis=-1, keepdims=True)
    d = z - mu
    var = jnp.mean(d * d, axis=-1, keepdims=True)
    return d * lax.rsqrt(var + LN_EPS) * g + b


def _silu(a):
    return a * jax.nn.sigmoid(a)


def _log_sigmoid(z):
    return jnp.minimum(z, 0.0) - jnp.log1p(jnp.exp(-jnp.abs(z)))


def _dot(a, b):
    return jnp.dot(a, b, preferred_element_type=F32)


def _dot_nt(a, b):
    return lax.dot_general(a, b, (((1,), (1,)), ((), ())), preferred_element_type=F32)


def _ada_kernel(c_ref, w_ref, b_ref, o_ref):
    s = _silu(c_ref[...]).astype(BF16)
    o_ref[...] = _dot(s, w_ref[...].astype(BF16)) + b_ref[...]


def _ada_call(c_all, w_ada, b_ada):
    depth, d, nm = w_ada.shape
    rows = c_all.shape[0]
    tn = ADA_COL_BLOCK
    return pl.pallas_call(
        _ada_kernel,
        grid=(depth, nm // tn),
        in_specs=[pl.BlockSpec((rows, d), lambda l, j: (0, 0)),
                  pl.BlockSpec((None, d, tn), lambda l, j: (l, 0, j)),
                  pl.BlockSpec((None, 1, tn), lambda l, j: (l, 0, j))],
        out_specs=pl.BlockSpec((None, rows, tn), lambda l, j: (l, 0, j)),
        out_shape=jax.ShapeDtypeStruct((depth, rows, nm), F32),
        compiler_params=_cparams(("arbitrary", "arbitrary"), big=True),
        name="ada_mods",
    )(c_all, w_ada, b_ada.reshape(depth, 1, nm))


class _Rows:
    def __init__(self, mods, layer, tm, blocks_per_seq, d):
        self.mods, self.layer, self.tm, self.bps, self.d = mods, layer, tm, blocks_per_seq, d

    def mod_spec(self, k):
        l, d = self.layer, self.d
        if self.bps is None:
            return pl.BlockSpec((None, self.tm, d), lambda i, *_: (l, 0, k))
        bps = self.bps
        return pl.BlockSpec((None, None, 1, d), lambda i, *_: (l, i // bps, 0, k))

    def row_spec(self, width):
        return pl.BlockSpec((self.tm, width), lambda i, *_: (i, 0))


def _ffn_body(x_ref, sh_ref, sc_ref, g_ref, win_ref, wout_ref, lng_ref, lnb_ref, o_ref, ff, alpha):
    x = x_ref[...]
    h = (x * (1.0 + sc_ref[...]) + sh_ref[...]).astype(BF16)
    acc = None
    for c in range(ff // FF_CHUNK):
        lo, hi = c * FF_CHUNK, (c + 1) * FF_CHUNK
        a = _dot(h, win_ref[:, lo:hi])
        b = _dot(h, win_ref[:, ff + lo:ff + hi])
        y = _dot((_silu(a) * b).astype(BF16), wout_ref[lo:hi, :])
        acc = y if acc is None else acc + y
    z = alpha * x + (0.5 * g_ref[...]) * acc
    o_ref[...] = _layernorm(z, lng_ref[...], lnb_ref[...])


def _ffn_kernel(*refs, ff, alpha):
    _ffn_body(*refs, ff, alpha)


def _ffn_decode_kernel(pt_ref, *refs, ff, alpha, pages, steps_per_seq):
    del pt_ref
    ffn_in, rest = refs[:8], refs[8:]
    q_ref, kn_ref, vn_ref, lfn_ref = rest[:4]
    k_refs, v_refs, lf_refs = rest[4:4 + pages], rest[4 + pages:4 + 2 * pages], rest[4 + 2 * pages:4 + 3 * pages]
    o_ref, dec_ref = rest[4 + 3 * pages:6 + 3 * pages]
    scratch = rest[6 + 3 * pages:]
    g = pl.program_id(0) % steps_per_seq
    pl.when(g == 0)(lambda: _decode_fox_init(q_ref, kn_ref, vn_ref, lfn_ref, *scratch))
    _decode_fox_pages(k_refs, v_refs, lf_refs, *scratch)
    _ffn_body(*ffn_in, o_ref, ff, alpha)
    pl.when(g == steps_per_seq - 1)(lambda: _decode_fox_finish(dec_ref, scratch[2], scratch[4]))


def _ffn_call(x, rows, ks, w_in, w_out, which, ln_g, ln_b, ln_idx, alpha, decode=None):
    n, d = x.shape
    ff = w_out.shape[2]
    l, j = which
    steps = n // rows.tm

    def w_spec(arr):
        return pl.BlockSpec((None, None) + arr.shape[2:], lambda i, *_: (l, j, 0, 0), pipeline_mode=pl.Buffered(1))

    in_specs = [rows.row_spec(d), rows.mod_spec(ks[0]), rows.mod_spec(ks[1]), rows.mod_spec(ks[2]),
                w_spec(w_in), w_spec(w_out),
                pl.BlockSpec((None, 1, d), lambda i, *_: (ln_idx, 0, 0)),
                pl.BlockSpec((None, 1, d), lambda i, *_: (ln_idx, 0, 0))]
    args = (x, rows.mods, rows.mods, rows.mods, w_in, w_out, ln_g, ln_b)
    x_shape = jax.ShapeDtypeStruct((n, d), F32)
    if decode is None:
        return pl.pallas_call(
            functools.partial(_ffn_kernel, ff=ff, alpha=alpha),
            grid=(steps,), in_specs=in_specs, out_specs=rows.row_spec(d), out_shape=x_shape,
            compiler_params=_cparams(("parallel",), big=True),
            name="ffn",
        )(*args)

    page_table, q, kn, vn, lfn, kt_pages, vt_pages, lft_pages, seq0, n_seq = decode
    npg = page_table.shape[1]
    w, page_len = kt_pages.shape[1], kt_pages.shape[2]
    steps_per_seq = steps // n_seq
    pages = npg // steps_per_seq
    assert steps_per_seq * n_seq == steps and pages * steps_per_seq == npg

    def page_spec(r, nrows):
        return pl.BlockSpec((None, nrows, page_len),
                            lambda i, pt: (pt[seq0 + i // steps_per_seq,
                                              npg - 1 - ((i % steps_per_seq) * pages + r)], 0, 0))

    row = pl.BlockSpec((None, 1, w), lambda i, pt: (seq0 + i // steps_per_seq, 0, 0))
    grid_spec = pltpu.PrefetchScalarGridSpec(
        num_scalar_prefetch=1,
        grid=(steps,),
        in_specs=in_specs
        + [row, row, row, pl.BlockSpec((None, 1, V7X_LANES), lambda i, pt: (seq0 + i // steps_per_seq, 0, 0))]
        + [page_spec(r, w) for r in range(pages)]
        + [page_spec(r, w) for r in range(pages)]
        + [page_spec(r, N_HEADS) for r in range(pages)],
        out_specs=[rows.row_spec(d), pl.BlockSpec((None, 1, w), lambda i, pt: (i // steps_per_seq, 0, 0))],
        scratch_shapes=[pltpu.VMEM((w, page_len), F32), pltpu.VMEM((N_HEADS, 1), F32), pltpu.VMEM((N_HEADS, 1), F32),
                        pltpu.VMEM((N_HEADS, 1), F32), pltpu.VMEM((w, page_len), F32)],
    )
    return pl.pallas_call(
        functools.partial(_ffn_decode_kernel, ff=ff, alpha=alpha, pages=pages, steps_per_seq=steps_per_seq),
        grid_spec=grid_spec,
        out_shape=[x_shape, jax.ShapeDtypeStruct((n_seq, 1, w), F32)],
        compiler_params=_cparams(("arbitrary",), big=True),
        name="ffn_decode",
    )(page_table, *args, q, kn, vn, lfn,
      *([kt_pages] * pages), *([vt_pages] * pages), *([lft_pages] * pages))


def _outproj_kernel(x_ref, g_ref, a0_ref, a1_ref, w_ref, lng_ref, lnb_ref, o_ref, *, alpha):
    half = a0_ref.shape[-1]
    y = _dot(a0_ref[...].astype(BF16), w_ref[0:half, :]) + _dot(a1_ref[...].astype(BF16), w_ref[half:, :])
    z = alpha * x_ref[...] + g_ref[...] * y
    o_ref[...] = _layernorm(z, lng_ref[...], lnb_ref[...])


def _outproj_call(x, rows, k_gate, a0, a1, w_out, ln_g, ln_b, ln_idx, alpha):
    n, d = x.shape
    return pl.pallas_call(
        functools.partial(_outproj_kernel, alpha=alpha),
        grid=(n // rows.tm,),
        in_specs=[rows.row_spec(d), rows.mod_spec(k_gate), rows.row_spec(a0.shape[1]), rows.row_spec(a1.shape[1]),
                  _const_spec(w_out),
                  pl.BlockSpec((None, 1, d), lambda i: (ln_idx, 0, 0)),
                  pl.BlockSpec((None, 1, d), lambda i: (ln_idx, 0, 0))],
        out_specs=rows.row_spec(d),
        out_shape=jax.ShapeDtypeStruct((n, d), F32),
        compiler_params=_cparams(("parallel",), big=True),
        name="mixer_outproj",
    )(x, rows.mods, a0, a1, w_out, ln_g, ln_b)


def _rotate_token_major(a, cos, sin_signed):
    width = a.shape[-1]
    lane = lax.broadcasted_iota(jnp.int32, a.shape, 1)
    first_half = (lane & (HEAD_DIM // 2)) == 0
    partner = jnp.where(first_half, pltpu.roll(a, width - HEAD_DIM // 2, 1), pltpu.roll(a, HEAD_DIM // 2, 1))
    return a * cos + partner * sin_signed


def _even_in_kernel(x_ref, sh_ref, sc_ref, wtok_ref, wt_ref, bf_ref, cos_ref, sin_ref, cost_ref, sint_ref,
                    q_o, qr_o, vr_o, sg_o, kt_o, vt_o, ktc_o, vtc_o, krt_o, lft_o):
    w = HEAD_WIDTH
    x = x_ref[...]
    h = (x * (1.0 + sc_ref[...]) + sh_ref[...]).astype(BF16)

    q_o[...] = (_dot(h, wtok_ref[:, 0:w]) * (QK_SCALE * LOG2E)).astype(BF16)
    qr = _dot(h, wtok_ref[:, w:2 * w])
    qr_o[...] = _rotate_token_major(qr, cos_ref[...], sin_ref[...]).astype(BF16)
    vr_o[...] = _dot(h, wtok_ref[:, 2 * w:3 * w]).astype(BF16)
    sg_o[...] = _silu(_dot(h, wtok_ref[:, 3 * w:4 * w])).astype(BF16)

    kt = _dot_nt(wt_ref[0:w, :], h)
    kt_o[...] = kt
    ktc_o[...] = kt.astype(BF16)
    vt = _dot_nt(wt_ref[w:2 * w, :], h)
    vt_o[...] = vt
    vtc_o[...] = vt.astype(BF16)

    krt = _dot_nt(wt_ref[2 * w:3 * w, :], h)
    cos_t, sin_t = cost_ref[...], sint_ref[...]
    half = HEAD_DIM // 2
    for hh in range(N_HEADS):
        x1 = krt[hh * HEAD_DIM:hh * HEAD_DIM + half, :]
        x2 = krt[hh * HEAD_DIM + half:(hh + 1) * HEAD_DIM, :]
        krt_o[hh * HEAD_DIM:hh * HEAD_DIM + half, :] = (x1 * cos_t - x2 * sin_t) * QK_SCALE
        krt_o[hh * HEAD_DIM + half:(hh + 1) * HEAD_DIM, :] = (x1 * sin_t + x2 * cos_t) * QK_SCALE

    fg = _dot_nt(wt_ref[3 * w:, :], h)[0:N_HEADS, :] + bf_ref[...]
    lft_o[...] = _log_sigmoid(fg)


def _even_in_call(x, rows, ks, w_tok, w_t, bf_col, rot, batch, seq):
    n, d = x.shape
    tm = rows.tm
    bps = seq // tm
    w = HEAD_WIDTH
    cos_tok, sin_tok, cos_t, sin_t = rot
    tok_out = pl.BlockSpec((tm, w), lambda i: (i, 0))
    t_out = pl.BlockSpec((None, w, tm), lambda i: (i // bps, 0, i % bps))
    tc_out = pl.BlockSpec((None, None, w, tm), lambda i: (i // bps, i % bps, 0, 0))
    tok_shape = jax.ShapeDtypeStruct((n, w), BF16)
    return pl.pallas_call(
        _even_in_kernel,
        grid=(n // tm,),
        in_specs=[rows.row_spec(d), rows.mod_spec(ks[0]), rows.mod_spec(ks[1]),
                  _const_spec(w_tok), _const_spec(w_t), _const_spec(bf_col),
                  pl.BlockSpec((tm, w), lambda i: (i % bps, 0)),
                  pl.BlockSpec((tm, w), lambda i: (i % bps, 0)),
                  pl.BlockSpec((HEAD_DIM // 2, tm), lambda i: (0, i % bps)),
                  pl.BlockSpec((HEAD_DIM // 2, tm), lambda i: (0, i % bps))],
        out_specs=[tok_out, tok_out, tok_out, tok_out, t_out, t_out, tc_out, tc_out, tc_out,
                   pl.BlockSpec((None, N_HEADS, tm), lambda i: (i // bps, 0, i % bps))],
        out_shape=[tok_shape, tok_shape, tok_shape, tok_shape,
                   jax.ShapeDtypeStruct((batch, w, seq), F32),
                   jax.ShapeDtypeStruct((batch, w, seq), F32),
                   jax.ShapeDtypeStruct((batch, bps, w, tm), BF16),
                   jax.ShapeDtypeStruct((batch, bps, w, tm), BF16),
                   jax.ShapeDtypeStruct((batch, bps, w, tm), F32),
                   jax.ShapeDtypeStruct((batch, N_HEADS, seq), F32)],
        compiler_params=_cparams(("parallel",), big=True),
        name="even_inproj",
    )(x, rows.mods, rows.mods, w_tok, w_t, bf_col, cos_tok, sin_tok, cos_t, sin_t)


def _cumsum_kernel(x_ref, o_ref):
    x = x_ref[...]
    n = x.shape[-1]
    lane = lax.broadcasted_iota(jnp.int32, x.shape, 1)
    shift = 1
    while shift < n:
        x = x + jnp.where(lane >= shift, pltpu.roll(x, shift, 1), 0.0)
        shift *= 2
    o_ref[...] = x * LOG2E


def _cumsum_call(lft):
    batch, heads, seq = lft.shape
    spec = pl.BlockSpec((None, heads, seq), lambda b: (b, 0, 0))
    return pl.pallas_call(
        _cumsum_kernel, grid=(batch,), in_specs=[spec], out_specs=spec,
        out_shape=jax.ShapeDtypeStruct(lft.shape, F32),
        compiler_params=_cparams(("parallel",)),
        name="logf_cumsum",
    )(lft)


def _split3(x):
    hi = x.astype(BF16).astype(F32)
    r = x - hi
    mid = r.astype(BF16).astype(F32)
    lo = (r - mid).astype(BF16).astype(F32)
    return hi, mid, lo


def _fox_kernel(q_ref, kt_ref, vt_ref, c_ref, o_ref):
    tq = q_ref.shape[0]
    tk = kt_ref.shape[-1]
    pw = 2 * HEAD_DIM
    slab_rows = 16
    i = pl.program_id(2)
    q = q_ref[...]
    lane = lax.broadcasted_iota(jnp.int32, (tq, pw), 1)
    q_aug = []
    for hd in range(2):
        base = HEAD_DIM if hd == 0 else 0
        hi, mid, lo = _split3(c_ref[hd, pl.ds(i, 1), :][:, 0:1])
        aug = jnp.where(lane == base + 3, hi, jnp.where(lane == base + 4, mid, jnp.where(lane == base + 5, lo,
              jnp.where((lane >= base) & (lane < base + 3), 1.0, 0.0))))
        own = (lane < HEAD_DIM) if hd == 0 else (lane >= HEAD_DIM)
        q_aug.append(jnp.where(own, q, aug.astype(BF16)))
    r16 = lax.broadcasted_iota(jnp.int32, (slab_rows, tk), 0)
    rowv = lax.broadcasted_iota(jnp.int32, (pw, tk), 0)

    def scores(j, hd):
        kt = kt_ref[j]
        hi, mid, lo = _split3(-c_ref[hd, pl.ds(j, 1), :])
        slab = jnp.where(r16 == 0, hi, jnp.where(r16 == 1, mid, jnp.where(r16 == 2, lo,
               jnp.where(r16 < 6, 1.0, 0.0)))).astype(BF16)
        if hd == 0:
            kt_aug = jnp.concatenate([kt[0:HEAD_DIM], slab, kt[HEAD_DIM + slab_rows:]], axis=0)
        else:
            kt_aug = jnp.concatenate([slab, kt[slab_rows:]], axis=0)
        return _dot(q_aug[hd], kt_aug)

    def values(j, hd):
        vt = vt_ref[j]
        return jnp.where(rowv == (HEAD_DIM if hd == 0 else 0), jnp.ones_like(vt), vt)

    def update(s, vt_aug, m, acc):
        m_new = jnp.maximum(m, jnp.max(s, axis=-1, keepdims=True))
        return m_new, jnp.exp2(m - m_new) * acc + _dot_nt(jnp.exp2(s - m_new).astype(BF16), vt_aug)

    def diag_scores(hd):
        row = lax.broadcasted_iota(jnp.int32, (tq, tk), 0)
        col = lax.broadcasted_iota(jnp.int32, (tq, tk), 1)
        return jnp.where(row >= col, scores(i, hd), -jnp.inf)

    def step2(t, carry):
        j = 2 * t
        return tuple(update(jnp.concatenate([scores(j, hd), scores(j + 1, hd)], axis=1),
                            jnp.concatenate([values(j, hd), values(j + 1, hd)], axis=1), *carry[hd])
                     for hd in range(2))

    def tail_pair(carry):
        return tuple(update(jnp.concatenate([scores(i - 1, hd), diag_scores(hd)], axis=1),
                            jnp.concatenate([values(i - 1, hd), values(i, hd)], axis=1), *carry[hd])[1]
                     for hd in range(2))

    def tail_single(carry):
        return tuple(update(diag_scores(hd), values(i, hd), *carry[hd])[1] for hd in range(2))

    init_head = (jnp.full((tq, 1), -jnp.inf, F32), jnp.zeros((tq, pw), F32))
    carry = lax.fori_loop(0, lax.shift_right_logical(i, 1), step2, (init_head, init_head))
    outs = lax.cond((i & 1) == 1, tail_pair, tail_single, carry)
    o0 = outs[0] / outs[0][:, HEAD_DIM:HEAD_DIM + 1]
    o1 = outs[1] / outs[1][:, 0:1]
    o_ref[...] = jnp.where(lane < HEAD_DIM, o0, o1).astype(BF16)


def _fox_call(q, ktc, vtc, c4, batch, seq):
    n, w = q.shape
    tq = ktc.shape[-1]
    nq = seq // tq
    pw = 2 * HEAD_DIM
    return pl.pallas_call(
        _fox_kernel,
        grid=(batch, HEAD_PAIRS, nq),
        in_specs=[pl.BlockSpec((tq, pw), lambda b, p, i: (b * nq + i, p)),
                  pl.BlockSpec((None, nq, pw, tq), lambda b, p, i: (b, 0, p, 0)),
                  pl.BlockSpec((None, nq, pw, tq), lambda b, p, i: (b, 0, p, 0)),
                  pl.BlockSpec((None, 2, nq, tq), lambda b, p, i: (b, p, 0, 0))],
        out_specs=pl.BlockSpec((tq, pw), lambda b, p, i: (b * nq + i, p)),
        out_shape=jax.ShapeDtypeStruct((n, w), BF16),
        compiler_params=_cparams(("parallel", "parallel", "arbitrary"), big=True),
        name="fox_prompt",
    )(q, ktc, vtc, c4)


def _ret_kernel(q_ref, kt_ref, v_ref, sg_ref, lgl_ref, lgr_ref, y_ref, s_ref):
    chunk = RET_CHUNK
    pw = 2 * HEAD_DIM
    nkb, _, tb = kt_ref.shape
    lg_lane = lgl_ref[...]
    lg_row = lgr_ref[...]
    lg_a, lg_b = lg_lane[:, 0:1], lg_lane[:, HEAD_DIM:HEAD_DIM + 1]

    ri = lax.broadcasted_iota(jnp.int32, (2 * chunk, chunk), 0)
    cj = lax.broadcasted_iota(jnp.int32, (2 * chunk, chunk), 1)
    first = ri < chunk
    diff = (jnp.where(first, ri, ri - chunk) - cj).astype(F32)
    decay_mask = jnp.where(diff >= 0, jnp.exp(jnp.maximum(diff, 0.0) * jnp.where(first, lg_a, lg_b)), 0.0)
    jj = lax.broadcasted_iota(jnp.int32, (pw, chunk), 1).astype(F32)
    col_decay = jnp.exp((chunk - 1.0 - jj) * lg_row)
    ii = lax.broadcasted_iota(jnp.int32, (chunk, pw), 0).astype(F32)
    row_decay = jnp.exp((ii + 1.0) * lg_lane)
    chunk_decay = jnp.exp(float(chunk) * lg_row)
    r2 = lax.broadcasted_iota(jnp.int32, (pw, pw), 0)
    c2 = lax.broadcasted_iota(jnp.int32, (pw, pw), 1)
    same_head = (r2 < HEAD_DIM) == (c2 < HEAD_DIM)
    seg_avg = jnp.where(same_head, 1.0 / HEAD_DIM, 0.0).astype(BF16)
    lane = lax.broadcasted_iota(jnp.int32, (chunk, pw), 1)

    def seg_mean(a):
        hi = a.astype(BF16)
        lo = (a - hi.astype(F32)).astype(BF16)
        return _dot(hi, seg_avg) + _dot(lo, seg_avg)

    def one_chunk(q, kt, v, sg, state):
        zero = jnp.zeros_like(q)
        q_stack = jnp.concatenate([jnp.where(lane < HEAD_DIM, q, zero), jnp.where(lane >= HEAD_DIM, q, zero)], axis=0)
        qk = _dot(q_stack, kt.astype(BF16)) * decay_mask
        kd = (kt * col_decay).astype(BF16)
        r = _dot(jnp.concatenate([qk.astype(BF16), kd], axis=0), v)
        inner = jnp.where(lane < HEAD_DIM, r[0:chunk], r[chunk:2 * chunk])
        update = jnp.where(same_head, r[2 * chunk:], 0.0)
        cross = _dot(q, state.astype(BF16)) * row_decay
        o = inner + cross
        mu = seg_mean(o)
        d = o - mu
        var = seg_mean(d * d)
        y = sg.astype(F32) * (d * lax.rsqrt(var + GN_EPS))
        return y.astype(BF16), chunk_decay * state + update

    def body(jb, state):
        kt_blk = kt_ref[jb]
        for sub in range(tb // chunk):
            t0 = pl.multiple_of(jb * tb + sub * chunk, chunk)
            y, state = one_chunk(q_ref[pl.ds(t0, chunk), :], kt_blk[:, sub * chunk:(sub + 1) * chunk],
                                 v_ref[pl.ds(t0, chunk), :], sg_ref[pl.ds(t0, chunk), :], state)
            y_ref[pl.ds(t0, chunk), :] = y
        return state

    state = lax.fori_loop(0, nkb, body, jnp.zeros((pw, pw), F32))
    s_ref[0] = state[0:HEAD_DIM, 0:HEAD_DIM]
    s_ref[1] = pltpu.roll(state, HEAD_DIM, 1)[HEAD_DIM:, 0:HEAD_DIM]


def _ret_call(qr, krt, vr, sg, lg_lane, lg_row, batch, seq):
    n, w = qr.shape
    nkb, tb = krt.shape[1], krt.shape[3]
    pw = 2 * HEAD_DIM
    seq_spec = pl.BlockSpec((seq, pw), lambda b, p: (b, p))
    return pl.pallas_call(
        _ret_kernel,
        grid=(batch, HEAD_PAIRS),
        in_specs=[seq_spec,
                  pl.BlockSpec((None, nkb, pw, tb), lambda b, p: (b, 0, p, 0)),
                  seq_spec, seq_spec,
                  pl.BlockSpec((None, 1, pw), lambda b, p: (p, 0, 0)),
                  pl.BlockSpec((None, pw, 1), lambda b, p: (p, 0, 0))],
        out_specs=[seq_spec, pl.BlockSpec((None, 2, HEAD_DIM, HEAD_DIM), lambda b, p: (b, p, 0, 0))],
        out_shape=[jax.ShapeDtypeStruct((n, w), BF16),
                   jax.ShapeDtypeStruct((batch, N_HEADS, HEAD_DIM, HEAD_DIM), F32)],
        compiler_params=_cparams(("parallel", "parallel"), big=True),
        name="ret_prompt",
    )(qr, krt, vr, sg, lg_lane, lg_row)


def _odd_prompt_kernel(x_ref, sh_ref, sc_ref, g_ref, win_ref, cw_ref, wout_ref, lng_ref, lnb_ref,
                       o_ref, st_ref, carry_ref, *, bps, alpha):
    i = pl.program_id(0)
    x = x_ref[...]
    tm, d = x.shape
    h = (x * (1.0 + sc_ref[...]) + sh_ref[...]).astype(BF16)
    b_gate = _dot(h, win_ref[:, 0:d])
    u = _dot(h, win_ref[:, d:2 * d]) * _dot(h, win_ref[:, 2 * d:3 * d])

    @pl.when(i % bps == 0)
    def _():
        carry_ref[...] = jnp.zeros_like(carry_ref)

    prev = carry_ref[...]
    p1, p2 = prev[7:8, :], prev[6:7, :]
    row = lax.broadcasted_iota(jnp.int32, (tm, d), 0)
    u1 = jnp.where(row == 0, p1, pltpu.roll(u, 1, 0))
    u2 = jnp.where(row == 0, p2, jnp.where(row == 1, p1, pltpu.roll(u, 2, 0)))
    cw = cw_ref[...]
    z = cw[0:1, :] * u2 + cw[1:2, :] * u1 + cw[2:3, :] * u
    carry_ref[...] = u[tm - 8:, :]
    st_ref[...] = u[tm - (CONV_WIDTH - 1):, :]
    y = _dot((b_gate * z).astype(BF16), wout_ref[...])
    zz = alpha * x + g_ref[...] * y
    o_ref[...] = _layernorm(zz, lng_ref[...], lnb_ref[...])


def _odd_prompt_call(x, rows, ks, w_in, conv_w, w_out, ln_g, ln_b, ln_idx, alpha, batch, seq):
    n, d = x.shape
    bps = seq // rows.tm
    return pl.pallas_call(
        functools.partial(_odd_prompt_kernel, bps=bps, alpha=alpha),
        grid=(n // rows.tm,),
        in_specs=[rows.row_spec(d), rows.mod_spec(ks[0]), rows.mod_spec(ks[1]), rows.mod_spec(ks[2]),
                  _const_spec(w_in), _const_spec(conv_w), _const_spec(w_out),
                  pl.BlockSpec((None, 1, d), lambda i: (ln_idx, 0, 0)),
                  pl.BlockSpec((None, 1, d), lambda i: (ln_idx, 0, 0))],
        out_specs=[rows.row_spec(d), pl.BlockSpec((None, CONV_WIDTH - 1, d), lambda i: (i // bps, 0, 0))],
        out_shape=[jax.ShapeDtypeStruct((n, d), F32), jax.ShapeDtypeStruct((batch, CONV_WIDTH - 1, d), F32)],
        scratch_shapes=[pltpu.VMEM((8, d), F32)],
        compiler_params=_cparams(("arbitrary",), big=True),
        name="odd_prompt",
    )(x, rows.mods, rows.mods, rows.mods, w_in, conv_w, w_out, ln_g, ln_b)


def _even_in_sample_kernel(x_ref, sh_ref, sc_ref, w_ref, bf_ref, cos_ref, sin_ref,
                           q_o, k_o, v_o, qr_o, kr_o, vr_o, sg_o, lf_o):
    w = HEAD_WIDTH
    h = (x_ref[...] * (1.0 + sc_ref[...]) + sh_ref[...]).astype(BF16)
    cos, sin = cos_ref[...], sin_ref[...]
    q_o[...] = _dot(h, w_ref[:, 0:w]) * QK_SCALE
    k_o[...] = _dot(h, w_ref[:, w:2 * w])
    v_o[...] = _dot(h, w_ref[:, 2 * w:3 * w])
    qr_o[...] = _rotate_token_major(_dot(h, w_ref[:, 3 * w:4 * w]), cos, sin)
    kr_o[...] = _rotate_token_major(_dot(h, w_ref[:, 4 * w:5 * w]), cos, sin) * QK_SCALE
    vr_o[...] = _dot(h, w_ref[:, 5 * w:6 * w])
    sg_o[...] = _silu(_dot(h, w_ref[:, 6 * w:7 * w]))
    lf_o[...] = _log_sigmoid(_dot(h, w_ref[:, 7 * w:]) + bf_ref[...])


def _even_in_sample_call(x, rows, ks, w_all, bf_row, cos, sin):
    n, d = x.shape
    w = HEAD_WIDTH
    full = pl.BlockSpec((n, w), lambda i: (0, 0))
    shp = jax.ShapeDtypeStruct((n, w), F32)
    return pl.pallas_call(
        _even_in_sample_kernel,
        grid=(1,),
        in_specs=[rows.row_spec(d), rows.mod_spec(ks[0]), rows.mod_spec(ks[1]),
                  _const_spec(w_all), _const_spec(bf_row), full, full],
        out_specs=[full] * 7 + [pl.BlockSpec((n, V7X_LANES), lambda i: (0, 0))],
        out_shape=[shp] * 7 + [jax.ShapeDtypeStruct((n, V7X_LANES), F32)],
        compiler_params=_cparams(("arbitrary",), big=True),
        name="even_inproj_sample",
    )(x, rows.mods, rows.mods, w_all, bf_row, cos, sin)


def _own_head_lanes():
    sub = lax.broadcasted_iota(jnp.int32, (N_HEADS, HEAD_WIDTH), 0)
    lane = lax.broadcasted_iota(jnp.int32, (N_HEADS, HEAD_WIDTH), 1)
    return (lane >= sub * HEAD_DIM) & (lane < (sub + 1) * HEAD_DIM)


def _decode_fox_init(q_ref, kn_ref, vn_ref, lfn_ref, qb_s, m_s, l_s, run_s, acc_s):
    w, page_len = qb_s.shape
    q = q_ref[...]
    q_bd = jnp.where(_own_head_lanes(), jnp.broadcast_to(q, (N_HEADS, w)), 0.0)
    m_s[...] = jnp.sum(q_bd * kn_ref[...], axis=-1, keepdims=True)
    l_s[...] = jnp.ones_like(l_s)
    qb_s[...] = jnp.broadcast_to(q, (page_len, w)).T
    lane_p = lax.broadcasted_iota(jnp.int32, (w, page_len), 1)
    acc_s[...] = jnp.where(lane_p == 0, jnp.broadcast_to(vn_ref[...], (page_len, w)).T, 0.0)
    s128 = lax.broadcasted_iota(jnp.int32, (N_HEADS, V7X_LANES), 0)
    l128 = lax.broadcasted_iota(jnp.int32, (N_HEADS, V7X_LANES), 1)
    lfn = jnp.broadcast_to(lfn_ref[...], (N_HEADS, V7X_LANES))
    run_s[...] = jnp.sum(jnp.where(s128 == l128, lfn, 0.0), axis=-1, keepdims=True)


def _decode_fox_finish(o_ref, l_s, acc_s):
    tot = jnp.sum(acc_s[...].T, axis=0, keepdims=True)
    l_b = jnp.broadcast_to(l_s[...], (N_HEADS, HEAD_WIDTH))
    o_ref[...] = tot / jnp.sum(jnp.where(_own_head_lanes(), l_b, 0.0), axis=0, keepdims=True)


def _decode_fox_pages(k_refs, v_refs, lf_refs, qb_s, m_s, l_s, run_s, acc_s):
    pages = len(k_refs)
    page_len = k_refs[0].shape[-1]
    hd = HEAD_DIM
    qb = qb_s[...]
    order = list(range(pages - 1, -1, -1))
    s = jnp.concatenate([(k_refs[r][...] * qb).reshape(N_HEADS, hd, page_len).sum(axis=1) for r in order], axis=1)
    lf = jnp.concatenate([lf_refs[r][...] for r in order], axis=1)
    n = pages * page_len
    lane_n = lax.broadcasted_iota(jnp.int32, (N_HEADS, n), 1)
    suf = lf
    shift = 1
    while shift < n:
        suf = suf + jnp.where(lane_n < n - shift, pltpu.roll(suf, n - shift, 1), 0.0)
        shift *= 2
    run = run_s[...]
    s = s + ((suf - lf) + run)
    m = m_s[...]
    m_new = jnp.maximum(m, jnp.max(s, axis=-1, keepdims=True))
    a = jnp.exp(m - m_new)
    p = jnp.exp(s - m_new)
    l_s[...] = a * l_s[...] + jnp.sum(p, axis=-1, keepdims=True)
    m_s[...] = m_new
    for h in range(N_HEADS):
        sl = slice(h * hd, (h + 1) * hd)
        upd = None
        for idx, r in enumerate(order):
            t = p[h:h + 1, idx * page_len:(idx + 1) * page_len] * v_refs[r][sl, :]
            upd = t if upd is None else upd + t
        acc_s[sl, :] = a[h:h + 1, :] * acc_s[sl, :] + upd
    run_s[...] = run + suf[:, 0:1]


def _decode_ret_kernel(q_ref, k_ref, v_ref, sg_ref, s_ref, gl_ref, y_ref, so_ref):
    q, k, v = q_ref[...], k_ref[...], v_ref[...]
    state = s_ref[...]
    g = gl_ref[...]
    inner = jnp.sum(q * k, axis=2, keepdims=True) * v
    cross = jnp.sum(q * state, axis=2, keepdims=True) * g
    so_ref[...] = g * state + k * v
    o = inner + cross
    mu = jnp.mean(o, axis=-1, keepdims=True)
    d = o - mu
    var = jnp.mean(d * d, axis=-1, keepdims=True)
    y_ref[...] = sg_ref[...] * (d * lax.rsqrt(var + GN_EPS))


def _decode_ret_call(q_col, k_col, v_row, sg_row, state, decay):
    nseq = state.shape[0]
    nb = 8
    hd = HEAD_DIM
    col = pl.BlockSpec((nb, N_HEADS, hd, 1), lambda i: (i, 0, 0, 0))
    row = pl.BlockSpec((nb, N_HEADS, 1, hd), lambda i: (i, 0, 0, 0))
    st = pl.BlockSpec((nb, N_HEADS, hd, hd), lambda i: (i, 0, 0, 0))
    return pl.pallas_call(
        _decode_ret_kernel,
        grid=(nseq // nb,),
        in_specs=[col, col, row, row, st, pl.BlockSpec((N_HEADS, 1, 1), lambda i: (0, 0, 0))],
        out_specs=[row, st],
        out_shape=[jax.ShapeDtypeStruct((nseq, N_HEADS, 1, hd), F32),
                   jax.ShapeDtypeStruct(state.shape, F32)],
        compiler_params=_cparams(("parallel",), big=True),
        name="ret_decode",
    )(q_col, k_col, v_row, sg_row, state, decay)


def _odd_sample_kernel(x_ref, sh_ref, sc_ref, g_ref, win_ref, cw_ref, b0_ref, b1_ref, wout_ref, lng_ref, lnb_ref,
                       o_ref, u_ref, *, alpha):
    x = x_ref[...]
    d = x.shape[-1]
    h = (x * (1.0 + sc_ref[...]) + sh_ref[...]).astype(BF16)
    b_gate = _dot(h, win_ref[:, 0:d])
    u = _dot(h, win_ref[:, d:2 * d]) * _dot(h, win_ref[:, 2 * d:3 * d])
    cw = cw_ref[...]
    z = cw[0:1, :] * b0_ref[...] + cw[1:2, :] * b1_ref[...] + cw[2:3, :] * u
    u_ref[...] = u
    y = _dot((b_gate * z).astype(BF16), wout_ref[...])
    o_ref[...] = _layernorm(alpha * x + g_ref[...] * y, lng_ref[...], lnb_ref[...])


def _odd_sample_call(x, rows, ks, w_in, conv_w, buf0, buf1, w_out, ln_g, ln_b, ln_idx, alpha):
    n, d = x.shape
    full = pl.BlockSpec((n, d), lambda i: (0, 0))
    return pl.pallas_call(
        functools.partial(_odd_sample_kernel, alpha=alpha),
        grid=(1,),
        in_specs=[rows.row_spec(d), rows.mod_spec(ks[0]), rows.mod_spec(ks[1]), rows.mod_spec(ks[2]),
                  _const_spec(w_in), _const_spec(conv_w), full, full, _const_spec(w_out),
                  pl.BlockSpec((None, 1, d), lambda i: (ln_idx, 0, 0)),
                  pl.BlockSpec((None, 1, d), lambda i: (ln_idx, 0, 0))],
        out_specs=[full, full],
        out_shape=[jax.ShapeDtypeStruct((n, d), F32), jax.ShapeDtypeStruct((n, d), F32)],
        compiler_params=_cparams(("arbitrary",), big=True),
        name="odd_sample",
    )(x, rows.mods, rows.mods, rows.mods, w_in, conv_w, buf0, buf1, w_out, ln_g, ln_b)


def _rotary_tables(pos):
    half = HEAD_DIM // 2
    inv = 1.0 / (RET_ANGLE_BASE ** jnp.linspace(0.0, 1.0, half, dtype=F32))
    ang = pos.astype(F32)[:, None] * inv[None, :]
    return jnp.cos(ang), jnp.sin(ang)


def _token_major_tables(cos, sin):
    cos_h = jnp.concatenate([cos, cos], axis=1)
    sin_h = jnp.concatenate([-sin, sin], axis=1)
    return jnp.tile(cos_h, (1, N_HEADS)), jnp.tile(sin_h, (1, N_HEADS))


def kernel(x_prompt, x_sample, cache_k, cache_v, cache_logf, state_ret, state_conv, page_table, c_prompt, c_sample,
           w_ada, b_ada, w_ffn_in, w_ffn_out, ln_g, ln_b, w_in_even, b_forget, w_out_even, w_in_odd, conv_w,
           w_out_odd):
    batch, seq, d = x_prompt.shape
    nseq = x_sample.shape[0]
    depth = w_ada.shape[0]
    past_len = page_table.shape[1] * cache_k.shape[2]
    alpha = (2.0 * depth) ** 0.25
    w = HEAD_WIDTH
    tm = ROW_BLOCK
    assert seq % tm == 0 and x_sample.shape[1] == 1 and d % V7X_LANES == 0
    assert cache_k.shape[3] == N_HEADS and cache_k.shape[4] == HEAD_DIM

    n_c = batch + nseq
    pad = (-n_c) % 8
    c_all = jnp.concatenate([c_prompt, c_sample, jnp.zeros((pad, d), F32)], axis=0)
    mods = _ada_call(c_all, w_ada, b_ada)
    mods_p = mods[:, :batch].reshape(depth, batch, 1, N_MOD * d)
    mods_s = mods[:, batch:n_c]

    ln_g3 = ln_g.reshape(depth * 3, 1, d)
    ln_b3 = ln_b.reshape(depth * 3, 1, d)
    w_ffn_in_b = w_ffn_in.astype(BF16)
    w_ffn_out_b = w_ffn_out.astype(BF16)

    xp = x_prompt.reshape(batch * seq, d)
    xs = x_sample.reshape(nseq, d)

    cos_p, sin_p = _rotary_tables(jnp.arange(seq, dtype=jnp.int32))
    rot_p = _token_major_tables(cos_p, sin_p) + (cos_p.T, sin_p.T)
    cos_s, sin_s = _rotary_tables(jnp.full((nseq,), past_len, dtype=jnp.int32))
    cos_s, sin_s = _token_major_tables(cos_s, sin_s)

    log_decay = jnp.log(1.0 - 2.0 ** (-5.0 - jnp.arange(N_HEADS, dtype=F32)))
    lg_pairs = jnp.repeat(log_decay.reshape(HEAD_PAIRS, 2), HEAD_DIM, axis=1)
    lg_lane = lg_pairs.reshape(HEAD_PAIRS, 1, 2 * HEAD_DIM)
    lg_row = lg_pairs.reshape(HEAD_PAIRS, 2 * HEAD_DIM, 1)
    step_decay = jnp.exp(log_decay).reshape(N_HEADS, 1, 1)

    outs_p = {"k": [], "v": [], "lf": [], "ret": [], "conv": []}
    outs_s = {"k": [], "v": [], "lf": [], "ret": [], "conv": []}

    assert depth == 2 and w_in_even.shape[0] == 1 and cache_k.shape[0] == 1
    n_hosts = 2 * depth
    assert nseq % n_hosts == 0
    wi = w_in_even[0]
    cuts = [0, w, 2 * w, 3 * w, 3 * w + N_HEADS, 4 * w + N_HEADS, 5 * w + N_HEADS, 6 * w + N_HEADS, 7 * w + N_HEADS]
    qf_w, kf_w, vf_w, fg_w, qr_w, kr_w, vr_w, gr_w = [wi[:, a:b] for a, b in zip(cuts[:-1], cuts[1:])]
    w_out_even_b = w_out_even[0].astype(BF16)
    bf = b_forget[0]
    w_in_odd_b = w_in_odd[0].astype(BF16)
    w_out_odd_b = w_out_odd[0].astype(BF16)

    def rows_s(l):
        return _Rows(mods_s, l, nseq, None, d)

    def rows_p(l):
        return _Rows(mods_p, l, tm, seq // tm, d)

    def ffn(x, rows, half, l, decode=None):
        ks = (0, 1, 2) if half == 0 else (6, 7, 8)
        return _ffn_call(x, rows, ks, w_ffn_in_b, w_ffn_out_b, (l, half), ln_g3, ln_b3, 3 * l + 2 * half, alpha, decode)

    xs = ffn(xs, rows_s(0), 0, 0)
    w_all = jnp.concatenate([qf_w, kf_w, vf_w, qr_w, kr_w, vr_w, gr_w, fg_w,
                             jnp.zeros((d, V7X_LANES - N_HEADS), F32)], axis=1).astype(BF16)
    bf_row = jnp.concatenate([bf, jnp.zeros((V7X_LANES - N_HEADS,), F32)]).reshape(1, V7X_LANES)
    (qs, ks_, vs, qrs, krs, vrs, sgs, lfs) = _even_in_sample_call(xs, rows_s(0), (3, 4), w_all, bf_row, cos_s, sin_s)
    n_phys, page_len = cache_k.shape[1], cache_k.shape[2]
    kt_pages = jnp.transpose(cache_k[0], (0, 2, 3, 1)).reshape(n_phys, w, page_len)
    vt_pages = jnp.transpose(cache_v[0], (0, 2, 3, 1)).reshape(n_phys, w, page_len)
    lft_pages = jnp.transpose(cache_logf[0], (0, 2, 1))
    per_host = nseq // n_hosts
    dec_args = (page_table, qs.reshape(nseq, 1, w), ks_.reshape(nseq, 1, w), vs.reshape(nseq, 1, w),
                lfs.reshape(nseq, 1, V7X_LANES), kt_pages, vt_pages, lft_pages)
    dec_out = []

    def ffn_host(x, half, l):
        x, o = ffn(x, rows_p(l), half, l, dec_args + (len(dec_out) * per_host, per_host))
        dec_out.append(o)
        return x

    xp = ffn_host(xp, 0, 0)
    w_tok = jnp.concatenate([qf_w, qr_w, vr_w, gr_w], axis=1).astype(BF16)
    w_t = jnp.concatenate([kf_w, vf_w, kr_w, fg_w, jnp.zeros((d, 8), F32)], axis=1).T.astype(BF16)
    (q, qr, vr, sg, kt, vt, ktc, vtc, krt, lft) = _even_in_call(
        xp, rows_p(0), (3, 4), w_tok, w_t, bf.reshape(N_HEADS, 1), rot_p, batch, seq)
    c_t = _cumsum_call(lft)
    of = _fox_call(q, ktc, vtc, c_t.reshape(batch, N_HEADS, seq // tm, tm), batch, seq)
    yr, s_new = _ret_call(qr, krt, vr, sg, lg_lane, lg_row, batch, seq)
    xp = _outproj_call(xp, rows_p(0), 5, of, yr, w_out_even_b, ln_g3, ln_b3, 1, alpha)
    outs_p["k"].append(jnp.transpose(kt.reshape(batch, N_HEADS, HEAD_DIM, seq), (0, 3, 1, 2)))
    outs_p["v"].append(jnp.transpose(vt.reshape(batch, N_HEADS, HEAD_DIM, seq), (0, 3, 1, 2)))
    outs_p["lf"].append(jnp.transpose(lft, (0, 2, 1)))
    outs_p["ret"].append(s_new)
    xp = ffn_host(xp, 1, 0)
    xp = ffn_host(xp, 0, 1)
    xp, conv_p = _odd_prompt_call(xp, rows_p(1), (3, 4, 5), w_in_odd_b, conv_w[0], w_out_odd_b, ln_g3, ln_b3,
                                  4, alpha, batch, seq)
    outs_p["conv"].append(conv_p)
    xp = ffn_host(xp, 1, 1)

    of_s = jnp.concatenate(dec_out, axis=0).reshape(nseq, w)
    lfs = lfs[:, :N_HEADS]
    col = (nseq, N_HEADS, HEAD_DIM, 1)
    rw = (nseq, N_HEADS, 1, HEAD_DIM)
    yr_s, s_new_s = _decode_ret_call(qrs.reshape(col), krs.reshape(col), vrs.reshape(rw), sgs.reshape(rw),
                                     state_ret[0], step_decay)
    xs = _outproj_call(xs, rows_s(0), 5, of_s, yr_s.reshape(nseq, w), w_out_even_b, ln_g3, ln_b3, 1, alpha)
    outs_s["k"].append(ks_.reshape(nseq, 1, N_HEADS, HEAD_DIM))
    outs_s["v"].append(vs.reshape(nseq, 1, N_HEADS, HEAD_DIM))
    outs_s["lf"].append(lfs.reshape(nseq, 1, N_HEADS))
    outs_s["ret"].append(s_new_s)
    xs = ffn(xs, rows_s(0), 1, 0)
    xs = ffn(xs, rows_s(1), 0, 1)
    buf = state_conv[0]
    xs, u_s = _odd_sample_call(xs, rows_s(1), (3, 4, 5), w_in_odd_b, conv_w[0], buf[:, 0], buf[:, 1], w_out_odd_b,
                               ln_g3, ln_b3, 4, alpha)
    outs_s["conv"].append(jnp.stack([buf[:, 1], u_s], axis=1))
    xs = ffn(xs, rows_s(1), 1, 1)

    def stk(lst):
        return jnp.stack(lst)

    return (xp.reshape(batch, seq, d), xs.reshape(nseq, 1, d),
            stk(outs_p["k"]), stk(outs_p["v"]), stk(outs_p["lf"]), stk(outs_p["ret"]), stk(outs_p["conv"]),
            stk(outs_s["k"]), stk(outs_s["v"]), stk(outs_s["lf"]), stk(outs_s["ret"]), stk(outs_s["conv"]))
```

```python
import functools

import jax
import jax.numpy as jnp
import numpy as np
from jax import lax
from jax.experimental import pallas as pl
from jax.experimental.pallas import tpu as pltpu

F32 = jnp.float32
BF16 = jnp.bfloat16

HEAD_DIM = 64
N_HEADS = 8
HEAD_WIDTH = N_HEADS * HEAD_DIM
N_MOD = 9
CONV_WIDTH = 3
RET_ANGLE_BASE = 10000.0
LN_EPS = 1e-5
GN_EPS = 1e-6
QK_SCALE = HEAD_DIM ** -0.5
LOG2E = 1.4426950408889634

V7X_LANES = 128
V7X_VMEM_LIMIT_BYTES = 56 * 1024 * 1024

ROW_BLOCK = 512
FF_CHUNK = 256
RET_CHUNK = 256
ADA_COL_BLOCK = 1152
HEAD_PAIRS = N_HEADS // 2


def _cparams(sem, big=False):
    return pltpu.CompilerParams(dimension_semantics=sem,
                                vmem_limit_bytes=V7X_VMEM_LIMIT_BYTES if big else None)


def _const_spec(arr):
    nd = arr.ndim
    return pl.BlockSpec(arr.shape, lambda *_: (0,) * nd, pipeline_mode=pl.Buffered(1))


def _layernorm(z, g, b):
    mu = jnp.mean(z, axis=-1, keepdims=True)
    d = z - mu
    var = jnp.mean(d * d, axis=-1, keepdims=True)
    return d * lax.rsqrt(var + LN_EPS) * g + b


def _silu(a):
    return a * jax.nn.sigmoid(a)


def _log_sigmoid(z):
    return jnp.minimum(z, 0.0) - jnp.log1p(jnp.exp(-jnp.abs(z)))


def _dot(a, b):
    return jnp.dot(a, b, preferred_element_type=F32)


def _dot_nt(a, b):
    return lax.dot_general(a, b, (((1,), (1,)), ((), ())), preferred_element_type=F32)


def _ada_kernel(c_ref, w_ref, b_ref, o_ref):
    s = _silu(c_ref[...]).astype(BF16)
    o_ref[...] = _dot(s, w_ref[...].astype(BF16)) + b_ref[...]


def _ada_call(c_all, w_ada, b_ada):
    depth, d, nm = w_ada.shape
    rows = c_all.shape[0]
    tn = ADA_COL_BLOCK
    return pl.pallas_call(
        _ada_kernel,
        grid=(depth, nm // tn),
        in_specs=[pl.BlockSpec((rows, d), lambda l, j: (0, 0)),
                  pl.BlockSpec((None, d, tn), lambda l, j: (l, 0, j)),
                  pl.BlockSpec((None, 1, tn), lambda l, j: (l, 0, j))],
        out_specs=pl.BlockSpec((None, rows, tn), lambda l, j: (l, 0, j)),
        out_shape=jax.ShapeDtypeStruct((depth, rows, nm), F32),
        compiler_params=_cparams(("arbitrary", "arbitrary"), big=True),
        name="ada_mods",
    )(c_all, w_ada, b_ada.reshape(depth, 1, nm))


class _Rows:
    def __init__(self, mods, layer, tm, blocks_per_seq, d):
        self.mods, self.layer, self.tm, self.bps, self.d = mods, layer, tm, blocks_per_seq, d

    def mod_spec(self, k):
        l, d = self.layer, self.d
        if self.bps is None:
            return pl.BlockSpec((None, self.tm, d), lambda i, *_: (l, 0, k))
        bps = self.bps
        return pl.BlockSpec((None, None, 1, d), lambda i, *_: (l, i // bps, 0, k))

    def row_spec(self, width):
        return pl.BlockSpec((self.tm, width), lambda i, *_: (i, 0))


def _ffn_body(x_ref, sh_ref, sc_ref, g_ref, win_ref, wout_ref, lng_ref, lnb_ref, o_ref, ff, alpha):
    x = x_ref[...]
    h = (x * (1.0 + sc_ref[...]) + sh_ref[...]).astype(BF16)
    acc = None
    for c in range(ff // FF_CHUNK):
        lo, hi = c * FF_CHUNK, (c + 1) * FF_CHUNK
        a = _dot(h, win_ref[:, lo:hi])
        b = _dot(h, win_ref[:, ff + lo:ff + hi])
        y = _dot((_silu(a) * b).astype(BF16), wout_ref[lo:hi, :])
        acc = y if acc is None else acc + y
    z = alpha * x + (0.5 * g_ref[...]) * acc
    o_ref[...] = _layernorm(z, lng_ref[...], lnb_ref[...])


def _ffn_kernel(*refs, ff, alpha):
    _ffn_body(*refs, ff, alpha)


def _ffn_decode_kernel(pt_ref, *refs, ff, alpha, pages, steps_per_seq):
    del pt_ref
    ffn_in, rest = refs[:8], refs[8:]
    q_ref, kn_ref, vn_ref, lfn_ref = rest[:4]
    k_refs, v_refs, lf_refs = rest[4:4 + pages], rest[4 + pages:4 + 2 * pages], rest[4 + 2 * pages:4 + 3 * pages]
    o_ref, dec_ref = rest[4 + 3 * pages:6 + 3 * pages]
    scratch = rest[6 + 3 * pages:]
    g = pl.program_id(0) % steps_per_seq
    pl.when(g == 0)(lambda: _decode_fox_init(q_ref, kn_ref, vn_ref, lfn_ref, *scratch))
    _decode_fox_pages(k_refs, v_refs, lf_refs, *scratch)
    _ffn_body(*ffn_in, o_ref, ff, alpha)
    pl.when(g == steps_per_seq - 1)(lambda: _decode_fox_finish(dec_ref, scratch[2], scratch[4]))


def _ffn_call(x, rows, ks, w_in, w_out, which, ln_g, ln_b, ln_idx, alpha, decode=None):
    n, d = x.shape
    ff = w_out.shape[2]
    l, j = which
    steps = n // rows.tm

    def w_spec(arr):
        return pl.BlockSpec((None, None) + arr.shape[2:], lambda i, *_: (l, j, 0, 0), pipeline_mode=pl.Buffered(1))

    in_specs = [rows.row_spec(d), rows.mod_spec(ks[0]), rows.mod_spec(ks[1]), rows.mod_spec(ks[2]),
                w_spec(w_in), w_spec(w_out),
                pl.BlockSpec((None, 1, d), lambda i, *_: (ln_idx, 0, 0)),
                pl.BlockSpec((None, 1, d), lambda i, *_: (ln_idx, 0, 0))]
    args = (x, rows.mods, rows.mods, rows.mods, w_in, w_out, ln_g, ln_b)
    x_shape = jax.ShapeDtypeStruct((n, d), F32)
    if decode is None:
        return pl.pallas_call(
            functools.partial(_ffn_kernel, ff=ff, alpha=alpha),
            grid=(steps,), in_specs=in_specs, out_specs=rows.row_spec(d), out_shape=x_shape,
            compiler_params=_cparams(("parallel",), big=True),
            name="ffn",
        )(*args)

    page_table, q, kn, vn, lfn, kt_pages, vt_pages, lft_pages, seq0, n_seq = decode
    npg = page_table.shape[1]
    w, page_len = kt_pages.shape[1], kt_pages.shape[2]
    steps_per_seq = steps // n_seq
    pages = npg // steps_per_seq
    assert steps_per_seq * n_seq == steps and pages * steps_per_seq == npg

    def page_spec(r, nrows):
        return pl.BlockSpec((None, nrows, page_len),
                            lambda i, pt: (pt[seq0 + i // steps_per_seq,
                                              npg - 1 - ((i % steps_per_seq) * pages + r)], 0, 0))

    row = pl.BlockSpec((None, 1, w), lambda i, pt: (seq0 + i // steps_per_seq, 0, 0))
    grid_spec = pltpu.PrefetchScalarGridSpec(
        num_scalar_prefetch=1,
        grid=(steps,),
        in_specs=in_specs
        + [row, row, row, pl.BlockSpec((None, 1, V7X_LANES), lambda i, pt: (seq0 + i // steps_per_seq, 0, 0))]
        + [page_spec(r, w) for r in range(pages)]
        + [page_spec(r, w) for r in range(pages)]
        + [page_spec(r, N_HEADS) for r in range(pages)],
        out_specs=[rows.row_spec(d), pl.BlockSpec((None, 1, w), lambda i, pt: (i // steps_per_seq, 0, 0))],
        scratch_shapes=[pltpu.VMEM((N_HEADS, w), F32), pltpu.VMEM((N_HEADS, 1), F32), pltpu.VMEM((N_HEADS, 1), F32),
                        pltpu.VMEM((N_HEADS, 1), F32), pltpu.VMEM((N_HEADS, w), F32)],
    )
    return pl.pallas_call(
        functools.partial(_ffn_decode_kernel, ff=ff, alpha=alpha, pages=pages, steps_per_seq=steps_per_seq),
        grid_spec=grid_spec,
        out_shape=[x_shape, jax.ShapeDtypeStruct((n_seq, 1, w), F32)],
        compiler_params=_cparams(("arbitrary",), big=True),
        name="ffn_decode",
    )(page_table, *args, q, kn, vn, lfn,
      *([kt_pages] * pages), *([vt_pages] * pages), *([lft_pages] * pages))


def _outproj_kernel(x_ref, g_ref, a0_ref, a1_ref, w_ref, lng_ref, lnb_ref, o_ref, *, alpha):
    half = a0_ref.shape[-1]
    y = _dot(a0_ref[...].astype(BF16), w_ref[0:half, :]) + _dot(a1_ref[...].astype(BF16), w_ref[half:, :])
    z = alpha * x_ref[...] + g_ref[...] * y
    o_ref[...] = _layernorm(z, lng_ref[...], lnb_ref[...])


def _outproj_call(x, rows, k_gate, a0, a1, w_out, ln_g, ln_b, ln_idx, alpha):
    n, d = x.shape
    return pl.pallas_call(
        functools.partial(_outproj_kernel, alpha=alpha),
        grid=(n // rows.tm,),
        in_specs=[rows.row_spec(d), rows.mod_spec(k_gate), rows.row_spec(a0.shape[1]), rows.row_spec(a1.shape[1]),
                  _const_spec(w_out),
                  pl.BlockSpec((None, 1, d), lambda i: (ln_idx, 0, 0)),
                  pl.BlockSpec((None, 1, d), lambda i: (ln_idx, 0, 0))],
        out_specs=rows.row_spec(d),
        out_shape=jax.ShapeDtypeStruct((n, d), F32),
        compiler_params=_cparams(("parallel",), big=True),
        name="mixer_outproj",
    )(x, rows.mods, a0, a1, w_out, ln_g, ln_b)


def _rotate_token_major(a, cos, sin_signed):
    width = a.shape[-1]
    lane = lax.broadcasted_iota(jnp.int32, a.shape, 1)
    first_half = (lane & (HEAD_DIM // 2)) == 0
    partner = jnp.where(first_half, pltpu.roll(a, width - HEAD_DIM // 2, 1), pltpu.roll(a, HEAD_DIM // 2, 1))
    return a * cos + partner * sin_signed


def _even_in_kernel(x_ref, sh_ref, sc_ref, wtok_ref, wt_ref, bf_ref, cos_ref, sin_ref, cost_ref, sint_ref,
                    q_o, qr_o, vr_o, sg_o, kt_o, vt_o, ktc_o, vtc_o, krt_o, lft_o):
    w = HEAD_WIDTH
    x = x_ref[...]
    h = (x * (1.0 + sc_ref[...]) + sh_ref[...]).astype(BF16)

    q_o[...] = (_dot(h, wtok_ref[:, 0:w]) * (QK_SCALE * LOG2E)).astype(BF16)
    qr = _dot(h, wtok_ref[:, w:2 * w])
    qr_o[...] = _rotate_token_major(qr, cos_ref[...], sin_ref[...]).astype(BF16)
    vr_o[...] = _dot(h, wtok_ref[:, 2 * w:3 * w]).astype(BF16)
    sg_o[...] = _silu(_dot(h, wtok_ref[:, 3 * w:4 * w])).astype(BF16)

    kt = _dot_nt(wt_ref[0:w, :], h)
    kt_o[...] = kt
    ktc_o[...] = kt.astype(BF16)
    vt = _dot_nt(wt_ref[w:2 * w, :], h)
    vt_o[...] = vt
    vtc_o[...] = vt.astype(BF16)

    krt = _dot_nt(wt_ref[2 * w:3 * w, :], h)
    cos_t, sin_t = cost_ref[...], sint_ref[...]
    half = HEAD_DIM // 2
    for hh in range(N_HEADS):
        x1 = krt[hh * HEAD_DIM:hh * HEAD_DIM + half, :]
        x2 = krt[hh * HEAD_DIM + half:(hh + 1) * HEAD_DIM, :]
        krt_o[hh * HEAD_DIM:hh * HEAD_DIM + half, :] = (x1 * cos_t - x2 * sin_t) * QK_SCALE
        krt_o[hh * HEAD_DIM + half:(hh + 1) * HEAD_DIM, :] = (x1 * sin_t + x2 * cos_t) * QK_SCALE

    fg = _dot_nt(wt_ref[3 * w:, :], h)[0:N_HEADS, :] + bf_ref[...]
    lft_o[...] = _log_sigmoid(fg)


def _even_in_call(x, rows, ks, w_tok, w_t, bf_col, rot, batch, seq):
    n, d = x.shape
    tm = rows.tm
    bps = seq // tm
    w = HEAD_WIDTH
    cos_tok, sin_tok, cos_t, sin_t = rot
    tok_out = pl.BlockSpec((tm, w), lambda i: (i, 0))
    t_out = pl.BlockSpec((None, w, tm), lambda i: (i // bps, 0, i % bps))
    tc_out = pl.BlockSpec((None, None, w, tm), lambda i: (i // bps, i % bps, 0, 0))
    tok_shape = jax.ShapeDtypeStruct((n, w), BF16)
    return pl.pallas_call(
        _even_in_kernel,
        grid=(n // tm,),
        in_specs=[rows.row_spec(d), rows.mod_spec(ks[0]), rows.mod_spec(ks[1]),
                  _const_spec(w_tok), _const_spec(w_t), _const_spec(bf_col),
                  pl.BlockSpec((tm, w), lambda i: (i % bps, 0)),
                  pl.BlockSpec((tm, w), lambda i: (i % bps, 0)),
                  pl.BlockSpec((HEAD_DIM // 2, tm), lambda i: (0, i % bps)),
                  pl.BlockSpec((HEAD_DIM // 2, tm), lambda i: (0, i % bps))],
        out_specs=[tok_out, tok_out, tok_out, tok_out, t_out, t_out, tc_out, tc_out, tc_out,
                   pl.BlockSpec((None, N_HEADS, tm), lambda i: (i // bps, 0, i % bps))],
        out_shape=[tok_shape, tok_shape, tok_shape, tok_shape,
                   jax.ShapeDtypeStruct((batch, w, seq), F32),
                   jax.ShapeDtypeStruct((batch, w, seq), F32),
                   jax.ShapeDtypeStruct((batch, bps, w, tm), BF16),
                   jax.ShapeDtypeStruct((batch, bps, w, tm), BF16),
                   jax.ShapeDtypeStruct((batch, bps, w, tm), F32),
                   jax.ShapeDtypeStruct((batch, N_HEADS, seq), F32)],
        compiler_params=_cparams(("parallel",), big=True),
        name="even_inproj",
    )(x, rows.mods, rows.mods, w_tok, w_t, bf_col, cos_tok, sin_tok, cos_t, sin_t)


def _cumsum_kernel(x_ref, o_ref):
    x = x_ref[...]
    n = x.shape[-1]
    lane = lax.broadcasted_iota(jnp.int32, x.shape, 1)
    shift = 1
    while shift < n:
        x = x + jnp.where(lane >= shift, pltpu.roll(x, shift, 1), 0.0)
        shift *= 2
    o_ref[...] = x * LOG2E


def _cumsum_call(lft):
    batch, heads, seq = lft.shape
    spec = pl.BlockSpec((None, heads, seq), lambda b: (b, 0, 0))
    return pl.pallas_call(
        _cumsum_kernel, grid=(batch,), in_specs=[spec], out_specs=spec,
        out_shape=jax.ShapeDtypeStruct(lft.shape, F32),
        compiler_params=_cparams(("parallel",)),
        name="logf_cumsum",
    )(lft)


def _split3(x):
    hi = x.astype(BF16).astype(F32)
    r = x - hi
    mid = r.astype(BF16).astype(F32)
    lo = (r - mid).astype(BF16).astype(F32)
    return hi, mid, lo


def _fox_kernel(q_ref, kt_ref, vt_ref, c_ref, o_ref):
    tq = q_ref.shape[0]
    tk = kt_ref.shape[-1]
    pw = 2 * HEAD_DIM
    slab_rows = 16
    i = pl.program_id(2)
    q = q_ref[...]
    lane = lax.broadcasted_iota(jnp.int32, (tq, pw), 1)
    q_aug = []
    for hd in range(2):
        base = HEAD_DIM if hd == 0 else 0
        hi, mid, lo = _split3(c_ref[hd, pl.ds(i, 1), :][:, 0:1])
        aug = jnp.where(lane == base + 3, hi, jnp.where(lane == base + 4, mid, jnp.where(lane == base + 5, lo,
              jnp.where((lane >= base) & (lane < base + 3), 1.0, 0.0))))
        own = (lane < HEAD_DIM) if hd == 0 else (lane >= HEAD_DIM)
        q_aug.append(jnp.where(own, q, aug.astype(BF16)))
    r16 = lax.broadcasted_iota(jnp.int32, (slab_rows, tk), 0)
    rowv = lax.broadcasted_iota(jnp.int32, (pw, tk), 0)

    def scores(j, hd):
        kt = kt_ref[j]
        hi, mid, lo = _split3(-c_ref[hd, pl.ds(j, 1), :])
        slab = jnp.where(r16 == 0, hi, jnp.where(r16 == 1, mid, jnp.where(r16 == 2, lo,
               jnp.where(r16 < 6, 1.0, 0.0)))).astype(BF16)
        if hd == 0:
            kt_aug = jnp.concatenate([kt[0:HEAD_DIM], slab, kt[HEAD_DIM + slab_rows:]], axis=0)
        else:
            kt_aug = jnp.concatenate([slab, kt[slab_rows:]], axis=0)
        return _dot(q_aug[hd], kt_aug)

    def values(j, hd):
        vt = vt_ref[j]
        return jnp.where(rowv == (HEAD_DIM if hd == 0 else 0), jnp.ones_like(vt), vt)

    def update(s, vt_aug, m, acc):
        m_new = jnp.maximum(m, jnp.max(s, axis=-1, keepdims=True))
        return m_new, jnp.exp2(m - m_new) * acc + _dot_nt(jnp.exp2(s - m_new).astype(BF16), vt_aug)

    def diag_scores(hd):
        row = lax.broadcasted_iota(jnp.int32, (tq, tk), 0)
        col = lax.broadcasted_iota(jnp.int32, (tq, tk), 1)
        return jnp.where(row >= col, scores(i, hd), -jnp.inf)

    def step2(t, carry):
        j = 2 * t
        return tuple(update(jnp.concatenate([scores(j, hd), scores(j + 1, hd)], axis=1),
                            jnp.concatenate([values(j, hd), values(j + 1, hd)], axis=1), *carry[hd])
                     for hd in range(2))

    def tail_pair(carry):
        return tuple(update(jnp.concatenate([scores(i - 1, hd), diag_scores(hd)], axis=1),
                            jnp.concatenate([values(i - 1, hd), values(i, hd)], axis=1), *carry[hd])[1]
                     for hd in range(2))

    def tail_single(carry):
        return tuple(update(diag_scores(hd), values(i, hd), *carry[hd])[1] for hd in range(2))

    init_head = (jnp.full((tq, 1), -jnp.inf, F32), jnp.zeros((tq, pw), F32))
    carry = lax.fori_loop(0, lax.shift_right_logical(i, 1), step2, (init_head, init_head))
    outs = lax.cond((i & 1) == 1, tail_pair, tail_single, carry)
    o0 = outs[0] / outs[0][:, HEAD_DIM:HEAD_DIM + 1]
    o1 = outs[1] / outs[1][:, 0:1]
    o_ref[...] = jnp.where(lane < HEAD_DIM, o0, o1).astype(BF16)


def _fox_call(q, ktc, vtc, c4, batch, seq):
    n, w = q.shape
    tq = ktc.shape[-1]
    nq = seq // tq
    pw = 2 * HEAD_DIM
    return pl.pallas_call(
        _fox_kernel,
        grid=(batch, HEAD_PAIRS, nq),
        in_specs=[pl.BlockSpec((tq, pw), lambda b, p, i: (b * nq + i, p)),
                  pl.BlockSpec((None, nq, pw, tq), lambda b, p, i: (b, 0, p, 0)),
                  pl.BlockSpec((None, nq, pw, tq), lambda b, p, i: (b, 0, p, 0)),
                  pl.BlockSpec((None, 2, nq, tq), lambda b, p, i: (b, p, 0, 0))],
        out_specs=pl.BlockSpec((tq, pw), lambda b, p, i: (b * nq + i, p)),
        out_shape=jax.ShapeDtypeStruct((n, w), BF16),
        compiler_params=_cparams(("parallel", "parallel", "arbitrary"), big=True),
        name="fox_prompt",
    )(q, ktc, vtc, c4)


def _ret_kernel(q_ref, kt_ref, v_ref, sg_ref, lgl_ref, lgr_ref, y_ref, s_ref):
    chunk = RET_CHUNK
    pw = 2 * HEAD_DIM
    nkb, _, tb = kt_ref.shape
    lg_lane = lgl_ref[...]
    lg_row = lgr_ref[...]
    lg_a, lg_b = lg_lane[:, 0:1], lg_lane[:, HEAD_DIM:HEAD_DIM + 1]

    ri = lax.broadcasted_iota(jnp.int32, (2 * chunk, chunk), 0)
    cj = lax.broadcasted_iota(jnp.int32, (2 * chunk, chunk), 1)
    first = ri < chunk
    diff = (jnp.where(first, ri, ri - chunk) - cj).astype(F32)
    decay_mask = jnp.where(diff >= 0, jnp.exp(jnp.maximum(diff, 0.0) * jnp.where(first, lg_a, lg_b)), 0.0)
    jj = lax.broadcasted_iota(jnp.int32, (pw, chunk), 1).astype(F32)
    col_decay = jnp.exp((chunk - 1.0 - jj) * lg_row)
    ii = lax.broadcasted_iota(jnp.int32, (chunk, pw), 0).astype(F32)
    row_decay = jnp.exp((ii + 1.0) * lg_lane)
    chunk_decay = jnp.exp(float(chunk) * lg_row)
    r2 = lax.broadcasted_iota(jnp.int32, (pw, pw), 0)
    c2 = lax.broadcasted_iota(jnp.int32, (pw, pw), 1)
    same_head = (r2 < HEAD_DIM) == (c2 < HEAD_DIM)
    seg_avg = jnp.where(same_head, 1.0 / HEAD_DIM, 0.0).astype(BF16)
    lane = lax.broadcasted_iota(jnp.int32, (chunk, pw), 1)

    def seg_mean(a):
        hi = a.astype(BF16)
        lo = (a - hi.astype(F32)).astype(BF16)
        return _dot(hi, seg_avg) + _dot(lo, seg_avg)

    def one_chunk(q, kt, v, sg, state):
        zero = jnp.zeros_like(q)
        q_stack = jnp.concatenate([jnp.where(lane < HEAD_DIM, q, zero), jnp.where(lane >= HEAD_DIM, q, zero)], axis=0)
        qk = _dot(q_stack, kt.astype(BF16)) * decay_mask
        kd = (kt * col_decay).astype(BF16)
        r = _dot(jnp.concatenate([qk.astype(BF16), kd], axis=0), v)
        inner = jnp.where(lane < HEAD_DIM, r[0:chunk], r[chunk:2 * chunk])
        update = jnp.where(same_head, r[2 * chunk:], 0.0)
        cross = _dot(q, state.astype(BF16)) * row_decay
        o = inner + cross
        mu = seg_mean(o)
        d = o - mu
        var = seg_mean(d * d)
        y = sg.astype(F32) * (d * lax.rsqrt(var + GN_EPS))
        return y.astype(BF16), chunk_decay * state + update

    def body(jb, state):
        kt_blk = kt_ref[jb]
        for sub in range(tb // chunk):
            t0 = pl.multiple_of(jb * tb + sub * chunk, chunk)
            y, state = one_chunk(q_ref[pl.ds(t0, chunk), :], kt_blk[:, sub * chunk:(sub + 1) * chunk],
                                 v_ref[pl.ds(t0, chunk), :], sg_ref[pl.ds(t0, chunk), :], state)
            y_ref[pl.ds(t0, chunk), :] = y
        return state

    state = lax.fori_loop(0, nkb, body, jnp.zeros((pw, pw), F32), unroll=4)
    s_ref[0] = state[0:HEAD_DIM, 0:HEAD_DIM]
    s_ref[1] = pltpu.roll(state, HEAD_DIM, 1)[HEAD_DIM:, 0:HEAD_DIM]


def _ret_call(qr, krt, vr, sg, lg_lane, lg_row, batch, seq):
    n, w = qr.shape
    nkb, tb = krt.shape[1], krt.shape[3]
    pw = 2 * HEAD_DIM
    seq_spec = pl.BlockSpec((seq, pw), lambda b, p: (b, p))
    return pl.pallas_call(
        _ret_kernel,
        grid=(batch, HEAD_PAIRS),
        in_specs=[seq_spec,
                  pl.BlockSpec((None, nkb, pw, tb), lambda b, p: (b, 0, p, 0)),
                  seq_spec, seq_spec,
                  pl.BlockSpec((None, 1, pw), lambda b, p: (p, 0, 0)),
                  pl.BlockSpec((None, pw, 1), lambda b, p: (p, 0, 0))],
        out_specs=[seq_spec, pl.BlockSpec((None, 2, HEAD_DIM, HEAD_DIM), lambda b, p: (b, p, 0, 0))],
        out_shape=[jax.ShapeDtypeStruct((n, w), BF16),
                   jax.ShapeDtypeStruct((batch, N_HEADS, HEAD_DIM, HEAD_DIM), F32)],
        compiler_params=_cparams(("parallel", "parallel"), big=True),
        name="ret_prompt",
    )(qr, krt, vr, sg, lg_lane, lg_row)


def _odd_prompt_kernel(x_ref, sh_ref, sc_ref, g_ref, win_ref, cw_ref, wout_ref, lng_ref, lnb_ref,
                       o_ref, st_ref, carry_ref, *, bps, alpha):
    i = pl.program_id(0)
    x = x_ref[...]
    tm, d = x.shape
    h = (x * (1.0 + sc_ref[...]) + sh_ref[...]).astype(BF16)
    b_gate = _dot(h, win_ref[:, 0:d])
    u = _dot(h, win_ref[:, d:2 * d]) * _dot(h, win_ref[:, 2 * d:3 * d])

    @pl.when(i % bps == 0)
    def _():
        carry_ref[...] = jnp.zeros_like(carry_ref)

    prev = carry_ref[...]
    p1, p2 = prev[7:8, :], prev[6:7, :]
    row = lax.broadcasted_iota(jnp.int32, (tm, d), 0)
    u1 = jnp.where(row == 0, p1, pltpu.roll(u, 1, 0))
    u2 = jnp.where(row == 0, p2, jnp.where(row == 1, p1, pltpu.roll(u, 2, 0)))
    cw = cw_ref[...]
    z = cw[0:1, :] * u2 + cw[1:2, :] * u1 + cw[2:3, :] * u
    carry_ref[...] = u[tm - 8:, :]
    st_ref[...] = u[tm - (CONV_WIDTH - 1):, :]
    y = _dot((b_gate * z).astype(BF16), wout_ref[...])
    zz = alpha * x + g_ref[...] * y
    o_ref[...] = _layernorm(zz, lng_ref[...], lnb_ref[...])


def _odd_prompt_call(x, rows, ks, w_in, conv_w, w_out, ln_g, ln_b, ln_idx, alpha, batch, seq):
    n, d = x.shape
    bps = seq // rows.tm
    return pl.pallas_call(
        functools.partial(_odd_prompt_kernel, bps=bps, alpha=alpha),
        grid=(n // rows.tm,),
        in_specs=[rows.row_spec(d), rows.mod_spec(ks[0]), rows.mod_spec(ks[1]), rows.mod_spec(ks[2]),
                  _const_spec(w_in), _const_spec(conv_w), _const_spec(w_out),
                  pl.BlockSpec((None, 1, d), lambda i: (ln_idx, 0, 0)),
                  pl.BlockSpec((None, 1, d), lambda i: (ln_idx, 0, 0))],
        out_specs=[rows.row_spec(d), pl.BlockSpec((None, CONV_WIDTH - 1, d), lambda i: (i // bps, 0, 0))],
        out_shape=[jax.ShapeDtypeStruct((n, d), F32), jax.ShapeDtypeStruct((batch, CONV_WIDTH - 1, d), F32)],
        scratch_shapes=[pltpu.VMEM((8, d), F32)],
        compiler_params=_cparams(("arbitrary",), big=True),
        name="odd_prompt",
    )(x, rows.mods, rows.mods, rows.mods, w_in, conv_w, w_out, ln_g, ln_b)


def _even_in_sample_kernel(x_ref, sh_ref, sc_ref, w_ref, bf_ref, cos_ref, sin_ref,
                           q_o, k_o, v_o, qr_o, kr_o, vr_o, sg_o, lf_o):
    w = HEAD_WIDTH
    h = (x_ref[...] * (1.0 + sc_ref[...]) + sh_ref[...]).astype(BF16)
    cos, sin = cos_ref[...], sin_ref[...]
    q_o[...] = _dot(h, w_ref[:, 0:w]) * QK_SCALE
    k_o[...] = _dot(h, w_ref[:, w:2 * w])
    v_o[...] = _dot(h, w_ref[:, 2 * w:3 * w])
    qr_o[...] = _rotate_token_major(_dot(h, w_ref[:, 3 * w:4 * w]), cos, sin)
    kr_o[...] = _rotate_token_major(_dot(h, w_ref[:, 4 * w:5 * w]), cos, sin) * QK_SCALE
    vr_o[...] = _dot(h, w_ref[:, 5 * w:6 * w])
    sg_o[...] = _silu(_dot(h, w_ref[:, 6 * w:7 * w]))
    lf_o[...] = _log_sigmoid(_dot(h, w_ref[:, 7 * w:]) + bf_ref[...])


def _even_in_sample_call(x, rows, ks, w_all, bf_row, cos, sin):
    n, d = x.shape
    w = HEAD_WIDTH
    full = pl.BlockSpec((n, w), lambda i: (0, 0))
    shp = jax.ShapeDtypeStruct((n, w), F32)
    return pl.pallas_call(
        _even_in_sample_kernel,
        grid=(1,),
        in_specs=[rows.row_spec(d), rows.mod_spec(ks[0]), rows.mod_spec(ks[1]),
                  _const_spec(w_all), _const_spec(bf_row), full, full],
        out_specs=[full] * 7 + [pl.BlockSpec((n, V7X_LANES), lambda i: (0, 0))],
        out_shape=[shp] * 7 + [jax.ShapeDtypeStruct((n, V7X_LANES), F32)],
        compiler_params=_cparams(("arbitrary",), big=True),
        name="even_inproj_sample",
    )(x, rows.mods, rows.mods, w_all, bf_row, cos, sin)


def _own_head_lanes():
    sub = lax.broadcasted_iota(jnp.int32, (N_HEADS, HEAD_WIDTH), 0)
    lane = lax.broadcasted_iota(jnp.int32, (N_HEADS, HEAD_WIDTH), 1)
    return (lane >= sub * HEAD_DIM) & (lane < (sub + 1) * HEAD_DIM)


def _decode_fox_init(q_ref, kn_ref, vn_ref, lfn_ref, qbd_s, m_s, l_s, run_s, acc_s):
    w = HEAD_WIDTH
    q_bd = jnp.where(_own_head_lanes(), jnp.broadcast_to(q_ref[...], (N_HEADS, w)), 0.0)
    qbd_s[...] = q_bd
    m_s[...] = jnp.sum(q_bd * kn_ref[...], axis=-1, keepdims=True)
    l_s[...] = jnp.ones_like(l_s)
    acc_s[...] = jnp.broadcast_to(vn_ref[...], (N_HEADS, w))
    s128 = lax.broadcasted_iota(jnp.int32, (N_HEADS, V7X_LANES), 0)
    l128 = lax.broadcasted_iota(jnp.int32, (N_HEADS, V7X_LANES), 1)
    lfn = jnp.broadcast_to(lfn_ref[...], (N_HEADS, V7X_LANES))
    run_s[...] = jnp.sum(jnp.where(s128 == l128, lfn, 0.0), axis=-1, keepdims=True)


def _decode_fox_finish(o_ref, l_s, acc_s):
    o = acc_s[...] / l_s[...]
    o_ref[...] = jnp.sum(jnp.where(_own_head_lanes(), o, 0.0), axis=0, keepdims=True)


def _decode_fox_pages(k_refs, v_refs, lf_refs, qbd_s, m_s, l_s, run_s, acc_s):
    pages = len(k_refs)
    page_len = k_refs[0].shape[-1]
    q_b = qbd_s[...].astype(BF16)
    order = list(range(pages - 1, -1, -1))
    s = jnp.concatenate([_dot(q_b, k_refs[r][...].astype(BF16)) for r in order], axis=1)
    lf = jnp.concatenate([lf_refs[r][...] for r in order], axis=1)
    n = pages * page_len
    lane_n = lax.broadcasted_iota(jnp.int32, (N_HEADS, n), 1)
    suf = lf
    shift = 1
    while shift < n:
        suf = suf + jnp.where(lane_n < n - shift, pltpu.roll(suf, n - shift, 1), 0.0)
        shift *= 2
    run = run_s[...]
    s = s + ((suf - lf) + run)
    m = m_s[...]
    m_new = jnp.maximum(m, jnp.max(s, axis=-1, keepdims=True))
    a = jnp.exp(m - m_new)
    p = jnp.exp(s - m_new)
    l_s[...] = a * l_s[...] + jnp.sum(p, axis=-1, keepdims=True)
    m_s[...] = m_new
    p_b = p.astype(BF16)
    pv = None
    for idx, r in enumerate(order):
        t = _dot_nt(p_b[:, idx * page_len:(idx + 1) * page_len], v_refs[r][...].astype(BF16))
        pv = t if pv is None else pv + t
    acc_s[...] = a * acc_s[...] + pv
    run_s[...] = run + suf[:, 0:1]


def _decode_ret_kernel(q_ref, k_ref, v_ref, sg_ref, s_ref, gl_ref, y_ref, so_ref):
    q, k, v = q_ref[...], k_ref[...], v_ref[...]
    state = s_ref[...]
    g = gl_ref[...]
    inner = jnp.sum(q * k, axis=2, keepdims=True) * v
    cross = jnp.sum(q * state, axis=2, keepdims=True) * g
    so_ref[...] = g * state + k * v
    o = inner + cross
    mu = jnp.mean(o, axis=-1, keepdims=True)
    d = o - mu
    var = jnp.mean(d * d, axis=-1, keepdims=True)
    y_ref[...] = sg_ref[...] * (d * lax.rsqrt(var + GN_EPS))


def _decode_ret_call(q_col, k_col, v_row, sg_row, state, decay):
    nseq = state.shape[0]
    nb = 8
    hd = HEAD_DIM
    col = pl.BlockSpec((nb, N_HEADS, hd, 1), lambda i: (i, 0, 0, 0))
    row = pl.BlockSpec((nb, N_HEADS, 1, hd), lambda i: (i, 0, 0, 0))
    st = pl.BlockSpec((nb, N_HEADS, hd, hd), lambda i: (i, 0, 0, 0))
    return pl.pallas_call(
        _decode_ret_kernel,
        grid=(nseq // nb,),
        in_specs=[col, col, row, row, st, pl.BlockSpec((N_HEADS, 1, 1), lambda i: (0, 0, 0))],
        out_specs=[row, st],
        out_shape=[jax.ShapeDtypeStruct((nseq, N_HEADS, 1, hd), F32),
                   jax.ShapeDtypeStruct(state.shape, F32)],
        compiler_params=_cparams(("parallel",), big=True),
        name="ret_decode",
    )(q_col, k_col, v_row, sg_row, state, decay)


def _odd_sample_kernel(x_ref, sh_ref, sc_ref, g_ref, win_ref, cw_ref, b0_ref, b1_ref, wout_ref, lng_ref, lnb_ref,
                       o_ref, u_ref, *, alpha):
    x = x_ref[...]
    d = x.shape[-1]
    h = (x * (1.0 + sc_ref[...]) + sh_ref[...]).astype(BF16)
    b_gate = _dot(h, win_ref[:, 0:d])
    u = _dot(h, win_ref[:, d:2 * d]) * _dot(h, win_ref[:, 2 * d:3 * d])
    cw = cw_ref[...]
    z = cw[0:1, :] * b0_ref[...] + cw[1:2, :] * b1_ref[...] + cw[2:3, :] * u
    u_ref[...] = u
    y = _dot((b_gate * z).astype(BF16), wout_ref[...])
    o_ref[...] = _layernorm(alpha * x + g_ref[...] * y, lng_ref[...], lnb_ref[...])


def _odd_sample_call(x, rows, ks, w_in, conv_w, buf0, buf1, w_out, ln_g, ln_b, ln_idx, alpha):
    n, d = x.shape
    full = pl.BlockSpec((n, d), lambda i: (0, 0))
    return pl.pallas_call(
        functools.partial(_odd_sample_kernel, alpha=alpha),
        grid=(1,),
        in_specs=[rows.row_spec(d), rows.mod_spec(ks[0]), rows.mod_spec(ks[1]), rows.mod_spec(ks[2]),
                  _const_spec(w_in), _const_spec(conv_w), full, full, _const_spec(w_out),
                  pl.BlockSpec((None, 1, d), lambda i: (ln_idx, 0, 0)),
                  pl.BlockSpec((None, 1, d), lambda i: (ln_idx, 0, 0))],
        out_specs=[full, full],
        out_shape=[jax.ShapeDtypeStruct((n, d), F32), jax.ShapeDtypeStruct((n, d), F32)],
        compiler_params=_cparams(("arbitrary",), big=True),
        name="odd_sample",
    )(x, rows.mods, rows.mods, rows.mods, w_in, conv_w, buf0, buf1, w_out, ln_g, ln_b)


def _rotary_tables(pos):
    half = HEAD_DIM // 2
    inv = 1.0 / (RET_ANGLE_BASE ** np.linspace(0.0, 1.0, half))
    ang = np.asarray(pos, np.float64)[:, None] * inv[None, :]
    return np.cos(ang), np.sin(ang)


def _token_major_tables(cos, sin):
    cos_h = np.concatenate([cos, cos], axis=1)
    sin_h = np.concatenate([-sin, sin], axis=1)
    return (jnp.asarray(np.tile(cos_h, (1, N_HEADS)), F32), jnp.asarray(np.tile(sin_h, (1, N_HEADS)), F32))


def kernel(x_prompt, x_sample, cache_k, cache_v, cache_logf, state_ret, state_conv, page_table, c_prompt, c_sample,
           w_ada, b_ada, w_ffn_in, w_ffn_out, ln_g, ln_b, w_in_even, b_forget, w_out_even, w_in_odd, conv_w,
           w_out_odd):
    batch, seq, d = x_prompt.shape
    nseq = x_sample.shape[0]
    depth = w_ada.shape[0]
    past_len = page_table.shape[1] * cache_k.shape[2]
    alpha = (2.0 * depth) ** 0.25
    w = HEAD_WIDTH
    tm = ROW_BLOCK
    assert seq % tm == 0 and x_sample.shape[1] == 1 and d % V7X_LANES == 0
    assert cache_k.shape[3] == N_HEADS and cache_k.shape[4] == HEAD_DIM

    n_c = batch + nseq
    pad = (-n_c) % 8
    c_all = jnp.concatenate([c_prompt, c_sample, jnp.zeros((pad, d), F32)], axis=0)
    mods = _ada_call(c_all, w_ada, b_ada)
    mods_p = mods[:, :batch].reshape(depth, batch, 1, N_MOD * d)
    mods_s = mods[:, batch:n_c]

    ln_g3 = ln_g.reshape(depth * 3, 1, d)
    ln_b3 = ln_b.reshape(depth * 3, 1, d)
    w_ffn_in_b = w_ffn_in.astype(BF16)
    w_ffn_out_b = w_ffn_out.astype(BF16)

    xp = x_prompt.reshape(batch * seq, d)
    xs = x_sample.reshape(nseq, d)

    cos_p, sin_p = _rotary_tables(np.arange(seq))
    rot_p = _token_major_tables(cos_p, sin_p) + (jnp.asarray(cos_p.T, F32), jnp.asarray(sin_p.T, F32))
    cos_s, sin_s = _token_major_tables(*_rotary_tables(np.full((nseq,), past_len)))

    log_decay = jnp.log(1.0 - 2.0 ** (-5.0 - jnp.arange(N_HEADS, dtype=F32)))
    lg_pairs = jnp.repeat(log_decay.reshape(HEAD_PAIRS, 2), HEAD_DIM, axis=1)
    lg_lane = lg_pairs.reshape(HEAD_PAIRS, 1, 2 * HEAD_DIM)
    lg_row = lg_pairs.reshape(HEAD_PAIRS, 2 * HEAD_DIM, 1)
    step_decay = jnp.exp(log_decay).reshape(N_HEADS, 1, 1)

    outs_p = {"k": [], "v": [], "lf": [], "ret": [], "conv": []}
    outs_s = {"k": [], "v": [], "lf": [], "ret": [], "conv": []}

    assert depth == 2 and w_in_even.shape[0] == 1 and cache_k.shape[0] == 1
    n_hosts = 2 * depth
    assert nseq % n_hosts == 0
    wi = w_in_even[0]
    cuts = [0, w, 2 * w, 3 * w, 3 * w + N_HEADS, 4 * w + N_HEADS, 5 * w + N_HEADS, 6 * w + N_HEADS, 7 * w + N_HEADS]
    qf_w, kf_w, vf_w, fg_w, qr_w, kr_w, vr_w, gr_w = [wi[:, a:b] for a, b in zip(cuts[:-1], cuts[1:])]
    w_out_even_b = w_out_even[0].astype(BF16)
    bf = b_forget[0]
    w_in_odd_b = w_in_odd[0].astype(BF16)
    w_out_odd_b = w_out_odd[0].astype(BF16)

    def rows_s(l):
        return _Rows(mods_s, l, nseq, None, d)

    def rows_p(l):
        return _Rows(mods_p, l, tm, seq // tm, d)

    def ffn(x, rows, half, l, decode=None):
        ks = (0, 1, 2) if half == 0 else (6, 7, 8)
        return _ffn_call(x, rows, ks, w_ffn_in_b, w_ffn_out_b, (l, half), ln_g3, ln_b3, 3 * l + 2 * half, alpha, decode)

    xs = ffn(xs, rows_s(0), 0, 0)
    w_all = jnp.concatenate([qf_w, kf_w, vf_w, qr_w, kr_w, vr_w, gr_w, fg_w,
                             jnp.zeros((d, V7X_LANES - N_HEADS), F32)], axis=1).astype(BF16)
    bf_row = jnp.concatenate([bf, jnp.zeros((V7X_LANES - N_HEADS,), F32)]).reshape(1, V7X_LANES)
    (qs, ks_, vs, qrs, krs, vrs, sgs, lfs) = _even_in_sample_call(xs, rows_s(0), (3, 4), w_all, bf_row, cos_s, sin_s)
    n_phys, page_len = cache_k.shape[1], cache_k.shape[2]
    kt_pages = jnp.transpose(cache_k[0], (0, 2, 3, 1)).reshape(n_phys, w, page_len)
    vt_pages = jnp.transpose(cache_v[0], (0, 2, 3, 1)).reshape(n_phys, w, page_len)
    lft_pages = jnp.transpose(cache_logf[0], (0, 2, 1))
    per_host = nseq // n_hosts
    dec_args = (page_table, qs.reshape(nseq, 1, w), ks_.reshape(nseq, 1, w), vs.reshape(nseq, 1, w),
                lfs.reshape(nseq, 1, V7X_LANES), kt_pages, vt_pages, lft_pages)
    dec_out = []

    def ffn_host(x, half, l):
        x, o = ffn(x, rows_p(l), half, l, dec_args + (len(dec_out) * per_host, per_host))
        dec_out.append(o)
        return x

    xp = ffn_host(xp, 0, 0)
    w_tok = jnp.concatenate([qf_w, qr_w, vr_w, gr_w], axis=1).astype(BF16)
    w_t = jnp.concatenate([kf_w, vf_w, kr_w, fg_w, jnp.zeros((d, 8), F32)], axis=1).T.astype(BF16)
    (q, qr, vr, sg, kt, vt, ktc, vtc, krt, lft) = _even_in_call(
        xp, rows_p(0), (3, 4), w_tok, w_t, bf.reshape(N_HEADS, 1), rot_p, batch, seq)
    c_t = _cumsum_call(lft)
    of = _fox_call(q, ktc, vtc, c_t.reshape(batch, N_HEADS, seq // tm, tm), batch, seq)
    yr, s_new = _ret_call(qr, krt, vr, sg, lg_lane, lg_row, batch, seq)
    xp = _outproj_call(xp, rows_p(0), 5, of, yr, w_out_even_b, ln_g3, ln_b3, 1, alpha)
    outs_p["k"].append(jnp.transpose(kt.reshape(batch, N_HEADS, HEAD_DIM, seq), (0, 3, 1, 2)))
    outs_p["v"].append(jnp.transpose(vt.reshape(batch, N_HEADS, HEAD_DIM, seq), (0, 3, 1, 2)))
    outs_p["lf"].append(jnp.transpose(lft, (0, 2, 1)))
    outs_p["ret"].append(s_new)
    xp = ffn_host(xp, 1, 0)
    xp = ffn_host(xp, 0, 1)
    xp, conv_p = _odd_prompt_call(xp, rows_p(1), (3, 4, 5), w_in_odd_b, conv_w[0], w_out_odd_b, ln_g3, ln_b3,
                                  4, alpha, batch, seq)
    outs_p["conv"].append(conv_p)
    xp = ffn_host(xp, 1, 1)

    of_s = jnp.concatenate(dec_out, axis=0).reshape(nseq, w)
    lfs = lfs[:, :N_HEADS]
    col = (nseq, N_HEADS, HEAD_DIM, 1)
    rw = (nseq, N_HEADS, 1, HEAD_DIM)
    yr_s, s_new_s = _decode_ret_call(qrs.reshape(col), krs.reshape(col), vrs.reshape(rw), sgs.reshape(rw),
                                     state_ret[0], step_decay)
    xs = _outproj_call(xs, rows_s(0), 5, of_s, yr_s.reshape(nseq, w), w_out_even_b, ln_g3, ln_b3, 1, alpha)
    outs_s["k"].append(ks_.reshape(nseq, 1, N_HEADS, HEAD_DIM))
    outs_s["v"].append(vs.reshape(nseq, 1, N_HEADS, HEAD_DIM))
    outs_s["lf"].append(lfs.reshape(nseq, 1, N_HEADS))
    outs_s["ret"].append(s_new_s)
    xs = ffn(xs, rows_s(0), 1, 0)
    xs = ffn(xs, rows_s(1), 0, 1)
    buf = state_conv[0]
    xs, u_s = _odd_sample_call(xs, rows_s(1), (3, 4, 5), w_in_odd_b, conv_w[0], buf[:, 0], buf[:, 1], w_out_odd_b,
                               ln_g3, ln_b3, 4, alpha)
    outs_s["conv"].append(jnp.stack([buf[:, 1], u_s], axis=1))
    xs = ffn(xs, rows_s(1), 1, 1)

    def stk(lst):
        return jnp.stack(lst)

    return (xp.reshape(batch, seq, d), xs.reshape(nseq, 1, d),
            stk(outs_p["k"]), stk(outs_p["v"]), stk(outs_p["lf"]), stk(outs_p["ret"]), stk(outs_p["conv"]),
            stk(outs_s["k"]), stk(outs_s["v"]), stk(outs_s["lf"]), stk(outs_s["ret"]), stk(outs_s["conv"]))
```

```python
import functools

import jax
import jax.numpy as jnp
import numpy as np
from jax import lax
from jax.experimental import pallas as pl
from jax.experimental.pallas import tpu as pltpu

F32 = jnp.float32
BF16 = jnp.bfloat16

HEAD_DIM = 64
N_HEADS = 8
HEAD_WIDTH = N_HEADS * HEAD_DIM
N_MOD = 9
CONV_WIDTH = 3
RET_ANGLE_BASE = 10000.0
LN_EPS = 1e-5
GN_EPS = 1e-6
QK_SCALE = HEAD_DIM ** -0.5
LOG2E = 1.4426950408889634

V7X_LANES = 128
V7X_VMEM_LIMIT_BYTES = 56 * 1024 * 1024

ROW_BLOCK = 512
FF_CHUNK = 256
RET_CHUNK = 256
ADA_COL_BLOCK = 1152
HEAD_PAIRS = N_HEADS // 2


def _cparams(sem, big=False):
    return pltpu.CompilerParams(dimension_semantics=sem,
                                vmem_limit_bytes=V7X_VMEM_LIMIT_BYTES if big else None)


def _const_spec(arr):
    nd = arr.ndim
    return pl.BlockSpec(arr.shape, lambda *_: (0,) * nd, pipeline_mode=pl.Buffered(1))


def _layernorm(z, g, b):
    mu = jnp.mean(z, axis=-1, keepdims=True)
    d = z - mu
    var = jnp.mean(d * d, axis=-1, keepdims=True)
    return d * lax.rsqrt(var + LN_EPS) * g + b


def _silu(a):
    return a * jax.nn.sigmoid(a)


def _log_sigmoid(z):
    return jnp.minimum(z, 0.0) - jnp.log1p(jnp.exp(-jnp.abs(z)))


def _dot(a, b):
    return jnp.dot(a, b, preferred_element_type=F32)


def _dot_nt(a, b):
    return lax.dot_general(a, b, (((1,), (1,)), ((), ())), preferred_element_type=F32)


def _ada_kernel(c_ref, w_ref, b_ref, o_ref):
    s = _silu(c_ref[...]).astype(BF16)
    o_ref[...] = _dot(s, w_ref[...].astype(BF16)) + b_ref[...]


def _ada_call(c_all, w_ada, b_ada):
    depth, d, nm = w_ada.shape
    rows = c_all.shape[0]
    tn = ADA_COL_BLOCK
    return pl.pallas_call(
        _ada_kernel,
        grid=(depth, nm // tn),
        in_specs=[pl.BlockSpec((rows, d), lambda l, j: (0, 0)),
                  pl.BlockSpec((None, d, tn), lambda l, j: (l, 0, j)),
                  pl.BlockSpec((None, 1, tn), lambda l, j: (l, 0, j))],
        out_specs=pl.BlockSpec((None, rows, tn), lambda l, j: (l, 0, j)),
        out_shape=jax.ShapeDtypeStruct((depth, rows, nm), F32),
        compiler_params=_cparams(("arbitrary", "arbitrary"), big=True),
        name="ada_mods",
    )(c_all, w_ada, b_ada.reshape(depth, 1, nm))


class _Rows:
    def __init__(self, mods, layer, tm, blocks_per_seq, d):
        self.mods, self.layer, self.tm, self.bps, self.d = mods, layer, tm, blocks_per_seq, d

    def mod_spec(self, k):
        l, d = self.layer, self.d
        if self.bps is None:
            return pl.BlockSpec((None, self.tm, d), lambda i, *_: (l, 0, k))
        bps = self.bps
        return pl.BlockSpec((None, None, 1, d), lambda i, *_: (l, i // bps, 0, k))

    def row_spec(self, width):
        return pl.BlockSpec((self.tm, width), lambda i, *_: (i, 0))


def _mixer_out_math(x, gate, a0, a1, w_ref, lng, lnb, alpha):
    half = a0.shape[-1]
    y = _dot(a0.astype(BF16), w_ref[0:half, :]) + _dot(a1.astype(BF16), w_ref[half:, :])
    return _layernorm(alpha * x + gate * y, lng, lnb)


def _ffn_rows(refs, ff, alpha, pre):
    x = refs[0][...]
    refs = refs[1:]
    if pre:
        gate, a0, a1, wo_ref, lng1, lnb1 = refs[:6]
        x = _mixer_out_math(x, gate[...], a0[...], a1[...], wo_ref, lng1[...], lnb1[...], alpha)
        refs = refs[6:]
    sh_ref, sc_ref, g_ref, win_ref, wout_ref, lng_ref, lnb_ref = refs
    h = (x * (1.0 + sc_ref[...]) + sh_ref[...]).astype(BF16)
    acc = None
    for c in range(ff // FF_CHUNK):
        lo, hi = c * FF_CHUNK, (c + 1) * FF_CHUNK
        a = _dot(h, win_ref[:, lo:hi])
        b = _dot(h, win_ref[:, ff + lo:ff + hi])
        y = _dot((_silu(a) * b).astype(BF16), wout_ref[lo:hi, :])
        acc = y if acc is None else acc + y
    z = alpha * x + (0.5 * g_ref[...]) * acc
    return _layernorm(z, lng_ref[...], lnb_ref[...])


def _ffn_kernel(*refs, ff, alpha, pre):
    refs[-1][...] = _ffn_rows(refs[:-1], ff, alpha, pre)


def _ffn_decode_kernel(pt_ref, *refs, ff, alpha, pre, pages, steps_per_seq):
    del pt_ref
    n_ffn = 14 if pre else 8
    ffn_in, rest = refs[:n_ffn], refs[n_ffn:]
    q_ref, kn_ref, vn_ref, lfn_ref = rest[:4]
    k_refs, v_refs, lf_refs = rest[4:4 + pages], rest[4 + pages:4 + 2 * pages], rest[4 + 2 * pages:4 + 3 * pages]
    o_ref, dec_ref = rest[4 + 3 * pages:6 + 3 * pages]
    scratch = rest[6 + 3 * pages:]
    g = pl.program_id(0) % steps_per_seq
    pl.when(g == 0)(lambda: _decode_fox_init(q_ref, kn_ref, vn_ref, lfn_ref, *scratch))
    _decode_fox_pages(k_refs, v_refs, lf_refs, *scratch)
    o_ref[...] = _ffn_rows(ffn_in, ff, alpha, pre)
    pl.when(g == steps_per_seq - 1)(lambda: _decode_fox_finish(dec_ref, scratch[2], scratch[4]))


def _ffn_call(x, rows, ks, w_in, w_out, which, ln_g, ln_b, ln_idx, alpha, decode=None, mixer_out=None):
    n, d = x.shape
    ff = w_out.shape[2]
    l, j = which
    steps = n // rows.tm

    def w_spec(arr):
        return pl.BlockSpec((None, None) + arr.shape[2:], lambda i, *_: (l, j, 0, 0), pipeline_mode=pl.Buffered(1))

    def ln_spec(idx):
        return pl.BlockSpec((None, 1, d), lambda i, *_: (idx, 0, 0))

    in_specs = [rows.row_spec(d)]
    args = [x]
    if mixer_out is not None:
        k_gate, a0, a1, w_o, ln_idx1 = mixer_out
        in_specs += [rows.mod_spec(k_gate), rows.row_spec(a0.shape[1]), rows.row_spec(a1.shape[1]), _const_spec(w_o),
                     ln_spec(ln_idx1), ln_spec(ln_idx1)]
        args += [rows.mods, a0, a1, w_o, ln_g, ln_b]
    in_specs += [rows.mod_spec(ks[0]), rows.mod_spec(ks[1]), rows.mod_spec(ks[2]), w_spec(w_in), w_spec(w_out),
                 ln_spec(ln_idx), ln_spec(ln_idx)]
    args += [rows.mods, rows.mods, rows.mods, w_in, w_out, ln_g, ln_b]
    pre = mixer_out is not None
    x_shape = jax.ShapeDtypeStruct((n, d), F32)
    if decode is None:
        return pl.pallas_call(
            functools.partial(_ffn_kernel, ff=ff, alpha=alpha, pre=pre),
            grid=(steps,), in_specs=in_specs, out_specs=rows.row_spec(d), out_shape=x_shape,
            compiler_params=_cparams(("parallel",), big=True),
            name="ffn",
        )(*args)

    page_table, q, kn, vn, lfn, kt_pages, vt_pages, lft_pages, seq0, n_seq = decode
    npg = page_table.shape[1]
    w, page_len = kt_pages.shape[1], kt_pages.shape[2]
    steps_per_seq = steps // n_seq
    pages = npg // steps_per_seq
    assert steps_per_seq * n_seq == steps and pages * steps_per_seq == npg

    def page_spec(r, nrows):
        return pl.BlockSpec((None, nrows, page_len),
                            lambda i, pt: (pt[seq0 + i // steps_per_seq,
                                              npg - 1 - ((i % steps_per_seq) * pages + r)], 0, 0))

    row = pl.BlockSpec((None, 1, w), lambda i, pt: (seq0 + i // steps_per_seq, 0, 0))
    grid_spec = pltpu.PrefetchScalarGridSpec(
        num_scalar_prefetch=1,
        grid=(steps,),
        in_specs=in_specs
        + [row, row, row, pl.BlockSpec((None, 1, V7X_LANES), lambda i, pt: (seq0 + i // steps_per_seq, 0, 0))]
        + [page_spec(r, w) for r in range(pages)]
        + [page_spec(r, w) for r in range(pages)]
        + [page_spec(r, N_HEADS) for r in range(pages)],
        out_specs=[rows.row_spec(d), pl.BlockSpec((None, 1, w), lambda i, pt: (i // steps_per_seq, 0, 0))],
        scratch_shapes=[pltpu.VMEM((N_HEADS, w), F32), pltpu.VMEM((N_HEADS, 1), F32), pltpu.VMEM((N_HEADS, 1), F32),
                        pltpu.VMEM((N_HEADS, 1), F32), pltpu.VMEM((N_HEADS, w), F32)],
    )
    return pl.pallas_call(
        functools.partial(_ffn_decode_kernel, ff=ff, alpha=alpha, pre=pre, pages=pages, steps_per_seq=steps_per_seq),
        grid_spec=grid_spec,
        out_shape=[x_shape, jax.ShapeDtypeStruct((n_seq, 1, w), F32)],
        compiler_params=_cparams(("arbitrary",), big=True),
        name="ffn_decode",
    )(page_table, *args, q, kn, vn, lfn,
      *([kt_pages] * pages), *([vt_pages] * pages), *([lft_pages] * pages))


def _outproj_kernel(x_ref, g_ref, a0_ref, a1_ref, w_ref, lng_ref, lnb_ref, o_ref, *, alpha):
    o_ref[...] = _mixer_out_math(x_ref[...], g_ref[...], a0_ref[...], a1_ref[...], w_ref, lng_ref[...], lnb_ref[...],
                                 alpha)


def _outproj_call(x, rows, k_gate, a0, a1, w_out, ln_g, ln_b, ln_idx, alpha):
    n, d = x.shape
    return pl.pallas_call(
        functools.partial(_outproj_kernel, alpha=alpha),
        grid=(n // rows.tm,),
        in_specs=[rows.row_spec(d), rows.mod_spec(k_gate), rows.row_spec(a0.shape[1]), rows.row_spec(a1.shape[1]),
                  _const_spec(w_out),
                  pl.BlockSpec((None, 1, d), lambda i: (ln_idx, 0, 0)),
                  pl.BlockSpec((None, 1, d), lambda i: (ln_idx, 0, 0))],
        out_specs=rows.row_spec(d),
        out_shape=jax.ShapeDtypeStruct((n, d), F32),
        compiler_params=_cparams(("parallel",), big=True),
        name="mixer_outproj",
    )(x, rows.mods, a0, a1, w_out, ln_g, ln_b)


def _rotate_token_major(a, cos, sin_signed):
    width = a.shape[-1]
    lane = lax.broadcasted_iota(jnp.int32, a.shape, 1)
    first_half = (lane & (HEAD_DIM // 2)) == 0
    partner = jnp.where(first_half, pltpu.roll(a, width - HEAD_DIM // 2, 1), pltpu.roll(a, HEAD_DIM // 2, 1))
    return a * cos + partner * sin_signed


_EVEN_ROWS = {}
_r0 = 0
for _name, _n in (("qf", HEAD_WIDTH), ("kf", HEAD_WIDTH), ("vf", HEAD_WIDTH), ("fg", N_HEADS),
                  ("qr", HEAD_WIDTH), ("kr", HEAD_WIDTH), ("vr", HEAD_WIDTH), ("gr", HEAD_WIDTH)):
    _EVEN_ROWS[_name] = (_r0, _r0 + _n)
    _r0 += _n
EVEN_IN_ROWS = _r0
FG_ROWS = 16


def _even_in_kernel(x_ref, sh_ref, sc_ref, w_ref, bf_ref, cos_ref, sin_ref, cost_ref, sint_ref,
                    q_o, qr_o, vr_o, sg_o, kt_o, vt_o, ktc_o, vtc_o, krt_o, lft_o, wtok_ref, wt_ref):
    w = HEAD_WIDTH

    @pl.when(pl.program_id(0) == 0)
    def _():
        for k, name in enumerate(("qf", "qr", "vr", "gr")):
            lo, hi = _EVEN_ROWS[name]
            wtok_ref[:, k * w:(k + 1) * w] = w_ref[lo:hi, :].T.astype(BF16)
        for k, name in enumerate(("kf", "vf", "kr")):
            lo, hi = _EVEN_ROWS[name]
            wt_ref[k * w:(k + 1) * w, :] = w_ref[lo:hi, :].astype(BF16)
        lo, hi = _EVEN_ROWS["fg"]
        fg_rows = jnp.concatenate([w_ref[lo:hi, :], jnp.zeros((FG_ROWS - N_HEADS, w_ref.shape[1]), F32)], axis=0)
        wt_ref[3 * w:, :] = fg_rows.astype(BF16)

    x = x_ref[...]
    h = (x * (1.0 + sc_ref[...]) + sh_ref[...]).astype(BF16)

    q_o[...] = (_dot(h, wtok_ref[:, 0:w]) * (QK_SCALE * LOG2E)).astype(BF16)
    qr = _dot(h, wtok_ref[:, w:2 * w])
    qr_o[...] = _rotate_token_major(qr, cos_ref[...], sin_ref[...]).astype(BF16)
    vr_o[...] = _dot(h, wtok_ref[:, 2 * w:3 * w]).astype(BF16)
    sg_o[...] = _silu(_dot(h, wtok_ref[:, 3 * w:4 * w])).astype(BF16)

    kt = _dot_nt(wt_ref[0:w, :], h)
    kt_o[...] = kt
    ktc_o[...] = kt.astype(BF16)
    vt = _dot_nt(wt_ref[w:2 * w, :], h)
    vt_o[...] = vt
    vtc_o[...] = vt.astype(BF16)

    krt = _dot_nt(wt_ref[2 * w:3 * w, :], h)
    cos_t, sin_t = cost_ref[...], sint_ref[...]
    half = HEAD_DIM // 2
    for hh in range(N_HEADS):
        x1 = krt[hh * HEAD_DIM:hh * HEAD_DIM + half, :]
        x2 = krt[hh * HEAD_DIM + half:(hh + 1) * HEAD_DIM, :]
        krt_o[hh * HEAD_DIM:hh * HEAD_DIM + half, :] = (x1 * cos_t - x2 * sin_t) * QK_SCALE
        krt_o[hh * HEAD_DIM + half:(hh + 1) * HEAD_DIM, :] = (x1 * sin_t + x2 * cos_t) * QK_SCALE

    fg = _dot_nt(wt_ref[3 * w:, :], h)[0:N_HEADS, :] + bf_ref[...]
    lft_o[...] = _log_sigmoid(fg)


def _even_in_call(x, rows, ks, w_t_f32, bf_col, rot, batch, seq):
    n, d = x.shape
    tm = rows.tm
    bps = seq // tm
    w = HEAD_WIDTH
    cos_tok, sin_tok, cos_t, sin_t = rot
    tok_out = pl.BlockSpec((tm, w), lambda i: (i, 0))
    t_out = pl.BlockSpec((None, w, tm), lambda i: (i // bps, 0, i % bps))
    tc_out = pl.BlockSpec((None, None, w, tm), lambda i: (i // bps, i % bps, 0, 0))
    tok_shape = jax.ShapeDtypeStruct((n, w), BF16)
    return pl.pallas_call(
        _even_in_kernel,
        grid=(n // tm,),
        in_specs=[rows.row_spec(d), rows.mod_spec(ks[0]), rows.mod_spec(ks[1]),
                  _const_spec(w_t_f32), _const_spec(bf_col),
                  pl.BlockSpec((tm, w), lambda i: (i % bps, 0)),
                  pl.BlockSpec((tm, w), lambda i: (i % bps, 0)),
                  pl.BlockSpec((HEAD_DIM // 2, tm), lambda i: (0, i % bps)),
                  pl.BlockSpec((HEAD_DIM // 2, tm), lambda i: (0, i % bps))],
        out_specs=[tok_out, tok_out, tok_out, tok_out, t_out, t_out, tc_out, tc_out, tc_out,
                   pl.BlockSpec((None, N_HEADS, tm), lambda i: (i // bps, 0, i % bps))],
        out_shape=[tok_shape, tok_shape, tok_shape, tok_shape,
                   jax.ShapeDtypeStruct((batch, w, seq), F32),
                   jax.ShapeDtypeStruct((batch, w, seq), F32),
                   jax.ShapeDtypeStruct((batch, bps, w, tm), BF16),
                   jax.ShapeDtypeStruct((batch, bps, w, tm), BF16),
                   jax.ShapeDtypeStruct((batch, bps, w, tm), F32),
                   jax.ShapeDtypeStruct((batch, N_HEADS, seq), F32)],
        scratch_shapes=[pltpu.VMEM((d, 4 * w), BF16), pltpu.VMEM((3 * w + FG_ROWS, d), BF16)],
        compiler_params=_cparams(("arbitrary",), big=True),
        name="even_inproj",
    )(x, rows.mods, rows.mods, w_t_f32, bf_col, cos_tok, sin_tok, cos_t, sin_t)


def _cumsum_kernel(x_ref, o_ref):
    x = x_ref[...]
    n = x.shape[-1]
    lane = lax.broadcasted_iota(jnp.int32, x.shape, 1)
    shift = 1
    while shift < n:
        x = x + jnp.where(lane >= shift, pltpu.roll(x, shift, 1), 0.0)
        shift *= 2
    o_ref[...] = x * LOG2E


def _cumsum_call(lft):
    batch, heads, seq = lft.shape
    spec = pl.BlockSpec((None, heads, seq), lambda b: (b, 0, 0))
    return pl.pallas_call(
        _cumsum_kernel, grid=(batch,), in_specs=[spec], out_specs=spec,
        out_shape=jax.ShapeDtypeStruct(lft.shape, F32),
        compiler_params=_cparams(("parallel",)),
        name="logf_cumsum",
    )(lft)


def _split3(x):
    hi = x.astype(BF16).astype(F32)
    r = x - hi
    mid = r.astype(BF16).astype(F32)
    lo = (r - mid).astype(BF16).astype(F32)
    return hi, mid, lo


def _fox_kernel(q_ref, kt_ref, vt_ref, c_ref, o_ref):
    tq = q_ref.shape[0]
    tk = kt_ref.shape[-1]
    pw = 2 * HEAD_DIM
    slab_rows = 16
    i = pl.program_id(2)
    q = q_ref[...]
    lane = lax.broadcasted_iota(jnp.int32, (tq, pw), 1)
    q_aug = []
    for hd in range(2):
        base = HEAD_DIM if hd == 0 else 0
        hi, mid, lo = _split3(c_ref[hd, pl.ds(i, 1), :][:, 0:1])
        aug = jnp.where(lane == base + 3, hi, jnp.where(lane == base + 4, mid, jnp.where(lane == base + 5, lo,
              jnp.where((lane >= base) & (lane < base + 3), 1.0, 0.0))))
        own = (lane < HEAD_DIM) if hd == 0 else (lane >= HEAD_DIM)
        q_aug.append(jnp.where(own, q, aug.astype(BF16)))
    r16 = lax.broadcasted_iota(jnp.int32, (slab_rows, tk), 0)
    rowv = lax.broadcasted_iota(jnp.int32, (pw, tk), 0)

    def scores(j, hd):
        kt = kt_ref[j]
        hi, mid, lo = _split3(-c_ref[hd, pl.ds(j, 1), :])
        slab = jnp.where(r16 == 0, hi, jnp.where(r16 == 1, mid, jnp.where(r16 == 2, lo,
               jnp.where(r16 < 6, 1.0, 0.0)))).astype(BF16)
        if hd == 0:
            kt_aug = jnp.concatenate([kt[0:HEAD_DIM], slab, kt[HEAD_DIM + slab_rows:]], axis=0)
        else:
            kt_aug = jnp.concatenate([slab, kt[slab_rows:]], axis=0)
        return _dot(q_aug[hd], kt_aug)

    def values(j, hd):
        vt = vt_ref[j]
        return jnp.where(rowv == (HEAD_DIM if hd == 0 else 0), jnp.ones_like(vt), vt)

    def update(s, vt_aug, m, acc):
        m_new = jnp.maximum(m, jnp.max(s, axis=-1, keepdims=True))
        return m_new, jnp.exp2(m - m_new) * acc + _dot_nt(jnp.exp2(s - m_new).astype(BF16), vt_aug)

    def diag_scores(hd):
        row = lax.broadcasted_iota(jnp.int32, (tq, tk), 0)
        col = lax.broadcasted_iota(jnp.int32, (tq, tk), 1)
        return jnp.where(row >= col, scores(i, hd), -jnp.inf)

    def step2(t, carry):
        j = 2 * t
        return tuple(update(jnp.concatenate([scores(j, hd), scores(j + 1, hd)], axis=1),
                            jnp.concatenate([values(j, hd), values(j + 1, hd)], axis=1), *carry[hd])
                     for hd in range(2))

    def tail_pair(carry):
        return tuple(update(jnp.concatenate([scores(i - 1, hd), diag_scores(hd)], axis=1),
                            jnp.concatenate([values(i - 1, hd), values(i, hd)], axis=1), *carry[hd])[1]
                     for hd in range(2))

    def tail_single(carry):
        return tuple(update(diag_scores(hd), values(i, hd), *carry[hd])[1] for hd in range(2))

    init_head = (jnp.full((tq, 1), -jnp.inf, F32), jnp.zeros((tq, pw), F32))
    carry = lax.fori_loop(0, lax.shift_right_logical(i, 1), step2, (init_head, init_head))
    outs = lax.cond((i & 1) == 1, tail_pair, tail_single, carry)
    o0 = outs[0] / outs[0][:, HEAD_DIM:HEAD_DIM + 1]
    o1 = outs[1] / outs[1][:, 0:1]
    o_ref[...] = jnp.where(lane < HEAD_DIM, o0, o1).astype(BF16)


def _fox_call(q, ktc, vtc, c4, batch, seq):
    n, w = q.shape
    tq = ktc.shape[-1]
    nq = seq // tq
    pw = 2 * HEAD_DIM
    return pl.pallas_call(
        _fox_kernel,
        grid=(batch, HEAD_PAIRS, nq),
        in_specs=[pl.BlockSpec((tq, pw), lambda b, p, i: (b * nq + i, p)),
                  pl.BlockSpec((None, nq, pw, tq), lambda b, p, i: (b, 0, p, 0)),
                  pl.BlockSpec((None, nq, pw, tq), lambda b, p, i: (b, 0, p, 0)),
                  pl.BlockSpec((None, 2, nq, tq), lambda b, p, i: (b, p, 0, 0))],
        out_specs=pl.BlockSpec((tq, pw), lambda b, p, i: (b * nq + i, p)),
        out_shape=jax.ShapeDtypeStruct((n, w), BF16),
        compiler_params=_cparams(("parallel", "parallel", "arbitrary"), big=True),
        name="fox_prompt",
    )(q, ktc, vtc, c4)


def _ret_kernel(q_ref, kt_ref, v_ref, sg_ref, lgl_ref, lgr_ref, y_ref, s_ref):
    chunk = RET_CHUNK
    pw = 2 * HEAD_DIM
    nkb, _, tb = kt_ref.shape
    lg_lane = lgl_ref[...]
    lg_row = lgr_ref[...]
    lg_a, lg_b = lg_lane[:, 0:1], lg_lane[:, HEAD_DIM:HEAD_DIM + 1]

    ri = lax.broadcasted_iota(jnp.int32, (2 * chunk, chunk), 0)
    cj = lax.broadcasted_iota(jnp.int32, (2 * chunk, chunk), 1)
    first = ri < chunk
    diff = (jnp.where(first, ri, ri - chunk) - cj).astype(F32)
    decay_mask = jnp.where(diff >= 0, jnp.exp(jnp.maximum(diff, 0.0) * jnp.where(first, lg_a, lg_b)), 0.0)
    jj = lax.broadcasted_iota(jnp.int32, (pw, chunk), 1).astype(F32)
    col_decay = jnp.exp((chunk - 1.0 - jj) * lg_row)
    ii = lax.broadcasted_iota(jnp.int32, (chunk, pw), 0).astype(F32)
    row_decay = jnp.exp((ii + 1.0) * lg_lane)
    chunk_decay = jnp.exp(float(chunk) * lg_row)
    r2 = lax.broadcasted_iota(jnp.int32, (pw, pw), 0)
    c2 = lax.broadcasted_iota(jnp.int32, (pw, pw), 1)
    same_head = (r2 < HEAD_DIM) == (c2 < HEAD_DIM)
    seg_avg = jnp.where(same_head, 1.0 / HEAD_DIM, 0.0).astype(BF16)
    lane = lax.broadcasted_iota(jnp.int32, (chunk, pw), 1)

    def seg_mean(a):
        hi = a.astype(BF16)
        lo = (a - hi.astype(F32)).astype(BF16)
        return _dot(hi, seg_avg) + _dot(lo, seg_avg)

    def one_chunk(q, kt, v, sg, state):
        zero = jnp.zeros_like(q)
        q_stack = jnp.concatenate([jnp.where(lane < HEAD_DIM, q, zero), jnp.where(lane >= HEAD_DIM, q, zero)], axis=0)
        qk = _dot(q_stack, kt.astype(BF16)) * decay_mask
        kd = (kt * col_decay).astype(BF16)
        r = _dot(jnp.concatenate([qk.astype(BF16), kd], axis=0), v)
        inner = jnp.where(lane < HEAD_DIM, r[0:chunk], r[chunk:2 * chunk])
        update = jnp.where(same_head, r[2 * chunk:], 0.0)
        cross = _dot(q, state.astype(BF16)) * row_decay
        o = inner + cross
        mu = seg_mean(o)
        d = o - mu
        var = seg_mean(d * d)
        y = sg.astype(F32) * (d * lax.rsqrt(var + GN_EPS))
        return y.astype(BF16), chunk_decay * state + update

    def body(jb, state):
        kt_blk = kt_ref[jb]
        for sub in range(tb // chunk):
            t0 = pl.multiple_of(jb * tb + sub * chunk, chunk)
            y, state = one_chunk(q_ref[pl.ds(t0, chunk), :], kt_blk[:, sub * chunk:(sub + 1) * chunk],
                                 v_ref[pl.ds(t0, chunk), :], sg_ref[pl.ds(t0, chunk), :], state)
            y_ref[pl.ds(t0, chunk), :] = y
        return state

    state = lax.fori_loop(0, nkb, body, jnp.zeros((pw, pw), F32), unroll=4)
    s_ref[0] = state[0:HEAD_DIM, 0:HEAD_DIM]
    s_ref[1] = pltpu.roll(state, HEAD_DIM, 1)[HEAD_DIM:, 0:HEAD_DIM]


def _ret_call(qr, krt, vr, sg, lg_lane, lg_row, batch, seq):
    n, w = qr.shape
    nkb, tb = krt.shape[1], krt.shape[3]
    pw = 2 * HEAD_DIM
    seq_spec = pl.BlockSpec((seq, pw), lambda b, p: (b, p))
    return pl.pallas_call(
        _ret_kernel,
        grid=(batch, HEAD_PAIRS),
        in_specs=[seq_spec,
                  pl.BlockSpec((None, nkb, pw, tb), lambda b, p: (b, 0, p, 0)),
                  seq_spec, seq_spec,
                  pl.BlockSpec((None, 1, pw), lambda b, p: (p, 0, 0)),
                  pl.BlockSpec((None, pw, 1), lambda b, p: (p, 0, 0))],
        out_specs=[seq_spec, pl.BlockSpec((None, 2, HEAD_DIM, HEAD_DIM), lambda b, p: (b, p, 0, 0))],
        out_shape=[jax.ShapeDtypeStruct((n, w), BF16),
                   jax.ShapeDtypeStruct((batch, N_HEADS, HEAD_DIM, HEAD_DIM), F32)],
        compiler_params=_cparams(("parallel", "parallel"), big=True),
        name="ret_prompt",
    )(qr, krt, vr, sg, lg_lane, lg_row)


def _odd_prompt_kernel(x_ref, sh_ref, sc_ref, g_ref, win_ref, cw_ref, wout_ref, lng_ref, lnb_ref,
                       o_ref, st_ref, carry_ref, *, bps, alpha):
    i = pl.program_id(0)
    x = x_ref[...]
    tm, d = x.shape
    h = (x * (1.0 + sc_ref[...]) + sh_ref[...]).astype(BF16)
    b_gate = _dot(h, win_ref[:, 0:d])
    u = _dot(h, win_ref[:, d:2 * d]) * _dot(h, win_ref[:, 2 * d:3 * d])

    @pl.when(i % bps == 0)
    def _():
        carry_ref[...] = jnp.zeros_like(carry_ref)

    prev = carry_ref[...]
    p1, p2 = prev[7:8, :], prev[6:7, :]
    row = lax.broadcasted_iota(jnp.int32, (tm, d), 0)
    u1 = jnp.where(row == 0, p1, pltpu.roll(u, 1, 0))
    u2 = jnp.where(row == 0, p2, jnp.where(row == 1, p1, pltpu.roll(u, 2, 0)))
    cw = cw_ref[...]
    z = cw[0:1, :] * u2 + cw[1:2, :] * u1 + cw[2:3, :] * u
    carry_ref[...] = u[tm - 8:, :]
    st_ref[...] = u[tm - (CONV_WIDTH - 1):, :]
    y = _dot((b_gate * z).astype(BF16), wout_ref[...])
    zz = alpha * x + g_ref[...] * y
    o_ref[...] = _layernorm(zz, lng_ref[...], lnb_ref[...])


def _odd_prompt_call(x, rows, ks, w_in, conv_w, w_out, ln_g, ln_b, ln_idx, alpha, batch, seq):
    n, d = x.shape
    bps = seq // rows.tm
    return pl.pallas_call(
        functools.partial(_odd_prompt_kernel, bps=bps, alpha=alpha),
        grid=(n // rows.tm,),
        in_specs=[rows.row_spec(d), rows.mod_spec(ks[0]), rows.mod_spec(ks[1]), rows.mod_spec(ks[2]),
                  _const_spec(w_in), _const_spec(conv_w), _const_spec(w_out),
                  pl.BlockSpec((None, 1, d), lambda i: (ln_idx, 0, 0)),
                  pl.BlockSpec((None, 1, d), lambda i: (ln_idx, 0, 0))],
        out_specs=[rows.row_spec(d), pl.BlockSpec((None, CONV_WIDTH - 1, d), lambda i: (i // bps, 0, 0))],
        out_shape=[jax.ShapeDtypeStruct((n, d), F32), jax.ShapeDtypeStruct((batch, CONV_WIDTH - 1, d), F32)],
        scratch_shapes=[pltpu.VMEM((8, d), F32)],
        compiler_params=_cparams(("arbitrary",), big=True),
        name="odd_prompt",
    )(x, rows.mods, rows.mods, rows.mods, w_in, conv_w, w_out, ln_g, ln_b)


def _even_in_sample_kernel(x_ref, sh_ref, sc_ref, w_ref, bf_ref, cost_ref, sint_ref,
                           q_o, k_o, v_o, qrt_o, krt_o, vr_o, sg_o, lf_o):
    d = w_ref.shape[1]
    h = (x_ref[...] * (1.0 + sc_ref[...]) + sh_ref[...]).astype(BF16)

    def rows(name):
        lo, hi = _EVEN_ROWS[name]
        return w_ref[lo:hi, :]

    def tok(name):
        return _dot_nt(h, rows(name).astype(BF16))

    q_o[...] = tok("qf") * QK_SCALE
    k_o[...] = tok("kf")
    v_o[...] = tok("vf")
    vr_o[...] = tok("vr")
    sg_o[...] = _silu(tok("gr"))
    fg_rows = jnp.concatenate([rows("fg"), jnp.zeros((V7X_LANES - N_HEADS, d), F32)], axis=0).astype(BF16)
    lf_o[...] = _log_sigmoid(_dot_nt(h, fg_rows) + bf_ref[...])

    cos_t, sin_t = cost_ref[...], sint_ref[...]
    half = HEAD_DIM // 2
    for name, out, scale in (("qr", qrt_o, 1.0), ("kr", krt_o, QK_SCALE)):
        t = _dot_nt(rows(name).astype(BF16), h)
        for hh in range(N_HEADS):
            x1 = t[hh * HEAD_DIM:hh * HEAD_DIM + half, :]
            x2 = t[hh * HEAD_DIM + half:(hh + 1) * HEAD_DIM, :]
            out[hh * HEAD_DIM:hh * HEAD_DIM + half, :] = (x1 * cos_t - x2 * sin_t) * scale
            out[hh * HEAD_DIM + half:(hh + 1) * HEAD_DIM, :] = (x1 * sin_t + x2 * cos_t) * scale


def _even_in_sample_call(x, rows, ks, w_t_f32, bf_row, cos_t, sin_t):
    n, d = x.shape
    w = HEAD_WIDTH
    full = pl.BlockSpec((n, w), lambda i: (0, 0))
    full_t = pl.BlockSpec((w, n), lambda i: (0, 0))
    rot = pl.BlockSpec((HEAD_DIM // 2, n), lambda i: (0, 0))
    shp = jax.ShapeDtypeStruct((n, w), F32)
    shp_t = jax.ShapeDtypeStruct((w, n), F32)
    return pl.pallas_call(
        _even_in_sample_kernel,
        grid=(1,),
        in_specs=[rows.row_spec(d), rows.mod_spec(ks[0]), rows.mod_spec(ks[1]),
                  _const_spec(w_t_f32), _const_spec(bf_row), rot, rot],
        out_specs=[full, full, full, full_t, full_t, full, full, pl.BlockSpec((n, V7X_LANES), lambda i: (0, 0))],
        out_shape=[shp, shp, shp, shp_t, shp_t, shp, shp, jax.ShapeDtypeStruct((n, V7X_LANES), F32)],
        compiler_params=_cparams(("arbitrary",), big=True),
        name="even_inproj_sample",
    )(x, rows.mods, rows.mods, w_t_f32, bf_row, cos_t, sin_t)


def _own_head_lanes():
    sub = lax.broadcasted_iota(jnp.int32, (N_HEADS, HEAD_WIDTH), 0)
    lane = lax.broadcasted_iota(jnp.int32, (N_HEADS, HEAD_WIDTH), 1)
    return (lane >= sub * HEAD_DIM) & (lane < (sub + 1) * HEAD_DIM)


def _decode_fox_init(q_ref, kn_ref, vn_ref, lfn_ref, qbd_s, m_s, l_s, run_s, acc_s):
    w = HEAD_WIDTH
    q_bd = jnp.where(_own_head_lanes(), jnp.broadcast_to(q_ref[...], (N_HEADS, w)), 0.0)
    qbd_s[...] = q_bd
    m_s[...] = jnp.sum(q_bd * kn_ref[...], axis=-1, keepdims=True)
    l_s[...] = jnp.ones_like(l_s)
    acc_s[...] = jnp.broadcast_to(vn_ref[...], (N_HEADS, w))
    s128 = lax.broadcasted_iota(jnp.int32, (N_HEADS, V7X_LANES), 0)
    l128 = lax.broadcasted_iota(jnp.int32, (N_HEADS, V7X_LANES), 1)
    lfn = jnp.broadcast_to(lfn_ref[...], (N_HEADS, V7X_LANES))
    run_s[...] = jnp.sum(jnp.where(s128 == l128, lfn, 0.0), axis=-1, keepdims=True)


def _decode_fox_finish(o_ref, l_s, acc_s):
    o = acc_s[...] / l_s[...]
    o_ref[...] = jnp.sum(jnp.where(_own_head_lanes(), o, 0.0), axis=0, keepdims=True)


def _decode_fox_pages(k_refs, v_refs, lf_refs, qbd_s, m_s, l_s, run_s, acc_s):
    pages = len(k_refs)
    page_len = k_refs[0].shape[-1]
    q_b = qbd_s[...].astype(BF16)
    order = list(range(pages - 1, -1, -1))
    s = jnp.concatenate([_dot(q_b, k_refs[r][...].astype(BF16)) for r in order], axis=1)
    lf = jnp.concatenate([lf_refs[r][...] for r in order], axis=1)
    n = pages * page_len
    lane_n = lax.broadcasted_iota(jnp.int32, (N_HEADS, n), 1)
    suf = lf
    shift = 1
    while shift < n:
        suf = suf + jnp.where(lane_n < n - shift, pltpu.roll(suf, n - shift, 1), 0.0)
        shift *= 2
    run = run_s[...]
    s = s + ((suf - lf) + run)
    m = m_s[...]
    m_new = jnp.maximum(m, jnp.max(s, axis=-1, keepdims=True))
    a = jnp.exp(m - m_new)
    p = jnp.exp(s - m_new)
    l_s[...] = a * l_s[...] + jnp.sum(p, axis=-1, keepdims=True)
    m_s[...] = m_new
    p_b = p.astype(BF16)
    pv = None
    for idx, r in enumerate(order):
        t = _dot_nt(p_b[:, idx * page_len:(idx + 1) * page_len], v_refs[r][...].astype(BF16))
        pv = t if pv is None else pv + t
    acc_s[...] = a * acc_s[...] + pv
    run_s[...] = run + suf[:, 0:1]


def _decode_ret_kernel(qt_ref, kt_ref, v_ref, sg_ref, s_ref, gl_ref, y_ref, so_ref):
    qt, kt = qt_ref[...], kt_ref[...]
    g = gl_ref[...]
    for b in range(qt.shape[-1]):
        q, k = qt[:, :, b:b + 1], kt[:, :, b:b + 1]
        v = v_ref[b]
        state = s_ref[b]
        inner = jnp.sum(q * k, axis=1, keepdims=True) * v
        cross = jnp.sum(q * state, axis=1, keepdims=True) * g
        so_ref[b] = g * state + k * v
        o = inner + cross
        mu = jnp.mean(o, axis=-1, keepdims=True)
        d = o - mu
        var = jnp.mean(d * d, axis=-1, keepdims=True)
        y_ref[b] = sg_ref[b] * (d * lax.rsqrt(var + GN_EPS))


def _decode_ret_call(q_t, k_t, v_row, sg_row, state, decay):
    nseq = state.shape[0]

    def whole(arr):
        nd = arr.ndim
        return pl.BlockSpec(arr.shape, lambda i: (0,) * nd)

    return pl.pallas_call(
        _decode_ret_kernel,
        grid=(1,),
        in_specs=[whole(a) for a in (q_t, k_t, v_row, sg_row, state, decay)],
        out_specs=[whole(v_row), whole(state)],
        out_shape=[jax.ShapeDtypeStruct(v_row.shape, F32), jax.ShapeDtypeStruct(state.shape, F32)],
        compiler_params=_cparams(("arbitrary",), big=True),
        name="ret_decode",
    )(q_t, k_t, v_row, sg_row, state, decay)


def _odd_sample_kernel(x_ref, sh_ref, sc_ref, g_ref, win_ref, cw_ref, b0_ref, b1_ref, wout_ref, lng_ref, lnb_ref,
                       o_ref, u_ref, *, alpha):
    x = x_ref[...]
    d = x.shape[-1]
    h = (x * (1.0 + sc_ref[...]) + sh_ref[...]).astype(BF16)
    b_gate = _dot(h, win_ref[:, 0:d])
    u = _dot(h, win_ref[:, d:2 * d]) * _dot(h, win_ref[:, 2 * d:3 * d])
    cw = cw_ref[...]
    z = cw[0:1, :] * b0_ref[...] + cw[1:2, :] * b1_ref[...] + cw[2:3, :] * u
    u_ref[...] = u
    y = _dot((b_gate * z).astype(BF16), wout_ref[...])
    o_ref[...] = _layernorm(alpha * x + g_ref[...] * y, lng_ref[...], lnb_ref[...])


def _odd_sample_call(x, rows, ks, w_in, conv_w, buf0, buf1, w_out, ln_g, ln_b, ln_idx, alpha):
    n, d = x.shape
    full = pl.BlockSpec((n, d), lambda i: (0, 0))
    return pl.pallas_call(
        functools.partial(_odd_sample_kernel, alpha=alpha),
        grid=(1,),
        in_specs=[rows.row_spec(d), rows.mod_spec(ks[0]), rows.mod_spec(ks[1]), rows.mod_spec(ks[2]),
                  _const_spec(w_in), _const_spec(conv_w), full, full, _const_spec(w_out),
                  pl.BlockSpec((None, 1, d), lambda i: (ln_idx, 0, 0)),
                  pl.BlockSpec((None, 1, d), lambda i: (ln_idx, 0, 0))],
        out_specs=[full, full],
        out_shape=[jax.ShapeDtypeStruct((n, d), F32), jax.ShapeDtypeStruct((n, d), F32)],
        compiler_params=_cparams(("arbitrary",), big=True),
        name="odd_sample",
    )(x, rows.mods, rows.mods, rows.mods, w_in, conv_w, buf0, buf1, w_out, ln_g, ln_b)


def _rotary_tables(pos):
    half = HEAD_DIM // 2
    inv = 1.0 / (RET_ANGLE_BASE ** np.linspace(0.0, 1.0, half))
    ang = np.asarray(pos, np.float64)[:, None] * inv[None, :]
    return np.cos(ang), np.sin(ang)


def _token_major_tables(cos, sin):
    cos_h = np.concatenate([cos, cos], axis=1)
    sin_h = np.concatenate([-sin, sin], axis=1)
    return (jnp.asarray(np.tile(cos_h, (1, N_HEADS)), F32), jnp.asarray(np.tile(sin_h, (1, N_HEADS)), F32))


def kernel(x_prompt, x_sample, cache_k, cache_v, cache_logf, state_ret, state_conv, page_table, c_prompt, c_sample,
           w_ada, b_ada, w_ffn_in, w_ffn_out, ln_g, ln_b, w_in_even, b_forget, w_out_even, w_in_odd, conv_w,
           w_out_odd):
    batch, seq, d = x_prompt.shape
    nseq = x_sample.shape[0]
    depth = w_ada.shape[0]
    past_len = page_table.shape[1] * cache_k.shape[2]
    alpha = (2.0 * depth) ** 0.25
    w = HEAD_WIDTH
    tm = ROW_BLOCK
    assert seq % tm == 0 and x_sample.shape[1] == 1 and d % V7X_LANES == 0
    assert cache_k.shape[3] == N_HEADS and cache_k.shape[4] == HEAD_DIM

    n_c = batch + nseq
    pad = (-n_c) % 8
    c_all = jnp.concatenate([c_prompt, c_sample, jnp.zeros((pad, d), F32)], axis=0)
    mods = _ada_call(c_all, w_ada, b_ada)
    mods_p = mods[:, :batch].reshape(depth, batch, 1, N_MOD * d)
    mods_s = mods[:, batch:n_c]

    ln_g3 = ln_g.reshape(depth * 3, 1, d)
    ln_b3 = ln_b.reshape(depth * 3, 1, d)
    w_ffn_in_b = w_ffn_in.astype(BF16)
    w_ffn_out_b = w_ffn_out.astype(BF16)

    xp = x_prompt.reshape(batch * seq, d)
    xs = x_sample.reshape(nseq, d)

    cos_p, sin_p = _rotary_tables(np.arange(seq))
    rot_p = _token_major_tables(cos_p, sin_p) + (jnp.asarray(cos_p.T, F32), jnp.asarray(sin_p.T, F32))
    cos_s, sin_s = (jnp.asarray(t.T, F32) for t in _rotary_tables(np.full((nseq,), past_len)))

    log_decay = jnp.log(1.0 - 2.0 ** (-5.0 - jnp.arange(N_HEADS, dtype=F32)))
    lg_pairs = jnp.repeat(log_decay.reshape(HEAD_PAIRS, 2), HEAD_DIM, axis=1)
    lg_lane = lg_pairs.reshape(HEAD_PAIRS, 1, 2 * HEAD_DIM)
    lg_row = lg_pairs.reshape(HEAD_PAIRS, 2 * HEAD_DIM, 1)
    step_decay = jnp.exp(log_decay).reshape(N_HEADS, 1, 1)

    outs_p = {"k": [], "v": [], "lf": [], "ret": [], "conv": []}
    outs_s = {"k": [], "v": [], "lf": [], "ret": [], "conv": []}

    assert depth == 2 and w_in_even.shape[0] == 1 and cache_k.shape[0] == 1
    n_hosts = 2 * depth
    assert nseq % n_hosts == 0
    assert w_in_even.shape[2] == EVEN_IN_ROWS
    w_even_t = jnp.transpose(w_in_even[0])
    w_out_even_b = w_out_even[0].astype(BF16)
    bf = b_forget[0]
    w_in_odd_b = w_in_odd[0].astype(BF16)
    w_out_odd_b = w_out_odd[0].astype(BF16)

    def rows_s(l):
        return _Rows(mods_s, l, nseq, None, d)

    def rows_p(l):
        return _Rows(mods_p, l, tm, seq // tm, d)

    def ffn(x, rows, half, l, decode=None, mixer_out=None):
        ks = (0, 1, 2) if half == 0 else (6, 7, 8)
        return _ffn_call(x, rows, ks, w_ffn_in_b, w_ffn_out_b, (l, half), ln_g3, ln_b3, 3 * l + 2 * half, alpha,
                         decode, mixer_out)

    xs = ffn(xs, rows_s(0), 0, 0)
    bf_row = jnp.concatenate([bf, jnp.zeros((V7X_LANES - N_HEADS,), F32)]).reshape(1, V7X_LANES)
    (qs, ks_, vs, qrs_t, krs_t, vrs, sgs, lfs) = _even_in_sample_call(xs, rows_s(0), (3, 4), w_even_t, bf_row,
                                                                       cos_s, sin_s)
    n_phys, page_len = cache_k.shape[1], cache_k.shape[2]
    kt_pages = jnp.transpose(cache_k[0], (0, 2, 3, 1)).reshape(n_phys, w, page_len)
    vt_pages = jnp.transpose(cache_v[0], (0, 2, 3, 1)).reshape(n_phys, w, page_len)
    lft_pages = jnp.transpose(cache_logf[0], (0, 2, 1))
    per_host = nseq // n_hosts
    dec_args = (page_table, qs.reshape(nseq, 1, w), ks_.reshape(nseq, 1, w), vs.reshape(nseq, 1, w),
                lfs.reshape(nseq, 1, V7X_LANES), kt_pages, vt_pages, lft_pages)
    dec_out = []

    def ffn_host(x, half, l, mixer_out=None):
        x, o = ffn(x, rows_p(l), half, l, dec_args + (len(dec_out) * per_host, per_host), mixer_out)
        dec_out.append(o)
        return x

    xp = ffn_host(xp, 0, 0)
    (q, qr, vr, sg, kt, vt, ktc, vtc, krt, lft) = _even_in_call(
        xp, rows_p(0), (3, 4), w_even_t, bf.reshape(N_HEADS, 1), rot_p, batch, seq)
    c_t = _cumsum_call(lft)
    of = _fox_call(q, ktc, vtc, c_t.reshape(batch, N_HEADS, seq // tm, tm), batch, seq)
    yr, s_new = _ret_call(qr, krt, vr, sg, lg_lane, lg_row, batch, seq)
    outs_p["k"].append(jnp.transpose(kt.reshape(batch, N_HEADS, HEAD_DIM, seq), (0, 3, 1, 2)))
    outs_p["v"].append(jnp.transpose(vt.reshape(batch, N_HEADS, HEAD_DIM, seq), (0, 3, 1, 2)))
    outs_p["lf"].append(jnp.transpose(lft, (0, 2, 1)))
    outs_p["ret"].append(s_new)
    xp = ffn_host(xp, 1, 0, mixer_out=(5, of, yr, w_out_even_b, 1))
    xp = ffn_host(xp, 0, 1)
    xp, conv_p = _odd_prompt_call(xp, rows_p(1), (3, 4, 5), w_in_odd_b, conv_w[0], w_out_odd_b, ln_g3, ln_b3,
                                  4, alpha, batch, seq)
    outs_p["conv"].append(conv_p)
    xp = ffn_host(xp, 1, 1)

    of_s = jnp.concatenate(dec_out, axis=0).reshape(nseq, w)
    lfs = lfs[:, :N_HEADS]
    hdn = (N_HEADS, HEAD_DIM, nseq)
    rw = (nseq, N_HEADS, 1, HEAD_DIM)
    yr_s, s_new_s = _decode_ret_call(qrs_t.reshape(hdn), krs_t.reshape(hdn), vrs.reshape(rw), sgs.reshape(rw),
                                     state_ret[0], step_decay)
    xs = _outproj_call(xs, rows_s(0), 5, of_s, yr_s.reshape(nseq, w), w_out_even_b, ln_g3, ln_b3, 1, alpha)
    outs_s["k"].append(ks_.reshape(nseq, 1, N_HEADS, HEAD_DIM))
    outs_s["v"].append(vs.reshape(nseq, 1, N_HEADS, HEAD_DIM))
    outs_s["lf"].append(lfs.reshape(nseq, 1, N_HEADS))
    outs_s["ret"].append(s_new_s)
    xs = ffn(xs, rows_s(0), 1, 0)
    xs = ffn(xs, rows_s(1), 0, 1)
    buf = state_conv[0]
    xs, u_s = _odd_sample_call(xs, rows_s(1), (3, 4, 5), w_in_odd_b, conv_w[0], buf[:, 0], buf[:, 1], w_out_odd_b,
                               ln_g3, ln_b3, 4, alpha)
    outs_s["conv"].append(jnp.stack([buf[:, 1], u_s], axis=1))
    xs = ffn(xs, rows_s(1), 1, 1)

    def stk(lst):
        return jnp.stack(lst)

    return (xp.reshape(batch, seq, d), xs.reshape(nseq, 1, d),
            stk(outs_p["k"]), stk(outs_p["v"]), stk(outs_p["lf"]), stk(outs_p["ret"]), stk(outs_p["conv"]),
            stk(outs_s["k"]), stk(outs_s["v"]), stk(outs_s["lf"]), stk(outs_s["ret"]), stk(outs_s["conv"]))
```

```python
import functools

import jax
import jax.numpy as jnp
import numpy as np
from jax import lax
from jax.experimental import pallas as pl
from jax.experimental.pallas import tpu as pltpu

F32 = jnp.float32
BF16 = jnp.bfloat16

HEAD_DIM = 64
N_HEADS = 8
HEAD_WIDTH = N_HEADS * HEAD_DIM
N_MOD = 9
CONV_WIDTH = 3
RET_ANGLE_BASE = 10000.0
LN_EPS = 1e-5
GN_EPS = 1e-6
QK_SCALE = HEAD_DIM ** -0.5
LOG2E = 1.4426950408889634

V7X_LANES = 128
V7X_VMEM_LIMIT_BYTES = 56 * 1024 * 1024

ROW_BLOCK = 512
FF_CHUNK = 256
FOX_BLOCKS_PER_UPDATE = 4
RET_CHUNK = 256
ADA_COL_BLOCK = 1152
HEAD_PAIRS = N_HEADS // 2


def _cparams(sem, big=False):
    return pltpu.CompilerParams(dimension_semantics=sem,
                                vmem_limit_bytes=V7X_VMEM_LIMIT_BYTES if big else None)


def _const_spec(arr):
    nd = arr.ndim
    return pl.BlockSpec(arr.shape, lambda *_: (0,) * nd, pipeline_mode=pl.Buffered(1))


def _layernorm(z, g, b):
    mu = jnp.mean(z, axis=-1, keepdims=True)
    d = z - mu
    var = jnp.mean(d * d, axis=-1, keepdims=True)
    return d * lax.rsqrt(var + LN_EPS) * g + b


def _silu(a):
    return a * jax.nn.sigmoid(a)


def _log_sigmoid(z):
    return jnp.minimum(z, 0.0) - jnp.log1p(jnp.exp(-jnp.abs(z)))


def _dot(a, b):
    return jnp.dot(a, b, preferred_element_type=F32)


def _dot_nt(a, b):
    return lax.dot_general(a, b, (((1,), (1,)), ((), ())), preferred_element_type=F32)


def _ada_kernel(c_ref, w_ref, b_ref, o_ref):
    s = _silu(c_ref[...]).astype(BF16)
    o_ref[...] = _dot(s, w_ref[...].astype(BF16)) + b_ref[...]


def _ada_call(c_all, w_ada, b_ada):
    depth, d, nm = w_ada.shape
    rows = c_all.shape[0]
    tn = ADA_COL_BLOCK
    return pl.pallas_call(
        _ada_kernel,
        grid=(depth, nm // tn),
        in_specs=[pl.BlockSpec((rows, d), lambda l, j: (0, 0)),
                  pl.BlockSpec((None, d, tn), lambda l, j: (l, 0, j)),
                  pl.BlockSpec((None, 1, tn), lambda l, j: (l, 0, j))],
        out_specs=pl.BlockSpec((None, rows, tn), lambda l, j: (l, 0, j)),
        out_shape=jax.ShapeDtypeStruct((depth, rows, nm), F32),
        compiler_params=_cparams(("arbitrary", "arbitrary"), big=True),
        name="ada_mods",
    )(c_all, w_ada, b_ada.reshape(depth, 1, nm))


class _Rows:
    def __init__(self, mods, layer, tm, blocks_per_seq, d):
        self.mods, self.layer, self.tm, self.bps, self.d = mods, layer, tm, blocks_per_seq, d

    def mod_spec(self, k):
        l, d = self.layer, self.d
        if self.bps is None:
            return pl.BlockSpec((None, self.tm, d), lambda i, *_: (l, 0, k))
        bps = self.bps
        return pl.BlockSpec((None, None, 1, d), lambda i, *_: (l, i // bps, 0, k))

    def row_spec(self, width):
        return pl.BlockSpec((self.tm, width), lambda i, *_: (i, 0))


def _mixer_out_math(x, gate, a0, a1, w_ref, lng, lnb, alpha):
    half = a0.shape[-1]
    y = _dot(a0.astype(BF16), w_ref[0:half, :]) + _dot(a1.astype(BF16), w_ref[half:, :])
    return _layernorm(alpha * x + gate * y, lng, lnb)


def _ffn_rows(refs, ff, alpha, pre):
    x = refs[0][...]
    refs = refs[1:]
    if pre:
        gate, a0, a1, wo_ref, lng1, lnb1 = refs[:6]
        x = _mixer_out_math(x, gate[...], a0[...], a1[...], wo_ref, lng1[...], lnb1[...], alpha)
        refs = refs[6:]
    sh_ref, sc_ref, g_ref, win_ref, wout_ref, lng_ref, lnb_ref = refs
    h = (x * (1.0 + sc_ref[...]) + sh_ref[...]).astype(BF16)
    acc = None
    for c in range(ff // FF_CHUNK):
        lo, hi = c * FF_CHUNK, (c + 1) * FF_CHUNK
        a = _dot(h, win_ref[:, lo:hi])
        b = _dot(h, win_ref[:, ff + lo:ff + hi])
        y = _dot((_silu(a) * b).astype(BF16), wout_ref[lo:hi, :])
        acc = y if acc is None else acc + y
    z = alpha * x + (0.5 * g_ref[...]) * acc
    return _layernorm(z, lng_ref[...], lnb_ref[...])


def _ffn_kernel(*refs, ff, alpha, pre):
    refs[-1][...] = _ffn_rows(refs[:-1], ff, alpha, pre)


def _ffn_decode_kernel(pt_ref, *refs, ff, alpha, pre, pages, steps_per_seq):
    del pt_ref
    n_ffn = 14 if pre else 8
    ffn_in, rest = refs[:n_ffn], refs[n_ffn:]
    q_ref, kn_ref, vn_ref, lfn_ref = rest[:4]
    k_refs, v_refs, lf_refs = rest[4:4 + pages], rest[4 + pages:4 + 2 * pages], rest[4 + 2 * pages:4 + 3 * pages]
    o_ref, dec_ref = rest[4 + 3 * pages:6 + 3 * pages]
    scratch = rest[6 + 3 * pages:]
    g = pl.program_id(0) % steps_per_seq
    pl.when(g == 0)(lambda: _decode_fox_init(q_ref, kn_ref, vn_ref, lfn_ref, *scratch))
    _decode_fox_pages(k_refs, v_refs, lf_refs, *scratch)
    o_ref[...] = _ffn_rows(ffn_in, ff, alpha, pre)
    pl.when(g == steps_per_seq - 1)(lambda: _decode_fox_finish(dec_ref, scratch[2], scratch[4]))


def _ffn_call(x, rows, ks, w_in, w_out, which, ln_g, ln_b, ln_idx, alpha, decode=None, mixer_out=None):
    n, d = x.shape
    ff = w_out.shape[2]
    l, j = which
    steps = n // rows.tm

    def w_spec(arr):
        return pl.BlockSpec((None, None) + arr.shape[2:], lambda i, *_: (l, j, 0, 0), pipeline_mode=pl.Buffered(1))

    def ln_spec(idx):
        return pl.BlockSpec((None, 1, d), lambda i, *_: (idx, 0, 0))

    in_specs = [rows.row_spec(d)]
    args = [x]
    if mixer_out is not None:
        k_gate, a0, a1, w_o, ln_idx1 = mixer_out
        in_specs += [rows.mod_spec(k_gate), rows.row_spec(a0.shape[1]), rows.row_spec(a1.shape[1]), _const_spec(w_o),
                     ln_spec(ln_idx1), ln_spec(ln_idx1)]
        args += [rows.mods, a0, a1, w_o, ln_g, ln_b]
    in_specs += [rows.mod_spec(ks[0]), rows.mod_spec(ks[1]), rows.mod_spec(ks[2]), w_spec(w_in), w_spec(w_out),
                 ln_spec(ln_idx), ln_spec(ln_idx)]
    args += [rows.mods, rows.mods, rows.mods, w_in, w_out, ln_g, ln_b]
    pre = mixer_out is not None
    x_shape = jax.ShapeDtypeStruct((n, d), F32)
    if decode is None:
        return pl.pallas_call(
            functools.partial(_ffn_kernel, ff=ff, alpha=alpha, pre=pre),
            grid=(steps,), in_specs=in_specs, out_specs=rows.row_spec(d), out_shape=x_shape,
            compiler_params=_cparams(("parallel",), big=True),
            name="ffn",
        )(*args)

    page_table, q, kn, vn, lfn, kt_pages, vt_pages, lft_pages, seq0, n_seq = decode
    npg = page_table.shape[1]
    w, page_len = kt_pages.shape[1], kt_pages.shape[2]
    steps_per_seq = steps // n_seq
    pages = npg // steps_per_seq
    assert steps_per_seq * n_seq == steps and pages * steps_per_seq == npg

    def page_spec(r, nrows):
        return pl.BlockSpec((None, nrows, page_len),
                            lambda i, pt: (pt[seq0 + i // steps_per_seq,
                                              npg - 1 - ((i % steps_per_seq) * pages + r)], 0, 0))

    row = pl.BlockSpec((None, 1, w), lambda i, pt: (seq0 + i // steps_per_seq, 0, 0))
    grid_spec = pltpu.PrefetchScalarGridSpec(
        num_scalar_prefetch=1,
        grid=(steps,),
        in_specs=in_specs
        + [row, row, row, pl.BlockSpec((None, 1, V7X_LANES), lambda i, pt: (seq0 + i // steps_per_seq, 0, 0))]
        + [page_spec(r, w) for r in range(pages)]
        + [page_spec(r, w) for r in range(pages)]
        + [page_spec(r, N_HEADS) for r in range(pages)],
        out_specs=[rows.row_spec(d), pl.BlockSpec((None, 1, w), lambda i, pt: (i // steps_per_seq, 0, 0))],
        scratch_shapes=[pltpu.VMEM((N_HEADS, w), F32), pltpu.VMEM((N_HEADS, 1), F32), pltpu.VMEM((N_HEADS, 1), F32),
                        pltpu.VMEM((N_HEADS, 1), F32), pltpu.VMEM((w, page_len), F32)],
    )
    return pl.pallas_call(
        functools.partial(_ffn_decode_kernel, ff=ff, alpha=alpha, pre=pre, pages=pages, steps_per_seq=steps_per_seq),
        grid_spec=grid_spec,
        out_shape=[x_shape, jax.ShapeDtypeStruct((n_seq, 1, w), F32)],
        compiler_params=_cparams(("arbitrary",), big=True),
        name="ffn_decode",
    )(page_table, *args, q, kn, vn, lfn,
      *([kt_pages] * pages), *([vt_pages] * pages), *([lft_pages] * pages))


def _outproj_kernel(x_ref, g_ref, a0_ref, a1_ref, w_ref, lng_ref, lnb_ref, o_ref, *, alpha):
    o_ref[...] = _mixer_out_math(x_ref[...], g_ref[...], a0_ref[...], a1_ref[...], w_ref, lng_ref[...], lnb_ref[...],
                                 alpha)


def _outproj_call(x, rows, k_gate, a0, a1, w_out, ln_g, ln_b, ln_idx, alpha):
    n, d = x.shape
    return pl.pallas_call(
        functools.partial(_outproj_kernel, alpha=alpha),
        grid=(n // rows.tm,),
        in_specs=[rows.row_spec(d), rows.mod_spec(k_gate), rows.row_spec(a0.shape[1]), rows.row_spec(a1.shape[1]),
                  _const_spec(w_out),
                  pl.BlockSpec((None, 1, d), lambda i: (ln_idx, 0, 0)),
                  pl.BlockSpec((None, 1, d), lambda i: (ln_idx, 0, 0))],
        out_specs=rows.row_spec(d),
        out_shape=jax.ShapeDtypeStruct((n, d), F32),
        compiler_params=_cparams(("parallel",), big=True),
        name="mixer_outproj",
    )(x, rows.mods, a0, a1, w_out, ln_g, ln_b)


def _rotate_token_major(a, cos, sin_signed):
    width = a.shape[-1]
    lane = lax.broadcasted_iota(jnp.int32, a.shape, 1)
    first_half = (lane & (HEAD_DIM // 2)) == 0
    partner = jnp.where(first_half, pltpu.roll(a, width - HEAD_DIM // 2, 1), pltpu.roll(a, HEAD_DIM // 2, 1))
    return a * cos + partner * sin_signed


_EVEN_ROWS = {}
_r0 = 0
for _name, _n in (("qf", HEAD_WIDTH), ("kf", HEAD_WIDTH), ("vf", HEAD_WIDTH), ("fg", N_HEADS),
                  ("qr", HEAD_WIDTH), ("kr", HEAD_WIDTH), ("vr", HEAD_WIDTH), ("gr", HEAD_WIDTH)):
    _EVEN_ROWS[_name] = (_r0, _r0 + _n)
    _r0 += _n
EVEN_IN_ROWS = _r0
FG_ROWS = 16


def _even_in_kernel(x_ref, sh_ref, sc_ref, w_ref, bf_ref, cos_ref, sin_ref, cost_ref, sint_ref,
                    q_o, qr_o, vr_o, sg_o, kt_o, vt_o, ktc_o, vtc_o, krt_o, lft_o, wtok_ref, wt_ref):
    w = HEAD_WIDTH

    @pl.when(pl.program_id(0) == 0)
    def _():
        for k, name in enumerate(("qf", "qr", "vr", "gr")):
            lo, hi = _EVEN_ROWS[name]
            wtok_ref[:, k * w:(k + 1) * w] = w_ref[lo:hi, :].T.astype(BF16)
        for k, name in enumerate(("kf", "vf", "kr")):
            lo, hi = _EVEN_ROWS[name]
            wt_ref[k * w:(k + 1) * w, :] = w_ref[lo:hi, :].astype(BF16)
        lo, hi = _EVEN_ROWS["fg"]
        fg_rows = jnp.concatenate([w_ref[lo:hi, :], jnp.zeros((FG_ROWS - N_HEADS, w_ref.shape[1]), F32)], axis=0)
        wt_ref[3 * w:, :] = fg_rows.astype(BF16)

    x = x_ref[...]
    h = (x * (1.0 + sc_ref[...]) + sh_ref[...]).astype(BF16)

    q_o[...] = (_dot(h, wtok_ref[:, 0:w]) * (QK_SCALE * LOG2E)).astype(BF16)
    qr = _dot(h, wtok_ref[:, w:2 * w])
    qr_o[...] = _rotate_token_major(qr, cos_ref[...], sin_ref[...]).astype(BF16)
    vr_o[...] = _dot(h, wtok_ref[:, 2 * w:3 * w]).astype(BF16)
    sg_o[...] = _silu(_dot(h, wtok_ref[:, 3 * w:4 * w])).astype(BF16)

    kt = _dot_nt(wt_ref[0:w, :], h)
    kt_o[...] = kt
    ktc_o[...] = kt.astype(BF16)
    vt = _dot_nt(wt_ref[w:2 * w, :], h)
    vt_o[...] = vt
    vtc_o[...] = vt.astype(BF16)

    krt = _dot_nt(wt_ref[2 * w:3 * w, :], h)
    cos_t, sin_t = cost_ref[...], sint_ref[...]
    half = HEAD_DIM // 2
    for hh in range(N_HEADS):
        x1 = krt[hh * HEAD_DIM:hh * HEAD_DIM + half, :]
        x2 = krt[hh * HEAD_DIM + half:(hh + 1) * HEAD_DIM, :]
        krt_o[hh * HEAD_DIM:hh * HEAD_DIM + half, :] = (x1 * cos_t - x2 * sin_t) * QK_SCALE
        krt_o[hh * HEAD_DIM + half:(hh + 1) * HEAD_DIM, :] = (x1 * sin_t + x2 * cos_t) * QK_SCALE

    fg = _dot_nt(wt_ref[3 * w:, :], h)[0:N_HEADS, :] + bf_ref[...]
    lft_o[...] = _log_sigmoid(fg)


def _even_in_call(x, rows, ks, w_t_f32, bf_col, rot, batch, seq):
    n, d = x.shape
    tm = rows.tm
    bps = seq // tm
    w = HEAD_WIDTH
    cos_tok, sin_tok, cos_t, sin_t = rot
    tok_out = pl.BlockSpec((tm, w), lambda i: (i, 0))
    t_out = pl.BlockSpec((None, w, tm), lambda i: (i // bps, 0, i % bps))
    tc_out = pl.BlockSpec((None, None, w, tm), lambda i: (i // bps, i % bps, 0, 0))
    tok_shape = jax.ShapeDtypeStruct((n, w), BF16)
    return pl.pallas_call(
        _even_in_kernel,
        grid=(n // tm,),
        in_specs=[rows.row_spec(d), rows.mod_spec(ks[0]), rows.mod_spec(ks[1]),
                  _const_spec(w_t_f32), _const_spec(bf_col),
                  pl.BlockSpec((tm, w), lambda i: (i % bps, 0)),
                  pl.BlockSpec((tm, w), lambda i: (i % bps, 0)),
                  pl.BlockSpec((HEAD_DIM // 2, tm), lambda i: (0, i % bps)),
                  pl.BlockSpec((HEAD_DIM // 2, tm), lambda i: (0, i % bps))],
        out_specs=[tok_out, tok_out, tok_out, tok_out, t_out, t_out, tc_out, tc_out, tc_out,
                   pl.BlockSpec((None, N_HEADS, tm), lambda i: (i // bps, 0, i % bps))],
        out_shape=[tok_shape, tok_shape, tok_shape, tok_shape,
                   jax.ShapeDtypeStruct((batch, w, seq), F32),
                   jax.ShapeDtypeStruct((batch, w, seq), F32),
                   jax.ShapeDtypeStruct((batch, bps, w, tm), BF16),
                   jax.ShapeDtypeStruct((batch, bps, w, tm), BF16),
                   jax.ShapeDtypeStruct((batch, bps, w, tm), F32),
                   jax.ShapeDtypeStruct((batch, N_HEADS, seq), F32)],
        scratch_shapes=[pltpu.VMEM((d, 4 * w), BF16), pltpu.VMEM((3 * w + FG_ROWS, d), BF16)],
        compiler_params=_cparams(("arbitrary",), big=True),
        name="even_inproj",
    )(x, rows.mods, rows.mods, w_t_f32, bf_col, cos_tok, sin_tok, cos_t, sin_t)


def _cumsum_kernel(x_ref, o_ref):
    x = x_ref[...]
    n = x.shape[-1]
    lane = lax.broadcasted_iota(jnp.int32, x.shape, 1)
    shift = 1
    while shift < n:
        x = x + jnp.where(lane >= shift, pltpu.roll(x, shift, 1), 0.0)
        shift *= 2
    o_ref[...] = x * LOG2E


def _cumsum_call(lft):
    batch, heads, seq = lft.shape
    spec = pl.BlockSpec((None, heads, seq), lambda b: (b, 0, 0))
    return pl.pallas_call(
        _cumsum_kernel, grid=(batch,), in_specs=[spec], out_specs=spec,
        out_shape=jax.ShapeDtypeStruct(lft.shape, F32),
        compiler_params=_cparams(("parallel",)),
        name="logf_cumsum",
    )(lft)


def _split3(x):
    hi = x.astype(BF16).astype(F32)
    r = x - hi
    mid = r.astype(BF16).astype(F32)
    lo = (r - mid).astype(BF16).astype(F32)
    return hi, mid, lo


def _fox_kernel(q_ref, kt_ref, vt_ref, c_ref, o_ref):
    tq = q_ref.shape[0]
    tk = kt_ref.shape[-1]
    pw = 2 * HEAD_DIM
    slab_rows = 16
    i = pl.program_id(2)
    q = q_ref[...]
    lane = lax.broadcasted_iota(jnp.int32, (tq, pw), 1)
    q_aug = []
    for hd in range(2):
        base = HEAD_DIM if hd == 0 else 0
        hi, mid, lo = _split3(c_ref[hd, pl.ds(i, 1), :][:, 0:1])
        aug = jnp.where(lane == base + 3, hi, jnp.where(lane == base + 4, mid, jnp.where(lane == base + 5, lo,
              jnp.where((lane >= base) & (lane < base + 3), 1.0, 0.0))))
        own = (lane < HEAD_DIM) if hd == 0 else (lane >= HEAD_DIM)
        q_aug.append(jnp.where(own, q, aug.astype(BF16)))
    r16 = lax.broadcasted_iota(jnp.int32, (slab_rows, tk), 0)
    rowv = lax.broadcasted_iota(jnp.int32, (pw, tk), 0)

    def scores(j, hd):
        kt = kt_ref[j]
        hi, mid, lo = _split3(-c_ref[hd, pl.ds(j, 1), :])
        slab = jnp.where(r16 == 0, hi, jnp.where(r16 == 1, mid, jnp.where(r16 == 2, lo,
               jnp.where(r16 < 6, 1.0, 0.0)))).astype(BF16)
        if hd == 0:
            kt_aug = jnp.concatenate([kt[0:HEAD_DIM], slab, kt[HEAD_DIM + slab_rows:]], axis=0)
        else:
            kt_aug = jnp.concatenate([slab, kt[slab_rows:]], axis=0)
        return _dot(q_aug[hd], kt_aug)

    def values(j, hd):
        vt = vt_ref[j]
        return jnp.where(rowv == (HEAD_DIM if hd == 0 else 0), jnp.ones_like(vt), vt)

    def update(s, vt_aug, m, acc):
        m_new = jnp.maximum(m, jnp.max(s, axis=-1, keepdims=True))
        return m_new, jnp.exp2(m - m_new) * acc + _dot_nt(jnp.exp2(s - m_new).astype(BF16), vt_aug)

    def diag_scores(hd):
        row = lax.broadcasted_iota(jnp.int32, (tq, tk), 0)
        col = lax.broadcasted_iota(jnp.int32, (tq, tk), 1)
        return jnp.where(row >= col, scores(i, hd), -jnp.inf)

    group = FOX_BLOCKS_PER_UPDATE

    def joint(carry, full_blocks, with_diag):
        out = []
        for hd in range(2):
            s = [scores(j, hd) for j in full_blocks] + ([diag_scores(hd)] if with_diag else [])
            v = [values(j, hd) for j in full_blocks] + ([values(i, hd)] if with_diag else [])
            out.append(update(jnp.concatenate(s, axis=1), jnp.concatenate(v, axis=1), *carry[hd]))
        return tuple(out)

    def tail(n_full):
        return lambda carry: tuple(c[1] for c in joint(carry, [i - n_full + k for k in range(n_full)], True))

    init_head = (jnp.full((tq, 1), -jnp.inf, F32), jnp.zeros((tq, pw), F32))
    carry = lax.fori_loop(0, lax.shift_right_logical(i, group.bit_length() - 1),
                          lambda t, c: joint(c, [group * t + k for k in range(group)], False),
                          (init_head, init_head))
    outs = lax.switch(i & (group - 1), [tail(n) for n in range(group)], carry)
    o0 = outs[0] / outs[0][:, HEAD_DIM:HEAD_DIM + 1]
    o1 = outs[1] / outs[1][:, 0:1]
    o_ref[...] = jnp.where(lane < HEAD_DIM, o0, o1).astype(BF16)


def _fox_call(q, ktc, vtc, c4, batch, seq):
    n, w = q.shape
    tq = ktc.shape[-1]
    nq = seq // tq
    pw = 2 * HEAD_DIM
    return pl.pallas_call(
        _fox_kernel,
        grid=(batch, HEAD_PAIRS, nq),
        in_specs=[pl.BlockSpec((tq, pw), lambda b, p, i: (b * nq + i, p)),
                  pl.BlockSpec((None, nq, pw, tq), lambda b, p, i: (b, 0, p, 0)),
                  pl.BlockSpec((None, nq, pw, tq), lambda b, p, i: (b, 0, p, 0)),
                  pl.BlockSpec((None, 2, nq, tq), lambda b, p, i: (b, p, 0, 0))],
        out_specs=pl.BlockSpec((tq, pw), lambda b, p, i: (b * nq + i, p)),
        out_shape=jax.ShapeDtypeStruct((n, w), BF16),
        compiler_params=_cparams(("parallel", "parallel", "arbitrary"), big=True),
        name="fox_prompt",
    )(q, ktc, vtc, c4)


def _ret_kernel(q_ref, kt_ref, v_ref, sg_ref, lgl_ref, lgr_ref, y_ref, s_ref):
    chunk = RET_CHUNK
    pw = 2 * HEAD_DIM
    nkb, _, tb = kt_ref.shape
    lg_lane = lgl_ref[...]
    lg_row = lgr_ref[...]
    lg_a, lg_b = lg_lane[:, 0:1], lg_lane[:, HEAD_DIM:HEAD_DIM + 1]

    ri = lax.broadcasted_iota(jnp.int32, (2 * chunk, chunk), 0)
    cj = lax.broadcasted_iota(jnp.int32, (2 * chunk, chunk), 1)
    first = ri < chunk
    diff = (jnp.where(first, ri, ri - chunk) - cj).astype(F32)
    decay_mask = jnp.where(diff >= 0, jnp.exp(jnp.maximum(diff, 0.0) * jnp.where(first, lg_a, lg_b)), 0.0)
    jj = lax.broadcasted_iota(jnp.int32, (pw, chunk), 1).astype(F32)
    col_decay = jnp.exp((chunk - 1.0 - jj) * lg_row)
    ii = lax.broadcasted_iota(jnp.int32, (chunk, pw), 0).astype(F32)
    row_decay = jnp.exp((ii + 1.0) * lg_lane)
    chunk_decay = jnp.exp(float(chunk) * lg_row)
    r2 = lax.broadcasted_iota(jnp.int32, (pw, pw), 0)
    c2 = lax.broadcasted_iota(jnp.int32, (pw, pw), 1)
    same_head = (r2 < HEAD_DIM) == (c2 < HEAD_DIM)
    seg_avg = jnp.where(same_head, 1.0 / HEAD_DIM, 0.0).astype(BF16)
    lane = lax.broadcasted_iota(jnp.int32, (chunk, pw), 1)

    def seg_mean(a):
        hi = a.astype(BF16)
        lo = (a - hi.astype(F32)).astype(BF16)
        return _dot(hi, seg_avg) + _dot(lo, seg_avg)

    def one_chunk(q, kt, v, sg, state):
        zero = jnp.zeros_like(q)
        q_stack = jnp.concatenate([jnp.where(lane < HEAD_DIM, q, zero), jnp.where(lane >= HEAD_DIM, q, zero)], axis=0)
        qk = _dot(q_stack, kt.astype(BF16)) * decay_mask
        kd = (kt * col_decay).astype(BF16)
        r = _dot(jnp.concatenate([qk.astype(BF16), kd], axis=0), v)
        inner = jnp.where(lane < HEAD_DIM, r[0:chunk], r[chunk:2 * chunk])
        update = jnp.where(same_head, r[2 * chunk:], 0.0)
        cross = _dot(q, state.astype(BF16)) * row_decay
        o = inner + cross
        mu = seg_mean(o)
        d = o - mu
        var = seg_mean(d * d)
        y = sg.astype(F32) * (d * lax.rsqrt(var + GN_EPS))
        return y.astype(BF16), chunk_decay * state + update

    def body(jb, state):
        kt_blk = kt_ref[jb]
        for sub in range(tb // chunk):
            t0 = pl.multiple_of(jb * tb + sub * chunk, chunk)
            y, state = one_chunk(q_ref[pl.ds(t0, chunk), :], kt_blk[:, sub * chunk:(sub + 1) * chunk],
                                 v_ref[pl.ds(t0, chunk), :], sg_ref[pl.ds(t0, chunk), :], state)
            y_ref[pl.ds(t0, chunk), :] = y
        return state

    state = lax.fori_loop(0, nkb, body, jnp.zeros((pw, pw), F32), unroll=4)
    s_ref[0] = state[0:HEAD_DIM, 0:HEAD_DIM]
    s_ref[1] = pltpu.roll(state, HEAD_DIM, 1)[HEAD_DIM:, 0:HEAD_DIM]


def _ret_call(qr, krt, vr, sg, lg_lane, lg_row, batch, seq):
    n, w = qr.shape
    nkb, tb = krt.shape[1], krt.shape[3]
    pw = 2 * HEAD_DIM
    seq_spec = pl.BlockSpec((seq, pw), lambda b, p: (b, p))
    return pl.pallas_call(
        _ret_kernel,
        grid=(batch, HEAD_PAIRS),
        in_specs=[seq_spec,
                  pl.BlockSpec((None, nkb, pw, tb), lambda b, p: (b, 0, p, 0)),
                  seq_spec, seq_spec,
                  pl.BlockSpec((None, 1, pw), lambda b, p: (p, 0, 0)),
                  pl.BlockSpec((None, pw, 1), lambda b, p: (p, 0, 0))],
        out_specs=[seq_spec, pl.BlockSpec((None, 2, HEAD_DIM, HEAD_DIM), lambda b, p: (b, p, 0, 0))],
        out_shape=[jax.ShapeDtypeStruct((n, w), BF16),
                   jax.ShapeDtypeStruct((batch, N_HEADS, HEAD_DIM, HEAD_DIM), F32)],
        compiler_params=_cparams(("parallel", "parallel"), big=True),
        name="ret_prompt",
    )(qr, krt, vr, sg, lg_lane, lg_row)


def _odd_prompt_kernel(x_ref, sh_ref, sc_ref, g_ref, win_ref, cw_ref, wout_ref, lng_ref, lnb_ref,
                       o_ref, st_ref, carry_ref, *, bps, alpha):
    i = pl.program_id(0)
    x = x_ref[...]
    tm, d = x.shape
    h = (x * (1.0 + sc_ref[...]) + sh_ref[...]).astype(BF16)
    b_gate = _dot(h, win_ref[:, 0:d])
    u = _dot(h, win_ref[:, d:2 * d]) * _dot(h, win_ref[:, 2 * d:3 * d])

    @pl.when(i % bps == 0)
    def _():
        carry_ref[...] = jnp.zeros_like(carry_ref)

    prev = carry_ref[...]
    p1, p2 = prev[7:8, :], prev[6:7, :]
    row = lax.broadcasted_iota(jnp.int32, (tm, d), 0)
    u1 = jnp.where(row == 0, p1, pltpu.roll(u, 1, 0))
    u2 = jnp.where(row == 0, p2, jnp.where(row == 1, p1, pltpu.roll(u, 2, 0)))
    cw = cw_ref[...]
    z = cw[0:1, :] * u2 + cw[1:2, :] * u1 + cw[2:3, :] * u
    carry_ref[...] = u[tm - 8:, :]
    st_ref[...] = u[tm - (CONV_WIDTH - 1):, :]
    y = _dot((b_gate * z).astype(BF16), wout_ref[...])
    zz = alpha * x + g_ref[...] * y
    o_ref[...] = _layernorm(zz, lng_ref[...], lnb_ref[...])


def _odd_prompt_call(x, rows, ks, w_in, conv_w, w_out, ln_g, ln_b, ln_idx, alpha, batch, seq):
    n, d = x.shape
    bps = seq // rows.tm
    return pl.pallas_call(
        functools.partial(_odd_prompt_kernel, bps=bps, alpha=alpha),
        grid=(n // rows.tm,),
        in_specs=[rows.row_spec(d), rows.mod_spec(ks[0]), rows.mod_spec(ks[1]), rows.mod_spec(ks[2]),
                  _const_spec(w_in), _const_spec(conv_w), _const_spec(w_out),
                  pl.BlockSpec((None, 1, d), lambda i: (ln_idx, 0, 0)),
                  pl.BlockSpec((None, 1, d), lambda i: (ln_idx, 0, 0))],
        out_specs=[rows.row_spec(d), pl.BlockSpec((None, CONV_WIDTH - 1, d), lambda i: (i // bps, 0, 0))],
        out_shape=[jax.ShapeDtypeStruct((n, d), F32), jax.ShapeDtypeStruct((batch, CONV_WIDTH - 1, d), F32)],
        scratch_shapes=[pltpu.VMEM((8, d), F32)],
        compiler_params=_cparams(("arbitrary",), big=True),
        name="odd_prompt",
    )(x, rows.mods, rows.mods, rows.mods, w_in, conv_w, w_out, ln_g, ln_b)


def _even_in_sample_kernel(x_ref, sh_ref, sc_ref, w_ref, bf_ref, cost_ref, sint_ref,
                           q_o, k_o, v_o, qrt_o, krt_o, vr_o, sg_o, lf_o):
    d = w_ref.shape[1]
    h = (x_ref[...] * (1.0 + sc_ref[...]) + sh_ref[...]).astype(BF16)

    def rows(name):
        lo, hi = _EVEN_ROWS[name]
        return w_ref[lo:hi, :]

    def tok(name):
        return _dot_nt(h, rows(name).astype(BF16))

    q_o[...] = tok("qf") * QK_SCALE
    k_o[...] = tok("kf")
    v_o[...] = tok("vf")
    vr_o[...] = tok("vr")
    sg_o[...] = _silu(tok("gr"))
    fg_rows = jnp.concatenate([rows("fg"), jnp.zeros((V7X_LANES - N_HEADS, d), F32)], axis=0).astype(BF16)
    lf_o[...] = _log_sigmoid(_dot_nt(h, fg_rows) + bf_ref[...])

    cos_t, sin_t = cost_ref[...], sint_ref[...]
    half = HEAD_DIM // 2
    for name, out, scale in (("qr", qrt_o, 1.0), ("kr", krt_o, QK_SCALE)):
        t = _dot_nt(rows(name).astype(BF16), h)
        for hh in range(N_HEADS):
            x1 = t[hh * HEAD_DIM:hh * HEAD_DIM + half, :]
            x2 = t[hh * HEAD_DIM + half:(hh + 1) * HEAD_DIM, :]
            out[hh * HEAD_DIM:hh * HEAD_DIM + half, :] = (x1 * cos_t - x2 * sin_t) * scale
            out[hh * HEAD_DIM + half:(hh + 1) * HEAD_DIM, :] = (x1 * sin_t + x2 * cos_t) * scale


def _even_in_sample_call(x, rows, ks, w_t_f32, bf_row, cos_t, sin_t):
    n, d = x.shape
    w = HEAD_WIDTH
    full = pl.BlockSpec((n, w), lambda i: (0, 0))
    full_t = pl.BlockSpec((w, n), lambda i: (0, 0))
    rot = pl.BlockSpec((HEAD_DIM // 2, n), lambda i: (0, 0))
    shp = jax.ShapeDtypeStruct((n, w), F32)
    shp_t = jax.ShapeDtypeStruct((w, n), F32)
    return pl.pallas_call(
        _even_in_sample_kernel,
        grid=(1,),
        in_specs=[rows.row_spec(d), rows.mod_spec(ks[0]), rows.mod_spec(ks[1]),
                  _const_spec(w_t_f32), _const_spec(bf_row), rot, rot],
        out_specs=[full, full, full, full_t, full_t, full, full, pl.BlockSpec((n, V7X_LANES), lambda i: (0, 0))],
        out_shape=[shp, shp, shp, shp_t, shp_t, shp, shp, jax.ShapeDtypeStruct((n, V7X_LANES), F32)],
        compiler_params=_cparams(("arbitrary",), big=True),
        name="even_inproj_sample",
    )(x, rows.mods, rows.mods, w_t_f32, bf_row, cos_t, sin_t)


def _own_head_lanes():
    sub = lax.broadcasted_iota(jnp.int32, (N_HEADS, HEAD_WIDTH), 0)
    lane = lax.broadcasted_iota(jnp.int32, (N_HEADS, HEAD_WIDTH), 1)
    return (lane >= sub * HEAD_DIM) & (lane < (sub + 1) * HEAD_DIM)


def _decode_fox_init(q_ref, kn_ref, vn_ref, lfn_ref, qbd_s, m_s, l_s, run_s, acc_s):
    w = HEAD_WIDTH
    q_bd = jnp.where(_own_head_lanes(), jnp.broadcast_to(q_ref[...], (N_HEADS, w)), 0.0)
    qbd_s[...] = q_bd
    m_s[...] = jnp.sum(q_bd * kn_ref[...], axis=-1, keepdims=True)
    l_s[...] = jnp.ones_like(l_s)
    page_len = acc_s.shape[1]
    lane_p = lax.broadcasted_iota(jnp.int32, (w, page_len), 1)
    acc_s[...] = jnp.where(lane_p == 0, jnp.broadcast_to(vn_ref[...], (page_len, w)).T, 0.0)
    s128 = lax.broadcasted_iota(jnp.int32, (N_HEADS, V7X_LANES), 0)
    l128 = lax.broadcasted_iota(jnp.int32, (N_HEADS, V7X_LANES), 1)
    lfn = jnp.broadcast_to(lfn_ref[...], (N_HEADS, V7X_LANES))
    run_s[...] = jnp.sum(jnp.where(s128 == l128, lfn, 0.0), axis=-1, keepdims=True)


def _decode_fox_finish(o_ref, l_s, acc_s):
    tot = jnp.sum(acc_s[...].T, axis=0, keepdims=True)
    l_b = jnp.broadcast_to(l_s[...], (N_HEADS, HEAD_WIDTH))
    o_ref[...] = tot / jnp.sum(jnp.where(_own_head_lanes(), l_b, 0.0), axis=0, keepdims=True)


def _decode_fox_pages(k_refs, v_refs, lf_refs, qbd_s, m_s, l_s, run_s, acc_s):
    pages = len(k_refs)
    page_len = k_refs[0].shape[-1]
    q_b = qbd_s[...].astype(BF16)
    order = list(range(pages - 1, -1, -1))
    s = jnp.concatenate([_dot(q_b, k_refs[r][...].astype(BF16)) for r in order], axis=1)
    lf = jnp.concatenate([lf_refs[r][...] for r in order], axis=1)
    n = pages * page_len
    lane_n = lax.broadcasted_iota(jnp.int32, (N_HEADS, n), 1)
    suf = lf
    shift = 1
    while shift < n:
        suf = suf + jnp.where(lane_n < n - shift, pltpu.roll(suf, n - shift, 1), 0.0)
        shift *= 2
    run = run_s[...]
    s = s + ((suf - lf) + run)
    m = m_s[...]
    m_new = jnp.maximum(m, jnp.max(s, axis=-1, keepdims=True))
    a = jnp.exp(m - m_new)
    p = jnp.exp(s - m_new)
    l_s[...] = a * l_s[...] + jnp.sum(p, axis=-1, keepdims=True)
    m_s[...] = m_new
    for h in range(N_HEADS):
        sl = slice(h * HEAD_DIM, (h + 1) * HEAD_DIM)
        upd = None
        for idx, r in enumerate(order):
            t = p[h:h + 1, idx * page_len:(idx + 1) * page_len] * v_refs[r][sl, :]
            upd = t if upd is None else upd + t
        acc_s[sl, :] = a[h:h + 1, :] * acc_s[sl, :] + upd
    run_s[...] = run + suf[:, 0:1]


def _decode_ret_kernel(qt_ref, kt_ref, v_ref, sg_ref, s_ref, gl_ref, y_ref, so_ref):
    qt, kt = qt_ref[...], kt_ref[...]
    g = gl_ref[...]
    for b in range(qt.shape[-1]):
        q, k = qt[:, :, b:b + 1], kt[:, :, b:b + 1]
        v = v_ref[b]
        state = s_ref[b]
        inner = jnp.sum(q * k, axis=1, keepdims=True) * v
        cross = jnp.sum(q * state, axis=1, keepdims=True) * g
        so_ref[b] = g * state + k * v
        o = inner + cross
        mu = jnp.mean(o, axis=-1, keepdims=True)
        d = o - mu
        var = jnp.mean(d * d, axis=-1, keepdims=True)
        y_ref[b] = sg_ref[b] * (d * lax.rsqrt(var + GN_EPS))


def _decode_ret_call(q_t, k_t, v_row, sg_row, state, decay):
    nseq = state.shape[0]

    def whole(arr):
        nd = arr.ndim
        return pl.BlockSpec(arr.shape, lambda i: (0,) * nd)

    return pl.pallas_call(
        _decode_ret_kernel,
        grid=(1,),
        in_specs=[whole(a) for a in (q_t, k_t, v_row, sg_row, state, decay)],
        out_specs=[whole(v_row), whole(state)],
        out_shape=[jax.ShapeDtypeStruct(v_row.shape, F32), jax.ShapeDtypeStruct(state.shape, F32)],
        compiler_params=_cparams(("arbitrary",), big=True),
        name="ret_decode",
    )(q_t, k_t, v_row, sg_row, state, decay)


def _odd_sample_kernel(x_ref, sh_ref, sc_ref, g_ref, win_ref, cw_ref, b0_ref, b1_ref, wout_ref, lng_ref, lnb_ref,
                       o_ref, u_ref, *, alpha):
    x = x_ref[...]
    d = x.shape[-1]
    h = (x * (1.0 + sc_ref[...]) + sh_ref[...]).astype(BF16)
    b_gate = _dot(h, win_ref[:, 0:d])
    u = _dot(h, win_ref[:, d:2 * d]) * _dot(h, win_ref[:, 2 * d:3 * d])
    cw = cw_ref[...]
    z = cw[0:1, :] * b0_ref[...] + cw[1:2, :] * b1_ref[...] + cw[2:3, :] * u
    u_ref[...] = u
    y = _dot((b_gate * z).astype(BF16), wout_ref[...])
    o_ref[...] = _layernorm(alpha * x + g_ref[...] * y, lng_ref[...], lnb_ref[...])


def _odd_sample_call(x, rows, ks, w_in, conv_w, buf0, buf1, w_out, ln_g, ln_b, ln_idx, alpha):
    n, d = x.shape
    full = pl.BlockSpec((n, d), lambda i: (0, 0))
    return pl.pallas_call(
        functools.partial(_odd_sample_kernel, alpha=alpha),
        grid=(1,),
        in_specs=[rows.row_spec(d), rows.mod_spec(ks[0]), rows.mod_spec(ks[1]), rows.mod_spec(ks[2]),
                  _const_spec(w_in), _const_spec(conv_w), full, full, _const_spec(w_out),
                  pl.BlockSpec((None, 1, d), lambda i: (ln_idx, 0, 0)),
                  pl.BlockSpec((None, 1, d), lambda i: (ln_idx, 0, 0))],
        out_specs=[full, full],
        out_shape=[jax.ShapeDtypeStruct((n, d), F32), jax.ShapeDtypeStruct((n, d), F32)],
        compiler_params=_cparams(("arbitrary",), big=True),
        name="odd_sample",
    )(x, rows.mods, rows.mods, rows.mods, w_in, conv_w, buf0, buf1, w_out, ln_g, ln_b)


def _rotary_tables(pos):
    half = HEAD_DIM // 2
    inv = 1.0 / (RET_ANGLE_BASE ** np.linspace(0.0, 1.0, half))
    ang = np.asarray(pos, np.float64)[:, None] * inv[None, :]
    return np.cos(ang), np.sin(ang)


def _token_major_tables(cos, sin):
    cos_h = np.concatenate([cos, cos], axis=1)
    sin_h = np.concatenate([-sin, sin], axis=1)
    return (jnp.asarray(np.tile(cos_h, (1, N_HEADS)), F32), jnp.asarray(np.tile(sin_h, (1, N_HEADS)), F32))


def kernel(x_prompt, x_sample, cache_k, cache_v, cache_logf, state_ret, state_conv, page_table, c_prompt, c_sample,
           w_ada, b_ada, w_ffn_in, w_ffn_out, ln_g, ln_b, w_in_even, b_forget, w_out_even, w_in_odd, conv_w,
           w_out_odd):
    batch, seq, d = x_prompt.shape
    nseq = x_sample.shape[0]
    depth = w_ada.shape[0]
    past_len = page_table.shape[1] * cache_k.shape[2]
    alpha = (2.0 * depth) ** 0.25
    w = HEAD_WIDTH
    tm = ROW_BLOCK
    assert seq % tm == 0 and x_sample.shape[1] == 1 and d % V7X_LANES == 0
    assert cache_k.shape[3] == N_HEADS and cache_k.shape[4] == HEAD_DIM

    n_c = batch + nseq
    pad = (-n_c) % 8
    c_all = jnp.concatenate([c_prompt, c_sample, jnp.zeros((pad, d), F32)], axis=0)
    mods = _ada_call(c_all, w_ada, b_ada)
    mods_p = mods[:, :batch].reshape(depth, batch, 1, N_MOD * d)
    mods_s = mods[:, batch:n_c]

    ln_g3 = ln_g.reshape(depth * 3, 1, d)
    ln_b3 = ln_b.reshape(depth * 3, 1, d)
    w_ffn_in_b = w_ffn_in.astype(BF16)
    w_ffn_out_b = w_ffn_out.astype(BF16)

    xp = x_prompt.reshape(batch * seq, d)
    xs = x_sample.reshape(nseq, d)

    cos_p, sin_p = _rotary_tables(np.arange(seq))
    rot_p = _token_major_tables(cos_p, sin_p) + (jnp.asarray(cos_p.T, F32), jnp.asarray(sin_p.T, F32))
    cos_s, sin_s = (jnp.asarray(t.T, F32) for t in _rotary_tables(np.full((nseq,), past_len)))

    log_decay = jnp.log(1.0 - 2.0 ** (-5.0 - jnp.arange(N_HEADS, dtype=F32)))
    lg_pairs = jnp.repeat(log_decay.reshape(HEAD_PAIRS, 2), HEAD_DIM, axis=1)
    lg_lane = lg_pairs.reshape(HEAD_PAIRS, 1, 2 * HEAD_DIM)
    lg_row = lg_pairs.reshape(HEAD_PAIRS, 2 * HEAD_DIM, 1)
    step_decay = jnp.exp(log_decay).reshape(N_HEADS, 1, 1)

    outs_p = {"k": [], "v": [], "lf": [], "ret": [], "conv": []}
    outs_s = {"k": [], "v": [], "lf": [], "ret": [], "conv": []}

    assert depth == 2 and w_in_even.shape[0] == 1 and cache_k.shape[0] == 1
    n_hosts = 2 * depth
    assert nseq % n_hosts == 0
    assert w_in_even.shape[2] == EVEN_IN_ROWS
    w_even_t = jnp.transpose(w_in_even[0])
    w_out_even_b = w_out_even[0].astype(BF16)
    bf = b_forget[0]
    w_in_odd_b = w_in_odd[0].astype(BF16)
    w_out_odd_b = w_out_odd[0].astype(BF16)

    def rows_s(l):
        return _Rows(mods_s, l, nseq, None, d)

    def rows_p(l):
        return _Rows(mods_p, l, tm, seq // tm, d)

    def ffn(x, rows, half, l, decode=None, mixer_out=None):
        ks = (0, 1, 2) if half == 0 else (6, 7, 8)
        return _ffn_call(x, rows, ks, w_ffn_in_b, w_ffn_out_b, (l, half), ln_g3, ln_b3, 3 * l + 2 * half, alpha,
                         decode, mixer_out)

    xs = ffn(xs, rows_s(0), 0, 0)
    bf_row = jnp.concatenate([bf, jnp.zeros((V7X_LANES - N_HEADS,), F32)]).reshape(1, V7X_LANES)
    (qs, ks_, vs, qrs_t, krs_t, vrs, sgs, lfs) = _even_in_sample_call(xs, rows_s(0), (3, 4), w_even_t, bf_row,
                                                                       cos_s, sin_s)
    n_phys, page_len = cache_k.shape[1], cache_k.shape[2]
    kt_pages = jnp.transpose(cache_k[0], (0, 2, 3, 1)).reshape(n_phys, w, page_len)
    vt_pages = jnp.transpose(cache_v[0], (0, 2, 3, 1)).reshape(n_phys, w, page_len)
    lft_pages = jnp.transpose(cache_logf[0], (0, 2, 1))
    per_host = nseq // n_hosts
    dec_args = (page_table, qs.reshape(nseq, 1, w), ks_.reshape(nseq, 1, w), vs.reshape(nseq, 1, w),
                lfs.reshape(nseq, 1, V7X_LANES), kt_pages, vt_pages, lft_pages)
    dec_out = []

    def ffn_host(x, half, l, mixer_out=None):
        x, o = ffn(x, rows_p(l), half, l, dec_args + (len(dec_out) * per_host, per_host), mixer_out)
        dec_out.append(o)
        return x

    xp = ffn_host(xp, 0, 0)
    (q, qr, vr, sg, kt, vt, ktc, vtc, krt, lft) = _even_in_call(
        xp, rows_p(0), (3, 4), w_even_t, bf.reshape(N_HEADS, 1), rot_p, batch, seq)
    c_t = _cumsum_call(lft)
    of = _fox_call(q, ktc, vtc, c_t.reshape(batch, N_HEADS, seq // tm, tm), batch, seq)
    yr, s_new = _ret_call(qr, krt, vr, sg, lg_lane, lg_row, batch, seq)
    outs_p["k"].append(jnp.transpose(kt.reshape(batch, N_HEADS, HEAD_DIM, seq), (0, 3, 1, 2)))
    outs_p["v"].append(jnp.transpose(vt.reshape(batch, N_HEADS, HEAD_DIM, seq), (0, 3, 1, 2)))
    outs_p["lf"].append(jnp.transpose(lft, (0, 2, 1)))
    outs_p["ret"].append(s_new)
    xp = ffn_host(xp, 1, 0, mixer_out=(5, of, yr, w_out_even_b, 1))
    xp = ffn_host(xp, 0, 1)
    xp, conv_p = _odd_prompt_call(xp, rows_p(1), (3, 4, 5), w_in_odd_b, conv_w[0], w_out_odd_b, ln_g3, ln_b3,
                                  4, alpha, batch, seq)
    outs_p["conv"].append(conv_p)
    xp = ffn_host(xp, 1, 1)

    of_s = jnp.concatenate(dec_out, axis=0).reshape(nseq, w)
    lfs = lfs[:, :N_HEADS]
    hdn = (N_HEADS, HEAD_DIM, nseq)
    rw = (nseq, N_HEADS, 1, HEAD_DIM)
    yr_s, s_new_s = _decode_ret_call(qrs_t.reshape(hdn), krs_t.reshape(hdn), vrs.reshape(rw), sgs.reshape(rw),
                                     state_ret[0], step_decay)
    xs = _outproj_call(xs, rows_s(0), 5, of_s, yr_s.reshape(nseq, w), w_out_even_b, ln_g3, ln_b3, 1, alpha)
    outs_s["k"].append(ks_.reshape(nseq, 1, N_HEADS, HEAD_DIM))
    outs_s["v"].append(vs.reshape(nseq, 1, N_HEADS, HEAD_DIM))
    outs_s["lf"].append(lfs.reshape(nseq, 1, N_HEADS))
    outs_s["ret"].append(s_new_s)
    xs = ffn(xs, rows_s(0), 1, 0)
    xs = ffn(xs, rows_s(1), 0, 1)
    buf = state_conv[0]
    xs, u_s = _odd_sample_call(xs, rows_s(1), (3, 4, 5), w_in_odd_b, conv_w[0], buf[:, 0], buf[:, 1], w_out_odd_b,
                               ln_g3, ln_b3, 4, alpha)
    outs_s["conv"].append(jnp.stack([buf[:, 1], u_s], axis=1))
    xs = ffn(xs, rows_s(1), 1, 1)

    def stk(lst):
        return jnp.stack(lst)

    return (xp.reshape(batch, seq, d), xs.reshape(nseq, 1, d),
            stk(outs_p["k"]), stk(outs_p["v"]), stk(outs_p["lf"]), stk(outs_p["ret"]), stk(outs_p["conv"]),
            stk(outs_s["k"]), stk(outs_s["v"]), stk(outs_s["lf"]), stk(outs_s["ret"]), stk(outs_s["conv"]))
```

```python
import functools

import jax
import jax.numpy as jnp
import numpy as np
from jax import lax
from jax.experimental import pallas as pl
from jax.experimental.pallas import tpu as pltpu

F32 = jnp.float32
BF16 = jnp.bfloat16

HEAD_DIM = 64
N_HEADS = 8
HEAD_WIDTH = N_HEADS * HEAD_DIM
N_MOD = 9
CONV_WIDTH = 3
RET_ANGLE_BASE = 10000.0
LN_EPS = 1e-5
GN_EPS = 1e-6
QK_SCALE = HEAD_DIM ** -0.5
LOG2E = 1.4426950408889634

V7X_LANES = 128
V7X_VMEM_LIMIT_BYTES = 56 * 1024 * 1024

ROW_BLOCK = 512
FF_CHUNK = 256
FOX_BLOCKS_PER_UPDATE = 4
RET_CHUNK = 256
ADA_COL_BLOCK = 1152
HEAD_PAIRS = N_HEADS // 2


def _cparams(sem, big=False):
    return pltpu.CompilerParams(dimension_semantics=sem,
                                vmem_limit_bytes=V7X_VMEM_LIMIT_BYTES if big else None)


def _const_spec(arr):
    nd = arr.ndim
    return pl.BlockSpec(arr.shape, lambda *_: (0,) * nd, pipeline_mode=pl.Buffered(1))


def _layernorm(z, g, b):
    mu = jnp.mean(z, axis=-1, keepdims=True)
    d = z - mu
    var = jnp.mean(d * d, axis=-1, keepdims=True)
    return d * lax.rsqrt(var + LN_EPS) * g + b


def _silu(a):
    return a * jax.nn.sigmoid(a)


def _log_sigmoid(z):
    return jnp.minimum(z, 0.0) - jnp.log1p(jnp.exp(-jnp.abs(z)))


def _dot(a, b):
    return jnp.dot(a, b, preferred_element_type=F32)


def _dot_nt(a, b):
    return lax.dot_general(a, b, (((1,), (1,)), ((), ())), preferred_element_type=F32)


def _ada_kernel(c_ref, w_ref, b_ref, o_ref):
    s = _silu(c_ref[...]).astype(BF16)
    o_ref[...] = _dot(s, w_ref[...].astype(BF16)) + b_ref[...]


def _ada_call(c_all, w_ada, b_ada):
    depth, d, nm = w_ada.shape
    rows = c_all.shape[0]
    tn = ADA_COL_BLOCK
    return pl.pallas_call(
        _ada_kernel,
        grid=(depth, nm // tn),
        in_specs=[pl.BlockSpec((rows, d), lambda l, j: (0, 0)),
                  pl.BlockSpec((None, d, tn), lambda l, j: (l, 0, j)),
                  pl.BlockSpec((None, 1, tn), lambda l, j: (l, 0, j))],
        out_specs=pl.BlockSpec((None, rows, tn), lambda l, j: (l, 0, j)),
        out_shape=jax.ShapeDtypeStruct((depth, rows, nm), F32),
        compiler_params=_cparams(("arbitrary", "arbitrary"), big=True),
        name="ada_mods",
    )(c_all, w_ada, b_ada.reshape(depth, 1, nm))


class _Rows:
    def __init__(self, mods, layer, tm, blocks_per_seq, d):
        self.mods, self.layer, self.tm, self.bps, self.d = mods, layer, tm, blocks_per_seq, d

    def mod_spec(self, k):
        l, d = self.layer, self.d
        if self.bps is None:
            return pl.BlockSpec((None, self.tm, d), lambda i, *_: (l, 0, k))
        bps = self.bps
        return pl.BlockSpec((None, None, 1, d), lambda i, *_: (l, i // bps, 0, k))

    def row_spec(self, width):
        return pl.BlockSpec((self.tm, width), lambda i, *_: (i, 0))


def _mixer_out_math(x, gate, a0, a1, w_ref, lng, lnb, alpha):
    half = a0.shape[-1]
    y = _dot(a0.astype(BF16), w_ref[0:half, :]) + _dot(a1.astype(BF16), w_ref[half:, :])
    return _layernorm(alpha * x + gate * y, lng, lnb)


def _ffn_rows(refs, ff, alpha, pre):
    x = refs[0][...]
    refs = refs[1:]
    if pre:
        gate, a0, a1, wo_ref, lng1, lnb1 = refs[:6]
        x = _mixer_out_math(x, gate[...], a0[...], a1[...], wo_ref, lng1[...], lnb1[...], alpha)
        refs = refs[6:]
    sh_ref, sc_ref, g_ref, win_ref, wout_ref, lng_ref, lnb_ref = refs
    h = (x * (1.0 + sc_ref[...]) + sh_ref[...]).astype(BF16)
    acc = None
    for c in range(ff // FF_CHUNK):
        lo, hi = c * FF_CHUNK, (c + 1) * FF_CHUNK
        a = _dot(h, win_ref[:, lo:hi])
        b = _dot(h, win_ref[:, ff + lo:ff + hi])
        y = _dot((_silu(a) * b).astype(BF16), wout_ref[lo:hi, :])
        acc = y if acc is None else acc + y
    z = alpha * x + (0.5 * g_ref[...]) * acc
    return _layernorm(z, lng_ref[...], lnb_ref[...])


def _ffn_kernel(*refs, ff, alpha, pre):
    refs[-1][...] = _ffn_rows(refs[:-1], ff, alpha, pre)


def _ffn_decode_kernel(pt_ref, *refs, ff, alpha, pre, pages, steps_per_seq, seq0):
    n_ffn = 14 if pre else 8
    ffn_in, rest = refs[:n_ffn], refs[n_ffn:]
    q_ref, kn_ref, vn_ref, lfn_ref, kt_hbm, vt_hbm, lf_hbm, o_ref, dec_ref = rest[:9]
    state = rest[9:14]
    kbuf, vbuf, lfbuf, sem = rest[14:]
    npg = pt_ref.shape[1]
    i = pl.program_id(0)
    last = pl.num_programs(0) - 1
    slot = i % 2

    def page_copies(step, into):
        seq = seq0 + step // steps_per_seq
        first = npg - 1 - (step % steps_per_seq) * pages
        out = []
        for r in range(pages):
            pg = pt_ref[seq, first - r]
            out += [pltpu.make_async_copy(kt_hbm.at[pg], kbuf.at[into, r], sem.at[into, 0]),
                    pltpu.make_async_copy(vt_hbm.at[pg], vbuf.at[into, r], sem.at[into, 1]),
                    pltpu.make_async_copy(lf_hbm.at[pg], lfbuf.at[into, r], sem.at[into, 2])]
        return out

    @pl.when(i == 0)
    def _():
        for c in page_copies(0, 0):
            c.start()

    nxt = jnp.minimum(i + 1, last)
    for c in page_copies(nxt, 1 - slot):
        c.start()
    for c in page_copies(i, slot):
        c.wait()

    g = i % steps_per_seq
    pl.when(g == 0)(lambda: _decode_fox_init(q_ref, kn_ref, vn_ref, lfn_ref, *state))
    _decode_fox_pages([kbuf.at[slot, r] for r in range(pages)], [vbuf.at[slot, r] for r in range(pages)],
                      [lfbuf.at[slot, r] for r in range(pages)], *state)
    o_ref[...] = _ffn_rows(ffn_in, ff, alpha, pre)
    pl.when(g == steps_per_seq - 1)(lambda: _decode_fox_finish(dec_ref, state[2], state[4]))

    @pl.when(i == last)
    def _():
        for c in page_copies(nxt, 1 - slot):
            c.wait()


def _ffn_call(x, rows, ks, w_in, w_out, which, ln_g, ln_b, ln_idx, alpha, decode=None, mixer_out=None):
    n, d = x.shape
    ff = w_out.shape[2]
    l, j = which
    steps = n // rows.tm

    def w_spec(arr):
        return pl.BlockSpec((None, None) + arr.shape[2:], lambda i, *_: (l, j, 0, 0), pipeline_mode=pl.Buffered(1))

    def ln_spec(idx):
        return pl.BlockSpec((None, 1, d), lambda i, *_: (idx, 0, 0))

    in_specs = [rows.row_spec(d)]
    args = [x]
    if mixer_out is not None:
        k_gate, a0, a1, w_o, ln_idx1 = mixer_out
        in_specs += [rows.mod_spec(k_gate), rows.row_spec(a0.shape[1]), rows.row_spec(a1.shape[1]), _const_spec(w_o),
                     ln_spec(ln_idx1), ln_spec(ln_idx1)]
        args += [rows.mods, a0, a1, w_o, ln_g, ln_b]
    in_specs += [rows.mod_spec(ks[0]), rows.mod_spec(ks[1]), rows.mod_spec(ks[2]), w_spec(w_in), w_spec(w_out),
                 ln_spec(ln_idx), ln_spec(ln_idx)]
    args += [rows.mods, rows.mods, rows.mods, w_in, w_out, ln_g, ln_b]
    pre = mixer_out is not None
    x_shape = jax.ShapeDtypeStruct((n, d), F32)
    if decode is None:
        return pl.pallas_call(
            functools.partial(_ffn_kernel, ff=ff, alpha=alpha, pre=pre),
            grid=(steps,), in_specs=in_specs, out_specs=rows.row_spec(d), out_shape=x_shape,
            compiler_params=_cparams(("parallel",), big=True),
            name="ffn",
        )(*args)

    page_table, q, kn, vn, lfn, kt_pages, vt_pages, lft_pages, seq0, n_seq = decode
    npg = page_table.shape[1]
    w, page_len = kt_pages.shape[1], kt_pages.shape[2]
    steps_per_seq = steps // n_seq
    pages = npg // steps_per_seq
    assert steps_per_seq * n_seq == steps and pages * steps_per_seq == npg

    row = pl.BlockSpec((None, 1, w), lambda i, pt: (seq0 + i // steps_per_seq, 0, 0))
    hbm = pl.BlockSpec(memory_space=pl.ANY)
    grid_spec = pltpu.PrefetchScalarGridSpec(
        num_scalar_prefetch=1,
        grid=(steps,),
        in_specs=in_specs
        + [row, row, row, pl.BlockSpec((None, 1, V7X_LANES), lambda i, pt: (seq0 + i // steps_per_seq, 0, 0))]
        + [hbm, hbm, hbm],
        out_specs=[rows.row_spec(d), pl.BlockSpec((None, 1, w), lambda i, pt: (i // steps_per_seq, 0, 0))],
        scratch_shapes=[pltpu.VMEM((N_HEADS, w), F32), pltpu.VMEM((N_HEADS, 1), F32), pltpu.VMEM((N_HEADS, 1), F32),
                        pltpu.VMEM((N_HEADS, 1), F32), pltpu.VMEM((w, page_len), F32),
                        pltpu.VMEM((2, pages, w, page_len), F32), pltpu.VMEM((2, pages, w, page_len), F32),
                        pltpu.VMEM((2, pages, N_HEADS, page_len), F32), pltpu.SemaphoreType.DMA((2, 3))],
    )
    return pl.pallas_call(
        functools.partial(_ffn_decode_kernel, ff=ff, alpha=alpha, pre=pre, pages=pages, steps_per_seq=steps_per_seq,
                          seq0=seq0),
        grid_spec=grid_spec,
        out_shape=[x_shape, jax.ShapeDtypeStruct((n_seq, 1, w), F32)],
        compiler_params=_cparams(("arbitrary",), big=True),
        name="ffn_decode",
    )(page_table, *args, q, kn, vn, lfn, kt_pages, vt_pages, lft_pages)


def _outproj_kernel(x_ref, g_ref, a0_ref, a1_ref, w_ref, lng_ref, lnb_ref, o_ref, *, alpha):
    o_ref[...] = _mixer_out_math(x_ref[...], g_ref[...], a0_ref[...], a1_ref[...], w_ref, lng_ref[...], lnb_ref[...],
                                 alpha)


def _outproj_call(x, rows, k_gate, a0, a1, w_out, ln_g, ln_b, ln_idx, alpha):
    n, d = x.shape
    return pl.pallas_call(
        functools.partial(_outproj_kernel, alpha=alpha),
        grid=(n // rows.tm,),
        in_specs=[rows.row_spec(d), rows.mod_spec(k_gate), rows.row_spec(a0.shape[1]), rows.row_spec(a1.shape[1]),
                  _const_spec(w_out),
                  pl.BlockSpec((None, 1, d), lambda i: (ln_idx, 0, 0)),
                  pl.BlockSpec((None, 1, d), lambda i: (ln_idx, 0, 0))],
        out_specs=rows.row_spec(d),
        out_shape=jax.ShapeDtypeStruct((n, d), F32),
        compiler_params=_cparams(("parallel",), big=True),
        name="mixer_outproj",
    )(x, rows.mods, a0, a1, w_out, ln_g, ln_b)


def _rotate_token_major(a, cos, sin_signed):
    width = a.shape[-1]
    lane = lax.broadcasted_iota(jnp.int32, a.shape, 1)
    first_half = (lane & (HEAD_DIM // 2)) == 0
    partner = jnp.where(first_half, pltpu.roll(a, width - HEAD_DIM // 2, 1), pltpu.roll(a, HEAD_DIM // 2, 1))
    return a * cos + partner * sin_signed


_EVEN_ROWS = {}
_r0 = 0
for _name, _n in (("qf", HEAD_WIDTH), ("kf", HEAD_WIDTH), ("vf", HEAD_WIDTH), ("fg", N_HEADS),
                  ("qr", HEAD_WIDTH), ("kr", HEAD_WIDTH), ("vr", HEAD_WIDTH), ("gr", HEAD_WIDTH)):
    _EVEN_ROWS[_name] = (_r0, _r0 + _n)
    _r0 += _n
EVEN_IN_ROWS = _r0
FG_ROWS = 16


def _even_in_kernel(x_ref, sh_ref, sc_ref, w_ref, bf_ref, cos_ref, sin_ref, cost_ref, sint_ref,
                    q_o, qr_o, vr_o, sg_o, kt_o, vt_o, ktc_o, vtc_o, krt_o, lft_o, wtok_ref, wt_ref):
    w = HEAD_WIDTH

    @pl.when(pl.program_id(0) == 0)
    def _():
        for k, name in enumerate(("qf", "qr", "vr", "gr")):
            lo, hi = _EVEN_ROWS[name]
            wtok_ref[:, k * w:(k + 1) * w] = w_ref[lo:hi, :].T.astype(BF16)
        for k, name in enumerate(("kf", "vf", "kr")):
            lo, hi = _EVEN_ROWS[name]
            wt_ref[k * w:(k + 1) * w, :] = w_ref[lo:hi, :].astype(BF16)
        lo, hi = _EVEN_ROWS["fg"]
        fg_rows = jnp.concatenate([w_ref[lo:hi, :], jnp.zeros((FG_ROWS - N_HEADS, w_ref.shape[1]), F32)], axis=0)
        wt_ref[3 * w:, :] = fg_rows.astype(BF16)

    x = x_ref[...]
    h = (x * (1.0 + sc_ref[...]) + sh_ref[...]).astype(BF16)

    q_o[...] = (_dot(h, wtok_ref[:, 0:w]) * (QK_SCALE * LOG2E)).astype(BF16)
    qr = _dot(h, wtok_ref[:, w:2 * w])
    qr_o[...] = _rotate_token_major(qr, cos_ref[...], sin_ref[...]).astype(BF16)
    vr_o[...] = _dot(h, wtok_ref[:, 2 * w:3 * w]).astype(BF16)
    sg_o[...] = _silu(_dot(h, wtok_ref[:, 3 * w:4 * w])).astype(BF16)

    kt = _dot_nt(wt_ref[0:w, :], h)
    kt_o[...] = kt
    ktc_o[...] = kt.astype(BF16)
    vt = _dot_nt(wt_ref[w:2 * w, :], h)
    vt_o[...] = vt
    vtc_o[...] = vt.astype(BF16)

    krt = _dot_nt(wt_ref[2 * w:3 * w, :], h)
    cos_t, sin_t = cost_ref[...], sint_ref[...]
    half = HEAD_DIM // 2
    for hh in range(N_HEADS):
        x1 = krt[hh * HEAD_DIM:hh * HEAD_DIM + half, :]
        x2 = krt[hh * HEAD_DIM + half:(hh + 1) * HEAD_DIM, :]
        krt_o[hh * HEAD_DIM:hh * HEAD_DIM + half, :] = (x1 * cos_t - x2 * sin_t) * QK_SCALE
        krt_o[hh * HEAD_DIM + half:(hh + 1) * HEAD_DIM, :] = (x1 * sin_t + x2 * cos_t) * QK_SCALE

    fg = _dot_nt(wt_ref[3 * w:, :], h)[0:N_HEADS, :] + bf_ref[...]
    lft_o[...] = _log_sigmoid(fg)


def _even_in_call(x, rows, ks, w_t_f32, bf_col, rot, batch, seq):
    n, d = x.shape
    tm = rows.tm
    bps = seq // tm
    w = HEAD_WIDTH
    cos_tok, sin_tok, cos_t, sin_t = rot
    tok_out = pl.BlockSpec((tm, w), lambda i: (i, 0))
    t_out = pl.BlockSpec((None, w, tm), lambda i: (i // bps, 0, i % bps))
    tc_out = pl.BlockSpec((None, None, w, tm), lambda i: (i // bps, i % bps, 0, 0))
    tok_shape = jax.ShapeDtypeStruct((n, w), BF16)
    return pl.pallas_call(
        _even_in_kernel,
        grid=(n // tm,),
        in_specs=[rows.row_spec(d), rows.mod_spec(ks[0]), rows.mod_spec(ks[1]),
                  _const_spec(w_t_f32), _const_spec(bf_col),
                  pl.BlockSpec((tm, w), lambda i: (i % bps, 0)),
                  pl.BlockSpec((tm, w), lambda i: (i % bps, 0)),
                  pl.BlockSpec((HEAD_DIM // 2, tm), lambda i: (0, i % bps)),
                  pl.BlockSpec((HEAD_DIM // 2, tm), lambda i: (0, i % bps))],
        out_specs=[tok_out, tok_out, tok_out, tok_out, t_out, t_out, tc_out, tc_out, tc_out,
                   pl.BlockSpec((None, N_HEADS, tm), lambda i: (i // bps, 0, i % bps))],
        out_shape=[tok_shape, tok_shape, tok_shape, tok_shape,
                   jax.ShapeDtypeStruct((batch, w, seq), F32),
                   jax.ShapeDtypeStruct((batch, w, seq), F32),
                   jax.ShapeDtypeStruct((batch, bps, w, tm), BF16),
                   jax.ShapeDtypeStruct((batch, bps, w, tm), BF16),
                   jax.ShapeDtypeStruct((batch, bps, w, tm), F32),
                   jax.ShapeDtypeStruct((batch, N_HEADS, seq), F32)],
        scratch_shapes=[pltpu.VMEM((d, 4 * w), BF16), pltpu.VMEM((3 * w + FG_ROWS, d), BF16)],
        compiler_params=_cparams(("arbitrary",), big=True),
        name="even_inproj",
    )(x, rows.mods, rows.mods, w_t_f32, bf_col, cos_tok, sin_tok, cos_t, sin_t)


def _cumsum_kernel(x_ref, o_ref):
    x = x_ref[...]
    n = x.shape[-1]
    lane = lax.broadcasted_iota(jnp.int32, x.shape, 1)
    shift = 1
    while shift < n:
        x = x + jnp.where(lane >= shift, pltpu.roll(x, shift, 1), 0.0)
        shift *= 2
    o_ref[...] = x * LOG2E


def _cumsum_call(lft):
    batch, heads, seq = lft.shape
    spec = pl.BlockSpec((None, heads, seq), lambda b: (b, 0, 0))
    return pl.pallas_call(
        _cumsum_kernel, grid=(batch,), in_specs=[spec], out_specs=spec,
        out_shape=jax.ShapeDtypeStruct(lft.shape, F32),
        compiler_params=_cparams(("parallel",)),
        name="logf_cumsum",
    )(lft)


def _split3(x):
    hi = x.astype(BF16).astype(F32)
    r = x - hi
    mid = r.astype(BF16).astype(F32)
    lo = (r - mid).astype(BF16).astype(F32)
    return hi, mid, lo


def _fox_kernel(q_ref, kt_ref, vt_ref, c_ref, o_ref):
    tq = q_ref.shape[0]
    tk = kt_ref.shape[-1]
    pw = 2 * HEAD_DIM
    slab_rows = 16
    i = pl.program_id(2)
    q = q_ref[...]
    lane = lax.broadcasted_iota(jnp.int32, (tq, pw), 1)
    q_aug = []
    for hd in range(2):
        base = HEAD_DIM if hd == 0 else 0
        hi, mid, lo = _split3(c_ref[hd, pl.ds(i, 1), :][:, 0:1])
        aug = jnp.where(lane == base + 3, hi, jnp.where(lane == base + 4, mid, jnp.where(lane == base + 5, lo,
              jnp.where((lane >= base) & (lane < base + 3), 1.0, 0.0))))
        own = (lane < HEAD_DIM) if hd == 0 else (lane >= HEAD_DIM)
        q_aug.append(jnp.where(own, q, aug.astype(BF16)))
    r16 = lax.broadcasted_iota(jnp.int32, (slab_rows, tk), 0)
    rowv = lax.broadcasted_iota(jnp.int32, (pw, tk), 0)

    def scores(j, hd):
        kt = kt_ref[j]
        hi, mid, lo = _split3(-c_ref[hd, pl.ds(j, 1), :])
        slab = jnp.where(r16 == 0, hi, jnp.where(r16 == 1, mid, jnp.where(r16 == 2, lo,
               jnp.where(r16 < 6, 1.0, 0.0)))).astype(BF16)
        if hd == 0:
            kt_aug = jnp.concatenate([kt[0:HEAD_DIM], slab, kt[HEAD_DIM + slab_rows:]], axis=0)
        else:
            kt_aug = jnp.concatenate([slab, kt[slab_rows:]], axis=0)
        return _dot(q_aug[hd], kt_aug)

    def values(j, hd):
        vt = vt_ref[j]
        return jnp.where(rowv == (HEAD_DIM if hd == 0 else 0), jnp.ones_like(vt), vt)

    def update(s, vt_aug, m, acc):
        m_new = jnp.maximum(m, jnp.max(s, axis=-1, keepdims=True))
        return m_new, jnp.exp2(m - m_new) * acc + _dot_nt(jnp.exp2(s - m_new).astype(BF16), vt_aug)

    def diag_scores(hd):
        row = lax.broadcasted_iota(jnp.int32, (tq, tk), 0)
        col = lax.broadcasted_iota(jnp.int32, (tq, tk), 1)
        return jnp.where(row >= col, scores(i, hd), -jnp.inf)

    group = FOX_BLOCKS_PER_UPDATE

    def joint(carry, full_blocks, with_diag):
        out = []
        for hd in range(2):
            s = [scores(j, hd) for j in full_blocks] + ([diag_scores(hd)] if with_diag else [])
            v = [values(j, hd) for j in full_blocks] + ([values(i, hd)] if with_diag else [])
            out.append(update(jnp.concatenate(s, axis=1), jnp.concatenate(v, axis=1), *carry[hd]))
        return tuple(out)

    def tail(n_full):
        return lambda carry: tuple(c[1] for c in joint(carry, [i - n_full + k for k in range(n_full)], True))

    init_head = (jnp.full((tq, 1), -jnp.inf, F32), jnp.zeros((tq, pw), F32))
    carry = lax.fori_loop(0, lax.shift_right_logical(i, group.bit_length() - 1),
                          lambda t, c: joint(c, [group * t + k for k in range(group)], False),
                          (init_head, init_head))
    outs = lax.switch(i & (group - 1), [tail(n) for n in range(group)], carry)
    o0 = outs[0] / outs[0][:, HEAD_DIM:HEAD_DIM + 1]
    o1 = outs[1] / outs[1][:, 0:1]
    o_ref[...] = jnp.where(lane < HEAD_DIM, o0, o1).astype(BF16)


def _fox_call(q, ktc, vtc, c4, batch, seq):
    n, w = q.shape
    tq = ktc.shape[-1]
    nq = seq // tq
    pw = 2 * HEAD_DIM
    return pl.pallas_call(
        _fox_kernel,
        grid=(batch, HEAD_PAIRS, nq),
        in_specs=[pl.BlockSpec((tq, pw), lambda b, p, i: (b * nq + i, p)),
                  pl.BlockSpec((None, nq, pw, tq), lambda b, p, i: (b, 0, p, 0)),
                  pl.BlockSpec((None, nq, pw, tq), lambda b, p, i: (b, 0, p, 0)),
                  pl.BlockSpec((None, 2, nq, tq), lambda b, p, i: (b, p, 0, 0))],
        out_specs=pl.BlockSpec((tq, pw), lambda b, p, i: (b * nq + i, p)),
        out_shape=jax.ShapeDtypeStruct((n, w), BF16),
        compiler_params=_cparams(("parallel", "parallel", "arbitrary"), big=True),
        name="fox_prompt",
    )(q, ktc, vtc, c4)


def _ret_kernel(q_ref, kt_ref, v_ref, sg_ref, lgl_ref, lgr_ref, y_ref, s_ref):
    chunk = RET_CHUNK
    pw = 2 * HEAD_DIM
    nkb, _, tb = kt_ref.shape
    lg_lane = lgl_ref[...]
    lg_row = lgr_ref[...]
    lg_a, lg_b = lg_lane[:, 0:1], lg_lane[:, HEAD_DIM:HEAD_DIM + 1]

    ri = lax.broadcasted_iota(jnp.int32, (2 * chunk, chunk), 0)
    cj = lax.broadcasted_iota(jnp.int32, (2 * chunk, chunk), 1)
    first = ri < chunk
    diff = (jnp.where(first, ri, ri - chunk) - cj).astype(F32)
    decay_mask = jnp.where(diff >= 0, jnp.exp(jnp.maximum(diff, 0.0) * jnp.where(first, lg_a, lg_b)), 0.0)
    jj = lax.broadcasted_iota(jnp.int32, (pw, chunk), 1).astype(F32)
    col_decay = jnp.exp((chunk - 1.0 - jj) * lg_row)
    ii = lax.broadcasted_iota(jnp.int32, (chunk, pw), 0).astype(F32)
    row_decay = jnp.exp((ii + 1.0) * lg_lane)
    chunk_decay = jnp.exp(float(chunk) * lg_row)
    r2 = lax.broadcasted_iota(jnp.int32, (pw, pw), 0)
    c2 = lax.broadcasted_iota(jnp.int32, (pw, pw), 1)
    same_head = (r2 < HEAD_DIM) == (c2 < HEAD_DIM)
    seg_avg = jnp.where(same_head, 1.0 / HEAD_DIM, 0.0).astype(BF16)
    lane = lax.broadcasted_iota(jnp.int32, (chunk, pw), 1)

    def seg_mean(a):
        hi = a.astype(BF16)
        lo = (a - hi.astype(F32)).astype(BF16)
        return _dot(hi, seg_avg) + _dot(lo, seg_avg)

    def one_chunk(q, kt, v, sg, state):
        zero = jnp.zeros_like(q)
        q_stack = jnp.concatenate([jnp.where(lane < HEAD_DIM, q, zero), jnp.where(lane >= HEAD_DIM, q, zero)], axis=0)
        qk = _dot(q_stack, kt.astype(BF16)) * decay_mask
        kd = (kt * col_decay).astype(BF16)
        r = _dot(jnp.concatenate([qk.astype(BF16), kd], axis=0), v)
        inner = jnp.where(lane < HEAD_DIM, r[0:chunk], r[chunk:2 * chunk])
        update = jnp.where(same_head, r[2 * chunk:], 0.0)
        cross = _dot(q, state.astype(BF16)) * row_decay
        o = inner + cross
        mu = seg_mean(o)
        d = o - mu
        var = seg_mean(d * d)
        y = sg.astype(F32) * (d * lax.rsqrt(var + GN_EPS))
        return y.astype(BF16), chunk_decay * state + update

    def body(jb, state):
        kt_blk = kt_ref[jb]
        for sub in range(tb // chunk):
            t0 = pl.multiple_of(jb * tb + sub * chunk, chunk)
            y, state = one_chunk(q_ref[pl.ds(t0, chunk), :], kt_blk[:, sub * chunk:(sub + 1) * chunk],
                                 v_ref[pl.ds(t0, chunk), :], sg_ref[pl.ds(t0, chunk), :], state)
            y_ref[pl.ds(t0, chunk), :] = y
        return state

    state = lax.fori_loop(0, nkb, body, jnp.zeros((pw, pw), F32), unroll=4)
    s_ref[0] = state[0:HEAD_DIM, 0:HEAD_DIM]
    s_ref[1] = pltpu.roll(state, HEAD_DIM, 1)[HEAD_DIM:, 0:HEAD_DIM]


def _ret_call(qr, krt, vr, sg, lg_lane, lg_row, batch, seq):
    n, w = qr.shape
    nkb, tb = krt.shape[1], krt.shape[3]
    pw = 2 * HEAD_DIM
    seq_spec = pl.BlockSpec((seq, pw), lambda b, p: (b, p))
    return pl.pallas_call(
        _ret_kernel,
        grid=(batch, HEAD_PAIRS),
        in_specs=[seq_spec,
                  pl.BlockSpec((None, nkb, pw, tb), lambda b, p: (b, 0, p, 0)),
                  seq_spec, seq_spec,
                  pl.BlockSpec((None, 1, pw), lambda b, p: (p, 0, 0)),
                  pl.BlockSpec((None, pw, 1), lambda b, p: (p, 0, 0))],
        out_specs=[seq_spec, pl.BlockSpec((None, 2, HEAD_DIM, HEAD_DIM), lambda b, p: (b, p, 0, 0))],
        out_shape=[jax.ShapeDtypeStruct((n, w), BF16),
                   jax.ShapeDtypeStruct((batch, N_HEADS, HEAD_DIM, HEAD_DIM), F32)],
        compiler_params=_cparams(("parallel", "parallel"), big=True),
        name="ret_prompt",
    )(qr, krt, vr, sg, lg_lane, lg_row)


def _odd_prompt_kernel(x_ref, sh_ref, sc_ref, g_ref, win_ref, cw_ref, wout_ref, lng_ref, lnb_ref,
                       o_ref, st_ref, carry_ref, *, bps, alpha):
    i = pl.program_id(0)
    x = x_ref[...]
    tm, d = x.shape
    h = (x * (1.0 + sc_ref[...]) + sh_ref[...]).astype(BF16)
    b_gate = _dot(h, win_ref[:, 0:d])
    u = _dot(h, win_ref[:, d:2 * d]) * _dot(h, win_ref[:, 2 * d:3 * d])

    @pl.when(i % bps == 0)
    def _():
        carry_ref[...] = jnp.zeros_like(carry_ref)

    prev = carry_ref[...]
    p1, p2 = prev[7:8, :], prev[6:7, :]
    row = lax.broadcasted_iota(jnp.int32, (tm, d), 0)
    u1 = jnp.where(row == 0, p1, pltpu.roll(u, 1, 0))
    u2 = jnp.where(row == 0, p2, jnp.where(row == 1, p1, pltpu.roll(u, 2, 0)))
    cw = cw_ref[...]
    z = cw[0:1, :] * u2 + cw[1:2, :] * u1 + cw[2:3, :] * u
    carry_ref[...] = u[tm - 8:, :]
    st_ref[...] = u[tm - (CONV_WIDTH - 1):, :]
    y = _dot((b_gate * z).astype(BF16), wout_ref[...])
    zz = alpha * x + g_ref[...] * y
    o_ref[...] = _layernorm(zz, lng_ref[...], lnb_ref[...])


def _odd_prompt_call(x, rows, ks, w_in, conv_w, w_out, ln_g, ln_b, ln_idx, alpha, batch, seq):
    n, d = x.shape
    bps = seq // rows.tm
    return pl.pallas_call(
        functools.partial(_odd_prompt_kernel, bps=bps, alpha=alpha),
        grid=(n // rows.tm,),
        in_specs=[rows.row_spec(d), rows.mod_spec(ks[0]), rows.mod_spec(ks[1]), rows.mod_spec(ks[2]),
                  _const_spec(w_in), _const_spec(conv_w), _const_spec(w_out),
                  pl.BlockSpec((None, 1, d), lambda i: (ln_idx, 0, 0)),
                  pl.BlockSpec((None, 1, d), lambda i: (ln_idx, 0, 0))],
        out_specs=[rows.row_spec(d), pl.BlockSpec((None, CONV_WIDTH - 1, d), lambda i: (i // bps, 0, 0))],
        out_shape=[jax.ShapeDtypeStruct((n, d), F32), jax.ShapeDtypeStruct((batch, CONV_WIDTH - 1, d), F32)],
        scratch_shapes=[pltpu.VMEM((8, d), F32)],
        compiler_params=_cparams(("arbitrary",), big=True),
        name="odd_prompt",
    )(x, rows.mods, rows.mods, rows.mods, w_in, conv_w, w_out, ln_g, ln_b)


def _even_in_sample_kernel(x_ref, sh_ref, sc_ref, w_ref, bf_ref, cost_ref, sint_ref,
                           q_o, k_o, v_o, qrt_o, krt_o, vr_o, sg_o, lf_o):
    d = w_ref.shape[1]
    h = (x_ref[...] * (1.0 + sc_ref[...]) + sh_ref[...]).astype(BF16)

    def rows(name):
        lo, hi = _EVEN_ROWS[name]
        return w_ref[lo:hi, :]

    def tok(name):
        return _dot_nt(h, rows(name).astype(BF16))

    q_o[...] = tok("qf") * QK_SCALE
    k_o[...] = tok("kf")
    v_o[...] = tok("vf")
    vr_o[...] = tok("vr")
    sg_o[...] = _silu(tok("gr"))
    fg_rows = jnp.concatenate([rows("fg"), jnp.zeros((V7X_LANES - N_HEADS, d), F32)], axis=0).astype(BF16)
    lf_o[...] = _log_sigmoid(_dot_nt(h, fg_rows) + bf_ref[...])

    cos_t, sin_t = cost_ref[...], sint_ref[...]
    half = HEAD_DIM // 2
    for name, out, scale in (("qr", qrt_o, 1.0), ("kr", krt_o, QK_SCALE)):
        t = _dot_nt(rows(name).astype(BF16), h)
        for hh in range(N_HEADS):
            x1 = t[hh * HEAD_DIM:hh * HEAD_DIM + half, :]
            x2 = t[hh * HEAD_DIM + half:(hh + 1) * HEAD_DIM, :]
            out[hh * HEAD_DIM:hh * HEAD_DIM + half, :] = (x1 * cos_t - x2 * sin_t) * scale
            out[hh * HEAD_DIM + half:(hh + 1) * HEAD_DIM, :] = (x1 * sin_t + x2 * cos_t) * scale


def _even_in_sample_call(x, rows, ks, w_t_f32, bf_row, cos_t, sin_t):
    n, d = x.shape
    w = HEAD_WIDTH
    full = pl.BlockSpec((n, w), lambda i: (0, 0))
    full_t = pl.BlockSpec((w, n), lambda i: (0, 0))
    rot = pl.BlockSpec((HEAD_DIM // 2, n), lambda i: (0, 0))
    shp = jax.ShapeDtypeStruct((n, w), F32)
    shp_t = jax.ShapeDtypeStruct((w, n), F32)
    return pl.pallas_call(
        _even_in_sample_kernel,
        grid=(1,),
        in_specs=[rows.row_spec(d), rows.mod_spec(ks[0]), rows.mod_spec(ks[1]),
                  _const_spec(w_t_f32), _const_spec(bf_row), rot, rot],
        out_specs=[full, full, full, full_t, full_t, full, full, pl.BlockSpec((n, V7X_LANES), lambda i: (0, 0))],
        out_shape=[shp, shp, shp, shp_t, shp_t, shp, shp, jax.ShapeDtypeStruct((n, V7X_LANES), F32)],
        compiler_params=_cparams(("arbitrary",), big=True),
        name="even_inproj_sample",
    )(x, rows.mods, rows.mods, w_t_f32, bf_row, cos_t, sin_t)


def _own_head_lanes():
    sub = lax.broadcasted_iota(jnp.int32, (N_HEADS, HEAD_WIDTH), 0)
    lane = lax.broadcasted_iota(jnp.int32, (N_HEADS, HEAD_WIDTH), 1)
    return (lane >= sub * HEAD_DIM) & (lane < (sub + 1) * HEAD_DIM)


def _decode_fox_init(q_ref, kn_ref, vn_ref, lfn_ref, qbd_s, m_s, l_s, run_s, acc_s):
    w = HEAD_WIDTH
    q_bd = jnp.where(_own_head_lanes(), jnp.broadcast_to(q_ref[...], (N_HEADS, w)), 0.0)
    qbd_s[...] = q_bd
    m_s[...] = jnp.sum(q_bd * kn_ref[...], axis=-1, keepdims=True)
    l_s[...] = jnp.ones_like(l_s)
    page_len = acc_s.shape[1]
    lane_p = lax.broadcasted_iota(jnp.int32, (w, page_len), 1)
    acc_s[...] = jnp.where(lane_p == 0, jnp.broadcast_to(vn_ref[...], (page_len, w)).T, 0.0)
    s128 = lax.broadcasted_iota(jnp.int32, (N_HEADS, V7X_LANES), 0)
    l128 = lax.broadcasted_iota(jnp.int32, (N_HEADS, V7X_LANES), 1)
    lfn = jnp.broadcast_to(lfn_ref[...], (N_HEADS, V7X_LANES))
    run_s[...] = jnp.sum(jnp.where(s128 == l128, lfn, 0.0), axis=-1, keepdims=True)


def _decode_fox_finish(o_ref, l_s, acc_s):
    tot = jnp.sum(acc_s[...].T, axis=0, keepdims=True)
    l_b = jnp.broadcast_to(l_s[...], (N_HEADS, HEAD_WIDTH))
    o_ref[...] = tot / jnp.sum(jnp.where(_own_head_lanes(), l_b, 0.0), axis=0, keepdims=True)


def _decode_fox_pages(k_refs, v_refs, lf_refs, qbd_s, m_s, l_s, run_s, acc_s):
    pages = len(k_refs)
    page_len = k_refs[0].shape[-1]
    q_b = qbd_s[...].astype(BF16)
    order = list(range(pages - 1, -1, -1))
    s = jnp.concatenate([_dot(q_b, k_refs[r][...].astype(BF16)) for r in order], axis=1)
    lf = jnp.concatenate([lf_refs[r][...] for r in order], axis=1)
    n = pages * page_len
    lane_n = lax.broadcasted_iota(jnp.int32, (N_HEADS, n), 1)
    suf = lf
    shift = 1
    while shift < n:
        suf = suf + jnp.where(lane_n < n - shift, pltpu.roll(suf, n - shift, 1), 0.0)
        shift *= 2
    run = run_s[...]
    s = s + ((suf - lf) + run)
    m = m_s[...]
    m_new = jnp.maximum(m, jnp.max(s, axis=-1, keepdims=True))
    a = jnp.exp(m - m_new)
    p = jnp.exp(s - m_new)
    l_s[...] = a * l_s[...] + jnp.sum(p, axis=-1, keepdims=True)
    m_s[...] = m_new
    for h in range(N_HEADS):
        sl = slice(h * HEAD_DIM, (h + 1) * HEAD_DIM)
        upd = None
        for idx, r in enumerate(order):
            t = p[h:h + 1, idx * page_len:(idx + 1) * page_len] * v_refs[r][sl, :]
            upd = t if upd is None else upd + t
        acc_s[sl, :] = a[h:h + 1, :] * acc_s[sl, :] + upd
    run_s[...] = run + suf[:, 0:1]


def _decode_ret_kernel(qt_ref, kt_ref, v_ref, sg_ref, s_ref, gl_ref, y_ref, so_ref):
    qt, kt = qt_ref[...], kt_ref[...]
    g = gl_ref[...]
    for b in range(qt.shape[-1]):
        q, k = qt[:, :, b:b + 1], kt[:, :, b:b + 1]
        v = v_ref[b]
        state = s_ref[b]
        inner = jnp.sum(q * k, axis=1, keepdims=True) * v
        cross = jnp.sum(q * state, axis=1, keepdims=True) * g
        so_ref[b] = g * state + k * v
        o = inner + cross
        mu = jnp.mean(o, axis=-1, keepdims=True)
        d = o - mu
        var = jnp.mean(d * d, axis=-1, keepdims=True)
        y_ref[b] = sg_ref[b] * (d * lax.rsqrt(var + GN_EPS))


def _decode_ret_call(q_t, k_t, v_row, sg_row, state, decay):
    nseq = state.shape[0]

    def whole(arr):
        nd = arr.ndim
        return pl.BlockSpec(arr.shape, lambda i: (0,) * nd)

    return pl.pallas_call(
        _decode_ret_kernel,
        grid=(1,),
        in_specs=[whole(a) for a in (q_t, k_t, v_row, sg_row, state, decay)],
        out_specs=[whole(v_row), whole(state)],
        out_shape=[jax.ShapeDtypeStruct(v_row.shape, F32), jax.ShapeDtypeStruct(state.shape, F32)],
        compiler_params=_cparams(("arbitrary",), big=True),
        name="ret_decode",
    )(q_t, k_t, v_row, sg_row, state, decay)


def _odd_sample_kernel(x_ref, sh_ref, sc_ref, g_ref, win_ref, cw_ref, b0_ref, b1_ref, wout_ref, lng_ref, lnb_ref,
                       o_ref, u_ref, *, alpha):
    x = x_ref[...]
    d = x.shape[-1]
    h = (x * (1.0 + sc_ref[...]) + sh_ref[...]).astype(BF16)
    b_gate = _dot(h, win_ref[:, 0:d])
    u = _dot(h, win_ref[:, d:2 * d]) * _dot(h, win_ref[:, 2 * d:3 * d])
    cw = cw_ref[...]
    z = cw[0:1, :] * b0_ref[...] + cw[1:2, :] * b1_ref[...] + cw[2:3, :] * u
    u_ref[...] = u
    y = _dot((b_gate * z).astype(BF16), wout_ref[...])
    o_ref[...] = _layernorm(alpha * x + g_ref[...] * y, lng_ref[...], lnb_ref[...])


def _odd_sample_call(x, rows, ks, w_in, conv_w, buf0, buf1, w_out, ln_g, ln_b, ln_idx, alpha):
    n, d = x.shape
    full = pl.BlockSpec((n, d), lambda i: (0, 0))
    return pl.pallas_call(
        functools.partial(_odd_sample_kernel, alpha=alpha),
        grid=(1,),
        in_specs=[rows.row_spec(d), rows.mod_spec(ks[0]), rows.mod_spec(ks[1]), rows.mod_spec(ks[2]),
                  _const_spec(w_in), _const_spec(conv_w), full, full, _const_spec(w_out),
                  pl.BlockSpec((None, 1, d), lambda i: (ln_idx, 0, 0)),
                  pl.BlockSpec((None, 1, d), lambda i: (ln_idx, 0, 0))],
        out_specs=[full, full],
        out_shape=[jax.ShapeDtypeStruct((n, d), F32), jax.ShapeDtypeStruct((n, d), F32)],
        compiler_params=_cparams(("arbitrary",), big=True),
        name="odd_sample",
    )(x, rows.mods, rows.mods, rows.mods, w_in, conv_w, buf0, buf1, w_out, ln_g, ln_b)


def _rotary_tables(pos):
    half = HEAD_DIM // 2
    inv = 1.0 / (RET_ANGLE_BASE ** np.linspace(0.0, 1.0, half))
    ang = np.asarray(pos, np.float64)[:, None] * inv[None, :]
    return np.cos(ang), np.sin(ang)


def _token_major_tables(cos, sin):
    cos_h = np.concatenate([cos, cos], axis=1)
    sin_h = np.concatenate([-sin, sin], axis=1)
    return (jnp.asarray(np.tile(cos_h, (1, N_HEADS)), F32), jnp.asarray(np.tile(sin_h, (1, N_HEADS)), F32))


def kernel(x_prompt, x_sample, cache_k, cache_v, cache_logf, state_ret, state_conv, page_table, c_prompt, c_sample,
           w_ada, b_ada, w_ffn_in, w_ffn_out, ln_g, ln_b, w_in_even, b_forget, w_out_even, w_in_odd, conv_w,
           w_out_odd):
    batch, seq, d = x_prompt.shape
    nseq = x_sample.shape[0]
    depth = w_ada.shape[0]
    past_len = page_table.shape[1] * cache_k.shape[2]
    alpha = (2.0 * depth) ** 0.25
    w = HEAD_WIDTH
    tm = ROW_BLOCK
    assert seq % tm == 0 and x_sample.shape[1] == 1 and d % V7X_LANES == 0
    assert cache_k.shape[3] == N_HEADS and cache_k.shape[4] == HEAD_DIM

    n_c = batch + nseq
    pad = (-n_c) % 8
    c_all = jnp.concatenate([c_prompt, c_sample, jnp.zeros((pad, d), F32)], axis=0)
    mods = _ada_call(c_all, w_ada, b_ada)
    mods_p = mods[:, :batch].reshape(depth, batch, 1, N_MOD * d)
    mods_s = mods[:, batch:n_c]

    ln_g3 = ln_g.reshape(depth * 3, 1, d)
    ln_b3 = ln_b.reshape(depth * 3, 1, d)
    w_ffn_in_b = w_ffn_in.astype(BF16)
    w_ffn_out_b = w_ffn_out.astype(BF16)

    xp = x_prompt.reshape(batch * seq, d)
    xs = x_sample.reshape(nseq, d)

    cos_p, sin_p = _rotary_tables(np.arange(seq))
    rot_p = _token_major_tables(cos_p, sin_p) + (jnp.asarray(cos_p.T, F32), jnp.asarray(sin_p.T, F32))
    cos_s, sin_s = (jnp.asarray(t.T, F32) for t in _rotary_tables(np.full((nseq,), past_len)))

    log_decay = jnp.log(1.0 - 2.0 ** (-5.0 - jnp.arange(N_HEADS, dtype=F32)))
    lg_pairs = jnp.repeat(log_decay.reshape(HEAD_PAIRS, 2), HEAD_DIM, axis=1)
    lg_lane = lg_pairs.reshape(HEAD_PAIRS, 1, 2 * HEAD_DIM)
    lg_row = lg_pairs.reshape(HEAD_PAIRS, 2 * HEAD_DIM, 1)
    step_decay = jnp.exp(log_decay).reshape(N_HEADS, 1, 1)

    outs_p = {"k": [], "v": [], "lf": [], "ret": [], "conv": []}
    outs_s = {"k": [], "v": [], "lf": [], "ret": [], "conv": []}

    assert depth == 2 and w_in_even.shape[0] == 1 and cache_k.shape[0] == 1
    n_hosts = 2 * depth
    assert nseq % n_hosts == 0
    assert w_in_even.shape[2] == EVEN_IN_ROWS
    w_even_t = jnp.transpose(w_in_even[0])
    w_out_even_b = w_out_even[0].astype(BF16)
    bf = b_forget[0]
    w_in_odd_b = w_in_odd[0].astype(BF16)
    w_out_odd_b = w_out_odd[0].astype(BF16)

    def rows_s(l):
        return _Rows(mods_s, l, nseq, None, d)

    def rows_p(l):
        return _Rows(mods_p, l, tm, seq // tm, d)

    def ffn(x, rows, half, l, decode=None, mixer_out=None):
        ks = (0, 1, 2) if half == 0 else (6, 7, 8)
        return _ffn_call(x, rows, ks, w_ffn_in_b, w_ffn_out_b, (l, half), ln_g3, ln_b3, 3 * l + 2 * half, alpha,
                         decode, mixer_out)

    xs = ffn(xs, rows_s(0), 0, 0)
    bf_row = jnp.concatenate([bf, jnp.zeros((V7X_LANES - N_HEADS,), F32)]).reshape(1, V7X_LANES)
    (qs, ks_, vs, qrs_t, krs_t, vrs, sgs, lfs) = _even_in_sample_call(xs, rows_s(0), (3, 4), w_even_t, bf_row,
                                                                       cos_s, sin_s)
    n_phys, page_len = cache_k.shape[1], cache_k.shape[2]
    kt_pages = jnp.transpose(cache_k[0], (0, 2, 3, 1)).reshape(n_phys, w, page_len)
    vt_pages = jnp.transpose(cache_v[0], (0, 2, 3, 1)).reshape(n_phys, w, page_len)
    lft_pages = jnp.transpose(cache_logf[0], (0, 2, 1))
    per_host = nseq // n_hosts
    dec_args = (page_table, qs.reshape(nseq, 1, w), ks_.reshape(nseq, 1, w), vs.reshape(nseq, 1, w),
                lfs.reshape(nseq, 1, V7X_LANES), kt_pages, vt_pages, lft_pages)
    dec_out = []

    def ffn_host(x, half, l, mixer_out=None):
        x, o = ffn(x, rows_p(l), half, l, dec_args + (len(dec_out) * per_host, per_host), mixer_out)
        dec_out.append(o)
        return x

    xp = ffn_host(xp, 0, 0)
    (q, qr, vr, sg, kt, vt, ktc, vtc, krt, lft) = _even_in_call(
        xp, rows_p(0), (3, 4), w_even_t, bf.reshape(N_HEADS, 1), rot_p, batch, seq)
    c_t = _cumsum_call(lft)
    of = _fox_call(q, ktc, vtc, c_t.reshape(batch, N_HEADS, seq // tm, tm), batch, seq)
    yr, s_new = _ret_call(qr, krt, vr, sg, lg_lane, lg_row, batch, seq)
    outs_p["k"].append(jnp.transpose(kt.reshape(batch, N_HEADS, HEAD_DIM, seq), (0, 3, 1, 2)))
    outs_p["v"].append(jnp.transpose(vt.reshape(batch, N_HEADS, HEAD_DIM, seq), (0, 3, 1, 2)))
    outs_p["lf"].append(jnp.transpose(lft, (0, 2, 1)))
    outs_p["ret"].append(s_new)
    xp = ffn_host(xp, 1, 0, mixer_out=(5, of, yr, w_out_even_b, 1))
    xp = ffn_host(xp, 0, 1)
    xp, conv_p = _odd_prompt_call(xp, rows_p(1), (3, 4, 5), w_in_odd_b, conv_w[0], w_out_odd_b, ln_g3, ln_b3,
                                  4, alpha, batch, seq)
    outs_p["conv"].append(conv_p)
    xp = ffn_host(xp, 1, 1)

    of_s = jnp.concatenate(dec_out, axis=0).reshape(nseq, w)
    lfs = lfs[:, :N_HEADS]
    hdn = (N_HEADS, HEAD_DIM, nseq)
    rw = (nseq, N_HEADS, 1, HEAD_DIM)
    yr_s, s_new_s = _decode_ret_call(qrs_t.reshape(hdn), krs_t.reshape(hdn), vrs.reshape(rw), sgs.reshape(rw),
                                     state_ret[0], step_decay)
    xs = _outproj_call(xs, rows_s(0), 5, of_s, yr_s.reshape(nseq, w), w_out_even_b, ln_g3, ln_b3, 1, alpha)
    outs_s["k"].append(ks_.reshape(nseq, 1, N_HEADS, HEAD_DIM))
    outs_s["v"].append(vs.reshape(nseq, 1, N_HEADS, HEAD_DIM))
    outs_s["lf"].append(lfs.reshape(nseq, 1, N_HEADS))
    outs_s["ret"].append(s_new_s)
    xs = ffn(xs, rows_s(0), 1, 0)
    xs = ffn(xs, rows_s(1), 0, 1)
    buf = state_conv[0]
    xs, u_s = _odd_sample_call(xs, rows_s(1), (3, 4, 5), w_in_odd_b, conv_w[0], buf[:, 0], buf[:, 1], w_out_odd_b,
                               ln_g3, ln_b3, 4, alpha)
    outs_s["conv"].append(jnp.stack([buf[:, 1], u_s], axis=1))
    xs = ffn(xs, rows_s(1), 1, 1)

    def stk(lst):
        return jnp.stack(lst)

    return (xp.reshape(batch, seq, d), xs.reshape(nseq, 1, d),
            stk(outs_p["k"]), stk(outs_p["v"]), stk(outs_p["lf"]), stk(outs_p["ret"]), stk(outs_p["conv"]),
            stk(outs_s["k"]), stk(outs_s["v"]), stk(outs_s["lf"]), stk(outs_s["ret"]), stk(outs_s["conv"]))
```

```python
import functools

import jax
import jax.numpy as jnp
import numpy as np
from jax import lax
from jax.experimental import pallas as pl
from jax.experimental.pallas import tpu as pltpu

F32 = jnp.float32
BF16 = jnp.bfloat16

HEAD_DIM = 64
N_HEADS = 8
HEAD_WIDTH = N_HEADS * HEAD_DIM
N_MOD = 9
CONV_WIDTH = 3
RET_ANGLE_BASE = 10000.0
LN_EPS = 1e-5
GN_EPS = 1e-6
QK_SCALE = HEAD_DIM ** -0.5
LOG2E = 1.4426950408889634

V7X_LANES = 128
V7X_VMEM_LIMIT_BYTES = 56 * 1024 * 1024

ROW_BLOCK = 512
FF_CHUNK = 256
FOX_BLOCKS_PER_UPDATE = 8
RET_CHUNK = 256
ADA_COL_BLOCK = 1152
HEAD_PAIRS = N_HEADS // 2


def _cparams(sem, big=False):
    return pltpu.CompilerParams(dimension_semantics=sem,
                                vmem_limit_bytes=V7X_VMEM_LIMIT_BYTES if big else None)


def _const_spec(arr):
    nd = arr.ndim
    return pl.BlockSpec(arr.shape, lambda *_: (0,) * nd, pipeline_mode=pl.Buffered(1))


def _layernorm(z, g, b):
    mu = jnp.mean(z, axis=-1, keepdims=True)
    d = z - mu
    var = jnp.mean(d * d, axis=-1, keepdims=True)
    return d * lax.rsqrt(var + LN_EPS) * g + b


def _silu(a):
    return a * jax.nn.sigmoid(a)


def _log_sigmoid(z):
    return jnp.minimum(z, 0.0) - jnp.log1p(jnp.exp(-jnp.abs(z)))


def _dot(a, b):
    return jnp.dot(a, b, preferred_element_type=F32)


def _dot_nt(a, b):
    return lax.dot_general(a, b, (((1,), (1,)), ((), ())), preferred_element_type=F32)


def _ada_kernel(c_ref, w_ref, b_ref, o_ref):
    s = _silu(c_ref[...]).astype(BF16)
    o_ref[...] = _dot(s, w_ref[...].astype(BF16)) + b_ref[...]


def _ada_call(c_all, w_ada, b_ada):
    depth, d, nm = w_ada.shape
    rows = c_all.shape[0]
    tn = ADA_COL_BLOCK
    return pl.pallas_call(
        _ada_kernel,
        grid=(depth, nm // tn),
        in_specs=[pl.BlockSpec((rows, d), lambda l, j: (0, 0)),
                  pl.BlockSpec((None, d, tn), lambda l, j: (l, 0, j)),
                  pl.BlockSpec((None, 1, tn), lambda l, j: (l, 0, j))],
        out_specs=pl.BlockSpec((None, rows, tn), lambda l, j: (l, 0, j)),
        out_shape=jax.ShapeDtypeStruct((depth, rows, nm), F32),
        compiler_params=_cparams(("arbitrary", "arbitrary"), big=True),
        name="ada_mods",
    )(c_all, w_ada, b_ada.reshape(depth, 1, nm))


class _Rows:
    def __init__(self, mods, layer, tm, blocks_per_seq, d):
        self.mods, self.layer, self.tm, self.bps, self.d = mods, layer, tm, blocks_per_seq, d

    def mod_spec(self, k):
        l, d = self.layer, self.d
        if self.bps is None:
            return pl.BlockSpec((None, self.tm, d), lambda i, *_: (l, 0, k))
        bps = self.bps
        return pl.BlockSpec((None, None, 1, d), lambda i, *_: (l, i // bps, 0, k))

    def row_spec(self, width):
        return pl.BlockSpec((self.tm, width), lambda i, *_: (i, 0))


def _mixer_out_math(x, gate, a0, a1, w_ref, lng, lnb, alpha):
    half = a0.shape[-1]
    y = _dot(a0.astype(BF16), w_ref[0:half, :]) + _dot(a1.astype(BF16), w_ref[half:, :])
    return _layernorm(alpha * x + gate * y, lng, lnb)


def _ffn_rows(refs, ff, alpha, pre):
    x = refs[0][...]
    refs = refs[1:]
    if pre:
        gate, a0, a1, wo_ref, lng1, lnb1 = refs[:6]
        x = _mixer_out_math(x, gate[...], a0[...], a1[...], wo_ref, lng1[...], lnb1[...], alpha)
        refs = refs[6:]
    sh_ref, sc_ref, g_ref, win_ref, wout_ref, lng_ref, lnb_ref = refs
    h = (x * (1.0 + sc_ref[...]) + sh_ref[...]).astype(BF16)
    acc = None
    for c in range(ff // FF_CHUNK):
        lo, hi = c * FF_CHUNK, (c + 1) * FF_CHUNK
        a = _dot(h, win_ref[:, lo:hi])
        b = _dot(h, win_ref[:, ff + lo:ff + hi])
        y = _dot((_silu(a) * b).astype(BF16), wout_ref[lo:hi, :])
        acc = y if acc is None else acc + y
    z = alpha * x + (0.5 * g_ref[...]) * acc
    return _layernorm(z, lng_ref[...], lnb_ref[...])


def _ffn_kernel(*refs, ff, alpha, pre):
    refs[-1][...] = _ffn_rows(refs[:-1], ff, alpha, pre)


def _ffn_decode_kernel(pt_ref, *refs, ff, alpha, pre, pages, steps_per_seq, seq0):
    n_ffn = 14 if pre else 8
    ffn_in, rest = refs[:n_ffn], refs[n_ffn:]
    q_ref, kn_ref, vn_ref, lfn_ref, kt_hbm, vt_hbm, lf_hbm, o_ref, dec_ref = rest[:9]
    state = rest[9:14]
    kbuf, vbuf, lfbuf, sem = rest[14:]
    npg = pt_ref.shape[1]
    i = pl.program_id(0)
    last = pl.num_programs(0) - 1
    slot = i % 2

    def page_copies(step, into):
        seq = seq0 + step // steps_per_seq
        first = npg - 1 - (step % steps_per_seq) * pages
        out = []
        for r in range(pages):
            pg = pt_ref[seq, first - r]
            out += [pltpu.make_async_copy(kt_hbm.at[pg], kbuf.at[into, r], sem.at[into, 0]),
                    pltpu.make_async_copy(vt_hbm.at[pg], vbuf.at[into, r], sem.at[into, 1]),
                    pltpu.make_async_copy(lf_hbm.at[pg], lfbuf.at[into, r], sem.at[into, 2])]
        return out

    @pl.when(i == 0)
    def _():
        for c in page_copies(0, 0):
            c.start()

    nxt = jnp.minimum(i + 1, last)
    for c in page_copies(nxt, 1 - slot):
        c.start()
    for c in page_copies(i, slot):
        c.wait()

    g = i % steps_per_seq
    pl.when(g == 0)(lambda: _decode_fox_init(q_ref, kn_ref, vn_ref, lfn_ref, *state))
    _decode_fox_pages([kbuf.at[slot, r] for r in range(pages)], [vbuf.at[slot, r] for r in range(pages)],
                      [lfbuf.at[slot, r] for r in range(pages)], *state)
    o_ref[...] = _ffn_rows(ffn_in, ff, alpha, pre)
    pl.when(g == steps_per_seq - 1)(lambda: _decode_fox_finish(dec_ref, state[2], state[4]))

    @pl.when(i == last)
    def _():
        for c in page_copies(nxt, 1 - slot):
            c.wait()


def _ffn_call(x, rows, ks, w_in, w_out, which, ln_g, ln_b, ln_idx, alpha, decode=None, mixer_out=None):
    n, d = x.shape
    ff = w_out.shape[2]
    l, j = which
    steps = n // rows.tm

    def w_spec(arr):
        return pl.BlockSpec((None, None) + arr.shape[2:], lambda i, *_: (l, j, 0, 0), pipeline_mode=pl.Buffered(1))

    def ln_spec(idx):
        return pl.BlockSpec((None, 1, d), lambda i, *_: (idx, 0, 0))

    in_specs = [rows.row_spec(d)]
    args = [x]
    if mixer_out is not None:
        k_gate, a0, a1, w_o, ln_idx1 = mixer_out
        in_specs += [rows.mod_spec(k_gate), rows.row_spec(a0.shape[1]), rows.row_spec(a1.shape[1]), _const_spec(w_o),
                     ln_spec(ln_idx1), ln_spec(ln_idx1)]
        args += [rows.mods, a0, a1, w_o, ln_g, ln_b]
    in_specs += [rows.mod_spec(ks[0]), rows.mod_spec(ks[1]), rows.mod_spec(ks[2]), w_spec(w_in), w_spec(w_out),
                 ln_spec(ln_idx), ln_spec(ln_idx)]
    args += [rows.mods, rows.mods, rows.mods, w_in, w_out, ln_g, ln_b]
    pre = mixer_out is not None
    x_shape = jax.ShapeDtypeStruct((n, d), F32)
    if decode is None:
        return pl.pallas_call(
            functools.partial(_ffn_kernel, ff=ff, alpha=alpha, pre=pre),
            grid=(steps,), in_specs=in_specs, out_specs=rows.row_spec(d), out_shape=x_shape,
            compiler_params=_cparams(("parallel",), big=True),
            name="ffn",
        )(*args)

    page_table, q, kn, vn, lfn, kt_pages, vt_pages, lft_pages, seq0, n_seq = decode
    npg = page_table.shape[1]
    w, page_len = kt_pages.shape[1], kt_pages.shape[2]
    steps_per_seq = steps // n_seq
    pages = npg // steps_per_seq
    assert steps_per_seq * n_seq == steps and pages * steps_per_seq == npg

    row = pl.BlockSpec((None, 1, w), lambda i, pt: (seq0 + i // steps_per_seq, 0, 0))
    hbm = pl.BlockSpec(memory_space=pl.ANY)
    grid_spec = pltpu.PrefetchScalarGridSpec(
        num_scalar_prefetch=1,
        grid=(steps,),
        in_specs=in_specs
        + [row, row, row, pl.BlockSpec((None, 1, V7X_LANES), lambda i, pt: (seq0 + i // steps_per_seq, 0, 0))]
        + [hbm, hbm, hbm],
        out_specs=[rows.row_spec(d), pl.BlockSpec((None, 1, w), lambda i, pt: (i // steps_per_seq, 0, 0))],
        scratch_shapes=[pltpu.VMEM((N_HEADS, w), F32), pltpu.VMEM((N_HEADS, 1), F32), pltpu.VMEM((N_HEADS, 1), F32),
                        pltpu.VMEM((N_HEADS, 1), F32), pltpu.VMEM((w, page_len), F32),
                        pltpu.VMEM((2, pages, w, page_len), F32), pltpu.VMEM((2, pages, w, page_len), F32),
                        pltpu.VMEM((2, pages, N_HEADS, page_len), F32), pltpu.SemaphoreType.DMA((2, 3))],
    )
    return pl.pallas_call(
        functools.partial(_ffn_decode_kernel, ff=ff, alpha=alpha, pre=pre, pages=pages, steps_per_seq=steps_per_seq,
                          seq0=seq0),
        grid_spec=grid_spec,
        out_shape=[x_shape, jax.ShapeDtypeStruct((n_seq, 1, w), F32)],
        compiler_params=_cparams(("arbitrary",), big=True),
        name="ffn_decode",
    )(page_table, *args, q, kn, vn, lfn, kt_pages, vt_pages, lft_pages)


def _outproj_kernel(x_ref, g_ref, a0_ref, a1_ref, w_ref, lng_ref, lnb_ref, o_ref, *, alpha):
    o_ref[...] = _mixer_out_math(x_ref[...], g_ref[...], a0_ref[...], a1_ref[...], w_ref, lng_ref[...], lnb_ref[...],
                                 alpha)


def _outproj_call(x, rows, k_gate, a0, a1, w_out, ln_g, ln_b, ln_idx, alpha):
    n, d = x.shape
    return pl.pallas_call(
        functools.partial(_outproj_kernel, alpha=alpha),
        grid=(n // rows.tm,),
        in_specs=[rows.row_spec(d), rows.mod_spec(k_gate), rows.row_spec(a0.shape[1]), rows.row_spec(a1.shape[1]),
                  _const_spec(w_out),
                  pl.BlockSpec((None, 1, d), lambda i: (ln_idx, 0, 0)),
                  pl.BlockSpec((None, 1, d), lambda i: (ln_idx, 0, 0))],
        out_specs=rows.row_spec(d),
        out_shape=jax.ShapeDtypeStruct((n, d), F32),
        compiler_params=_cparams(("parallel",), big=True),
        name="mixer_outproj",
    )(x, rows.mods, a0, a1, w_out, ln_g, ln_b)


def _rotate_token_major(a, cos, sin_signed):
    width = a.shape[-1]
    lane = lax.broadcasted_iota(jnp.int32, a.shape, 1)
    first_half = (lane & (HEAD_DIM // 2)) == 0
    partner = jnp.where(first_half, pltpu.roll(a, width - HEAD_DIM // 2, 1), pltpu.roll(a, HEAD_DIM // 2, 1))
    return a * cos + partner * sin_signed


_EVEN_ROWS = {}
_r0 = 0
for _name, _n in (("qf", HEAD_WIDTH), ("kf", HEAD_WIDTH), ("vf", HEAD_WIDTH), ("fg", N_HEADS),
                  ("qr", HEAD_WIDTH), ("kr", HEAD_WIDTH), ("vr", HEAD_WIDTH), ("gr", HEAD_WIDTH)):
    _EVEN_ROWS[_name] = (_r0, _r0 + _n)
    _r0 += _n
EVEN_IN_ROWS = _r0
FG_ROWS = 16


def _even_in_kernel(x_ref, sh_ref, sc_ref, w_ref, bf_ref, cos_ref, sin_ref, cost_ref, sint_ref,
                    q_o, qr_o, vr_o, sg_o, kt_o, vt_o, ktc_o, vtc_o, krt_o, lft_o, wtok_ref, wt_ref):
    w = HEAD_WIDTH

    @pl.when(pl.program_id(0) == 0)
    def _():
        for k, name in enumerate(("qf", "qr", "vr", "gr")):
            lo, hi = _EVEN_ROWS[name]
            wtok_ref[:, k * w:(k + 1) * w] = w_ref[lo:hi, :].T.astype(BF16)
        for k, name in enumerate(("kf", "vf", "kr")):
            lo, hi = _EVEN_ROWS[name]
            wt_ref[k * w:(k + 1) * w, :] = w_ref[lo:hi, :].astype(BF16)
        lo, hi = _EVEN_ROWS["fg"]
        fg_rows = jnp.concatenate([w_ref[lo:hi, :], jnp.zeros((FG_ROWS - N_HEADS, w_ref.shape[1]), F32)], axis=0)
        wt_ref[3 * w:, :] = fg_rows.astype(BF16)

    x = x_ref[...]
    h = (x * (1.0 + sc_ref[...]) + sh_ref[...]).astype(BF16)

    q_o[...] = (_dot(h, wtok_ref[:, 0:w]) * (QK_SCALE * LOG2E)).astype(BF16)
    qr = _dot(h, wtok_ref[:, w:2 * w])
    qr_o[...] = _rotate_token_major(qr, cos_ref[...], sin_ref[...]).astype(BF16)
    vr_o[...] = _dot(h, wtok_ref[:, 2 * w:3 * w]).astype(BF16)
    sg_o[...] = _silu(_dot(h, wtok_ref[:, 3 * w:4 * w])).astype(BF16)

    kt = _dot_nt(wt_ref[0:w, :], h)
    kt_o[...] = kt
    ktc_o[...] = kt.astype(BF16)
    vt = _dot_nt(wt_ref[w:2 * w, :], h)
    vt_o[...] = vt
    vtc_o[...] = vt.astype(BF16)

    krt = _dot_nt(wt_ref[2 * w:3 * w, :], h)
    cos_t, sin_t = cost_ref[...], sint_ref[...]
    half = HEAD_DIM // 2
    for hh in range(N_HEADS):
        x1 = krt[hh * HEAD_DIM:hh * HEAD_DIM + half, :]
        x2 = krt[hh * HEAD_DIM + half:(hh + 1) * HEAD_DIM, :]
        krt_o[hh * HEAD_DIM:hh * HEAD_DIM + half, :] = (x1 * cos_t - x2 * sin_t) * QK_SCALE
        krt_o[hh * HEAD_DIM + half:(hh + 1) * HEAD_DIM, :] = (x1 * sin_t + x2 * cos_t) * QK_SCALE

    fg = _dot_nt(wt_ref[3 * w:, :], h)[0:N_HEADS, :] + bf_ref[...]
    lft_o[...] = _log_sigmoid(fg)


def _even_in_call(x, rows, ks, w_t_f32, bf_col, rot, batch, seq):
    n, d = x.shape
    tm = rows.tm
    bps = seq // tm
    w = HEAD_WIDTH
    cos_tok, sin_tok, cos_t, sin_t = rot
    tok_out = pl.BlockSpec((tm, w), lambda i: (i, 0))
    t_out = pl.BlockSpec((None, w, tm), lambda i: (i // bps, 0, i % bps))
    tc_out = pl.BlockSpec((None, None, w, tm), lambda i: (i // bps, i % bps, 0, 0))
    tok_shape = jax.ShapeDtypeStruct((n, w), BF16)
    return pl.pallas_call(
        _even_in_kernel,
        grid=(n // tm,),
        in_specs=[rows.row_spec(d), rows.mod_spec(ks[0]), rows.mod_spec(ks[1]),
                  _const_spec(w_t_f32), _const_spec(bf_col),
                  pl.BlockSpec((tm, w), lambda i: (i % bps, 0)),
                  pl.BlockSpec((tm, w), lambda i: (i % bps, 0)),
                  pl.BlockSpec((HEAD_DIM // 2, tm), lambda i: (0, i % bps)),
                  pl.BlockSpec((HEAD_DIM // 2, tm), lambda i: (0, i % bps))],
        out_specs=[tok_out, tok_out, tok_out, tok_out, t_out, t_out, tc_out, tc_out, tc_out,
                   pl.BlockSpec((None, N_HEADS, tm), lambda i: (i // bps, 0, i % bps))],
        out_shape=[tok_shape, tok_shape, tok_shape, tok_shape,
                   jax.ShapeDtypeStruct((batch, w, seq), F32),
                   jax.ShapeDtypeStruct((batch, w, seq), F32),
                   jax.ShapeDtypeStruct((batch, bps, w, tm), BF16),
                   jax.ShapeDtypeStruct((batch, bps, w, tm), BF16),
                   jax.ShapeDtypeStruct((batch, bps, w, tm), F32),
                   jax.ShapeDtypeStruct((batch, N_HEADS, seq), F32)],
        scratch_shapes=[pltpu.VMEM((d, 4 * w), BF16), pltpu.VMEM((3 * w + FG_ROWS, d), BF16)],
        compiler_params=_cparams(("arbitrary",), big=True),
        name="even_inproj",
    )(x, rows.mods, rows.mods, w_t_f32, bf_col, cos_tok, sin_tok, cos_t, sin_t)


def _cumsum_kernel(x_ref, o_ref):
    x = x_ref[...]
    n = x.shape[-1]
    lane = lax.broadcasted_iota(jnp.int32, x.shape, 1)
    shift = 1
    while shift < n:
        x = x + jnp.where(lane >= shift, pltpu.roll(x, shift, 1), 0.0)
        shift *= 2
    o_ref[...] = x * LOG2E


def _cumsum_call(lft):
    batch, heads, seq = lft.shape
    spec = pl.BlockSpec((None, heads, seq), lambda b: (b, 0, 0))
    return pl.pallas_call(
        _cumsum_kernel, grid=(batch,), in_specs=[spec], out_specs=spec,
        out_shape=jax.ShapeDtypeStruct(lft.shape, F32),
        compiler_params=_cparams(("parallel",)),
        name="logf_cumsum",
    )(lft)


def _split3(x):
    hi = x.astype(BF16).astype(F32)
    r = x - hi
    mid = r.astype(BF16).astype(F32)
    lo = (r - mid).astype(BF16).astype(F32)
    return hi, mid, lo


def _fox_kernel(q_ref, kt_ref, vt_ref, c_ref, o_ref):
    tq = q_ref.shape[0]
    tk = kt_ref.shape[-1]
    pw = 2 * HEAD_DIM
    slab_rows = 16
    i = pl.program_id(2)
    q = q_ref[...]
    lane = lax.broadcasted_iota(jnp.int32, (tq, pw), 1)
    q_aug = []
    for hd in range(2):
        base = HEAD_DIM if hd == 0 else 0
        hi, mid, lo = _split3(c_ref[hd, pl.ds(i, 1), :][:, 0:1])
        aug = jnp.where(lane == base + 3, hi, jnp.where(lane == base + 4, mid, jnp.where(lane == base + 5, lo,
              jnp.where((lane >= base) & (lane < base + 3), 1.0, 0.0))))
        own = (lane < HEAD_DIM) if hd == 0 else (lane >= HEAD_DIM)
        q_aug.append(jnp.where(own, q, aug.astype(BF16)))
    r16 = lax.broadcasted_iota(jnp.int32, (slab_rows, tk), 0)
    rowv = lax.broadcasted_iota(jnp.int32, (pw, tk), 0)

    def scores(j, hd):
        kt = kt_ref[j]
        hi, mid, lo = _split3(-c_ref[hd, pl.ds(j, 1), :])
        slab = jnp.where(r16 == 0, hi, jnp.where(r16 == 1, mid, jnp.where(r16 == 2, lo,
               jnp.where(r16 < 6, 1.0, 0.0)))).astype(BF16)
        if hd == 0:
            kt_aug = jnp.concatenate([kt[0:HEAD_DIM], slab, kt[HEAD_DIM + slab_rows:]], axis=0)
        else:
            kt_aug = jnp.concatenate([slab, kt[slab_rows:]], axis=0)
        return _dot(q_aug[hd], kt_aug)

    def values(j, hd):
        vt = vt_ref[j]
        return jnp.where(rowv == (HEAD_DIM if hd == 0 else 0), jnp.ones_like(vt), vt)

    def update(s, vt_aug, m, acc):
        m_new = jnp.maximum(m, jnp.max(s, axis=-1, keepdims=True))
        return m_new, jnp.exp2(m - m_new) * acc + _dot_nt(jnp.exp2(s - m_new).astype(BF16), vt_aug)

    def diag_scores(hd):
        row = lax.broadcasted_iota(jnp.int32, (tq, tk), 0)
        col = lax.broadcasted_iota(jnp.int32, (tq, tk), 1)
        return jnp.where(row >= col, scores(i, hd), -jnp.inf)

    group = FOX_BLOCKS_PER_UPDATE

    def joint(carry, full_blocks, with_diag):
        out = []
        for hd in range(2):
            s = [scores(j, hd) for j in full_blocks] + ([diag_scores(hd)] if with_diag else [])
            v = [values(j, hd) for j in full_blocks] + ([values(i, hd)] if with_diag else [])
            out.append(update(jnp.concatenate(s, axis=1), jnp.concatenate(v, axis=1), *carry[hd]))
        return tuple(out)

    def tail(n_full):
        return lambda carry: tuple(c[1] for c in joint(carry, [i - n_full + k for k in range(n_full)], True))

    init_head = (jnp.full((tq, 1), -jnp.inf, F32), jnp.zeros((tq, pw), F32))
    carry = lax.fori_loop(0, lax.shift_right_logical(i, group.bit_length() - 1),
                          lambda t, c: joint(c, [group * t + k for k in range(group)], False),
                          (init_head, init_head))
    outs = lax.switch(i & (group - 1), [tail(n) for n in range(group)], carry)
    o0 = outs[0] / outs[0][:, HEAD_DIM:HEAD_DIM + 1]
    o1 = outs[1] / outs[1][:, 0:1]
    o_ref[...] = jnp.where(lane < HEAD_DIM, o0, o1).astype(BF16)


def _fox_call(q, ktc, vtc, c4, batch, seq):
    n, w = q.shape
    tq = ktc.shape[-1]
    nq = seq // tq
    pw = 2 * HEAD_DIM
    return pl.pallas_call(
        _fox_kernel,
        grid=(batch, HEAD_PAIRS, nq),
        in_specs=[pl.BlockSpec((tq, pw), lambda b, p, i: (b * nq + i, p)),
                  pl.BlockSpec((None, nq, pw, tq), lambda b, p, i: (b, 0, p, 0)),
                  pl.BlockSpec((None, nq, pw, tq), lambda b, p, i: (b, 0, p, 0)),
                  pl.BlockSpec((None, 2, nq, tq), lambda b, p, i: (b, p, 0, 0))],
        out_specs=pl.BlockSpec((tq, pw), lambda b, p, i: (b * nq + i, p)),
        out_shape=jax.ShapeDtypeStruct((n, w), BF16),
        compiler_params=_cparams(("parallel", "parallel", "arbitrary"), big=True),
        name="fox_prompt",
    )(q, ktc, vtc, c4)


def _ret_kernel(q_ref, kt_ref, v_ref, sg_ref, lgl_ref, lgr_ref, y_ref, s_ref):
    chunk = RET_CHUNK
    pw = 2 * HEAD_DIM
    nkb, _, tb = kt_ref.shape
    lg_lane = lgl_ref[...]
    lg_row = lgr_ref[...]
    lg_a, lg_b = lg_lane[:, 0:1], lg_lane[:, HEAD_DIM:HEAD_DIM + 1]

    ri = lax.broadcasted_iota(jnp.int32, (2 * chunk, chunk), 0)
    cj = lax.broadcasted_iota(jnp.int32, (2 * chunk, chunk), 1)
    first = ri < chunk
    diff = (jnp.where(first, ri, ri - chunk) - cj).astype(F32)
    decay_mask = jnp.where(diff >= 0, jnp.exp(jnp.maximum(diff, 0.0) * jnp.where(first, lg_a, lg_b)), 0.0)
    jj = lax.broadcasted_iota(jnp.int32, (pw, chunk), 1).astype(F32)
    col_decay = jnp.exp((chunk - 1.0 - jj) * lg_row)
    ii = lax.broadcasted_iota(jnp.int32, (chunk, pw), 0).astype(F32)
    row_decay = jnp.exp((ii + 1.0) * lg_lane)
    chunk_decay = jnp.exp(float(chunk) * lg_row)
    r2 = lax.broadcasted_iota(jnp.int32, (pw, pw), 0)
    c2 = lax.broadcasted_iota(jnp.int32, (pw, pw), 1)
    same_head = (r2 < HEAD_DIM) == (c2 < HEAD_DIM)
    seg_avg = jnp.where(same_head, 1.0 / HEAD_DIM, 0.0).astype(BF16)
    lane = lax.broadcasted_iota(jnp.int32, (chunk, pw), 1)

    def seg_mean(a):
        return _dot(a.astype(BF16), seg_avg)

    def one_chunk(q, kt, v, sg, state):
        zero = jnp.zeros_like(q)
        q_stack = jnp.concatenate([jnp.where(lane < HEAD_DIM, q, zero), jnp.where(lane >= HEAD_DIM, q, zero)], axis=0)
        qk = _dot(q_stack, kt.astype(BF16)) * decay_mask
        kd = (kt * col_decay).astype(BF16)
        r = _dot(jnp.concatenate([qk.astype(BF16), kd], axis=0), v)
        inner = jnp.where(lane < HEAD_DIM, r[0:chunk], r[chunk:2 * chunk])
        update = jnp.where(same_head, r[2 * chunk:], 0.0)
        cross = _dot(q, state.astype(BF16)) * row_decay
        o = inner + cross
        mu = seg_mean(o)
        d = o - mu
        var = seg_mean(d * d)
        y = sg.astype(F32) * (d * lax.rsqrt(var + GN_EPS))
        return y.astype(BF16), chunk_decay * state + update

    def body(jb, state):
        kt_blk = kt_ref[jb]
        for sub in range(tb // chunk):
            t0 = pl.multiple_of(jb * tb + sub * chunk, chunk)
            y, state = one_chunk(q_ref[pl.ds(t0, chunk), :], kt_blk[:, sub * chunk:(sub + 1) * chunk],
                                 v_ref[pl.ds(t0, chunk), :], sg_ref[pl.ds(t0, chunk), :], state)
            y_ref[pl.ds(t0, chunk), :] = y
        return state

    state = lax.fori_loop(0, nkb, body, jnp.zeros((pw, pw), F32), unroll=8)
    s_ref[0] = state[0:HEAD_DIM, 0:HEAD_DIM]
    s_ref[1] = pltpu.roll(state, HEAD_DIM, 1)[HEAD_DIM:, 0:HEAD_DIM]


def _ret_call(qr, krt, vr, sg, lg_lane, lg_row, batch, seq):
    n, w = qr.shape
    nkb, tb = krt.shape[1], krt.shape[3]
    pw = 2 * HEAD_DIM
    seq_spec = pl.BlockSpec((seq, pw), lambda b, p: (b, p))
    return pl.pallas_call(
        _ret_kernel,
        grid=(batch, HEAD_PAIRS),
        in_specs=[seq_spec,
                  pl.BlockSpec((None, nkb, pw, tb), lambda b, p: (b, 0, p, 0)),
                  seq_spec, seq_spec,
                  pl.BlockSpec((None, 1, pw), lambda b, p: (p, 0, 0)),
                  pl.BlockSpec((None, pw, 1), lambda b, p: (p, 0, 0))],
        out_specs=[seq_spec, pl.BlockSpec((None, 2, HEAD_DIM, HEAD_DIM), lambda b, p: (b, p, 0, 0))],
        out_shape=[jax.ShapeDtypeStruct((n, w), BF16),
                   jax.ShapeDtypeStruct((batch, N_HEADS, HEAD_DIM, HEAD_DIM), F32)],
        compiler_params=_cparams(("parallel", "parallel"), big=True),
        name="ret_prompt",
    )(qr, krt, vr, sg, lg_lane, lg_row)


def _odd_prompt_kernel(x_ref, sh_ref, sc_ref, g_ref, win_ref, cw_ref, wout_ref, lng_ref, lnb_ref,
                       o_ref, st_ref, carry_ref, *, bps, alpha):
    i = pl.program_id(0)
    x = x_ref[...]
    tm, d = x.shape
    h = (x * (1.0 + sc_ref[...]) + sh_ref[...]).astype(BF16)
    b_gate = _dot(h, win_ref[:, 0:d])
    u = _dot(h, win_ref[:, d:2 * d]) * _dot(h, win_ref[:, 2 * d:3 * d])

    @pl.when(i % bps == 0)
    def _():
        carry_ref[...] = jnp.zeros_like(carry_ref)

    prev = carry_ref[...]
    p1, p2 = prev[7:8, :], prev[6:7, :]
    row = lax.broadcasted_iota(jnp.int32, (tm, d), 0)
    u1 = jnp.where(row == 0, p1, pltpu.roll(u, 1, 0))
    u2 = jnp.where(row == 0, p2, jnp.where(row == 1, p1, pltpu.roll(u, 2, 0)))
    cw = cw_ref[...]
    z = cw[0:1, :] * u2 + cw[1:2, :] * u1 + cw[2:3, :] * u
    carry_ref[...] = u[tm - 8:, :]
    st_ref[...] = u[tm - (CONV_WIDTH - 1):, :]
    y = _dot((b_gate * z).astype(BF16), wout_ref[...])
    zz = alpha * x + g_ref[...] * y
    o_ref[...] = _layernorm(zz, lng_ref[...], lnb_ref[...])


def _odd_prompt_call(x, rows, ks, w_in, conv_w, w_out, ln_g, ln_b, ln_idx, alpha, batch, seq):
    n, d = x.shape
    bps = seq // rows.tm
    return pl.pallas_call(
        functools.partial(_odd_prompt_kernel, bps=bps, alpha=alpha),
        grid=(n // rows.tm,),
        in_specs=[rows.row_spec(d), rows.mod_spec(ks[0]), rows.mod_spec(ks[1]), rows.mod_spec(ks[2]),
                  _const_spec(w_in), _const_spec(conv_w), _const_spec(w_out),
                  pl.BlockSpec((None, 1, d), lambda i: (ln_idx, 0, 0)),
                  pl.BlockSpec((None, 1, d), lambda i: (ln_idx, 0, 0))],
        out_specs=[rows.row_spec(d), pl.BlockSpec((None, CONV_WIDTH - 1, d), lambda i: (i // bps, 0, 0))],
        out_shape=[jax.ShapeDtypeStruct((n, d), F32), jax.ShapeDtypeStruct((batch, CONV_WIDTH - 1, d), F32)],
        scratch_shapes=[pltpu.VMEM((8, d), F32)],
        compiler_params=_cparams(("arbitrary",), big=True),
        name="odd_prompt",
    )(x, rows.mods, rows.mods, rows.mods, w_in, conv_w, w_out, ln_g, ln_b)


def _even_in_sample_kernel(x_ref, sh_ref, sc_ref, w_ref, bf_ref, cost_ref, sint_ref,
                           q_o, k_o, v_o, qrt_o, krt_o, vr_o, sg_o, lf_o):
    d = w_ref.shape[1]
    h = (x_ref[...] * (1.0 + sc_ref[...]) + sh_ref[...]).astype(BF16)

    def rows(name):
        lo, hi = _EVEN_ROWS[name]
        return w_ref[lo:hi, :]

    def tok(name):
        return _dot_nt(h, rows(name).astype(BF16))

    q_o[...] = tok("qf") * QK_SCALE
    k_o[...] = tok("kf")
    v_o[...] = tok("vf")
    vr_o[...] = tok("vr")
    sg_o[...] = _silu(tok("gr"))
    fg_rows = jnp.concatenate([rows("fg"), jnp.zeros((V7X_LANES - N_HEADS, d), F32)], axis=0).astype(BF16)
    lf_o[...] = _log_sigmoid(_dot_nt(h, fg_rows) + bf_ref[...])

    cos_t, sin_t = cost_ref[...], sint_ref[...]
    half = HEAD_DIM // 2
    for name, out, scale in (("qr", qrt_o, 1.0), ("kr", krt_o, QK_SCALE)):
        t = _dot_nt(rows(name).astype(BF16), h)
        for hh in range(N_HEADS):
            x1 = t[hh * HEAD_DIM:hh * HEAD_DIM + half, :]
            x2 = t[hh * HEAD_DIM + half:(hh + 1) * HEAD_DIM, :]
            out[hh * HEAD_DIM:hh * HEAD_DIM + half, :] = (x1 * cos_t - x2 * sin_t) * scale
            out[hh * HEAD_DIM + half:(hh + 1) * HEAD_DIM, :] = (x1 * sin_t + x2 * cos_t) * scale


def _even_in_sample_call(x, rows, ks, w_t_f32, bf_row, cos_t, sin_t):
    n, d = x.shape
    w = HEAD_WIDTH
    full = pl.BlockSpec((n, w), lambda i: (0, 0))
    full_t = pl.BlockSpec((w, n), lambda i: (0, 0))
    rot = pl.BlockSpec((HEAD_DIM // 2, n), lambda i: (0, 0))
    shp = jax.ShapeDtypeStruct((n, w), F32)
    shp_t = jax.ShapeDtypeStruct((w, n), F32)
    return pl.pallas_call(
        _even_in_sample_kernel,
        grid=(1,),
        in_specs=[rows.row_spec(d), rows.mod_spec(ks[0]), rows.mod_spec(ks[1]),
                  _const_spec(w_t_f32), _const_spec(bf_row), rot, rot],
        out_specs=[full, full, full, full_t, full_t, full, full, pl.BlockSpec((n, V7X_LANES), lambda i: (0, 0))],
        out_shape=[shp, shp, shp, shp_t, shp_t, shp, shp, jax.ShapeDtypeStruct((n, V7X_LANES), F32)],
        compiler_params=_cparams(("arbitrary",), big=True),
        name="even_inproj_sample",
    )(x, rows.mods, rows.mods, w_t_f32, bf_row, cos_t, sin_t)


def _own_head_lanes():
    sub = lax.broadcasted_iota(jnp.int32, (N_HEADS, HEAD_WIDTH), 0)
    lane = lax.broadcasted_iota(jnp.int32, (N_HEADS, HEAD_WIDTH), 1)
    return (lane >= sub * HEAD_DIM) & (lane < (sub + 1) * HEAD_DIM)


def _decode_fox_init(q_ref, kn_ref, vn_ref, lfn_ref, qbd_s, m_s, l_s, run_s, acc_s):
    w = HEAD_WIDTH
    q_bd = jnp.where(_own_head_lanes(), jnp.broadcast_to(q_ref[...], (N_HEADS, w)), 0.0)
    qbd_s[...] = q_bd
    m_s[...] = jnp.sum(q_bd * kn_ref[...], axis=-1, keepdims=True)
    l_s[...] = jnp.ones_like(l_s)
    page_len = acc_s.shape[1]
    lane_p = lax.broadcasted_iota(jnp.int32, (w, page_len), 1)
    acc_s[...] = jnp.where(lane_p == 0, jnp.broadcast_to(vn_ref[...], (page_len, w)).T, 0.0)
    s128 = lax.broadcasted_iota(jnp.int32, (N_HEADS, V7X_LANES), 0)
    l128 = lax.broadcasted_iota(jnp.int32, (N_HEADS, V7X_LANES), 1)
    lfn = jnp.broadcast_to(lfn_ref[...], (N_HEADS, V7X_LANES))
    run_s[...] = jnp.sum(jnp.where(s128 == l128, lfn, 0.0), axis=-1, keepdims=True)


def _decode_fox_finish(o_ref, l_s, acc_s):
    tot = jnp.sum(acc_s[...].T, axis=0, keepdims=True)
    l_b = jnp.broadcast_to(l_s[...], (N_HEADS, HEAD_WIDTH))
    o_ref[...] = tot / jnp.sum(jnp.where(_own_head_lanes(), l_b, 0.0), axis=0, keepdims=True)


def _decode_fox_pages(k_refs, v_refs, lf_refs, qbd_s, m_s, l_s, run_s, acc_s):
    pages = len(k_refs)
    page_len = k_refs[0].shape[-1]
    q_b = qbd_s[...].astype(BF16)
    order = list(range(pages - 1, -1, -1))
    s = jnp.concatenate([_dot(q_b, k_refs[r][...].astype(BF16)) for r in order], axis=1)
    lf = jnp.concatenate([lf_refs[r][...] for r in order], axis=1)
    n = pages * page_len
    lane_n = lax.broadcasted_iota(jnp.int32, (N_HEADS, n), 1)
    suf = lf
    shift = 1
    while shift < n:
        suf = suf + jnp.where(lane_n < n - shift, pltpu.roll(suf, n - shift, 1), 0.0)
        shift *= 2
    run = run_s[...]
    s = s + ((suf - lf) + run)
    m = m_s[...]
    m_new = jnp.maximum(m, jnp.max(s, axis=-1, keepdims=True))
    a = jnp.exp(m - m_new)
    p = jnp.exp(s - m_new)
    l_s[...] = a * l_s[...] + jnp.sum(p, axis=-1, keepdims=True)
    m_s[...] = m_new
    for h in range(N_HEADS):
        sl = slice(h * HEAD_DIM, (h + 1) * HEAD_DIM)
        upd = None
        for idx, r in enumerate(order):
            t = p[h:h + 1, idx * page_len:(idx + 1) * page_len] * v_refs[r][sl, :]
            upd = t if upd is None else upd + t
        acc_s[sl, :] = a[h:h + 1, :] * acc_s[sl, :] + upd
    run_s[...] = run + suf[:, 0:1]


def _decode_ret_kernel(qt_ref, kt_ref, v_ref, sg_ref, s_ref, gl_ref, y_ref, so_ref):
    qt, kt = qt_ref[...], kt_ref[...]
    g = gl_ref[...]
    for b in range(qt.shape[-1]):
        q, k = qt[:, :, b:b + 1], kt[:, :, b:b + 1]
        v = v_ref[b]
        state = s_ref[b]
        inner = jnp.sum(q * k, axis=1, keepdims=True) * v
        cross = jnp.sum(q * state, axis=1, keepdims=True) * g
        so_ref[b] = g * state + k * v
        o = inner + cross
        mu = jnp.mean(o, axis=-1, keepdims=True)
        d = o - mu
        var = jnp.mean(d * d, axis=-1, keepdims=True)
        y_ref[b] = sg_ref[b] * (d * lax.rsqrt(var + GN_EPS))


def _decode_ret_call(q_t, k_t, v_row, sg_row, state, decay):
    nseq = state.shape[0]

    def whole(arr):
        nd = arr.ndim
        return pl.BlockSpec(arr.shape, lambda i: (0,) * nd)

    return pl.pallas_call(
        _decode_ret_kernel,
        grid=(1,),
        in_specs=[whole(a) for a in (q_t, k_t, v_row, sg_row, state, decay)],
        out_specs=[whole(v_row), whole(state)],
        out_shape=[jax.ShapeDtypeStruct(v_row.shape, F32), jax.ShapeDtypeStruct(state.shape, F32)],
        compiler_params=_cparams(("arbitrary",), big=True),
        name="ret_decode",
    )(q_t, k_t, v_row, sg_row, state, decay)


def _odd_sample_kernel(x_ref, sh_ref, sc_ref, g_ref, win_ref, cw_ref, b0_ref, b1_ref, wout_ref, lng_ref, lnb_ref,
                       o_ref, u_ref, *, alpha):
    x = x_ref[...]
    d = x.shape[-1]
    h = (x * (1.0 + sc_ref[...]) + sh_ref[...]).astype(BF16)
    b_gate = _dot(h, win_ref[:, 0:d])
    u = _dot(h, win_ref[:, d:2 * d]) * _dot(h, win_ref[:, 2 * d:3 * d])
    cw = cw_ref[...]
    z = cw[0:1, :] * b0_ref[...] + cw[1:2, :] * b1_ref[...] + cw[2:3, :] * u
    u_ref[...] = u
    y = _dot((b_gate * z).astype(BF16), wout_ref[...])
    o_ref[...] = _layernorm(alpha * x + g_ref[...] * y, lng_ref[...], lnb_ref[...])


def _odd_sample_call(x, rows, ks, w_in, conv_w, buf0, buf1, w_out, ln_g, ln_b, ln_idx, alpha):
    n, d = x.shape
    full = pl.BlockSpec((n, d), lambda i: (0, 0))
    return pl.pallas_call(
        functools.partial(_odd_sample_kernel, alpha=alpha),
        grid=(1,),
        in_specs=[rows.row_spec(d), rows.mod_spec(ks[0]), rows.mod_spec(ks[1]), rows.mod_spec(ks[2]),
                  _const_spec(w_in), _const_spec(conv_w), full, full, _const_spec(w_out),
                  pl.BlockSpec((None, 1, d), lambda i: (ln_idx, 0, 0)),
                  pl.BlockSpec((None, 1, d), lambda i: (ln_idx, 0, 0))],
        out_specs=[full, full],
        out_shape=[jax.ShapeDtypeStruct((n, d), F32), jax.ShapeDtypeStruct((n, d), F32)],
        compiler_params=_cparams(("arbitrary",), big=True),
        name="odd_sample",
    )(x, rows.mods, rows.mods, rows.mods, w_in, conv_w, buf0, buf1, w_out, ln_g, ln_b)


def _rotary_tables(pos):
    half = HEAD_DIM // 2
    inv = 1.0 / (RET_ANGLE_BASE ** np.linspace(0.0, 1.0, half))
    ang = np.asarray(pos, np.float64)[:, None] * inv[None, :]
    return np.cos(ang), np.sin(ang)


def _token_major_tables(cos, sin):
    cos_h = np.concatenate([cos, cos], axis=1)
    sin_h = np.concatenate([-sin, sin], axis=1)
    return (jnp.asarray(np.tile(cos_h, (1, N_HEADS)), F32), jnp.asarray(np.tile(sin_h, (1, N_HEADS)), F32))


def kernel(x_prompt, x_sample, cache_k, cache_v, cache_logf, state_ret, state_conv, page_table, c_prompt, c_sample,
           w_ada, b_ada, w_ffn_in, w_ffn_out, ln_g, ln_b, w_in_even, b_forget, w_out_even, w_in_odd, conv_w,
           w_out_odd):
    batch, seq, d = x_prompt.shape
    nseq = x_sample.shape[0]
    depth = w_ada.shape[0]
    past_len = page_table.shape[1] * cache_k.shape[2]
    alpha = (2.0 * depth) ** 0.25
    w = HEAD_WIDTH
    tm = ROW_BLOCK
    assert seq % tm == 0 and x_sample.shape[1] == 1 and d % V7X_LANES == 0
    assert cache_k.shape[3] == N_HEADS and cache_k.shape[4] == HEAD_DIM

    n_c = batch + nseq
    pad = (-n_c) % 8
    c_all = jnp.concatenate([c_prompt, c_sample, jnp.zeros((pad, d), F32)], axis=0)
    mods = _ada_call(c_all, w_ada, b_ada)
    mods_p = mods[:, :batch].reshape(depth, batch, 1, N_MOD * d)
    mods_s = mods[:, batch:n_c]

    ln_g3 = ln_g.reshape(depth * 3, 1, d)
    ln_b3 = ln_b.reshape(depth * 3, 1, d)
    w_ffn_in_b = w_ffn_in.astype(BF16)
    w_ffn_out_b = w_ffn_out.astype(BF16)

    xp = x_prompt.reshape(batch * seq, d)
    xs = x_sample.reshape(nseq, d)

    cos_p, sin_p = _rotary_tables(np.arange(seq))
    rot_p = _token_major_tables(cos_p, sin_p) + (jnp.asarray(cos_p.T, F32), jnp.asarray(sin_p.T, F32))
    cos_s, sin_s = (jnp.asarray(t.T, F32) for t in _rotary_tables(np.full((nseq,), past_len)))

    log_decay = jnp.log(1.0 - 2.0 ** (-5.0 - jnp.arange(N_HEADS, dtype=F32)))
    lg_pairs = jnp.repeat(log_decay.reshape(HEAD_PAIRS, 2), HEAD_DIM, axis=1)
    lg_lane = lg_pairs.reshape(HEAD_PAIRS, 1, 2 * HEAD_DIM)
    lg_row = lg_pairs.reshape(HEAD_PAIRS, 2 * HEAD_DIM, 1)
    step_decay = jnp.exp(log_decay).reshape(N_HEADS, 1, 1)

    outs_p = {"k": [], "v": [], "lf": [], "ret": [], "conv": []}
    outs_s = {"k": [], "v": [], "lf": [], "ret": [], "conv": []}

    assert depth == 2 and w_in_even.shape[0] == 1 and cache_k.shape[0] == 1
    n_hosts = 2 * depth
    assert nseq % n_hosts == 0
    assert w_in_even.shape[2] == EVEN_IN_ROWS
    w_even_t = jnp.transpose(w_in_even[0])
    w_out_even_b = w_out_even[0].astype(BF16)
    bf = b_forget[0]
    w_in_odd_b = w_in_odd[0].astype(BF16)
    w_out_odd_b = w_out_odd[0].astype(BF16)

    def rows_s(l):
        return _Rows(mods_s, l, nseq, None, d)

    def rows_p(l):
        return _Rows(mods_p, l, tm, seq // tm, d)

    def ffn(x, rows, half, l, decode=None, mixer_out=None):
        ks = (0, 1, 2) if half == 0 else (6, 7, 8)
        return _ffn_call(x, rows, ks, w_ffn_in_b, w_ffn_out_b, (l, half), ln_g3, ln_b3, 3 * l + 2 * half, alpha,
                         decode, mixer_out)

    xs = ffn(xs, rows_s(0), 0, 0)
    bf_row = jnp.concatenate([bf, jnp.zeros((V7X_LANES - N_HEADS,), F32)]).reshape(1, V7X_LANES)
    (qs, ks_, vs, qrs_t, krs_t, vrs, sgs, lfs) = _even_in_sample_call(xs, rows_s(0), (3, 4), w_even_t, bf_row,
                                                                       cos_s, sin_s)
    n_phys, page_len = cache_k.shape[1], cache_k.shape[2]
    kt_pages = jnp.transpose(cache_k[0], (0, 2, 3, 1)).reshape(n_phys, w, page_len)
    vt_pages = jnp.transpose(cache_v[0], (0, 2, 3, 1)).reshape(n_phys, w, page_len)
    lft_pages = jnp.transpose(cache_logf[0], (0, 2, 1))
    per_host = nseq // n_hosts
    dec_args = (page_table, qs.reshape(nseq, 1, w), ks_.reshape(nseq, 1, w), vs.reshape(nseq, 1, w),
                lfs.reshape(nseq, 1, V7X_LANES), kt_pages, vt_pages, lft_pages)
    dec_out = []

    def ffn_host(x, half, l, mixer_out=None):
        x, o = ffn(x, rows_p(l), half, l, dec_args + (len(dec_out) * per_host, per_host), mixer_out)
        dec_out.append(o)
        return x

    xp = ffn_host(xp, 0, 0)
    (q, qr, vr, sg, kt, vt, ktc, vtc, krt, lft) = _even_in_call(
        xp, rows_p(0), (3, 4), w_even_t, bf.reshape(N_HEADS, 1), rot_p, batch, seq)
    c_t = _cumsum_call(lft)
    of = _fox_call(q, ktc, vtc, c_t.reshape(batch, N_HEADS, seq // tm, tm), batch, seq)
    yr, s_new = _ret_call(qr, krt, vr, sg, lg_lane, lg_row, batch, seq)
    outs_p["k"].append(jnp.transpose(kt.reshape(batch, N_HEADS, HEAD_DIM, seq), (0, 3, 1, 2)))
    outs_p["v"].append(jnp.transpose(vt.reshape(batch, N_HEADS, HEAD_DIM, seq), (0, 3, 1, 2)))
    outs_p["lf"].append(jnp.transpose(lft, (0, 2, 1)))
    outs_p["ret"].append(s_new)
    xp = ffn_host(xp, 1, 0, mixer_out=(5, of, yr, w_out_even_b, 1))
    xp = ffn_host(xp, 0, 1)
    xp, conv_p = _odd_prompt_call(xp, rows_p(1), (3, 4, 5), w_in_odd_b, conv_w[0], w_out_odd_b, ln_g3, ln_b3,
                                  4, alpha, batch, seq)
    outs_p["conv"].append(conv_p)
    xp = ffn_host(xp, 1, 1)

    of_s = jnp.concatenate(dec_out, axis=0).reshape(nseq, w)
    lfs = lfs[:, :N_HEADS]
    hdn = (N_HEADS, HEAD_DIM, nseq)
    rw = (nseq, N_HEADS, 1, HEAD_DIM)
    yr_s, s_new_s = _decode_ret_call(qrs_t.reshape(hdn), krs_t.reshape(hdn), vrs.reshape(rw), sgs.reshape(rw),
                                     state_ret[0], step_decay)
    xs = _outproj_call(xs, rows_s(0), 5, of_s, yr_s.reshape(nseq, w), w_out_even_b, ln_g3, ln_b3, 1, alpha)
    outs_s["k"].append(ks_.reshape(nseq, 1, N_HEADS, HEAD_DIM))
    outs_s["v"].append(vs.reshape(nseq, 1, N_HEADS, HEAD_DIM))
    outs_s["lf"].append(lfs.reshape(nseq, 1, N_HEADS))
    outs_s["ret"].append(s_new_s)
    xs = ffn(xs, rows_s(0), 1, 0)
    xs = ffn(xs, rows_s(1), 0, 1)
    buf = state_conv[0]
    xs, u_s = _odd_sample_call(xs, rows_s(1), (3, 4, 5), w_in_odd_b, conv_w[0], buf[:, 0], buf[:, 1], w_out_odd_b,
                               ln_g3, ln_b3, 4, alpha)
    outs_s["conv"].append(jnp.stack([buf[:, 1], u_s], axis=1))
    xs = ffn(xs, rows_s(1), 1, 1)

    def stk(lst):
        return jnp.stack(lst)

    return (xp.reshape(batch, seq, d), xs.reshape(nseq, 1, d),
            stk(outs_p["k"]), stk(outs_p["v"]), stk(outs_p["lf"]), stk(outs_p["ret"]), stk(outs_p["conv"]),
            stk(outs_s["k"]), stk(outs_s["v"]), stk(outs_s["lf"]), stk(outs_s["ret"]), stk(outs_s["conv"]))
```

```python
import functools

import jax
import jax.numpy as jnp
import numpy as np
from jax import lax
from jax.experimental import pallas as pl
from jax.experimental.pallas import tpu as pltpu

F32 = jnp.float32
BF16 = jnp.bfloat16

HEAD_DIM = 64
N_HEADS = 8
HEAD_WIDTH = N_HEADS * HEAD_DIM
N_MOD = 9
CONV_WIDTH = 3
RET_ANGLE_BASE = 10000.0
LN_EPS = 1e-5
GN_EPS = 1e-6
QK_SCALE = HEAD_DIM ** -0.5
LOG2E = 1.4426950408889634

V7X_LANES = 128
V7X_VMEM_LIMIT_BYTES = 56 * 1024 * 1024

ROW_BLOCK = 512
FF_CHUNK = 256
FOX_BLOCKS_PER_UPDATE = 4
RET_CHUNK = 256
ADA_COL_BLOCK = 1152
HEAD_PAIRS = N_HEADS // 2


def _cparams(sem, big=False):
    return pltpu.CompilerParams(dimension_semantics=sem,
                                vmem_limit_bytes=V7X_VMEM_LIMIT_BYTES if big else None)


def _const_spec(arr):
    nd = arr.ndim
    return pl.BlockSpec(arr.shape, lambda *_: (0,) * nd, pipeline_mode=pl.Buffered(1))


def _layernorm(z, g, b):
    mu = jnp.mean(z, axis=-1, keepdims=True)
    d = z - mu
    var = jnp.mean(d * d, axis=-1, keepdims=True)
    return d * lax.rsqrt(var + LN_EPS) * g + b


def _silu(a):
    return a * jax.nn.sigmoid(a)


def _log_sigmoid(z):
    return jnp.minimum(z, 0.0) - jnp.log1p(jnp.exp(-jnp.abs(z)))


def _dot(a, b):
    return jnp.dot(a, b, preferred_element_type=F32)


def _dot_nt(a, b):
    return lax.dot_general(a, b, (((1,), (1,)), ((), ())), preferred_element_type=F32)


def _ada_kernel(c_ref, w_ref, b_ref, o_ref):
    s = _silu(c_ref[...]).astype(BF16)
    o_ref[...] = _dot(s, w_ref[...].astype(BF16)) + b_ref[...]


def _ada_call(c_all, w_ada, b_ada):
    depth, d, nm = w_ada.shape
    rows = c_all.shape[0]
    tn = ADA_COL_BLOCK
    return pl.pallas_call(
        _ada_kernel,
        grid=(depth, nm // tn),
        in_specs=[pl.BlockSpec((rows, d), lambda l, j: (0, 0)),
                  pl.BlockSpec((None, d, tn), lambda l, j: (l, 0, j)),
                  pl.BlockSpec((None, 1, tn), lambda l, j: (l, 0, j))],
        out_specs=pl.BlockSpec((None, rows, tn), lambda l, j: (l, 0, j)),
        out_shape=jax.ShapeDtypeStruct((depth, rows, nm), F32),
        compiler_params=_cparams(("arbitrary", "arbitrary"), big=True),
        name="ada_mods",
    )(c_all, w_ada, b_ada.reshape(depth, 1, nm))


class _Rows:
    def __init__(self, mods, layer, tm, blocks_per_seq, d):
        self.mods, self.layer, self.tm, self.bps, self.d = mods, layer, tm, blocks_per_seq, d

    def mod_spec(self, k):
        l, d = self.layer, self.d
        if self.bps is None:
            return pl.BlockSpec((None, self.tm, d), lambda i, *_: (l, 0, k))
        bps = self.bps
        return pl.BlockSpec((None, None, 1, d), lambda i, *_: (l, i // bps, 0, k))

    def row_spec(self, width):
        return pl.BlockSpec((self.tm, width), lambda i, *_: (i, 0))


def _mixer_out_math(x, gate, a0, a1, w_ref, lng, lnb, alpha):
    half = a0.shape[-1]
    y = _dot(a0.astype(BF16), w_ref[0:half, :]) + _dot(a1.astype(BF16), w_ref[half:, :])
    return _layernorm(alpha * x + gate * y, lng, lnb)


def _ffn_rows(refs, ff, alpha, pre):
    x = refs[0][...]
    refs = refs[1:]
    if pre:
        gate, a0, a1, wo_ref, lng1, lnb1 = refs[:6]
        x = _mixer_out_math(x, gate[...], a0[...], a1[...], wo_ref, lng1[...], lnb1[...], alpha)
        refs = refs[6:]
    sh_ref, sc_ref, g_ref, win_ref, wout_ref, lng_ref, lnb_ref = refs
    h = (x * (1.0 + sc_ref[...]) + sh_ref[...]).astype(BF16)
    acc = None
    for c in range(ff // FF_CHUNK):
        lo, hi = c * FF_CHUNK, (c + 1) * FF_CHUNK
        a = _dot(h, win_ref[:, lo:hi])
        b = _dot(h, win_ref[:, ff + lo:ff + hi])
        y = _dot((_silu(a) * b).astype(BF16), wout_ref[lo:hi, :])
        acc = y if acc is None else acc + y
    z = alpha * x + (0.5 * g_ref[...]) * acc
    return _layernorm(z, lng_ref[...], lnb_ref[...])


def _ffn_kernel(*refs, ff, alpha, pre):
    refs[-1][...] = _ffn_rows(refs[:-1], ff, alpha, pre)


def _ffn_decode_kernel(pt_ref, *refs, ff, alpha, pre, pages, steps_per_seq, seq0):
    n_ffn = 14 if pre else 8
    ffn_in, rest = refs[:n_ffn], refs[n_ffn:]
    q_ref, kn_ref, vn_ref, lfn_ref, kt_hbm, vt_hbm, lf_hbm, o_ref, dec_ref = rest[:9]
    state = rest[9:14]
    kbuf, vbuf, lfbuf, sem = rest[14:]
    npg = pt_ref.shape[1]
    i = pl.program_id(0)
    last = pl.num_programs(0) - 1
    slot = i % 2

    def page_copies(step, into):
        seq = seq0 + step // steps_per_seq
        first = npg - 1 - (step % steps_per_seq) * pages
        out = []
        for r in range(pages):
            pg = pt_ref[seq, first - r]
            out += [pltpu.make_async_copy(kt_hbm.at[pg], kbuf.at[into, r], sem.at[into, 0]),
                    pltpu.make_async_copy(vt_hbm.at[pg], vbuf.at[into, r], sem.at[into, 1]),
                    pltpu.make_async_copy(lf_hbm.at[pg], lfbuf.at[into, r], sem.at[into, 2])]
        return out

    @pl.when(i == 0)
    def _():
        for c in page_copies(0, 0):
            c.start()

    nxt = jnp.minimum(i + 1, last)
    for c in page_copies(nxt, 1 - slot):
        c.start()
    for c in page_copies(i, slot):
        c.wait()

    g = i % steps_per_seq
    pl.when(g == 0)(lambda: _decode_fox_init(q_ref, kn_ref, vn_ref, lfn_ref, *state))
    _decode_fox_pages([kbuf.at[slot, r] for r in range(pages)], [vbuf.at[slot, r] for r in range(pages)],
                      [lfbuf.at[slot, r] for r in range(pages)], *state)
    o_ref[...] = _ffn_rows(ffn_in, ff, alpha, pre)
    pl.when(g == steps_per_seq - 1)(lambda: _decode_fox_finish(dec_ref, state[2], state[4]))

    @pl.when(i == last)
    def _():
        for c in page_copies(nxt, 1 - slot):
            c.wait()


def _ffn_call(x, rows, ks, w_in, w_out, which, ln_g, ln_b, ln_idx, alpha, decode=None, mixer_out=None):
    n, d = x.shape
    ff = w_out.shape[2]
    l, j = which
    steps = n // rows.tm

    def w_spec(arr):
        return pl.BlockSpec((None, None) + arr.shape[2:], lambda i, *_: (l, j, 0, 0), pipeline_mode=pl.Buffered(1))

    def ln_spec(idx):
        return pl.BlockSpec((None, 1, d), lambda i, *_: (idx, 0, 0))

    in_specs = [rows.row_spec(d)]
    args = [x]
    if mixer_out is not None:
        k_gate, a0, a1, w_o, ln_idx1 = mixer_out
        in_specs += [rows.mod_spec(k_gate), rows.row_spec(a0.shape[1]), rows.row_spec(a1.shape[1]), _const_spec(w_o),
                     ln_spec(ln_idx1), ln_spec(ln_idx1)]
        args += [rows.mods, a0, a1, w_o, ln_g, ln_b]
    in_specs += [rows.mod_spec(ks[0]), rows.mod_spec(ks[1]), rows.mod_spec(ks[2]), w_spec(w_in), w_spec(w_out),
                 ln_spec(ln_idx), ln_spec(ln_idx)]
    args += [rows.mods, rows.mods, rows.mods, w_in, w_out, ln_g, ln_b]
    pre = mixer_out is not None
    x_shape = jax.ShapeDtypeStruct((n, d), F32)
    if decode is None:
        return pl.pallas_call(
            functools.partial(_ffn_kernel, ff=ff, alpha=alpha, pre=pre),
            grid=(steps,), in_specs=in_specs, out_specs=rows.row_spec(d), out_shape=x_shape,
            compiler_params=_cparams(("parallel",), big=True),
            name="ffn",
        )(*args)

    page_table, q, kn, vn, lfn, kt_pages, vt_pages, lft_pages, seq0, n_seq = decode
    npg = page_table.shape[1]
    w, page_len = kt_pages.shape[1], kt_pages.shape[2]
    steps_per_seq = steps // n_seq
    pages = npg // steps_per_seq
    assert steps_per_seq * n_seq == steps and pages * steps_per_seq == npg

    row = pl.BlockSpec((None, 1, w), lambda i, pt: (seq0 + i // steps_per_seq, 0, 0))
    hbm = pl.BlockSpec(memory_space=pl.ANY)
    grid_spec = pltpu.PrefetchScalarGridSpec(
        num_scalar_prefetch=1,
        grid=(steps,),
        in_specs=in_specs
        + [row, row, row, pl.BlockSpec((None, 1, V7X_LANES), lambda i, pt: (seq0 + i // steps_per_seq, 0, 0))]
        + [hbm, hbm, hbm],
        out_specs=[rows.row_spec(d), pl.BlockSpec((None, 1, w), lambda i, pt: (i // steps_per_seq, 0, 0))],
        scratch_shapes=[pltpu.VMEM((N_HEADS, w), F32), pltpu.VMEM((N_HEADS, 1), F32), pltpu.VMEM((N_HEADS, 1), F32),
                        pltpu.VMEM((N_HEADS, 1), F32), pltpu.VMEM((w, page_len), F32),
                        pltpu.VMEM((2, pages, w, page_len), F32), pltpu.VMEM((2, pages, w, page_len), F32),
                        pltpu.VMEM((2, pages, N_HEADS, page_len), F32), pltpu.SemaphoreType.DMA((2, 3))],
    )
    return pl.pallas_call(
        functools.partial(_ffn_decode_kernel, ff=ff, alpha=alpha, pre=pre, pages=pages, steps_per_seq=steps_per_seq,
                          seq0=seq0),
        grid_spec=grid_spec,
        out_shape=[x_shape, jax.ShapeDtypeStruct((n_seq, 1, w), F32)],
        compiler_params=_cparams(("arbitrary",), big=True),
        name="ffn_decode",
    )(page_table, *args, q, kn, vn, lfn, kt_pages, vt_pages, lft_pages)


def _outproj_kernel(x_ref, g_ref, a0_ref, a1_ref, w_ref, lng_ref, lnb_ref, o_ref, *, alpha):
    o_ref[...] = _mixer_out_math(x_ref[...], g_ref[...], a0_ref[...], a1_ref[...], w_ref, lng_ref[...], lnb_ref[...],
                                 alpha)


def _outproj_call(x, rows, k_gate, a0, a1, w_out, ln_g, ln_b, ln_idx, alpha):
    n, d = x.shape
    return pl.pallas_call(
        functools.partial(_outproj_kernel, alpha=alpha),
        grid=(n // rows.tm,),
        in_specs=[rows.row_spec(d), rows.mod_spec(k_gate), rows.row_spec(a0.shape[1]), rows.row_spec(a1.shape[1]),
                  _const_spec(w_out),
                  pl.BlockSpec((None, 1, d), lambda i: (ln_idx, 0, 0)),
                  pl.BlockSpec((None, 1, d), lambda i: (ln_idx, 0, 0))],
        out_specs=rows.row_spec(d),
        out_shape=jax.ShapeDtypeStruct((n, d), F32),
        compiler_params=_cparams(("parallel",), big=True),
        name="mixer_outproj",
    )(x, rows.mods, a0, a1, w_out, ln_g, ln_b)


def _rotate_token_major(a, cos, sin_signed):
    width = a.shape[-1]
    lane = lax.broadcasted_iota(jnp.int32, a.shape, 1)
    first_half = (lane & (HEAD_DIM // 2)) == 0
    partner = jnp.where(first_half, pltpu.roll(a, width - HEAD_DIM // 2, 1), pltpu.roll(a, HEAD_DIM // 2, 1))
    return a * cos + partner * sin_signed


_EVEN_ROWS = {}
_r0 = 0
for _name, _n in (("qf", HEAD_WIDTH), ("kf", HEAD_WIDTH), ("vf", HEAD_WIDTH), ("fg", N_HEADS),
                  ("qr", HEAD_WIDTH), ("kr", HEAD_WIDTH), ("vr", HEAD_WIDTH), ("gr", HEAD_WIDTH)):
    _EVEN_ROWS[_name] = (_r0, _r0 + _n)
    _r0 += _n
EVEN_IN_ROWS = _r0
FG_ROWS = 16


def _even_in_kernel(x_ref, sh_ref, sc_ref, w_ref, bf_ref, cos_ref, sin_ref, cost_ref, sint_ref,
                    q_o, qr_o, vr_o, sg_o, kt_o, vt_o, ktc_o, vtc_o, krt_o, lft_o, wtok_ref, wt_ref):
    w = HEAD_WIDTH

    @pl.when(pl.program_id(0) == 0)
    def _():
        for k, name in enumerate(("qf", "qr", "vr", "gr")):
            lo, hi = _EVEN_ROWS[name]
            wtok_ref[:, k * w:(k + 1) * w] = w_ref[lo:hi, :].T.astype(BF16)
        for k, name in enumerate(("kf", "vf", "kr")):
            lo, hi = _EVEN_ROWS[name]
            wt_ref[k * w:(k + 1) * w, :] = w_ref[lo:hi, :].astype(BF16)
        lo, hi = _EVEN_ROWS["fg"]
        fg_rows = jnp.concatenate([w_ref[lo:hi, :], jnp.zeros((FG_ROWS - N_HEADS, w_ref.shape[1]), F32)], axis=0)
        wt_ref[3 * w:, :] = fg_rows.astype(BF16)

    x = x_ref[...]
    h = (x * (1.0 + sc_ref[...]) + sh_ref[...]).astype(BF16)

    q_o[...] = (_dot(h, wtok_ref[:, 0:w]) * (QK_SCALE * LOG2E)).astype(BF16)
    qr = _dot(h, wtok_ref[:, w:2 * w])
    qr_o[...] = _rotate_token_major(qr, cos_ref[...], sin_ref[...]).astype(BF16)
    vr_o[...] = _dot(h, wtok_ref[:, 2 * w:3 * w]).astype(BF16)
    sg_o[...] = _silu(_dot(h, wtok_ref[:, 3 * w:4 * w])).astype(BF16)

    kt = _dot_nt(wt_ref[0:w, :], h)
    kt_o[...] = kt
    ktc_o[...] = kt.astype(BF16)
    vt = _dot_nt(wt_ref[w:2 * w, :], h)
    vt_o[...] = vt
    vtc_o[...] = vt.astype(BF16)

    krt = _dot_nt(wt_ref[2 * w:3 * w, :], h)
    cos_t, sin_t = cost_ref[...], sint_ref[...]
    half = HEAD_DIM // 2
    for hh in range(N_HEADS):
        x1 = krt[hh * HEAD_DIM:hh * HEAD_DIM + half, :]
        x2 = krt[hh * HEAD_DIM + half:(hh + 1) * HEAD_DIM, :]
        krt_o[hh * HEAD_DIM:hh * HEAD_DIM + half, :] = (x1 * cos_t - x2 * sin_t) * QK_SCALE
        krt_o[hh * HEAD_DIM + half:(hh + 1) * HEAD_DIM, :] = (x1 * sin_t + x2 * cos_t) * QK_SCALE

    fg = _dot_nt(wt_ref[3 * w:, :], h)[0:N_HEADS, :] + bf_ref[...]
    lft_o[...] = _log_sigmoid(fg)


def _even_in_call(x, rows, ks, w_t_f32, bf_col, rot, batch, seq):
    n, d = x.shape
    tm = rows.tm
    bps = seq // tm
    w = HEAD_WIDTH
    cos_tok, sin_tok, cos_t, sin_t = rot
    tok_out = pl.BlockSpec((tm, w), lambda i: (i, 0))
    t_out = pl.BlockSpec((None, w, tm), lambda i: (i // bps, 0, i % bps))
    tc_out = pl.BlockSpec((None, None, w, tm), lambda i: (i // bps, i % bps, 0, 0))
    tok_shape = jax.ShapeDtypeStruct((n, w), BF16)
    return pl.pallas_call(
        _even_in_kernel,
        grid=(n // tm,),
        in_specs=[rows.row_spec(d), rows.mod_spec(ks[0]), rows.mod_spec(ks[1]),
                  _const_spec(w_t_f32), _const_spec(bf_col),
                  pl.BlockSpec((tm, w), lambda i: (i % bps, 0)),
                  pl.BlockSpec((tm, w), lambda i: (i % bps, 0)),
                  pl.BlockSpec((HEAD_DIM // 2, tm), lambda i: (0, i % bps)),
                  pl.BlockSpec((HEAD_DIM // 2, tm), lambda i: (0, i % bps))],
        out_specs=[tok_out, tok_out, tok_out, tok_out, t_out, t_out, tc_out, tc_out, tc_out,
                   pl.BlockSpec((None, N_HEADS, tm), lambda i: (i // bps, 0, i % bps))],
        out_shape=[tok_shape, tok_shape, tok_shape, tok_shape,
                   jax.ShapeDtypeStruct((batch, w, seq), F32),
                   jax.ShapeDtypeStruct((batch, w, seq), F32),
                   jax.ShapeDtypeStruct((batch, bps, w, tm), BF16),
                   jax.ShapeDtypeStruct((batch, bps, w, tm), BF16),
                   jax.ShapeDtypeStruct((batch, bps, w, tm), F32),
                   jax.ShapeDtypeStruct((batch, N_HEADS, seq), F32)],
        scratch_shapes=[pltpu.VMEM((d, 4 * w), BF16), pltpu.VMEM((3 * w + FG_ROWS, d), BF16)],
        compiler_params=_cparams(("arbitrary",), big=True),
        name="even_inproj",
    )(x, rows.mods, rows.mods, w_t_f32, bf_col, cos_tok, sin_tok, cos_t, sin_t)


def _cumsum_kernel(x_ref, o_ref):
    x = x_ref[...]
    n = x.shape[-1]
    lane = lax.broadcasted_iota(jnp.int32, x.shape, 1)
    shift = 1
    while shift < n:
        x = x + jnp.where(lane >= shift, pltpu.roll(x, shift, 1), 0.0)
        shift *= 2
    o_ref[...] = x * LOG2E


def _cumsum_call(lft):
    batch, heads, seq = lft.shape
    spec = pl.BlockSpec((None, heads, seq), lambda b: (b, 0, 0))
    return pl.pallas_call(
        _cumsum_kernel, grid=(batch,), in_specs=[spec], out_specs=spec,
        out_shape=jax.ShapeDtypeStruct(lft.shape, F32),
        compiler_params=_cparams(("parallel",)),
        name="logf_cumsum",
    )(lft)


def _split3(x):
    hi = x.astype(BF16).astype(F32)
    r = x - hi
    mid = r.astype(BF16).astype(F32)
    lo = (r - mid).astype(BF16).astype(F32)
    return hi, mid, lo


def _fox_kernel(q_ref, kt_ref, vt_ref, c_ref, o_ref):
    tq = q_ref.shape[0]
    tk = kt_ref.shape[-1]
    pw = 2 * HEAD_DIM
    slab_rows = 16
    i = pl.program_id(2)
    q = q_ref[...]
    lane = lax.broadcasted_iota(jnp.int32, (tq, pw), 1)
    q_aug = []
    for hd in range(2):
        base = HEAD_DIM if hd == 0 else 0
        hi, mid, lo = _split3(c_ref[hd, pl.ds(i, 1), :][:, 0:1])
        aug = jnp.where(lane == base + 3, hi, jnp.where(lane == base + 4, mid, jnp.where(lane == base + 5, lo,
              jnp.where((lane >= base) & (lane < base + 3), 1.0, 0.0))))
        own = (lane < HEAD_DIM) if hd == 0 else (lane >= HEAD_DIM)
        q_aug.append(jnp.where(own, q, aug.astype(BF16)))
    r16 = lax.broadcasted_iota(jnp.int32, (slab_rows, tk), 0)
    rowv = lax.broadcasted_iota(jnp.int32, (pw, tk), 0)

    def scores(j, hd):
        kt = kt_ref[j]
        hi, mid, lo = _split3(-c_ref[hd, pl.ds(j, 1), :])
        slab = jnp.where(r16 == 0, hi, jnp.where(r16 == 1, mid, jnp.where(r16 == 2, lo,
               jnp.where(r16 < 6, 1.0, 0.0)))).astype(BF16)
        if hd == 0:
            kt_aug = jnp.concatenate([kt[0:HEAD_DIM], slab, kt[HEAD_DIM + slab_rows:]], axis=0)
        else:
            kt_aug = jnp.concatenate([slab, kt[slab_rows:]], axis=0)
        return _dot(q_aug[hd], kt_aug)

    def values(j, hd):
        vt = vt_ref[j]
        return jnp.where(rowv == (HEAD_DIM if hd == 0 else 0), jnp.ones_like(vt), vt)

    def update(s, vt_aug, m, acc):
        m_new = jnp.maximum(m, jnp.max(s, axis=-1, keepdims=True))
        return m_new, jnp.exp2(m - m_new) * acc + _dot_nt(jnp.exp2(s - m_new).astype(BF16), vt_aug)

    def diag_scores(hd):
        row = lax.broadcasted_iota(jnp.int32, (tq, tk), 0)
        col = lax.broadcasted_iota(jnp.int32, (tq, tk), 1)
        return jnp.where(row >= col, scores(i, hd), -jnp.inf)

    group = FOX_BLOCKS_PER_UPDATE

    def joint(carry, full_blocks, with_diag):
        out = []
        for hd in range(2):
            s = [scores(j, hd) for j in full_blocks] + ([diag_scores(hd)] if with_diag else [])
            v = [values(j, hd) for j in full_blocks] + ([values(i, hd)] if with_diag else [])
            out.append(update(jnp.concatenate(s, axis=1), jnp.concatenate(v, axis=1), *carry[hd]))
        return tuple(out)

    def tail(n_full):
        return lambda carry: tuple(c[1] for c in joint(carry, [i - n_full + k for k in range(n_full)], True))

    init_head = (jnp.full((tq, 1), -jnp.inf, F32), jnp.zeros((tq, pw), F32))
    carry = lax.fori_loop(0, lax.shift_right_logical(i, group.bit_length() - 1),
                          lambda t, c: joint(c, [group * t + k for k in range(group)], False),
                          (init_head, init_head))
    outs = lax.switch(i & (group - 1), [tail(n) for n in range(group)], carry)
    o0 = outs[0] / outs[0][:, HEAD_DIM:HEAD_DIM + 1]
    o1 = outs[1] / outs[1][:, 0:1]
    o_ref[...] = jnp.where(lane < HEAD_DIM, o0, o1).astype(BF16)


def _fox_call(q, ktc, vtc, c4, batch, seq):
    n, w = q.shape
    tq = ktc.shape[-1]
    nq = seq // tq
    pw = 2 * HEAD_DIM
    return pl.pallas_call(
        _fox_kernel,
        grid=(batch, HEAD_PAIRS, nq),
        in_specs=[pl.BlockSpec((tq, pw), lambda b, p, i: (b * nq + i, p)),
                  pl.BlockSpec((None, nq, pw, tq), lambda b, p, i: (b, 0, p, 0)),
                  pl.BlockSpec((None, nq, pw, tq), lambda b, p, i: (b, 0, p, 0)),
                  pl.BlockSpec((None, 2, nq, tq), lambda b, p, i: (b, p, 0, 0))],
        out_specs=pl.BlockSpec((tq, pw), lambda b, p, i: (b * nq + i, p)),
        out_shape=jax.ShapeDtypeStruct((n, w), BF16),
        compiler_params=_cparams(("parallel", "parallel", "arbitrary"), big=True),
        name="fox_prompt",
    )(q, ktc, vtc, c4)


def _ret_kernel(q_ref, kt_ref, v_ref, sg_ref, lgl_ref, lgr_ref, y_ref, s_ref):
    chunk = RET_CHUNK
    pw = 2 * HEAD_DIM
    nkb, _, tb = kt_ref.shape
    lg_lane = lgl_ref[...]
    lg_row = lgr_ref[...]
    lg_a, lg_b = lg_lane[:, 0:1], lg_lane[:, HEAD_DIM:HEAD_DIM + 1]

    ri = lax.broadcasted_iota(jnp.int32, (2 * chunk, chunk), 0)
    cj = lax.broadcasted_iota(jnp.int32, (2 * chunk, chunk), 1)
    first = ri < chunk
    diff = (jnp.where(first, ri, ri - chunk) - cj).astype(F32)
    decay_mask = jnp.where(diff >= 0, jnp.exp(jnp.maximum(diff, 0.0) * jnp.where(first, lg_a, lg_b)), 0.0)
    jj = lax.broadcasted_iota(jnp.int32, (pw, chunk), 1).astype(F32)
    col_decay = jnp.exp((chunk - 1.0 - jj) * lg_row)
    ii = lax.broadcasted_iota(jnp.int32, (chunk, pw), 0).astype(F32)
    row_decay = jnp.exp((ii + 1.0) * lg_lane)
    chunk_decay = jnp.exp(float(chunk) * lg_row)
    r2 = lax.broadcasted_iota(jnp.int32, (pw, pw), 0)
    c2 = lax.broadcasted_iota(jnp.int32, (pw, pw), 1)
    same_head = (r2 < HEAD_DIM) == (c2 < HEAD_DIM)
    seg_avg = jnp.where(same_head, 1.0 / HEAD_DIM, 0.0).astype(BF16)
    lane = lax.broadcasted_iota(jnp.int32, (chunk, pw), 1)

    def seg_mean(a):
        return _dot(a.astype(BF16), seg_avg)

    def one_chunk(q, kt, v, sg, state):
        zero = jnp.zeros_like(q)
        q_stack = jnp.concatenate([jnp.where(lane < HEAD_DIM, q, zero), jnp.where(lane >= HEAD_DIM, q, zero)], axis=0)
        qk = _dot(q_stack, kt.astype(BF16)) * decay_mask
        kd = (kt * col_decay).astype(BF16)
        r = _dot(jnp.concatenate([qk.astype(BF16), kd], axis=0), v)
        inner = jnp.where(lane < HEAD_DIM, r[0:chunk], r[chunk:2 * chunk])
        update = jnp.where(same_head, r[2 * chunk:], 0.0)
        cross = _dot(q, state.astype(BF16)) * row_decay
        o = inner + cross
        mu = seg_mean(o)
        d = o - mu
        var = seg_mean(d * d)
        y = sg.astype(F32) * (d * lax.rsqrt(var + GN_EPS))
        return y.astype(BF16), chunk_decay * state + update

    def body(jb, state):
        kt_blk = kt_ref[jb]
        for sub in range(tb // chunk):
            t0 = pl.multiple_of(jb * tb + sub * chunk, chunk)
            y, state = one_chunk(q_ref[pl.ds(t0, chunk), :], kt_blk[:, sub * chunk:(sub + 1) * chunk],
                                 v_ref[pl.ds(t0, chunk), :], sg_ref[pl.ds(t0, chunk), :], state)
            y_ref[pl.ds(t0, chunk), :] = y
        return state

    state = lax.fori_loop(0, nkb, body, jnp.zeros((pw, pw), F32), unroll=8)
    s_ref[0] = state[0:HEAD_DIM, 0:HEAD_DIM]
    s_ref[1] = pltpu.roll(state, HEAD_DIM, 1)[HEAD_DIM:, 0:HEAD_DIM]


def _ret_call(qr, krt, vr, sg, lg_lane, lg_row, batch, seq):
    n, w = qr.shape
    nkb, tb = krt.shape[1], krt.shape[3]
    pw = 2 * HEAD_DIM
    seq_spec = pl.BlockSpec((seq, pw), lambda b, p: (b, p))
    return pl.pallas_call(
        _ret_kernel,
        grid=(batch, HEAD_PAIRS),
        in_specs=[seq_spec,
                  pl.BlockSpec((None, nkb, pw, tb), lambda b, p: (b, 0, p, 0)),
                  seq_spec, seq_spec,
                  pl.BlockSpec((None, 1, pw), lambda b, p: (p, 0, 0)),
                  pl.BlockSpec((None, pw, 1), lambda b, p: (p, 0, 0))],
        out_specs=[seq_spec, pl.BlockSpec((None, 2, HEAD_DIM, HEAD_DIM), lambda b, p: (b, p, 0, 0))],
        out_shape=[jax.ShapeDtypeStruct((n, w), BF16),
                   jax.ShapeDtypeStruct((batch, N_HEADS, HEAD_DIM, HEAD_DIM), F32)],
        compiler_params=_cparams(("parallel", "parallel"), big=True),
        name="ret_prompt",
    )(qr, krt, vr, sg, lg_lane, lg_row)


def _odd_prompt_kernel(x_ref, sh_ref, sc_ref, g_ref, win_ref, cw_ref, wout_ref, lng_ref, lnb_ref,
                       o_ref, st_ref, carry_ref, *, bps, alpha):
    i = pl.program_id(0)
    x = x_ref[...]
    tm, d = x.shape
    h = (x * (1.0 + sc_ref[...]) + sh_ref[...]).astype(BF16)
    b_gate = _dot(h, win_ref[:, 0:d])
    u = _dot(h, win_ref[:, d:2 * d]) * _dot(h, win_ref[:, 2 * d:3 * d])

    @pl.when(i % bps == 0)
    def _():
        carry_ref[...] = jnp.zeros_like(carry_ref)

    prev = carry_ref[...]
    p1, p2 = prev[7:8, :], prev[6:7, :]
    row = lax.broadcasted_iota(jnp.int32, (tm, d), 0)
    u1 = jnp.where(row == 0, p1, pltpu.roll(u, 1, 0))
    u2 = jnp.where(row == 0, p2, jnp.where(row == 1, p1, pltpu.roll(u, 2, 0)))
    cw = cw_ref[...]
    z = cw[0:1, :] * u2 + cw[1:2, :] * u1 + cw[2:3, :] * u
    carry_ref[...] = u[tm - 8:, :]
    st_ref[...] = u[tm - (CONV_WIDTH - 1):, :]
    y = _dot((b_gate * z).astype(BF16), wout_ref[...])
    zz = alpha * x + g_ref[...] * y
    o_ref[...] = _layernorm(zz, lng_ref[...], lnb_ref[...])


def _odd_prompt_call(x, rows, ks, w_in, conv_w, w_out, ln_g, ln_b, ln_idx, alpha, batch, seq):
    n, d = x.shape
    bps = seq // rows.tm
    return pl.pallas_call(
        functools.partial(_odd_prompt_kernel, bps=bps, alpha=alpha),
        grid=(n // rows.tm,),
        in_specs=[rows.row_spec(d), rows.mod_spec(ks[0]), rows.mod_spec(ks[1]), rows.mod_spec(ks[2]),
                  _const_spec(w_in), _const_spec(conv_w), _const_spec(w_out),
                  pl.BlockSpec((None, 1, d), lambda i: (ln_idx, 0, 0)),
                  pl.BlockSpec((None, 1, d), lambda i: (ln_idx, 0, 0))],
        out_specs=[rows.row_spec(d), pl.BlockSpec((None, CONV_WIDTH - 1, d), lambda i: (i // bps, 0, 0))],
        out_shape=[jax.ShapeDtypeStruct((n, d), F32), jax.ShapeDtypeStruct((batch, CONV_WIDTH - 1, d), F32)],
        scratch_shapes=[pltpu.VMEM((8, d), F32)],
        compiler_params=_cparams(("arbitrary",), big=True),
        name="odd_prompt",
    )(x, rows.mods, rows.mods, rows.mods, w_in, conv_w, w_out, ln_g, ln_b)


def _even_in_sample_kernel(x_ref, sh_ref, sc_ref, w_ref, bf_ref, cost_ref, sint_ref,
                           q_o, k_o, v_o, qrt_o, krt_o, vr_o, sg_o, lf_o):
    d = w_ref.shape[1]
    h = (x_ref[...] * (1.0 + sc_ref[...]) + sh_ref[...]).astype(BF16)

    def rows(name):
        lo, hi = _EVEN_ROWS[name]
        return w_ref[lo:hi, :]

    def tok(name):
        return _dot_nt(h, rows(name).astype(BF16))

    q_o[...] = tok("qf") * QK_SCALE
    k_o[...] = tok("kf")
    v_o[...] = tok("vf")
    vr_o[...] = tok("vr")
    sg_o[...] = _silu(tok("gr"))
    fg_rows = jnp.concatenate([rows("fg"), jnp.zeros((V7X_LANES - N_HEADS, d), F32)], axis=0).astype(BF16)
    lf_o[...] = _log_sigmoid(_dot_nt(h, fg_rows) + bf_ref[...])

    cos_t, sin_t = cost_ref[...], sint_ref[...]
    half = HEAD_DIM // 2
    for name, out, scale in (("qr", qrt_o, 1.0), ("kr", krt_o, QK_SCALE)):
        t = _dot_nt(rows(name).astype(BF16), h)
        for hh in range(N_HEADS):
            x1 = t[hh * HEAD_DIM:hh * HEAD_DIM + half, :]
            x2 = t[hh * HEAD_DIM + half:(hh + 1) * HEAD_DIM, :]
            out[hh * HEAD_DIM:hh * HEAD_DIM + half, :] = (x1 * cos_t - x2 * sin_t) * scale
            out[hh * HEAD_DIM + half:(hh + 1) * HEAD_DIM, :] = (x1 * sin_t + x2 * cos_t) * scale


def _even_in_sample_call(x, rows, ks, w_t_f32, bf_row, cos_t, sin_t):
    n, d = x.shape
    w = HEAD_WIDTH
    full = pl.BlockSpec((n, w), lambda i: (0, 0))
    full_t = pl.BlockSpec((w, n), lambda i: (0, 0))
    rot = pl.BlockSpec((HEAD_DIM // 2, n), lambda i: (0, 0))
    shp = jax.ShapeDtypeStruct((n, w), F32)
    shp_t = jax.ShapeDtypeStruct((w, n), F32)
    return pl.pallas_call(
        _even_in_sample_kernel,
        grid=(1,),
        in_specs=[rows.row_spec(d), rows.mod_spec(ks[0]), rows.mod_spec(ks[1]),
                  _const_spec(w_t_f32), _const_spec(bf_row), rot, rot],
        out_specs=[full, full, full, full_t, full_t, full, full, pl.BlockSpec((n, V7X_LANES), lambda i: (0, 0))],
        out_shape=[shp, shp, shp, shp_t, shp_t, shp, shp, jax.ShapeDtypeStruct((n, V7X_LANES), F32)],
        compiler_params=_cparams(("arbitrary",), big=True),
        name="even_inproj_sample",
    )(x, rows.mods, rows.mods, w_t_f32, bf_row, cos_t, sin_t)


def _own_head_lanes():
    sub = lax.broadcasted_iota(jnp.int32, (N_HEADS, HEAD_WIDTH), 0)
    lane = lax.broadcasted_iota(jnp.int32, (N_HEADS, HEAD_WIDTH), 1)
    return (lane >= sub * HEAD_DIM) & (lane < (sub + 1) * HEAD_DIM)


def _decode_fox_init(q_ref, kn_ref, vn_ref, lfn_ref, qbd_s, m_s, l_s, run_s, acc_s):
    w = HEAD_WIDTH
    q_bd = jnp.where(_own_head_lanes(), jnp.broadcast_to(q_ref[...], (N_HEADS, w)), 0.0)
    qbd_s[...] = q_bd
    m_s[...] = jnp.sum(q_bd * kn_ref[...], axis=-1, keepdims=True)
    l_s[...] = jnp.ones_like(l_s)
    page_len = acc_s.shape[1]
    lane_p = lax.broadcasted_iota(jnp.int32, (w, page_len), 1)
    acc_s[...] = jnp.where(lane_p == 0, jnp.broadcast_to(vn_ref[...], (page_len, w)).T, 0.0)
    s128 = lax.broadcasted_iota(jnp.int32, (N_HEADS, V7X_LANES), 0)
    l128 = lax.broadcasted_iota(jnp.int32, (N_HEADS, V7X_LANES), 1)
    lfn = jnp.broadcast_to(lfn_ref[...], (N_HEADS, V7X_LANES))
    run_s[...] = jnp.sum(jnp.where(s128 == l128, lfn, 0.0), axis=-1, keepdims=True)


def _decode_fox_finish(o_ref, l_s, acc_s):
    tot = jnp.sum(acc_s[...].T, axis=0, keepdims=True)
    l_b = jnp.broadcast_to(l_s[...], (N_HEADS, HEAD_WIDTH))
    o_ref[...] = tot / jnp.sum(jnp.where(_own_head_lanes(), l_b, 0.0), axis=0, keepdims=True)


def _decode_fox_pages(k_refs, v_refs, lf_refs, qbd_s, m_s, l_s, run_s, acc_s):
    pages = len(k_refs)
    page_len = k_refs[0].shape[-1]
    q_b = qbd_s[...].astype(BF16)
    order = list(range(pages - 1, -1, -1))
    s = jnp.concatenate([_dot(q_b, k_refs[r][...].astype(BF16)) for r in order], axis=1)
    lf = jnp.concatenate([lf_refs[r][...] for r in order], axis=1)
    n = pages * page_len
    lane_n = lax.broadcasted_iota(jnp.int32, (N_HEADS, n), 1)
    suf = lf
    shift = 1
    while shift < n:
        suf = suf + jnp.where(lane_n < n - shift, pltpu.roll(suf, n - shift, 1), 0.0)
        shift *= 2
    run = run_s[...]
    s = s + ((suf - lf) + run)
    m = m_s[...]
    m_new = jnp.maximum(m, jnp.max(s, axis=-1, keepdims=True))
    a = jnp.exp(m - m_new)
    p = jnp.exp(s - m_new)
    l_s[...] = a * l_s[...] + jnp.sum(p, axis=-1, keepdims=True)
    m_s[...] = m_new
    for h in range(N_HEADS):
        sl = slice(h * HEAD_DIM, (h + 1) * HEAD_DIM)
        upd = None
        for idx, r in enumerate(order):
            t = p[h:h + 1, idx * page_len:(idx + 1) * page_len] * v_refs[r][sl, :]
            upd = t if upd is None else upd + t
        acc_s[sl, :] = a[h:h + 1, :] * acc_s[sl, :] + upd
    run_s[...] = run + suf[:, 0:1]


def _decode_ret_kernel(qt_ref, kt_ref, v_ref, sg_ref, s_ref, gl_ref, y_ref, so_ref):
    qt, kt = qt_ref[...], kt_ref[...]
    g = gl_ref[...]
    for b in range(qt.shape[-1]):
        q, k = qt[:, :, b:b + 1], kt[:, :, b:b + 1]
        v = v_ref[b]
        state = s_ref[b]
        inner = jnp.sum(q * k, axis=1, keepdims=True) * v
        cross = jnp.sum(q * state, axis=1, keepdims=True) * g
        so_ref[b] = g * state + k * v
        o = inner + cross
        mu = jnp.mean(o, axis=-1, keepdims=True)
        d = o - mu
        var = jnp.mean(d * d, axis=-1, keepdims=True)
        y_ref[b] = sg_ref[b] * (d * lax.rsqrt(var + GN_EPS))


def _decode_ret_call(q_t, k_t, v_row, sg_row, state, decay):
    nseq = state.shape[0]

    def whole(arr):
        nd = arr.ndim
        return pl.BlockSpec(arr.shape, lambda i: (0,) * nd)

    return pl.pallas_call(
        _decode_ret_kernel,
        grid=(1,),
        in_specs=[whole(a) for a in (q_t, k_t, v_row, sg_row, state, decay)],
        out_specs=[whole(v_row), whole(state)],
        out_shape=[jax.ShapeDtypeStruct(v_row.shape, F32), jax.ShapeDtypeStruct(state.shape, F32)],
        compiler_params=_cparams(("arbitrary",), big=True),
        name="ret_decode",
    )(q_t, k_t, v_row, sg_row, state, decay)


def _odd_sample_kernel(x_ref, sh_ref, sc_ref, g_ref, win_ref, cw_ref, b0_ref, b1_ref, wout_ref, lng_ref, lnb_ref,
                       o_ref, u_ref, *, alpha):
    x = x_ref[...]
    d = x.shape[-1]
    h = (x * (1.0 + sc_ref[...]) + sh_ref[...]).astype(BF16)
    b_gate = _dot(h, win_ref[:, 0:d])
    u = _dot(h, win_ref[:, d:2 * d]) * _dot(h, win_ref[:, 2 * d:3 * d])
    cw = cw_ref[...]
    z = cw[0:1, :] * b0_ref[...] + cw[1:2, :] * b1_ref[...] + cw[2:3, :] * u
    u_ref[...] = u
    y = _dot((b_gate * z).astype(BF16), wout_ref[...])
    o_ref[...] = _layernorm(alpha * x + g_ref[...] * y, lng_ref[...], lnb_ref[...])


def _odd_sample_call(x, rows, ks, w_in, conv_w, buf0, buf1, w_out, ln_g, ln_b, ln_idx, alpha):
    n, d = x.shape
    full = pl.BlockSpec((n, d), lambda i: (0, 0))
    return pl.pallas_call(
        functools.partial(_odd_sample_kernel, alpha=alpha),
        grid=(1,),
        in_specs=[rows.row_spec(d), rows.mod_spec(ks[0]), rows.mod_spec(ks[1]), rows.mod_spec(ks[2]),
                  _const_spec(w_in), _const_spec(conv_w), full, full, _const_spec(w_out),
                  pl.BlockSpec((None, 1, d), lambda i: (ln_idx, 0, 0)),
                  pl.BlockSpec((None, 1, d), lambda i: (ln_idx, 0, 0))],
        out_specs=[full, full],
        out_shape=[jax.ShapeDtypeStruct((n, d), F32), jax.ShapeDtypeStruct((n, d), F32)],
        compiler_params=_cparams(("arbitrary",), big=True),
        name="odd_sample",
    )(x, rows.mods, rows.mods, rows.mods, w_in, conv_w, buf0, buf1, w_out, ln_g, ln_b)


def _rotary_tables(pos):
    half = HEAD_DIM // 2
    inv = 1.0 / (RET_ANGLE_BASE ** np.linspace(0.0, 1.0, half))
    ang = np.asarray(pos, np.float64)[:, None] * inv[None, :]
    return np.cos(ang), np.sin(ang)


def _token_major_tables(cos, sin):
    cos_h = np.concatenate([cos, cos], axis=1)
    sin_h = np.concatenate([-sin, sin], axis=1)
    return (jnp.asarray(np.tile(cos_h, (1, N_HEADS)), F32), jnp.asarray(np.tile(sin_h, (1, N_HEADS)), F32))


def kernel(x_prompt, x_sample, cache_k, cache_v, cache_logf, state_ret, state_conv, page_table, c_prompt, c_sample,
           w_ada, b_ada, w_ffn_in, w_ffn_out, ln_g, ln_b, w_in_even, b_forget, w_out_even, w_in_odd, conv_w,
           w_out_odd):
    batch, seq, d = x_prompt.shape
    nseq = x_sample.shape[0]
    depth = w_ada.shape[0]
    past_len = page_table.shape[1] * cache_k.shape[2]
    alpha = (2.0 * depth) ** 0.25
    w = HEAD_WIDTH
    tm = ROW_BLOCK
    assert seq % tm == 0 and x_sample.shape[1] == 1 and d % V7X_LANES == 0
    assert cache_k.shape[3] == N_HEADS and cache_k.shape[4] == HEAD_DIM

    n_c = batch + nseq
    pad = (-n_c) % 8
    c_all = jnp.concatenate([c_prompt, c_sample, jnp.zeros((pad, d), F32)], axis=0)
    mods = _ada_call(c_all, w_ada, b_ada)
    mods_p = mods[:, :batch].reshape(depth, batch, 1, N_MOD * d)
    mods_s = mods[:, batch:n_c]

    ln_g3 = ln_g.reshape(depth * 3, 1, d)
    ln_b3 = ln_b.reshape(depth * 3, 1, d)
    w_ffn_in_b = w_ffn_in.astype(BF16)
    w_ffn_out_b = w_ffn_out.astype(BF16)

    xp = x_prompt.reshape(batch * seq, d)
    xs = x_sample.reshape(nseq, d)

    cos_p, sin_p = _rotary_tables(np.arange(seq))
    rot_p = _token_major_tables(cos_p, sin_p) + (jnp.asarray(cos_p.T, F32), jnp.asarray(sin_p.T, F32))
    cos_s, sin_s = (jnp.asarray(t.T, F32) for t in _rotary_tables(np.full((nseq,), past_len)))

    log_decay = jnp.log(1.0 - 2.0 ** (-5.0 - jnp.arange(N_HEADS, dtype=F32)))
    lg_pairs = jnp.repeat(log_decay.reshape(HEAD_PAIRS, 2), HEAD_DIM, axis=1)
    lg_lane = lg_pairs.reshape(HEAD_PAIRS, 1, 2 * HEAD_DIM)
    lg_row = lg_pairs.reshape(HEAD_PAIRS, 2 * HEAD_DIM, 1)
    step_decay = jnp.exp(log_decay).reshape(N_HEADS, 1, 1)

    outs_p = {"k": [], "v": [], "lf": [], "ret": [], "conv": []}
    outs_s = {"k": [], "v": [], "lf": [], "ret": [], "conv": []}

    assert depth == 2 and w_in_even.shape[0] == 1 and cache_k.shape[0] == 1
    n_hosts = 2 * depth
    assert nseq % n_hosts == 0
    assert w_in_even.shape[2] == EVEN_IN_ROWS
    w_even_t = jnp.transpose(w_in_even[0])
    w_out_even_b = w_out_even[0].astype(BF16)
    bf = b_forget[0]
    w_in_odd_b = w_in_odd[0].astype(BF16)
    w_out_odd_b = w_out_odd[0].astype(BF16)

    def rows_s(l):
        return _Rows(mods_s, l, nseq, None, d)

    def rows_p(l):
        return _Rows(mods_p, l, tm, seq // tm, d)

    def ffn(x, rows, half, l, decode=None, mixer_out=None):
        ks = (0, 1, 2) if half == 0 else (6, 7, 8)
        return _ffn_call(x, rows, ks, w_ffn_in_b, w_ffn_out_b, (l, half), ln_g3, ln_b3, 3 * l + 2 * half, alpha,
                         decode, mixer_out)

    xs = ffn(xs, rows_s(0), 0, 0)
    bf_row = jnp.concatenate([bf, jnp.zeros((V7X_LANES - N_HEADS,), F32)]).reshape(1, V7X_LANES)
    (qs, ks_, vs, qrs_t, krs_t, vrs, sgs, lfs) = _even_in_sample_call(xs, rows_s(0), (3, 4), w_even_t, bf_row,
                                                                       cos_s, sin_s)
    n_phys, page_len = cache_k.shape[1], cache_k.shape[2]
    kt_pages = jnp.transpose(cache_k[0], (0, 2, 3, 1)).reshape(n_phys, w, page_len)
    vt_pages = jnp.transpose(cache_v[0], (0, 2, 3, 1)).reshape(n_phys, w, page_len)
    lft_pages = jnp.transpose(cache_logf[0], (0, 2, 1))
    per_host = nseq // n_hosts
    dec_args = (page_table, qs.reshape(nseq, 1, w), ks_.reshape(nseq, 1, w), vs.reshape(nseq, 1, w),
                lfs.reshape(nseq, 1, V7X_LANES), kt_pages, vt_pages, lft_pages)
    dec_out = []

    def ffn_host(x, half, l, mixer_out=None):
        x, o = ffn(x, rows_p(l), half, l, dec_args + (len(dec_out) * per_host, per_host), mixer_out)
        dec_out.append(o)
        return x

    xp = ffn_host(xp, 0, 0)
    (q, qr, vr, sg, kt, vt, ktc, vtc, krt, lft) = _even_in_call(
        xp, rows_p(0), (3, 4), w_even_t, bf.reshape(N_HEADS, 1), rot_p, batch, seq)
    c_t = _cumsum_call(lft)
    of = _fox_call(q, ktc, vtc, c_t.reshape(batch, N_HEADS, seq // tm, tm), batch, seq)
    yr, s_new = _ret_call(qr, krt, vr, sg, lg_lane, lg_row, batch, seq)
    outs_p["k"].append(jnp.transpose(kt.reshape(batch, N_HEADS, HEAD_DIM, seq), (0, 3, 1, 2)))
    outs_p["v"].append(jnp.transpose(vt.reshape(batch, N_HEADS, HEAD_DIM, seq), (0, 3, 1, 2)))
    outs_p["lf"].append(jnp.transpose(lft, (0, 2, 1)))
    outs_p["ret"].append(s_new)
    xp = ffn_host(xp, 1, 0, mixer_out=(5, of, yr, w_out_even_b, 1))
    xp = ffn_host(xp, 0, 1)
    xp, conv_p = _odd_prompt_call(xp, rows_p(1), (3, 4, 5), w_in_odd_b, conv_w[0], w_out_odd_b, ln_g3, ln_b3,
                                  4, alpha, batch, seq)
    outs_p["conv"].append(conv_p)
    xp = ffn_host(xp, 1, 1)

    of_s = jnp.concatenate(dec_out, axis=0).reshape(nseq, w)
    lfs = lfs[:, :N_HEADS]
    hdn = (N_HEADS, HEAD_DIM, nseq)
    rw = (nseq, N_HEADS, 1, HEAD_DIM)
    yr_s, s_new_s = _decode_ret_call(qrs_t.reshape(hdn), krs_t.reshape(hdn), vrs.reshape(rw), sgs.reshape(rw),
                                     state_ret[0], step_decay)
    xs = _outproj_call(xs, rows_s(0), 5, of_s, yr_s.reshape(nseq, w), w_out_even_b, ln_g3, ln_b3, 1, alpha)
    outs_s["k"].append(ks_.reshape(nseq, 1, N_HEADS, HEAD_DIM))
    outs_s["v"].append(vs.reshape(nseq, 1, N_HEADS, HEAD_DIM))
    outs_s["lf"].append(lfs.reshape(nseq, 1, N_HEADS))
    outs_s["ret"].append(s_new_s)
    xs = ffn(xs, rows_s(0), 1, 0)
    xs = ffn(xs, rows_s(1), 0, 1)
    buf = state_conv[0]
    xs, u_s = _odd_sample_call(xs, rows_s(1), (3, 4, 5), w_in_odd_b, conv_w[0], buf[:, 0], buf[:, 1], w_out_odd_b,
                               ln_g3, ln_b3, 4, alpha)
    outs_s["conv"].append(jnp.stack([buf[:, 1], u_s], axis=1))
    xs = ffn(xs, rows_s(1), 1, 1)

    def stk(lst):
        return jnp.stack(lst)

    return (xp.reshape(batch, seq, d), xs.reshape(nseq, 1, d),
            stk(outs_p["k"]), stk(outs_p["v"]), stk(outs_p["lf"]), stk(outs_p["ret"]), stk(outs_p["conv"]),
            stk(outs_s["k"]), stk(outs_s["v"]), stk(outs_s["lf"]), stk(outs_s["ret"]), stk(outs_s["conv"]))
```

```python
import functools

import jax
import jax.numpy as jnp
import numpy as np
from jax import lax
from jax.experimental import pallas as pl
from jax.experimental.pallas import tpu as pltpu

F32 = jnp.float32
BF16 = jnp.bfloat16

HEAD_DIM = 64
N_HEADS = 8
HEAD_WIDTH = N_HEADS * HEAD_DIM
N_MOD = 9
CONV_WIDTH = 3
RET_ANGLE_BASE = 10000.0
LN_EPS = 1e-5
GN_EPS = 1e-6
QK_SCALE = HEAD_DIM ** -0.5
LOG2E = 1.4426950408889634

V7X_LANES = 128
V7X_BF16_SUBLANES = 16
V7X_VMEM_LIMIT_BYTES = 56 * 1024 * 1024

ROW_BLOCK = 512
FF_CHUNK = 256
FOX_BLOCKS_PER_UPDATE = 4
RET_CHUNK = 256
ADA_COL_BLOCK = 1152
HEAD_PAIRS = N_HEADS // 2


def _cparams(sem, big=False):
    return pltpu.CompilerParams(dimension_semantics=sem,
                                vmem_limit_bytes=V7X_VMEM_LIMIT_BYTES if big else None)


def _const_spec(arr):
    nd = arr.ndim
    return pl.BlockSpec(arr.shape, lambda *_: (0,) * nd, pipeline_mode=pl.Buffered(1))


def _layernorm(z, g, b):
    mu = jnp.mean(z, axis=-1, keepdims=True)
    d = z - mu
    var = jnp.mean(d * d, axis=-1, keepdims=True)
    return d * lax.rsqrt(var + LN_EPS) * g + b


def _silu(a):
    return a * jax.nn.sigmoid(a)


def _log_sigmoid(z):
    return jnp.minimum(z, 0.0) - jnp.log1p(jnp.exp(-jnp.abs(z)))


def _dot(a, b):
    return jnp.dot(a, b, preferred_element_type=F32)


def _dot_nt(a, b):
    return lax.dot_general(a, b, (((1,), (1,)), ((), ())), preferred_element_type=F32)


def _ada_kernel(c_ref, w_ref, b_ref, o_ref):
    s = _silu(c_ref[...]).astype(BF16)
    o_ref[...] = _dot(s, w_ref[...].astype(BF16)) + b_ref[...]


def _ada_call(c_all, w_ada, b_ada):
    depth, d, nm = w_ada.shape
    rows = c_all.shape[0]
    tn = ADA_COL_BLOCK
    return pl.pallas_call(
        _ada_kernel,
        grid=(depth, nm // tn),
        in_specs=[pl.BlockSpec((rows, d), lambda l, j: (0, 0)),
                  pl.BlockSpec((None, d, tn), lambda l, j: (l, 0, j)),
                  pl.BlockSpec((None, 1, tn), lambda l, j: (l, 0, j))],
        out_specs=pl.BlockSpec((None, rows, tn), lambda l, j: (l, 0, j)),
        out_shape=jax.ShapeDtypeStruct((depth, rows, nm), F32),
        compiler_params=_cparams(("arbitrary", "arbitrary"), big=True),
        name="ada_mods",
    )(c_all, w_ada, b_ada.reshape(depth, 1, nm))


class _SideCast:
    def __init__(self, w_in, w_out, which, steps):
        l, j = which
        d, ff2 = w_in.shape[2:]
        ff, d2 = w_out.shape[2:]
        rin, nout = d // steps, steps // 2
        rout = ff // nout
        assert rin * steps == d and rout * nout == ff and rin % V7X_BF16_SUBLANES == 0 and rout % V7X_BF16_SUBLANES == 0
        self.args = (w_in, w_out)
        self.in_specs = [pl.BlockSpec((None, None, rin, ff2), lambda i, *_: (l, j, i, 0)),
                         pl.BlockSpec((None, None, rout, d2), lambda i, *_: (l, j, jnp.minimum(i, nout - 1), 0))]
        self.out_specs = [pl.BlockSpec((rin, ff2), lambda i, *_: (i, 0)),
                          pl.BlockSpec((rout, d2), lambda i, *_: (jnp.minimum(i, nout - 1), 0))]
        self.out_shape = [jax.ShapeDtypeStruct((d, ff2), BF16), jax.ShapeDtypeStruct((ff, d2), BF16)]


def _with_side_cast(kernel, n_prefetch, n_in, n_out):
    def wrapped(*refs):
        a, b = n_prefetch + n_in, n_prefetch + n_in + 2 + n_out
        wi_ref, wo_ref = refs[a:a + 2]
        wib_ref, wob_ref = refs[b:b + 2]
        wib_ref[...] = wi_ref[...].astype(BF16)
        wob_ref[...] = wo_ref[...].astype(BF16)
        kernel(*refs[:a], *refs[a + 2:b], *refs[b + 2:])

    return wrapped


def _host_call(kernel, side, n_prefetch, in_specs, out_specs, out_shape, args, make_call):
    in_specs, out_specs, out_shape, args = list(in_specs), list(out_specs), list(out_shape), list(args)
    if side is not None:
        kernel = _with_side_cast(kernel, n_prefetch, len(in_specs), len(out_specs))
        in_specs += side.in_specs
        out_specs += side.out_specs
        out_shape += side.out_shape
        args += side.args
    return make_call(kernel, in_specs, out_specs, out_shape)(*args)


class _Rows:
    def __init__(self, mods, layer, tm, blocks_per_seq, d):
        self.mods, self.layer, self.tm, self.bps, self.d = mods, layer, tm, blocks_per_seq, d

    def mod_spec(self, k):
        l, d = self.layer, self.d
        if self.bps is None:
            return pl.BlockSpec((None, self.tm, d), lambda i, *_: (l, 0, k))
        bps = self.bps
        return pl.BlockSpec((None, None, 1, d), lambda i, *_: (l, i // bps, 0, k))

    def row_spec(self, width):
        return pl.BlockSpec((self.tm, width), lambda i, *_: (i, 0))


def _mixer_out_math(x, gate, a0, a1, w_ref, lng, lnb, alpha):
    half = a0.shape[-1]
    y = _dot(a0.astype(BF16), w_ref[0:half, :]) + _dot(a1.astype(BF16), w_ref[half:, :])
    return _layernorm(alpha * x + gate * y, lng, lnb)


def _ffn_rows(refs, ff, alpha, pre):
    x = refs[0][...]
    refs = refs[1:]
    if pre:
        gate, a0, a1, wo_ref, lng1, lnb1 = refs[:6]
        x = _mixer_out_math(x, gate[...], a0[...], a1[...], wo_ref, lng1[...], lnb1[...], alpha)
        refs = refs[6:]
    sh_ref, sc_ref, g_ref, win_ref, wout_ref, lng_ref, lnb_ref = refs
    h = (x * (1.0 + sc_ref[...]) + sh_ref[...]).astype(BF16)
    acc = None
    for c in range(ff // FF_CHUNK):
        lo, hi = c * FF_CHUNK, (c + 1) * FF_CHUNK
        a = _dot(h, win_ref[:, lo:hi])
        b = _dot(h, win_ref[:, ff + lo:ff + hi])
        y = _dot((_silu(a) * b).astype(BF16), wout_ref[lo:hi, :])
        acc = y if acc is None else acc + y
    z = alpha * x + (0.5 * g_ref[...]) * acc
    return _layernorm(z, lng_ref[...], lnb_ref[...])


def _ffn_kernel(*refs, ff, alpha, pre):
    refs[-1][...] = _ffn_rows(refs[:-1], ff, alpha, pre)


def _ffn_decode_kernel(pt_ref, *refs, ff, alpha, pre, pages, steps_per_seq, seq0):
    n_ffn = 14 if pre else 8
    ffn_in, rest = refs[:n_ffn], refs[n_ffn:]
    q_ref, kn_ref, vn_ref, lfn_ref, kt_hbm, vt_hbm, lf_hbm, o_ref, dec_ref = rest[:9]
    state = rest[9:14]
    kbuf, vbuf, lfbuf, sem = rest[14:]
    npg = pt_ref.shape[1]
    i = pl.program_id(0)
    last = pl.num_programs(0) - 1
    slot = i % 2

    def page_copies(step, into):
        seq = seq0 + step // steps_per_seq
        first = npg - 1 - (step % steps_per_seq) * pages
        out = []
        for r in range(pages):
            pg = pt_ref[seq, first - r]
            out += [pltpu.make_async_copy(kt_hbm.at[pg], kbuf.at[into, r], sem.at[into, 0]),
                    pltpu.make_async_copy(vt_hbm.at[pg], vbuf.at[into, r], sem.at[into, 1]),
                    pltpu.make_async_copy(lf_hbm.at[pg], lfbuf.at[into, r], sem.at[into, 2])]
        return out

    @pl.when(i == 0)
    def _():
        for c in page_copies(0, 0):
            c.start()

    nxt = jnp.minimum(i + 1, last)
    for c in page_copies(nxt, 1 - slot):
        c.start()
    for c in page_copies(i, slot):
        c.wait()

    g = i % steps_per_seq
    pl.when(g == 0)(lambda: _decode_fox_init(q_ref, kn_ref, vn_ref, lfn_ref, *state))
    _decode_fox_pages([kbuf.at[slot, r] for r in range(pages)], [vbuf.at[slot, r] for r in range(pages)],
                      [lfbuf.at[slot, r] for r in range(pages)], *state)
    o_ref[...] = _ffn_rows(ffn_in, ff, alpha, pre)
    pl.when(g == steps_per_seq - 1)(lambda: _decode_fox_finish(dec_ref, state[2], state[4]))

    @pl.when(i == last)
    def _():
        for c in page_copies(nxt, 1 - slot):
            c.wait()


def _ffn_call(x, rows, ks, w_in, w_out, ln_g, ln_b, ln_idx, alpha, decode=None, mixer_out=None, side=None):
    n, d = x.shape
    ff = w_out.shape[0]
    steps = n // rows.tm
    w_spec = _const_spec

    def ln_spec(idx):
        return pl.BlockSpec((None, 1, d), lambda i, *_: (idx, 0, 0))

    in_specs = [rows.row_spec(d)]
    args = [x]
    if mixer_out is not None:
        k_gate, a0, a1, w_o, ln_idx1 = mixer_out
        in_specs += [rows.mod_spec(k_gate), rows.row_spec(a0.shape[1]), rows.row_spec(a1.shape[1]), _const_spec(w_o),
                     ln_spec(ln_idx1), ln_spec(ln_idx1)]
        args += [rows.mods, a0, a1, w_o, ln_g, ln_b]
    in_specs += [rows.mod_spec(ks[0]), rows.mod_spec(ks[1]), rows.mod_spec(ks[2]), w_spec(w_in), w_spec(w_out),
                 ln_spec(ln_idx), ln_spec(ln_idx)]
    args += [rows.mods, rows.mods, rows.mods, w_in, w_out, ln_g, ln_b]
    pre = mixer_out is not None
    x_shape = jax.ShapeDtypeStruct((n, d), F32)
    if decode is None:
        assert side is None
        return pl.pallas_call(
            functools.partial(_ffn_kernel, ff=ff, alpha=alpha, pre=pre),
            grid=(steps,), in_specs=in_specs, out_specs=rows.row_spec(d), out_shape=x_shape,
            compiler_params=_cparams(("parallel",), big=True),
            name="ffn",
        )(*args)

    page_table, q, kn, vn, lfn, kt_pages, vt_pages, lft_pages, seq0, n_seq = decode
    npg = page_table.shape[1]
    w, page_len = kt_pages.shape[1], kt_pages.shape[2]
    steps_per_seq = steps // n_seq
    pages = npg // steps_per_seq
    assert steps_per_seq * n_seq == steps and pages * steps_per_seq == npg

    row = pl.BlockSpec((None, 1, w), lambda i, pt: (seq0 + i // steps_per_seq, 0, 0))
    hbm = pl.BlockSpec(memory_space=pl.ANY)
    in_specs = in_specs + [row, row, row,
                           pl.BlockSpec((None, 1, V7X_LANES), lambda i, pt: (seq0 + i // steps_per_seq, 0, 0)),
                           hbm, hbm, hbm]
    out_specs = [rows.row_spec(d), pl.BlockSpec((None, 1, w), lambda i, pt: (i // steps_per_seq, 0, 0))]
    out_shape = [x_shape, jax.ShapeDtypeStruct((n_seq, 1, w), F32)]

    def make_call(kernel, ins, outs, shapes):
        grid_spec = pltpu.PrefetchScalarGridSpec(
            num_scalar_prefetch=1, grid=(steps,), in_specs=ins, out_specs=outs,
            scratch_shapes=[pltpu.VMEM((N_HEADS, w), F32), pltpu.VMEM((N_HEADS, 1), F32), pltpu.VMEM((N_HEADS, 1), F32),
                            pltpu.VMEM((N_HEADS, 1), F32), pltpu.VMEM((w, page_len), F32),
                            pltpu.VMEM((2, pages, w, page_len), F32), pltpu.VMEM((2, pages, w, page_len), F32),
                            pltpu.VMEM((2, pages, N_HEADS, page_len), F32), pltpu.SemaphoreType.DMA((2, 3))])
        return pl.pallas_call(kernel, grid_spec=grid_spec, out_shape=shapes,
                              compiler_params=_cparams(("arbitrary",), big=True), name="ffn_decode")

    kernel = functools.partial(_ffn_decode_kernel, ff=ff, alpha=alpha, pre=pre, pages=pages,
                               steps_per_seq=steps_per_seq, seq0=seq0)
    return _host_call(kernel, side, 1, in_specs, out_specs, out_shape,
                      (*args, q, kn, vn, lfn, kt_pages, vt_pages, lft_pages), lambda k, i, o, s: (
                          lambda *a: make_call(k, i, o, s)(page_table, *a)))


def _outproj_kernel(x_ref, g_ref, a0_ref, a1_ref, w_ref, lng_ref, lnb_ref, o_ref, *, alpha):
    o_ref[...] = _mixer_out_math(x_ref[...], g_ref[...], a0_ref[...], a1_ref[...], w_ref, lng_ref[...], lnb_ref[...],
                                 alpha)


def _outproj_call(x, rows, k_gate, a0, a1, w_out, ln_g, ln_b, ln_idx, alpha):
    n, d = x.shape
    return pl.pallas_call(
        functools.partial(_outproj_kernel, alpha=alpha),
        grid=(n // rows.tm,),
        in_specs=[rows.row_spec(d), rows.mod_spec(k_gate), rows.row_spec(a0.shape[1]), rows.row_spec(a1.shape[1]),
                  _const_spec(w_out),
                  pl.BlockSpec((None, 1, d), lambda i: (ln_idx, 0, 0)),
                  pl.BlockSpec((None, 1, d), lambda i: (ln_idx, 0, 0))],
        out_specs=rows.row_spec(d),
        out_shape=jax.ShapeDtypeStruct((n, d), F32),
        compiler_params=_cparams(("parallel",), big=True),
        name="mixer_outproj",
    )(x, rows.mods, a0, a1, w_out, ln_g, ln_b)


def _rotate_token_major(a, cos, sin_signed):
    width = a.shape[-1]
    lane = lax.broadcasted_iota(jnp.int32, a.shape, 1)
    first_half = (lane & (HEAD_DIM // 2)) == 0
    partner = jnp.where(first_half, pltpu.roll(a, width - HEAD_DIM // 2, 1), pltpu.roll(a, HEAD_DIM // 2, 1))
    return a * cos + partner * sin_signed


_EVEN_ROWS = {}
_r0 = 0
for _name, _n in (("qf", HEAD_WIDTH), ("kf", HEAD_WIDTH), ("vf", HEAD_WIDTH), ("fg", N_HEADS),
                  ("qr", HEAD_WIDTH), ("kr", HEAD_WIDTH), ("vr", HEAD_WIDTH), ("gr", HEAD_WIDTH)):
    _EVEN_ROWS[_name] = (_r0, _r0 + _n)
    _r0 += _n
EVEN_IN_ROWS = _r0
FG_ROWS = V7X_BF16_SUBLANES


def _even_in_kernel(x_ref, sh_ref, sc_ref, w_ref, bf_ref, cos_ref, sin_ref, cost_ref, sint_ref,
                    q_o, qr_o, vr_o, sg_o, kt_o, vt_o, ktc_o, vtc_o, krt_o, lft_o, wtok_ref, wt_ref):
    w = HEAD_WIDTH

    @pl.when(pl.program_id(0) == 0)
    def _():
        for k, name in enumerate(("qf", "qr", "vr", "gr")):
            lo, hi = _EVEN_ROWS[name]
            wtok_ref[:, k * w:(k + 1) * w] = w_ref[lo:hi, :].T.astype(BF16)
        for k, name in enumerate(("kf", "vf", "kr")):
            lo, hi = _EVEN_ROWS[name]
            wt_ref[k * w:(k + 1) * w, :] = w_ref[lo:hi, :].astype(BF16)
        lo, hi = _EVEN_ROWS["fg"]
        fg_rows = jnp.concatenate([w_ref[lo:hi, :], jnp.zeros((FG_ROWS - N_HEADS, w_ref.shape[1]), F32)], axis=0)
        wt_ref[3 * w:, :] = fg_rows.astype(BF16)

    x = x_ref[...]
    h = (x * (1.0 + sc_ref[...]) + sh_ref[...]).astype(BF16)

    q_o[...] = (_dot(h, wtok_ref[:, 0:w]) * (QK_SCALE * LOG2E)).astype(BF16)
    qr = _dot(h, wtok_ref[:, w:2 * w])
    qr_o[...] = _rotate_token_major(qr, cos_ref[...], sin_ref[...]).astype(BF16)
    vr_o[...] = _dot(h, wtok_ref[:, 2 * w:3 * w]).astype(BF16)
    sg_o[...] = _silu(_dot(h, wtok_ref[:, 3 * w:4 * w])).astype(BF16)

    kt = _dot_nt(wt_ref[0:w, :], h)
    kt_o[...] = kt
    ktc_o[...] = kt.astype(BF16)
    vt = _dot_nt(wt_ref[w:2 * w, :], h)
    vt_o[...] = vt
    vtc_o[...] = vt.astype(BF16)

    krt = _dot_nt(wt_ref[2 * w:3 * w, :], h)
    cos_t, sin_t = cost_ref[...], sint_ref[...]
    half = HEAD_DIM // 2
    for hh in range(N_HEADS):
        x1 = krt[hh * HEAD_DIM:hh * HEAD_DIM + half, :]
        x2 = krt[hh * HEAD_DIM + half:(hh + 1) * HEAD_DIM, :]
        krt_o[hh * HEAD_DIM:hh * HEAD_DIM + half, :] = (x1 * cos_t - x2 * sin_t) * QK_SCALE
        krt_o[hh * HEAD_DIM + half:(hh + 1) * HEAD_DIM, :] = (x1 * sin_t + x2 * cos_t) * QK_SCALE

    fg = _dot_nt(wt_ref[3 * w:, :], h)[0:N_HEADS, :] + bf_ref[...]
    lft_o[...] = _log_sigmoid(fg)


def _even_in_call(x, rows, ks, w_t_f32, bf_col, rot, batch, seq, side=None):
    n, d = x.shape
    tm = rows.tm
    bps = seq // tm
    w = HEAD_WIDTH
    cos_tok, sin_tok, cos_t, sin_t = rot
    tok_out = pl.BlockSpec((tm, w), lambda i: (i, 0))
    t_out = pl.BlockSpec((None, w, tm), lambda i: (i // bps, 0, i % bps))
    tc_out = pl.BlockSpec((None, None, w, tm), lambda i: (i // bps, i % bps, 0, 0))
    tok_shape = jax.ShapeDtypeStruct((n, w), BF16)
    in_specs = [rows.row_spec(d), rows.mod_spec(ks[0]), rows.mod_spec(ks[1]),
                _const_spec(w_t_f32), _const_spec(bf_col),
                pl.BlockSpec((tm, w), lambda i: (i % bps, 0)),
                pl.BlockSpec((tm, w), lambda i: (i % bps, 0)),
                pl.BlockSpec((HEAD_DIM // 2, tm), lambda i: (0, i % bps)),
                pl.BlockSpec((HEAD_DIM // 2, tm), lambda i: (0, i % bps))]
    out_specs = [tok_out, tok_out, tok_out, tok_out, t_out, t_out, tc_out, tc_out, tc_out,
                 pl.BlockSpec((None, N_HEADS, tm), lambda i: (i // bps, 0, i % bps))]
    out_shape = [tok_shape, tok_shape, tok_shape, tok_shape,
                 jax.ShapeDtypeStruct((batch, w, seq), F32),
                 jax.ShapeDtypeStruct((batch, w, seq), F32),
                 jax.ShapeDtypeStruct((batch, bps, w, tm), BF16),
                 jax.ShapeDtypeStruct((batch, bps, w, tm), BF16),
                 jax.ShapeDtypeStruct((batch, bps, w, tm), F32),
                 jax.ShapeDtypeStruct((batch, N_HEADS, seq), F32)]

    def make_call(kernel, ins, outs, shapes):
        return pl.pallas_call(
            kernel, grid=(n // tm,), in_specs=ins, out_specs=outs, out_shape=shapes,
            scratch_shapes=[pltpu.VMEM((d, 4 * w), BF16), pltpu.VMEM((3 * w + FG_ROWS, d), BF16)],
            compiler_params=_cparams(("arbitrary",), big=True),
            name="even_inproj")

    return _host_call(_even_in_kernel, side, 0, in_specs, out_specs, out_shape,
                      (x, rows.mods, rows.mods, w_t_f32, bf_col, cos_tok, sin_tok, cos_t, sin_t), make_call)


def _cumsum_kernel(x_ref, o_ref):
    x = x_ref[...]
    n = x.shape[-1]
    lane = lax.broadcasted_iota(jnp.int32, x.shape, 1)
    shift = 1
    while shift < n:
        x = x + jnp.where(lane >= shift, pltpu.roll(x, shift, 1), 0.0)
        shift *= 2
    o_ref[...] = x * LOG2E


def _cumsum_call(lft):
    batch, heads, seq = lft.shape
    spec = pl.BlockSpec((None, heads, seq), lambda b: (b, 0, 0))
    return pl.pallas_call(
        _cumsum_kernel, grid=(batch,), in_specs=[spec], out_specs=spec,
        out_shape=jax.ShapeDtypeStruct(lft.shape, F32),
        compiler_params=_cparams(("parallel",)),
        name="logf_cumsum",
    )(lft)


def _split3(x):
    hi = x.astype(BF16).astype(F32)
    r = x - hi
    mid = r.astype(BF16).astype(F32)
    lo = (r - mid).astype(BF16).astype(F32)
    return hi, mid, lo


def _fox_kernel(q_ref, kt_ref, vt_ref, c_ref, o_ref):
    tq = q_ref.shape[0]
    tk = kt_ref.shape[-1]
    pw = 2 * HEAD_DIM
    slab_rows = V7X_BF16_SUBLANES
    i = pl.program_id(2)
    q = q_ref[...]
    lane = lax.broadcasted_iota(jnp.int32, (tq, pw), 1)
    q_aug = []
    for hd in range(2):
        base = HEAD_DIM if hd == 0 else 0
        hi, mid, lo = _split3(c_ref[hd, pl.ds(i, 1), :][:, 0:1])
        aug = jnp.where(lane == base + 3, hi, jnp.where(lane == base + 4, mid, jnp.where(lane == base + 5, lo,
              jnp.where((lane >= base) & (lane < base + 3), 1.0, 0.0))))
        own = (lane < HEAD_DIM) if hd == 0 else (lane >= HEAD_DIM)
        q_aug.append(jnp.where(own, q, aug.astype(BF16)))
    r16 = lax.broadcasted_iota(jnp.int32, (slab_rows, tk), 0)
    rowv = lax.broadcasted_iota(jnp.int32, (pw, tk), 0)

    def scores(j, hd):
        kt = kt_ref[j]
        hi, mid, lo = _split3(-c_ref[hd, pl.ds(j, 1), :])
        slab = jnp.where(r16 == 0, hi, jnp.where(r16 == 1, mid, jnp.where(r16 == 2, lo,
               jnp.where(r16 < 6, 1.0, 0.0)))).astype(BF16)
        if hd == 0:
            kt_aug = jnp.concatenate([kt[0:HEAD_DIM], slab, kt[HEAD_DIM + slab_rows:]], axis=0)
        else:
            kt_aug = jnp.concatenate([slab, kt[slab_rows:]], axis=0)
        return _dot(q_aug[hd], kt_aug)

    def values(j, hd):
        vt = vt_ref[j]
        return jnp.where(rowv == (HEAD_DIM if hd == 0 else 0), jnp.ones_like(vt), vt)

    def update(s, vt_aug, m, acc):
        m_new = jnp.maximum(m, jnp.max(s, axis=-1, keepdims=True))
        return m_new, jnp.exp2(m - m_new) * acc + _dot_nt(jnp.exp2(s - m_new).astype(BF16), vt_aug)

    def diag_scores(hd):
        row = lax.broadcasted_iota(jnp.int32, (tq, tk), 0)
        col = lax.broadcasted_iota(jnp.int32, (tq, tk), 1)
        return jnp.where(row >= col, scores(i, hd), -jnp.inf)

    group = FOX_BLOCKS_PER_UPDATE

    def joint(carry, full_blocks, with_diag):
        out = []
        for hd in range(2):
            s = [scores(j, hd) for j in full_blocks] + ([diag_scores(hd)] if with_diag else [])
            v = [values(j, hd) for j in full_blocks] + ([values(i, hd)] if with_diag else [])
            out.append(update(jnp.concatenate(s, axis=1), jnp.concatenate(v, axis=1), *carry[hd]))
        return tuple(out)

    def tail(n_full):
        return lambda carry: tuple(c[1] for c in joint(carry, [i - n_full + k for k in range(n_full)], True))

    init_head = (jnp.full((tq, 1), -jnp.inf, F32), jnp.zeros((tq, pw), F32))
    carry = lax.fori_loop(0, lax.shift_right_logical(i, group.bit_length() - 1),
                          lambda t, c: joint(c, [group * t + k for k in range(group)], False),
                          (init_head, init_head))
    outs = lax.switch(i & (group - 1), [tail(n) for n in range(group)], carry)
    o0 = outs[0] / outs[0][:, HEAD_DIM:HEAD_DIM + 1]
    o1 = outs[1] / outs[1][:, 0:1]
    o_ref[...] = jnp.where(lane < HEAD_DIM, o0, o1).astype(BF16)


def _fox_call(q, ktc, vtc, c4, batch, seq):
    n, w = q.shape
    tq = ktc.shape[-1]
    nq = seq // tq
    pw = 2 * HEAD_DIM
    return pl.pallas_call(
        _fox_kernel,
        grid=(batch, HEAD_PAIRS, nq),
        in_specs=[pl.BlockSpec((tq, pw), lambda b, p, i: (b * nq + i, p)),
                  pl.BlockSpec((None, nq, pw, tq), lambda b, p, i: (b, 0, p, 0)),
                  pl.BlockSpec((None, nq, pw, tq), lambda b, p, i: (b, 0, p, 0)),
                  pl.BlockSpec((None, 2, nq, tq), lambda b, p, i: (b, p, 0, 0))],
        out_specs=pl.BlockSpec((tq, pw), lambda b, p, i: (b * nq + i, p)),
        out_shape=jax.ShapeDtypeStruct((n, w), BF16),
        compiler_params=_cparams(("parallel", "parallel", "arbitrary"), big=True),
        name="fox_prompt",
    )(q, ktc, vtc, c4)


def _ret_kernel(q_ref, kt_ref, v_ref, sg_ref, lgl_ref, lgr_ref, y_ref, s_ref):
    chunk = RET_CHUNK
    pw = 2 * HEAD_DIM
    nkb, _, tb = kt_ref.shape
    lg_lane = lgl_ref[...]
    lg_row = lgr_ref[...]
    lg_a, lg_b = lg_lane[:, 0:1], lg_lane[:, HEAD_DIM:HEAD_DIM + 1]

    ri = lax.broadcasted_iota(jnp.int32, (2 * chunk, chunk), 0)
    cj = lax.broadcasted_iota(jnp.int32, (2 * chunk, chunk), 1)
    first = ri < chunk
    diff = (jnp.where(first, ri, ri - chunk) - cj).astype(F32)
    decay_mask = jnp.where(diff >= 0, jnp.exp(jnp.maximum(diff, 0.0) * jnp.where(first, lg_a, lg_b)), 0.0)
    jj = lax.broadcasted_iota(jnp.int32, (pw, chunk), 1).astype(F32)
    col_decay = jnp.exp((chunk - 1.0 - jj) * lg_row)
    ii = lax.broadcasted_iota(jnp.int32, (chunk, pw), 0).astype(F32)
    row_decay = jnp.exp((ii + 1.0) * lg_lane)
    chunk_decay = jnp.exp(float(chunk) * lg_row)
    r2 = lax.broadcasted_iota(jnp.int32, (pw, pw), 0)
    c2 = lax.broadcasted_iota(jnp.int32, (pw, pw), 1)
    same_head = (r2 < HEAD_DIM) == (c2 < HEAD_DIM)
    seg_avg = jnp.where(same_head, 1.0 / HEAD_DIM, 0.0).astype(BF16)
    lane = lax.broadcasted_iota(jnp.int32, (chunk, pw), 1)

    def seg_mean(a):
        return _dot(a.astype(BF16), seg_avg)

    def one_chunk(q, kt, v, sg, state):
        zero = jnp.zeros_like(q)
        q_stack = jnp.concatenate([jnp.where(lane < HEAD_DIM, q, zero), jnp.where(lane >= HEAD_DIM, q, zero)], axis=0)
        qk = _dot(q_stack, kt.astype(BF16)) * decay_mask
        kd = (kt * col_decay).astype(BF16)
        r = _dot(jnp.concatenate([qk.astype(BF16), kd], axis=0), v)
        inner = jnp.where(lane < HEAD_DIM, r[0:chunk], r[chunk:2 * chunk])
        update = jnp.where(same_head, r[2 * chunk:], 0.0)
        cross = _dot(q, state.astype(BF16)) * row_decay
        o = inner + cross
        mu = seg_mean(o)
        d = o - mu
        var = seg_mean(d * d)
        y = sg.astype(F32) * (d * lax.rsqrt(var + GN_EPS))
        return y.astype(BF16), chunk_decay * state + update

    def body(jb, state):
        kt_blk = kt_ref[jb]
        for sub in range(tb // chunk):
            t0 = pl.multiple_of(jb * tb + sub * chunk, chunk)
            y, state = one_chunk(q_ref[pl.ds(t0, chunk), :], kt_blk[:, sub * chunk:(sub + 1) * chunk],
                                 v_ref[pl.ds(t0, chunk), :], sg_ref[pl.ds(t0, chunk), :], state)
            y_ref[pl.ds(t0, chunk), :] = y
        return state

    state = lax.fori_loop(0, nkb, body, jnp.zeros((pw, pw), F32), unroll=8)
    s_ref[0] = state[0:HEAD_DIM, 0:HEAD_DIM]
    s_ref[1] = pltpu.roll(state, HEAD_DIM, 1)[HEAD_DIM:, 0:HEAD_DIM]


def _ret_call(qr, krt, vr, sg, lg_lane, lg_row, batch, seq):
    n, w = qr.shape
    nkb, tb = krt.shape[1], krt.shape[3]
    pw = 2 * HEAD_DIM
    seq_spec = pl.BlockSpec((seq, pw), lambda b, p: (b, p))
    return pl.pallas_call(
        _ret_kernel,
        grid=(batch, HEAD_PAIRS),
        in_specs=[seq_spec,
                  pl.BlockSpec((None, nkb, pw, tb), lambda b, p: (b, 0, p, 0)),
                  seq_spec, seq_spec,
                  pl.BlockSpec((None, 1, pw), lambda b, p: (p, 0, 0)),
                  pl.BlockSpec((None, pw, 1), lambda b, p: (p, 0, 0))],
        out_specs=[seq_spec, pl.BlockSpec((None, 2, HEAD_DIM, HEAD_DIM), lambda b, p: (b, p, 0, 0))],
        out_shape=[jax.ShapeDtypeStruct((n, w), BF16),
                   jax.ShapeDtypeStruct((batch, N_HEADS, HEAD_DIM, HEAD_DIM), F32)],
        compiler_params=_cparams(("parallel", "parallel"), big=True),
        name="ret_prompt",
    )(qr, krt, vr, sg, lg_lane, lg_row)


def _odd_prompt_kernel(x_ref, sh_ref, sc_ref, g_ref, win_ref, cw_ref, wout_ref, lng_ref, lnb_ref,
                       o_ref, st_ref, carry_ref, *, bps, alpha):
    i = pl.program_id(0)
    x = x_ref[...]
    tm, d = x.shape
    h = (x * (1.0 + sc_ref[...]) + sh_ref[...]).astype(BF16)
    b_gate = _dot(h, win_ref[:, 0:d])
    u = _dot(h, win_ref[:, d:2 * d]) * _dot(h, win_ref[:, 2 * d:3 * d])

    @pl.when(i % bps == 0)
    def _():
        carry_ref[...] = jnp.zeros_like(carry_ref)

    prev = carry_ref[...]
    p1, p2 = prev[7:8, :], prev[6:7, :]
    row = lax.broadcasted_iota(jnp.int32, (tm, d), 0)
    u1 = jnp.where(row == 0, p1, pltpu.roll(u, 1, 0))
    u2 = jnp.where(row == 0, p2, jnp.where(row == 1, p1, pltpu.roll(u, 2, 0)))
    cw = cw_ref[...]
    z = cw[0:1, :] * u2 + cw[1:2, :] * u1 + cw[2:3, :] * u
    carry_ref[...] = u[tm - 8:, :]
    st_ref[...] = u[tm - (CONV_WIDTH - 1):, :]
    y = _dot((b_gate * z).astype(BF16), wout_ref[...])
    zz = alpha * x + g_ref[...] * y
    o_ref[...] = _layernorm(zz, lng_ref[...], lnb_ref[...])


def _odd_prompt_call(x, rows, ks, w_in, conv_w, w_out, ln_g, ln_b, ln_idx, alpha, batch, seq, side=None):
    n, d = x.shape
    bps = seq // rows.tm
    in_specs = [rows.row_spec(d), rows.mod_spec(ks[0]), rows.mod_spec(ks[1]), rows.mod_spec(ks[2]),
                _const_spec(w_in), _const_spec(conv_w), _const_spec(w_out),
                pl.BlockSpec((None, 1, d), lambda i: (ln_idx, 0, 0)),
                pl.BlockSpec((None, 1, d), lambda i: (ln_idx, 0, 0))]
    out_specs = [rows.row_spec(d), pl.BlockSpec((None, CONV_WIDTH - 1, d), lambda i: (i // bps, 0, 0))]
    out_shape = [jax.ShapeDtypeStruct((n, d), F32), jax.ShapeDtypeStruct((batch, CONV_WIDTH - 1, d), F32)]

    def make_call(kernel, ins, outs, shapes):
        return pl.pallas_call(
            kernel, grid=(n // rows.tm,), in_specs=ins, out_specs=outs, out_shape=shapes,
            scratch_shapes=[pltpu.VMEM((8, d), F32)],
            compiler_params=_cparams(("arbitrary",), big=True),
            name="odd_prompt")

    return _host_call(functools.partial(_odd_prompt_kernel, bps=bps, alpha=alpha), side, 0, in_specs, out_specs,
                      out_shape, (x, rows.mods, rows.mods, rows.mods, w_in, conv_w, w_out, ln_g, ln_b), make_call)


def _even_in_sample_kernel(x_ref, sh_ref, sc_ref, w_ref, bf_ref, cost_ref, sint_ref,
                           q_o, k_o, v_o, qrt_o, krt_o, vr_o, sg_o, lf_o):
    d = w_ref.shape[1]
    h = (x_ref[...] * (1.0 + sc_ref[...]) + sh_ref[...]).astype(BF16)

    def rows(name):
        lo, hi = _EVEN_ROWS[name]
        return w_ref[lo:hi, :]

    def tok(name):
        return _dot_nt(h, rows(name).astype(BF16))

    q_o[...] = tok("qf") * QK_SCALE
    k_o[...] = tok("kf")
    v_o[...] = tok("vf")
    vr_o[...] = tok("vr")
    sg_o[...] = _silu(tok("gr"))
    fg_rows = jnp.concatenate([rows("fg"), jnp.zeros((V7X_LANES - N_HEADS, d), F32)], axis=0).astype(BF16)
    lf_o[...] = _log_sigmoid(_dot_nt(h, fg_rows) + bf_ref[...])

    cos_t, sin_t = cost_ref[...], sint_ref[...]
    half = HEAD_DIM // 2
    for name, out, scale in (("qr", qrt_o, 1.0), ("kr", krt_o, QK_SCALE)):
        t = _dot_nt(rows(name).astype(BF16), h)
        for hh in range(N_HEADS):
            x1 = t[hh * HEAD_DIM:hh * HEAD_DIM + half, :]
            x2 = t[hh * HEAD_DIM + half:(hh + 1) * HEAD_DIM, :]
            out[hh * HEAD_DIM:hh * HEAD_DIM + half, :] = (x1 * cos_t - x2 * sin_t) * scale
            out[hh * HEAD_DIM + half:(hh + 1) * HEAD_DIM, :] = (x1 * sin_t + x2 * cos_t) * scale


def _even_in_sample_call(x, rows, ks, w_t_f32, bf_row, cos_t, sin_t):
    n, d = x.shape
    w = HEAD_WIDTH
    full = pl.BlockSpec((n, w), lambda i: (0, 0))
    full_t = pl.BlockSpec((w, n), lambda i: (0, 0))
    rot = pl.BlockSpec((HEAD_DIM // 2, n), lambda i: (0, 0))
    shp = jax.ShapeDtypeStruct((n, w), F32)
    shp_t = jax.ShapeDtypeStruct((w, n), F32)
    return pl.pallas_call(
        _even_in_sample_kernel,
        grid=(1,),
        in_specs=[rows.row_spec(d), rows.mod_spec(ks[0]), rows.mod_spec(ks[1]),
                  _const_spec(w_t_f32), _const_spec(bf_row), rot, rot],
        out_specs=[full, full, full, full_t, full_t, full, full, pl.BlockSpec((n, V7X_LANES), lambda i: (0, 0))],
        out_shape=[shp, shp, shp, shp_t, shp_t, shp, shp, jax.ShapeDtypeStruct((n, V7X_LANES), F32)],
        compiler_params=_cparams(("arbitrary",), big=True),
        name="even_inproj_sample",
    )(x, rows.mods, rows.mods, w_t_f32, bf_row, cos_t, sin_t)


def _own_head_lanes():
    sub = lax.broadcasted_iota(jnp.int32, (N_HEADS, HEAD_WIDTH), 0)
    lane = lax.broadcasted_iota(jnp.int32, (N_HEADS, HEAD_WIDTH), 1)
    return (lane >= sub * HEAD_DIM) & (lane < (sub + 1) * HEAD_DIM)


def _decode_fox_init(q_ref, kn_ref, vn_ref, lfn_ref, qbd_s, m_s, l_s, run_s, acc_s):
    w = HEAD_WIDTH
    q_bd = jnp.where(_own_head_lanes(), jnp.broadcast_to(q_ref[...], (N_HEADS, w)), 0.0)
    qbd_s[...] = q_bd
    m_s[...] = jnp.sum(q_bd * kn_ref[...], axis=-1, keepdims=True)
    l_s[...] = jnp.ones_like(l_s)
    page_len = acc_s.shape[1]
    lane_p = lax.broadcasted_iota(jnp.int32, (w, page_len), 1)
    acc_s[...] = jnp.where(lane_p == 0, jnp.broadcast_to(vn_ref[...], (page_len, w)).T, 0.0)
    s128 = lax.broadcasted_iota(jnp.int32, (N_HEADS, V7X_LANES), 0)
    l128 = lax.broadcasted_iota(jnp.int32, (N_HEADS, V7X_LANES), 1)
    lfn = jnp.broadcast_to(lfn_ref[...], (N_HEADS, V7X_LANES))
    run_s[...] = jnp.sum(jnp.where(s128 == l128, lfn, 0.0), axis=-1, keepdims=True)


def _decode_fox_finish(o_ref, l_s, acc_s):
    tot = jnp.sum(acc_s[...].T, axis=0, keepdims=True)
    l_b = jnp.broadcast_to(l_s[...], (N_HEADS, HEAD_WIDTH))
    o_ref[...] = tot / jnp.sum(jnp.where(_own_head_lanes(), l_b, 0.0), axis=0, keepdims=True)


def _decode_fox_pages(k_refs, v_refs, lf_refs, qbd_s, m_s, l_s, run_s, acc_s):
    pages = len(k_refs)
    page_len = k_refs[0].shape[-1]
    q_b = qbd_s[...].astype(BF16)
    order = list(range(pages - 1, -1, -1))
    s = jnp.concatenate([_dot(q_b, k_refs[r][...].astype(BF16)) for r in order], axis=1)
    lf = jnp.concatenate([lf_refs[r][...] for r in order], axis=1)
    n = pages * page_len
    lane_n = lax.broadcasted_iota(jnp.int32, (N_HEADS, n), 1)
    suf = lf
    shift = 1
    while shift < n:
        suf = suf + jnp.where(lane_n < n - shift, pltpu.roll(suf, n - shift, 1), 0.0)
        shift *= 2
    run = run_s[...]
    s = s + ((suf - lf) + run)
    m = m_s[...]
    m_new = jnp.maximum(m, jnp.max(s, axis=-1, keepdims=True))
    a = jnp.exp(m - m_new)
    p = jnp.exp(s - m_new)
    l_s[...] = a * l_s[...] + jnp.sum(p, axis=-1, keepdims=True)
    m_s[...] = m_new
    for h in range(N_HEADS):
        sl = slice(h * HEAD_DIM, (h + 1) * HEAD_DIM)
        upd = None
        for idx, r in enumerate(order):
            t = p[h:h + 1, idx * page_len:(idx + 1) * page_len] * v_refs[r][sl, :]
            upd = t if upd is None else upd + t
        acc_s[sl, :] = a[h:h + 1, :] * acc_s[sl, :] + upd
    run_s[...] = run + suf[:, 0:1]


def _decode_ret_kernel(qt_ref, kt_ref, v_ref, sg_ref, s_ref, gl_ref, y_ref, so_ref):
    qt, kt = qt_ref[...], kt_ref[...]
    g = gl_ref[...]
    for b in range(qt.shape[-1]):
        q, k = qt[:, :, b:b + 1], kt[:, :, b:b + 1]
        v = v_ref[b]
        state = s_ref[b]
        inner = jnp.sum(q * k, axis=1, keepdims=True) * v
        cross = jnp.sum(q * state, axis=1, keepdims=True) * g
        so_ref[b] = g * state + k * v
        o = inner + cross
        mu = jnp.mean(o, axis=-1, keepdims=True)
        d = o - mu
        var = jnp.mean(d * d, axis=-1, keepdims=True)
        y_ref[b] = sg_ref[b] * (d * lax.rsqrt(var + GN_EPS))


def _decode_ret_call(q_t, k_t, v_row, sg_row, state, decay):
    nseq = state.shape[0]

    def whole(arr):
        nd = arr.ndim
        return pl.BlockSpec(arr.shape, lambda i: (0,) * nd)

    return pl.pallas_call(
        _decode_ret_kernel,
        grid=(1,),
        in_specs=[whole(a) for a in (q_t, k_t, v_row, sg_row, state, decay)],
        out_specs=[whole(v_row), whole(state)],
        out_shape=[jax.ShapeDtypeStruct(v_row.shape, F32), jax.ShapeDtypeStruct(state.shape, F32)],
        compiler_params=_cparams(("arbitrary",), big=True),
        name="ret_decode",
    )(q_t, k_t, v_row, sg_row, state, decay)


def _odd_sample_kernel(x_ref, sh_ref, sc_ref, g_ref, win_ref, cw_ref, b0_ref, b1_ref, wout_ref, lng_ref, lnb_ref,
                       o_ref, u_ref, *, alpha):
    x = x_ref[...]
    d = x.shape[-1]
    h = (x * (1.0 + sc_ref[...]) + sh_ref[...]).astype(BF16)
    b_gate = _dot(h, win_ref[:, 0:d])
    u = _dot(h, win_ref[:, d:2 * d]) * _dot(h, win_ref[:, 2 * d:3 * d])
    cw = cw_ref[...]
    z = cw[0:1, :] * b0_ref[...] + cw[1:2, :] * b1_ref[...] + cw[2:3, :] * u
    u_ref[...] = u
    y = _dot((b_gate * z).astype(BF16), wout_ref[...])
    o_ref[...] = _layernorm(alpha * x + g_ref[...] * y, lng_ref[...], lnb_ref[...])


def _odd_sample_call(x, rows, ks, w_in, conv_w, buf0, buf1, w_out, ln_g, ln_b, ln_idx, alpha):
    n, d = x.shape
    full = pl.BlockSpec((n, d), lambda i: (0, 0))
    return pl.pallas_call(
        functools.partial(_odd_sample_kernel, alpha=alpha),
        grid=(1,),
        in_specs=[rows.row_spec(d), rows.mod_spec(ks[0]), rows.mod_spec(ks[1]), rows.mod_spec(ks[2]),
                  _const_spec(w_in), _const_spec(conv_w), full, full, _const_spec(w_out),
                  pl.BlockSpec((None, 1, d), lambda i: (ln_idx, 0, 0)),
                  pl.BlockSpec((None, 1, d), lambda i: (ln_idx, 0, 0))],
        out_specs=[full, full],
        out_shape=[jax.ShapeDtypeStruct((n, d), F32), jax.ShapeDtypeStruct((n, d), F32)],
        compiler_params=_cparams(("arbitrary",), big=True),
        name="odd_sample",
    )(x, rows.mods, rows.mods, rows.mods, w_in, conv_w, buf0, buf1, w_out, ln_g, ln_b)


def _rotary_tables(pos):
    half = HEAD_DIM // 2
    inv = 1.0 / (RET_ANGLE_BASE ** np.linspace(0.0, 1.0, half))
    ang = np.asarray(pos, np.float64)[:, None] * inv[None, :]
    return np.cos(ang), np.sin(ang)


def _token_major_tables(cos, sin):
    cos_h = np.concatenate([cos, cos], axis=1)
    sin_h = np.concatenate([-sin, sin], axis=1)
    return (jnp.asarray(np.tile(cos_h, (1, N_HEADS)), F32), jnp.asarray(np.tile(sin_h, (1, N_HEADS)), F32))


def kernel(x_prompt, x_sample, cache_k, cache_v, cache_logf, state_ret, state_conv, page_table, c_prompt, c_sample,
           w_ada, b_ada, w_ffn_in, w_ffn_out, ln_g, ln_b, w_in_even, b_forget, w_out_even, w_in_odd, conv_w,
           w_out_odd):
    batch, seq, d = x_prompt.shape
    nseq = x_sample.shape[0]
    depth = w_ada.shape[0]
    past_len = page_table.shape[1] * cache_k.shape[2]
    alpha = (2.0 * depth) ** 0.25
    w = HEAD_WIDTH
    tm = ROW_BLOCK
    assert seq % tm == 0 and x_sample.shape[1] == 1 and d % V7X_LANES == 0
    assert cache_k.shape[3] == N_HEADS and cache_k.shape[4] == HEAD_DIM

    n_c = batch + nseq
    pad = (-n_c) % 8
    c_all = jnp.concatenate([c_prompt, c_sample, jnp.zeros((pad, d), F32)], axis=0)
    mods = _ada_call(c_all, w_ada, b_ada)
    mods_p = mods[:, :batch].reshape(depth, batch, 1, N_MOD * d)
    mods_s = mods[:, batch:n_c]

    ln_g3 = ln_g.reshape(depth * 3, 1, d)
    ln_b3 = ln_b.reshape(depth * 3, 1, d)
    w_bf = {(0, 0): (w_ffn_in[0, 0].astype(BF16), w_ffn_out[0, 0].astype(BF16))}

    def side_cast(l, half):
        return _SideCast(w_ffn_in, w_ffn_out, (l, half), batch * seq // tm)

    xp = x_prompt.reshape(batch * seq, d)
    xs = x_sample.reshape(nseq, d)

    cos_p, sin_p = _rotary_tables(np.arange(seq))
    rot_p = _token_major_tables(cos_p, sin_p) + (jnp.asarray(cos_p.T, F32), jnp.asarray(sin_p.T, F32))
    cos_s, sin_s = (jnp.asarray(t.T, F32) for t in _rotary_tables(np.full((nseq,), past_len)))

    log_decay = jnp.log(1.0 - 2.0 ** (-5.0 - jnp.arange(N_HEADS, dtype=F32)))
    lg_pairs = jnp.repeat(log_decay.reshape(HEAD_PAIRS, 2), HEAD_DIM, axis=1)
    lg_lane = lg_pairs.reshape(HEAD_PAIRS, 1, 2 * HEAD_DIM)
    lg_row = lg_pairs.reshape(HEAD_PAIRS, 2 * HEAD_DIM, 1)
    step_decay = jnp.exp(log_decay).reshape(N_HEADS, 1, 1)

    outs_p = {"k": [], "v": [], "lf": [], "ret": [], "conv": []}
    outs_s = {"k": [], "v": [], "lf": [], "ret": [], "conv": []}

    assert depth == 2 and w_in_even.shape[0] == 1 and cache_k.shape[0] == 1
    n_hosts = 2 * depth
    assert nseq % n_hosts == 0
    assert w_in_even.shape[2] == EVEN_IN_ROWS
    w_even_t = jnp.transpose(w_in_even[0])
    w_out_even_b = w_out_even[0].astype(BF16)
    bf = b_forget[0]
    w_in_odd_b = w_in_odd[0].astype(BF16)
    w_out_odd_b = w_out_odd[0].astype(BF16)

    def rows_s(l):
        return _Rows(mods_s, l, nseq, None, d)

    def rows_p(l):
        return _Rows(mods_p, l, tm, seq // tm, d)

    def ffn(x, rows, half, l, decode=None, mixer_out=None, side=None):
        ks = (0, 1, 2) if half == 0 else (6, 7, 8)
        return _ffn_call(x, rows, ks, *w_bf[(l, half)], ln_g3, ln_b3, 3 * l + 2 * half, alpha, decode, mixer_out, side)

    xs = ffn(xs, rows_s(0), 0, 0)
    bf_row = jnp.concatenate([bf, jnp.zeros((V7X_LANES - N_HEADS,), F32)]).reshape(1, V7X_LANES)
    (qs, ks_, vs, qrs_t, krs_t, vrs, sgs, lfs) = _even_in_sample_call(xs, rows_s(0), (3, 4), w_even_t, bf_row,
                                                                       cos_s, sin_s)
    n_phys, page_len = cache_k.shape[1], cache_k.shape[2]
    kt_pages = jnp.transpose(cache_k[0], (0, 2, 3, 1)).reshape(n_phys, w, page_len)
    vt_pages = jnp.transpose(cache_v[0], (0, 2, 3, 1)).reshape(n_phys, w, page_len)
    lft_pages = jnp.transpose(cache_logf[0], (0, 2, 1))
    per_host = nseq // n_hosts
    dec_args = (page_table, qs.reshape(nseq, 1, w), ks_.reshape(nseq, 1, w), vs.reshape(nseq, 1, w),
                lfs.reshape(nseq, 1, V7X_LANES), kt_pages, vt_pages, lft_pages)
    dec_out = []

    def ffn_host(x, half, l, mixer_out=None, cast_for=None):
        side = None if cast_for is None else side_cast(*cast_for)
        res = ffn(x, rows_p(l), half, l, dec_args + (len(dec_out) * per_host, per_host), mixer_out, side)
        dec_out.append(res[1])
        if cast_for is not None:
            w_bf[cast_for] = tuple(res[2:])
        return res[0]

    xp = ffn_host(xp, 0, 0, cast_for=(1, 0))
    (q, qr, vr, sg, kt, vt, ktc, vtc, krt, lft, *w_bf[(0, 1)]) = _even_in_call(
        xp, rows_p(0), (3, 4), w_even_t, bf.reshape(N_HEADS, 1), rot_p, batch, seq, side_cast(0, 1))
    c_t = _cumsum_call(lft)
    of = _fox_call(q, ktc, vtc, c_t.reshape(batch, N_HEADS, seq // tm, tm), batch, seq)
    yr, s_new = _ret_call(qr, krt, vr, sg, lg_lane, lg_row, batch, seq)
    outs_p["k"].append(jnp.transpose(kt.reshape(batch, N_HEADS, HEAD_DIM, seq), (0, 3, 1, 2)))
    outs_p["v"].append(jnp.transpose(vt.reshape(batch, N_HEADS, HEAD_DIM, seq), (0, 3, 1, 2)))
    outs_p["lf"].append(jnp.transpose(lft, (0, 2, 1)))
    outs_p["ret"].append(s_new)
    xp = ffn_host(xp, 1, 0, mixer_out=(5, of, yr, w_out_even_b, 1))
    xp = ffn_host(xp, 0, 1)
    xp, conv_p, *w_bf[(1, 1)] = _odd_prompt_call(xp, rows_p(1), (3, 4, 5), w_in_odd_b, conv_w[0], w_out_odd_b,
                                                 ln_g3, ln_b3, 4, alpha, batch, seq, side_cast(1, 1))
    outs_p["conv"].append(conv_p)
    xp = ffn_host(xp, 1, 1)

    of_s = jnp.concatenate(dec_out, axis=0).reshape(nseq, w)
    lfs = lfs[:, :N_HEADS]
    hdn = (N_HEADS, HEAD_DIM, nseq)
    rw = (nseq, N_HEADS, 1, HEAD_DIM)
    yr_s, s_new_s = _decode_ret_call(qrs_t.reshape(hdn), krs_t.reshape(hdn), vrs.reshape(rw), sgs.reshape(rw),
                                     state_ret[0], step_decay)
    xs = _outproj_call(xs, rows_s(0), 5, of_s, yr_s.reshape(nseq, w), w_out_even_b, ln_g3, ln_b3, 1, alpha)
    outs_s["k"].append(ks_.reshape(nseq, 1, N_HEADS, HEAD_DIM))
    outs_s["v"].append(vs.reshape(nseq, 1, N_HEADS, HEAD_DIM))
    outs_s["lf"].append(lfs.reshape(nseq, 1, N_HEADS))
    outs_s["ret"].append(s_new_s)
    xs = ffn(xs, rows_s(0), 1, 0)
    xs = ffn(xs, rows_s(1), 0, 1)
    buf = state_conv[0]
    xs, u_s = _odd_sample_call(xs, rows_s(1), (3, 4, 5), w_in_odd_b, conv_w[0], buf[:, 0], buf[:, 1], w_out_odd_b,
                               ln_g3, ln_b3, 4, alpha)
    outs_s["conv"].append(jnp.stack([buf[:, 1], u_s], axis=1))
    xs = ffn(xs, rows_s(1), 1, 1)

    def stk(lst):
        return jnp.stack(lst)

    return (xp.reshape(batch, seq, d), xs.reshape(nseq, 1, d),
            stk(outs_p["k"]), stk(outs_p["v"]), stk(outs_p["lf"]), stk(outs_p["ret"]), stk(outs_p["conv"]),
            stk(outs_s["k"]), stk(outs_s["v"]), stk(outs_s["lf"]), stk(outs_s["ret"]), stk(outs_s["conv"]))
```

```python
import functools

import jax
import jax.numpy as jnp
import numpy as np
from jax import lax
from jax.experimental import pallas as pl
from jax.experimental.pallas import tpu as pltpu

F32 = jnp.float32
BF16 = jnp.bfloat16

HEAD_DIM = 64
N_HEADS = 8
HEAD_WIDTH = N_HEADS * HEAD_DIM
N_MOD = 9
CONV_WIDTH = 3
RET_ANGLE_BASE = 10000.0
LN_EPS = 1e-5
GN_EPS = 1e-6
QK_SCALE = HEAD_DIM ** -0.5
LOG2E = 1.4426950408889634

V7X_LANES = 128
V7X_BF16_SUBLANES = 16
V7X_VMEM_LIMIT_BYTES = 56 * 1024 * 1024

ROW_BLOCK = 512
FF_CHUNK = 256
FOX_BLOCKS_PER_UPDATE = 4
RET_CHUNK = 256
ADA_COL_BLOCK = 1152
HEAD_PAIRS = N_HEADS // 2


def _cparams(sem, big=False):
    return pltpu.CompilerParams(dimension_semantics=sem,
                                vmem_limit_bytes=V7X_VMEM_LIMIT_BYTES if big else None)


def _const_spec(arr):
    nd = arr.ndim
    return pl.BlockSpec(arr.shape, lambda *_: (0,) * nd, pipeline_mode=pl.Buffered(1))


def _layernorm(z, g, b):
    mu = jnp.mean(z, axis=-1, keepdims=True)
    d = z - mu
    var = jnp.mean(d * d, axis=-1, keepdims=True)
    return d * lax.rsqrt(var + LN_EPS) * g + b


def _silu(a):
    return a * jax.nn.sigmoid(a)


def _log_sigmoid(z):
    return jnp.minimum(z, 0.0) - jnp.log1p(jnp.exp(-jnp.abs(z)))


def _dot(a, b):
    return jnp.dot(a, b, preferred_element_type=F32)


def _dot_nt(a, b):
    return lax.dot_general(a, b, (((1,), (1,)), ((), ())), preferred_element_type=F32)


def _ada_kernel(c_ref, w_ref, b_ref, o_ref):
    s = _silu(c_ref[...]).astype(BF16)
    o_ref[...] = _dot(s, w_ref[...].astype(BF16)) + b_ref[...]


def _ada_call(c_all, w_ada, b_ada):
    depth, d, nm = w_ada.shape
    rows = c_all.shape[0]
    tn = ADA_COL_BLOCK
    return pl.pallas_call(
        _ada_kernel,
        grid=(depth, nm // tn),
        in_specs=[pl.BlockSpec((rows, d), lambda l, j: (0, 0)),
                  pl.BlockSpec((None, d, tn), lambda l, j: (l, 0, j)),
                  pl.BlockSpec((None, 1, tn), lambda l, j: (l, 0, j))],
        out_specs=pl.BlockSpec((None, rows, tn), lambda l, j: (l, 0, j)),
        out_shape=jax.ShapeDtypeStruct((depth, rows, nm), F32),
        compiler_params=_cparams(("arbitrary", "arbitrary"), big=True),
        name="ada_mods",
    )(c_all, w_ada, b_ada.reshape(depth, 1, nm))


class _SideCast:
    def __init__(self, w_in, w_out, which, steps):
        l, j = which
        d, ff2 = w_in.shape[2:]
        ff, d2 = w_out.shape[2:]
        rin, nout = d // steps, steps // 2
        rout = ff // nout
        assert rin * steps == d and rout * nout == ff and rin % V7X_BF16_SUBLANES == 0 and rout % V7X_BF16_SUBLANES == 0
        self.args = (w_in, w_out)
        self.in_specs = [pl.BlockSpec((None, None, rin, ff2), lambda i, *_: (l, j, i, 0)),
                         pl.BlockSpec((None, None, rout, d2), lambda i, *_: (l, j, jnp.minimum(i, nout - 1), 0))]
        self.out_specs = [pl.BlockSpec((rin, ff2), lambda i, *_: (i, 0)),
                          pl.BlockSpec((rout, d2), lambda i, *_: (jnp.minimum(i, nout - 1), 0))]
        self.out_shape = [jax.ShapeDtypeStruct((d, ff2), BF16), jax.ShapeDtypeStruct((ff, d2), BF16)]


def _with_side_cast(kernel, n_prefetch, n_in, n_out):
    def wrapped(*refs):
        a, b = n_prefetch + n_in, n_prefetch + n_in + 2 + n_out
        wi_ref, wo_ref = refs[a:a + 2]
        wib_ref, wob_ref = refs[b:b + 2]
        wib_ref[...] = wi_ref[...].astype(BF16)
        wob_ref[...] = wo_ref[...].astype(BF16)
        kernel(*refs[:a], *refs[a + 2:b], *refs[b + 2:])

    return wrapped


class _Rows:
    def __init__(self, mods, layer, tm, blocks_per_seq, d):
        self.mods, self.layer, self.tm, self.bps, self.d = mods, layer, tm, blocks_per_seq, d

    def mod_spec(self, k):
        l, d = self.layer, self.d
        if self.bps is None:
            return pl.BlockSpec((None, self.tm, d), lambda i, *_: (l, 0, k))
        bps = self.bps
        return pl.BlockSpec((None, None, 1, d), lambda i, *_: (l, i // bps, 0, k))

    def row_spec(self, width):
        return pl.BlockSpec((self.tm, width), lambda i, *_: (i, 0))


def _mixer_out_math(x, gate, a0, a1, w_ref, lng, lnb, alpha):
    half = a0.shape[-1]
    y = _dot(a0.astype(BF16), w_ref[0:half, :]) + _dot(a1.astype(BF16), w_ref[half:, :])
    return _layernorm(alpha * x + gate * y, lng, lnb)


def _ffn_rows(refs, ff, alpha, pre):
    x = refs[0][...]
    refs = refs[1:]
    if pre:
        gate, a0, a1, wo_ref, lng1, lnb1 = refs[:6]
        x = _mixer_out_math(x, gate[...], a0[...], a1[...], wo_ref, lng1[...], lnb1[...], alpha)
        refs = refs[6:]
    sh_ref, sc_ref, g_ref, win_ref, wout_ref, lng_ref, lnb_ref = refs
    h = (x * (1.0 + sc_ref[...]) + sh_ref[...]).astype(BF16)
    acc = None
    for c in range(ff // FF_CHUNK):
        lo, hi = c * FF_CHUNK, (c + 1) * FF_CHUNK
        a = _dot(h, win_ref[:, lo:hi])
        b = _dot(h, win_ref[:, ff + lo:ff + hi])
        y = _dot((_silu(a) * b).astype(BF16), wout_ref[lo:hi, :])
        acc = y if acc is None else acc + y
    z = alpha * x + (0.5 * g_ref[...]) * acc
    return _layernorm(z, lng_ref[...], lnb_ref[...])


def _ffn_kernel(*refs, ff, alpha, pre):
    refs[-1][...] = _ffn_rows(refs[:-1], ff, alpha, pre)


def _with_decode(kernel, n_in, n_out, pages, steps_per_seq, seq0):
    def wrapped(pt_ref, *refs):
        ins = refs[:n_in]
        q_ref, kn_ref, vn_ref, lfn_ref, kt_hbm, vt_hbm, lf_hbm = refs[n_in:n_in + 7]
        outs = refs[n_in + 7:n_in + 7 + n_out]
        dec_ref = refs[n_in + 7 + n_out]
        scratch = refs[n_in + 8 + n_out:]
        own, state = scratch[:-9], scratch[-9:-4]
        kbuf, vbuf, lfbuf, sem = scratch[-4:]
        npg = pt_ref.shape[1]
        i = pl.program_id(0)
        last = pl.num_programs(0) - 1
        slot = i % 2

        def page_copies(step, into):
            seq = seq0 + step // steps_per_seq
            first = npg - 1 - (step % steps_per_seq) * pages
            out = []
            for r in range(pages):
                pg = pt_ref[seq, first - r]
                out += [pltpu.make_async_copy(kt_hbm.at[pg], kbuf.at[into, r], sem.at[into, 0]),
                        pltpu.make_async_copy(vt_hbm.at[pg], vbuf.at[into, r], sem.at[into, 1]),
                        pltpu.make_async_copy(lf_hbm.at[pg], lfbuf.at[into, r], sem.at[into, 2])]
            return out

        @pl.when(i == 0)
        def _():
            for c in page_copies(0, 0):
                c.start()

        nxt = jnp.minimum(i + 1, last)
        for c in page_copies(nxt, 1 - slot):
            c.start()
        for c in page_copies(i, slot):
            c.wait()

        g = i % steps_per_seq
        pl.when(g == 0)(lambda: _decode_fox_init(q_ref, kn_ref, vn_ref, lfn_ref, *state))
        _decode_fox_pages([kbuf.at[slot, r] for r in range(pages)], [vbuf.at[slot, r] for r in range(pages)],
                          [lfbuf.at[slot, r] for r in range(pages)], *state)
        kernel(*ins, *outs, *own)
        pl.when(g == steps_per_seq - 1)(lambda: _decode_fox_finish(dec_ref, state[2], state[4]))

        @pl.when(i == last)
        def _():
            for c in page_copies(nxt, 1 - slot):
                c.wait()

    return wrapped


def _run_host(name, kernel, steps, in_specs, out_specs, out_shape, args, scratch_shapes=(), decode=None, side=None):
    in_specs, out_specs, out_shape, args = list(in_specs), list(out_specs), list(out_shape), list(args)
    scratch = list(scratch_shapes)
    page_table = None
    if decode is not None:
        page_table, q, kn, vn, lfn, kt_pages, vt_pages, lft_pages, seq0, n_seq = decode
        npg = page_table.shape[1]
        w, page_len = kt_pages.shape[1], kt_pages.shape[2]
        steps_per_seq = steps // n_seq
        pages = npg // steps_per_seq
        assert steps_per_seq * n_seq == steps and pages * steps_per_seq == npg
        kernel = _with_decode(kernel, len(in_specs), len(out_specs), pages, steps_per_seq, seq0)
        row = pl.BlockSpec((None, 1, w), lambda i, pt: (seq0 + i // steps_per_seq, 0, 0))
        hbm = pl.BlockSpec(memory_space=pl.ANY)
        in_specs += [row, row, row,
                     pl.BlockSpec((None, 1, V7X_LANES), lambda i, pt: (seq0 + i // steps_per_seq, 0, 0)), hbm, hbm, hbm]
        args += [q, kn, vn, lfn, kt_pages, vt_pages, lft_pages]
        out_specs += [pl.BlockSpec((None, 1, w), lambda i, pt: (i // steps_per_seq, 0, 0))]
        out_shape += [jax.ShapeDtypeStruct((n_seq, 1, w), F32)]
        scratch += [pltpu.VMEM((N_HEADS, w), F32), pltpu.VMEM((N_HEADS, 1), F32), pltpu.VMEM((N_HEADS, 1), F32),
                    pltpu.VMEM((N_HEADS, 1), F32), pltpu.VMEM((w, page_len), F32),
                    pltpu.VMEM((2, pages, w, page_len), F32), pltpu.VMEM((2, pages, w, page_len), F32),
                    pltpu.VMEM((2, pages, N_HEADS, page_len), F32), pltpu.SemaphoreType.DMA((2, 3))]
    n_prefetch = 0 if page_table is None else 1
    if side is not None:
        kernel = _with_side_cast(kernel, n_prefetch, len(in_specs), len(out_specs))
        in_specs += side.in_specs
        out_specs += side.out_specs
        out_shape += side.out_shape
        args += side.args
    params = _cparams(("arbitrary",), big=True)
    if page_table is None:
        return pl.pallas_call(kernel, grid=(steps,), in_specs=in_specs, out_specs=out_specs, out_shape=out_shape,
                              scratch_shapes=scratch, compiler_params=params, name=name)(*args)
    grid_spec = pltpu.PrefetchScalarGridSpec(num_scalar_prefetch=1, grid=(steps,), in_specs=in_specs,
                                             out_specs=out_specs, scratch_shapes=scratch)
    return pl.pallas_call(kernel, grid_spec=grid_spec, out_shape=out_shape, compiler_params=params,
                          name=name + "_decode")(page_table, *args)


def _ffn_call(x, rows, ks, w_in, w_out, ln_g, ln_b, ln_idx, alpha, decode=None, mixer_out=None, side=None):
    n, d = x.shape
    ff = w_out.shape[0]
    steps = n // rows.tm
    w_spec = _const_spec

    def ln_spec(idx):
        return pl.BlockSpec((None, 1, d), lambda i, *_: (idx, 0, 0))

    in_specs = [rows.row_spec(d)]
    args = [x]
    if mixer_out is not None:
        k_gate, a0, a1, w_o, ln_idx1 = mixer_out
        in_specs += [rows.mod_spec(k_gate), rows.row_spec(a0.shape[1]), rows.row_spec(a1.shape[1]), _const_spec(w_o),
                     ln_spec(ln_idx1), ln_spec(ln_idx1)]
        args += [rows.mods, a0, a1, w_o, ln_g, ln_b]
    in_specs += [rows.mod_spec(ks[0]), rows.mod_spec(ks[1]), rows.mod_spec(ks[2]), w_spec(w_in), w_spec(w_out),
                 ln_spec(ln_idx), ln_spec(ln_idx)]
    args += [rows.mods, rows.mods, rows.mods, w_in, w_out, ln_g, ln_b]
    pre = mixer_out is not None
    x_shape = jax.ShapeDtypeStruct((n, d), F32)
    if decode is None:
        assert side is None
        return pl.pallas_call(
            functools.partial(_ffn_kernel, ff=ff, alpha=alpha, pre=pre),
            grid=(steps,), in_specs=in_specs, out_specs=rows.row_spec(d), out_shape=x_shape,
            compiler_params=_cparams(("parallel",), big=True),
            name="ffn",
        )(*args)

    return _run_host("ffn", functools.partial(_ffn_kernel, ff=ff, alpha=alpha, pre=pre), steps, in_specs,
                     [rows.row_spec(d)], [x_shape], args, decode=decode, side=side)


def _outproj_kernel(x_ref, g_ref, a0_ref, a1_ref, w_ref, lng_ref, lnb_ref, o_ref, *, alpha):
    o_ref[...] = _mixer_out_math(x_ref[...], g_ref[...], a0_ref[...], a1_ref[...], w_ref, lng_ref[...], lnb_ref[...],
                                 alpha)


def _outproj_call(x, rows, k_gate, a0, a1, w_out, ln_g, ln_b, ln_idx, alpha):
    n, d = x.shape
    return pl.pallas_call(
        functools.partial(_outproj_kernel, alpha=alpha),
        grid=(n // rows.tm,),
        in_specs=[rows.row_spec(d), rows.mod_spec(k_gate), rows.row_spec(a0.shape[1]), rows.row_spec(a1.shape[1]),
                  _const_spec(w_out),
                  pl.BlockSpec((None, 1, d), lambda i: (ln_idx, 0, 0)),
                  pl.BlockSpec((None, 1, d), lambda i: (ln_idx, 0, 0))],
        out_specs=rows.row_spec(d),
        out_shape=jax.ShapeDtypeStruct((n, d), F32),
        compiler_params=_cparams(("parallel",), big=True),
        name="mixer_outproj",
    )(x, rows.mods, a0, a1, w_out, ln_g, ln_b)


def _rotate_token_major(a, cos, sin_signed):
    width = a.shape[-1]
    lane = lax.broadcasted_iota(jnp.int32, a.shape, 1)
    first_half = (lane & (HEAD_DIM // 2)) == 0
    partner = jnp.where(first_half, pltpu.roll(a, width - HEAD_DIM // 2, 1), pltpu.roll(a, HEAD_DIM // 2, 1))
    return a * cos + partner * sin_signed


_EVEN_ROWS = {}
_r0 = 0
for _name, _n in (("qf", HEAD_WIDTH), ("kf", HEAD_WIDTH), ("vf", HEAD_WIDTH), ("fg", N_HEADS),
                  ("qr", HEAD_WIDTH), ("kr", HEAD_WIDTH), ("vr", HEAD_WIDTH), ("gr", HEAD_WIDTH)):
    _EVEN_ROWS[_name] = (_r0, _r0 + _n)
    _r0 += _n
EVEN_IN_ROWS = _r0
FG_ROWS = V7X_BF16_SUBLANES


def _even_in_kernel(x_ref, sh_ref, sc_ref, w_ref, bf_ref, cos_ref, sin_ref, cost_ref, sint_ref,
                    q_o, qr_o, vr_o, sg_o, kt_o, vt_o, ktc_o, vtc_o, krt_o, lft_o, wtok_ref, wt_ref):
    w = HEAD_WIDTH

    @pl.when(pl.program_id(0) == 0)
    def _():
        for k, name in enumerate(("qf", "qr", "vr", "gr")):
            lo, hi = _EVEN_ROWS[name]
            wtok_ref[:, k * w:(k + 1) * w] = w_ref[lo:hi, :].T.astype(BF16)
        for k, name in enumerate(("kf", "vf", "kr")):
            lo, hi = _EVEN_ROWS[name]
            wt_ref[k * w:(k + 1) * w, :] = w_ref[lo:hi, :].astype(BF16)
        lo, hi = _EVEN_ROWS["fg"]
        fg_rows = jnp.concatenate([w_ref[lo:hi, :], jnp.zeros((FG_ROWS - N_HEADS, w_ref.shape[1]), F32)], axis=0)
        wt_ref[3 * w:, :] = fg_rows.astype(BF16)

    x = x_ref[...]
    h = (x * (1.0 + sc_ref[...]) + sh_ref[...]).astype(BF16)

    q_o[...] = (_dot(h, wtok_ref[:, 0:w]) * (QK_SCALE * LOG2E)).astype(BF16)
    qr = _dot(h, wtok_ref[:, w:2 * w])
    qr_o[...] = _rotate_token_major(qr, cos_ref[...], sin_ref[...]).astype(BF16)
    vr_o[...] = _dot(h, wtok_ref[:, 2 * w:3 * w]).astype(BF16)
    sg_o[...] = _silu(_dot(h, wtok_ref[:, 3 * w:4 * w])).astype(BF16)

    kt = _dot_nt(wt_ref[0:w, :], h)
    kt_o[...] = kt
    ktc_o[...] = kt.astype(BF16)
    vt = _dot_nt(wt_ref[w:2 * w, :], h)
    vt_o[...] = vt
    vtc_o[...] = vt.astype(BF16)

    krt = _dot_nt(wt_ref[2 * w:3 * w, :], h)
    cos_t, sin_t = cost_ref[...], sint_ref[...]
    half = HEAD_DIM // 2
    for hh in range(N_HEADS):
        x1 = krt[hh * HEAD_DIM:hh * HEAD_DIM + half, :]
        x2 = krt[hh * HEAD_DIM + half:(hh + 1) * HEAD_DIM, :]
        krt_o[hh * HEAD_DIM:hh * HEAD_DIM + half, :] = (x1 * cos_t - x2 * sin_t) * QK_SCALE
        krt_o[hh * HEAD_DIM + half:(hh + 1) * HEAD_DIM, :] = (x1 * sin_t + x2 * cos_t) * QK_SCALE

    fg = _dot_nt(wt_ref[3 * w:, :], h)[0:N_HEADS, :] + bf_ref[...]
    lft_o[...] = _log_sigmoid(fg)


def _even_in_call(x, rows, ks, w_t_f32, bf_col, rot, batch, seq, decode=None, side=None):
    n, d = x.shape
    tm = rows.tm
    bps = seq // tm
    w = HEAD_WIDTH
    cos_tok, sin_tok, cos_t, sin_t = rot
    tok_out = pl.BlockSpec((tm, w), lambda i, *_: (i, 0))
    t_out = pl.BlockSpec((None, w, tm), lambda i, *_: (i // bps, 0, i % bps))
    tc_out = pl.BlockSpec((None, None, w, tm), lambda i, *_: (i // bps, i % bps, 0, 0))
    tok_shape = jax.ShapeDtypeStruct((n, w), BF16)
    in_specs = [rows.row_spec(d), rows.mod_spec(ks[0]), rows.mod_spec(ks[1]),
                _const_spec(w_t_f32), _const_spec(bf_col),
                pl.BlockSpec((tm, w), lambda i, *_: (i % bps, 0)),
                pl.BlockSpec((tm, w), lambda i, *_: (i % bps, 0)),
                pl.BlockSpec((HEAD_DIM // 2, tm), lambda i, *_: (0, i % bps)),
                pl.BlockSpec((HEAD_DIM // 2, tm), lambda i, *_: (0, i % bps))]
    out_specs = [tok_out, tok_out, tok_out, tok_out, t_out, t_out, tc_out, tc_out, tc_out,
                 pl.BlockSpec((None, N_HEADS, tm), lambda i, *_: (i // bps, 0, i % bps))]
    out_shape = [tok_shape, tok_shape, tok_shape, tok_shape,
                 jax.ShapeDtypeStruct((batch, w, seq), F32),
                 jax.ShapeDtypeStruct((batch, w, seq), F32),
                 jax.ShapeDtypeStruct((batch, bps, w, tm), BF16),
                 jax.ShapeDtypeStruct((batch, bps, w, tm), BF16),
                 jax.ShapeDtypeStruct((batch, bps, w, tm), F32),
                 jax.ShapeDtypeStruct((batch, N_HEADS, seq), F32)]

    return _run_host("even_inproj", _even_in_kernel, n // tm, in_specs, out_specs, out_shape,
                     (x, rows.mods, rows.mods, w_t_f32, bf_col, cos_tok, sin_tok, cos_t, sin_t),
                     [pltpu.VMEM((d, 4 * w), BF16), pltpu.VMEM((3 * w + FG_ROWS, d), BF16)], decode, side)


def _cumsum_kernel(x_ref, o_ref):
    x = x_ref[...]
    n = x.shape[-1]
    lane = lax.broadcasted_iota(jnp.int32, x.shape, 1)
    shift = 1
    while shift < n:
        x = x + jnp.where(lane >= shift, pltpu.roll(x, shift, 1), 0.0)
        shift *= 2
    o_ref[...] = x * LOG2E


def _cumsum_call(lft):
    batch, heads, seq = lft.shape
    spec = pl.BlockSpec((None, heads, seq), lambda b: (b, 0, 0))
    return pl.pallas_call(
        _cumsum_kernel, grid=(batch,), in_specs=[spec], out_specs=spec,
        out_shape=jax.ShapeDtypeStruct(lft.shape, F32),
        compiler_params=_cparams(("parallel",)),
        name="logf_cumsum",
    )(lft)


def _split3(x):
    hi = x.astype(BF16).astype(F32)
    r = x - hi
    mid = r.astype(BF16).astype(F32)
    lo = (r - mid).astype(BF16).astype(F32)
    return hi, mid, lo


def _fox_kernel(q_ref, kt_ref, vt_ref, c_ref, o_ref):
    tq = q_ref.shape[0]
    tk = kt_ref.shape[-1]
    pw = 2 * HEAD_DIM
    slab_rows = V7X_BF16_SUBLANES
    i = pl.program_id(2)
    q = q_ref[...]
    lane = lax.broadcasted_iota(jnp.int32, (tq, pw), 1)
    q_aug = []
    for hd in range(2):
        base = HEAD_DIM if hd == 0 else 0
        hi, mid, lo = _split3(c_ref[hd, pl.ds(i, 1), :][:, 0:1])
        aug = jnp.where(lane == base + 3, hi, jnp.where(lane == base + 4, mid, jnp.where(lane == base + 5, lo,
              jnp.where((lane >= base) & (lane < base + 3), 1.0, 0.0))))
        own = (lane < HEAD_DIM) if hd == 0 else (lane >= HEAD_DIM)
        q_aug.append(jnp.where(own, q, aug.astype(BF16)))
    r16 = lax.broadcasted_iota(jnp.int32, (slab_rows, tk), 0)
    rowv = lax.broadcasted_iota(jnp.int32, (pw, tk), 0)

    def scores(j, hd):
        kt = kt_ref[j]
        hi, mid, lo = _split3(-c_ref[hd, pl.ds(j, 1), :])
        slab = jnp.where(r16 == 0, hi, jnp.where(r16 == 1, mid, jnp.where(r16 == 2, lo,
               jnp.where(r16 < 6, 1.0, 0.0)))).astype(BF16)
        if hd == 0:
            kt_aug = jnp.concatenate([kt[0:HEAD_DIM], slab, kt[HEAD_DIM + slab_rows:]], axis=0)
        else:
            kt_aug = jnp.concatenate([slab, kt[slab_rows:]], axis=0)
        return _dot(q_aug[hd], kt_aug)

    def values(j, hd):
        vt = vt_ref[j]
        return jnp.where(rowv == (HEAD_DIM if hd == 0 else 0), jnp.ones_like(vt), vt)

    def update(s, vt_aug, m, acc):
        m_new = jnp.maximum(m, jnp.max(s, axis=-1, keepdims=True))
        return m_new, jnp.exp2(m - m_new) * acc + _dot_nt(jnp.exp2(s - m_new).astype(BF16), vt_aug)

    def diag_scores(hd):
        row = lax.broadcasted_iota(jnp.int32, (tq, tk), 0)
        col = lax.broadcasted_iota(jnp.int32, (tq, tk), 1)
        return jnp.where(row >= col, scores(i, hd), -jnp.inf)

    group = FOX_BLOCKS_PER_UPDATE

    def joint(carry, full_blocks, with_diag):
        out = []
        for hd in range(2):
            s = [scores(j, hd) for j in full_blocks] + ([diag_scores(hd)] if with_diag else [])
            v = [values(j, hd) for j in full_blocks] + ([values(i, hd)] if with_diag else [])
            out.append(update(jnp.concatenate(s, axis=1), jnp.concatenate(v, axis=1), *carry[hd]))
        return tuple(out)

    def tail(n_full):
        return lambda carry: tuple(c[1] for c in joint(carry, [i - n_full + k for k in range(n_full)], True))

    init_head = (jnp.full((tq, 1), -jnp.inf, F32), jnp.zeros((tq, pw), F32))
    carry = lax.fori_loop(0, lax.shift_right_logical(i, group.bit_length() - 1),
                          lambda t, c: joint(c, [group * t + k for k in range(group)], False),
                          (init_head, init_head))
    outs = lax.switch(i & (group - 1), [tail(n) for n in range(group)], carry)
    o0 = outs[0] / outs[0][:, HEAD_DIM:HEAD_DIM + 1]
    o1 = outs[1] / outs[1][:, 0:1]
    o_ref[...] = jnp.where(lane < HEAD_DIM, o0, o1).astype(BF16)


def _fox_call(q, ktc, vtc, c4, batch, seq):
    n, w = q.shape
    tq = ktc.shape[-1]
    nq = seq // tq
    pw = 2 * HEAD_DIM
    return pl.pallas_call(
        _fox_kernel,
        grid=(batch, HEAD_PAIRS, nq),
        in_specs=[pl.BlockSpec((tq, pw), lambda b, p, i: (b * nq + i, p)),
                  pl.BlockSpec((None, nq, pw, tq), lambda b, p, i: (b, 0, p, 0)),
                  pl.BlockSpec((None, nq, pw, tq), lambda b, p, i: (b, 0, p, 0)),
                  pl.BlockSpec((None, 2, nq, tq), lambda b, p, i: (b, p, 0, 0))],
        out_specs=pl.BlockSpec((tq, pw), lambda b, p, i: (b * nq + i, p)),
        out_shape=jax.ShapeDtypeStruct((n, w), BF16),
        compiler_params=_cparams(("parallel", "parallel", "arbitrary"), big=True),
        name="fox_prompt",
    )(q, ktc, vtc, c4)


def _ret_kernel(q_ref, kt_ref, v_ref, sg_ref, lgl_ref, lgr_ref, y_ref, s_ref):
    chunk = RET_CHUNK
    pw = 2 * HEAD_DIM
    nkb, _, tb = kt_ref.shape
    lg_lane = lgl_ref[...]
    lg_row = lgr_ref[...]
    lg_a, lg_b = lg_lane[:, 0:1], lg_lane[:, HEAD_DIM:HEAD_DIM + 1]

    ri = lax.broadcasted_iota(jnp.int32, (2 * chunk, chunk), 0)
    cj = lax.broadcasted_iota(jnp.int32, (2 * chunk, chunk), 1)
    first = ri < chunk
    diff = (jnp.where(first, ri, ri - chunk) - cj).astype(F32)
    decay_mask = jnp.where(diff >= 0, jnp.exp(jnp.maximum(diff, 0.0) * jnp.where(first, lg_a, lg_b)), 0.0)
    jj = lax.broadcasted_iota(jnp.int32, (pw, chunk), 1).astype(F32)
    col_decay = jnp.exp((chunk - 1.0 - jj) * lg_row)
    ii = lax.broadcasted_iota(jnp.int32, (chunk, pw), 0).astype(F32)
    row_decay = jnp.exp((ii + 1.0) * lg_lane)
    chunk_decay = jnp.exp(float(chunk) * lg_row)
    r2 = lax.broadcasted_iota(jnp.int32, (pw, pw), 0)
    c2 = lax.broadcasted_iota(jnp.int32, (pw, pw), 1)
    same_head = (r2 < HEAD_DIM) == (c2 < HEAD_DIM)
    seg_avg = jnp.where(same_head, 1.0 / HEAD_DIM, 0.0).astype(BF16)
    lane = lax.broadcasted_iota(jnp.int32, (chunk, pw), 1)

    def seg_mean(a):
        return _dot(a.astype(BF16), seg_avg)

    def one_chunk(q, kt, v, sg, state):
        zero = jnp.zeros_like(q)
        q_stack = jnp.concatenate([jnp.where(lane < HEAD_DIM, q, zero), jnp.where(lane >= HEAD_DIM, q, zero)], axis=0)
        qk = _dot(q_stack, kt.astype(BF16)) * decay_mask
        kd = (kt * col_decay).astype(BF16)
        r = _dot(jnp.concatenate([qk.astype(BF16), kd], axis=0), v)
        inner = jnp.where(lane < HEAD_DIM, r[0:chunk], r[chunk:2 * chunk])
        update = jnp.where(same_head, r[2 * chunk:], 0.0)
        cross = _dot(q, state.astype(BF16)) * row_decay
        o = inner + cross
        mu = seg_mean(o)
        d = o - mu
        var = seg_mean(d * d)
        y = sg.astype(F32) * (d * lax.rsqrt(var + GN_EPS))
        return y.astype(BF16), chunk_decay * state + update

    def body(jb, state):
        kt_blk = kt_ref[jb]
        for sub in range(tb // chunk):
            t0 = pl.multiple_of(jb * tb + sub * chunk, chunk)
            y, state = one_chunk(q_ref[pl.ds(t0, chunk), :], kt_blk[:, sub * chunk:(sub + 1) * chunk],
                                 v_ref[pl.ds(t0, chunk), :], sg_ref[pl.ds(t0, chunk), :], state)
            y_ref[pl.ds(t0, chunk), :] = y
        return state

    state = lax.fori_loop(0, nkb, body, jnp.zeros((pw, pw), F32), unroll=8)
    s_ref[0] = state[0:HEAD_DIM, 0:HEAD_DIM]
    s_ref[1] = pltpu.roll(state, HEAD_DIM, 1)[HEAD_DIM:, 0:HEAD_DIM]


def _ret_call(qr, krt, vr, sg, lg_lane, lg_row, batch, seq):
    n, w = qr.shape
    nkb, tb = krt.shape[1], krt.shape[3]
    pw = 2 * HEAD_DIM
    seq_spec = pl.BlockSpec((seq, pw), lambda b, p: (b, p))
    return pl.pallas_call(
        _ret_kernel,
        grid=(batch, HEAD_PAIRS),
        in_specs=[seq_spec,
                  pl.BlockSpec((None, nkb, pw, tb), lambda b, p: (b, 0, p, 0)),
                  seq_spec, seq_spec,
                  pl.BlockSpec((None, 1, pw), lambda b, p: (p, 0, 0)),
                  pl.BlockSpec((None, pw, 1), lambda b, p: (p, 0, 0))],
        out_specs=[seq_spec, pl.BlockSpec((None, 2, HEAD_DIM, HEAD_DIM), lambda b, p: (b, p, 0, 0))],
        out_shape=[jax.ShapeDtypeStruct((n, w), BF16),
                   jax.ShapeDtypeStruct((batch, N_HEADS, HEAD_DIM, HEAD_DIM), F32)],
        compiler_params=_cparams(("parallel", "parallel"), big=True),
        name="ret_prompt",
    )(qr, krt, vr, sg, lg_lane, lg_row)


def _odd_prompt_kernel(x_ref, sh_ref, sc_ref, g_ref, win_ref, cw_ref, wout_ref, lng_ref, lnb_ref,
                       o_ref, st_ref, carry_ref, *, bps, alpha):
    i = pl.program_id(0)
    x = x_ref[...]
    tm, d = x.shape
    h = (x * (1.0 + sc_ref[...]) + sh_ref[...]).astype(BF16)
    b_gate = _dot(h, win_ref[:, 0:d])
    u = _dot(h, win_ref[:, d:2 * d]) * _dot(h, win_ref[:, 2 * d:3 * d])

    @pl.when(i % bps == 0)
    def _():
        carry_ref[...] = jnp.zeros_like(carry_ref)

    prev = carry_ref[...]
    p1, p2 = prev[7:8, :], prev[6:7, :]
    row = lax.broadcasted_iota(jnp.int32, (tm, d), 0)
    u1 = jnp.where(row == 0, p1, pltpu.roll(u, 1, 0))
    u2 = jnp.where(row == 0, p2, jnp.where(row == 1, p1, pltpu.roll(u, 2, 0)))
    cw = cw_ref[...]
    z = cw[0:1, :] * u2 + cw[1:2, :] * u1 + cw[2:3, :] * u
    carry_ref[...] = u[tm - 8:, :]
    st_ref[...] = u[tm - (CONV_WIDTH - 1):, :]
    y = _dot((b_gate * z).astype(BF16), wout_ref[...])
    zz = alpha * x + g_ref[...] * y
    o_ref[...] = _layernorm(zz, lng_ref[...], lnb_ref[...])


def _odd_prompt_call(x, rows, ks, w_in, conv_w, w_out, ln_g, ln_b, ln_idx, alpha, batch, seq, decode=None,
                     side=None):
    n, d = x.shape
    bps = seq // rows.tm
    in_specs = [rows.row_spec(d), rows.mod_spec(ks[0]), rows.mod_spec(ks[1]), rows.mod_spec(ks[2]),
                _const_spec(w_in), _const_spec(conv_w), _const_spec(w_out),
                pl.BlockSpec((None, 1, d), lambda i, *_: (ln_idx, 0, 0)),
                pl.BlockSpec((None, 1, d), lambda i, *_: (ln_idx, 0, 0))]
    out_specs = [rows.row_spec(d), pl.BlockSpec((None, CONV_WIDTH - 1, d), lambda i, *_: (i // bps, 0, 0))]
    out_shape = [jax.ShapeDtypeStruct((n, d), F32), jax.ShapeDtypeStruct((batch, CONV_WIDTH - 1, d), F32)]

    return _run_host("odd_prompt", functools.partial(_odd_prompt_kernel, bps=bps, alpha=alpha), n // rows.tm,
                     in_specs, out_specs, out_shape, (x, rows.mods, rows.mods, rows.mods, w_in, conv_w, w_out, ln_g, ln_b),
                     [pltpu.VMEM((8, d), F32)], decode, side)


def _even_in_sample_kernel(x_ref, sh_ref, sc_ref, w_ref, bf_ref, cost_ref, sint_ref,
                           q_o, k_o, v_o, qrt_o, krt_o, vr_o, sg_o, lf_o):
    d = w_ref.shape[1]
    h = (x_ref[...] * (1.0 + sc_ref[...]) + sh_ref[...]).astype(BF16)

    def rows(name):
        lo, hi = _EVEN_ROWS[name]
        return w_ref[lo:hi, :]

    def tok(name):
        return _dot_nt(h, rows(name).astype(BF16))

    q_o[...] = tok("qf") * QK_SCALE
    k_o[...] = tok("kf")
    v_o[...] = tok("vf")
    vr_o[...] = tok("vr")
    sg_o[...] = _silu(tok("gr"))
    fg_rows = jnp.concatenate([rows("fg"), jnp.zeros((V7X_LANES - N_HEADS, d), F32)], axis=0).astype(BF16)
    lf_o[...] = _log_sigmoid(_dot_nt(h, fg_rows) + bf_ref[...])

    cos_t, sin_t = cost_ref[...], sint_ref[...]
    half = HEAD_DIM // 2
    for name, out, scale in (("qr", qrt_o, 1.0), ("kr", krt_o, QK_SCALE)):
        t = _dot_nt(rows(name).astype(BF16), h)
        for hh in range(N_HEADS):
            x1 = t[hh * HEAD_DIM:hh * HEAD_DIM + half, :]
            x2 = t[hh * HEAD_DIM + half:(hh + 1) * HEAD_DIM, :]
            out[hh * HEAD_DIM:hh * HEAD_DIM + half, :] = (x1 * cos_t - x2 * sin_t) * scale
            out[hh * HEAD_DIM + half:(hh + 1) * HEAD_DIM, :] = (x1 * sin_t + x2 * cos_t) * scale


def _even_in_sample_call(x, rows, ks, w_t_f32, bf_row, cos_t, sin_t):
    n, d = x.shape
    w = HEAD_WIDTH
    full = pl.BlockSpec((n, w), lambda i: (0, 0))
    full_t = pl.BlockSpec((w, n), lambda i: (0, 0))
    rot = pl.BlockSpec((HEAD_DIM // 2, n), lambda i: (0, 0))
    shp = jax.ShapeDtypeStruct((n, w), F32)
    shp_t = jax.ShapeDtypeStruct((w, n), F32)
    return pl.pallas_call(
        _even_in_sample_kernel,
        grid=(1,),
        in_specs=[rows.row_spec(d), rows.mod_spec(ks[0]), rows.mod_spec(ks[1]),
                  _const_spec(w_t_f32), _const_spec(bf_row), rot, rot],
        out_specs=[full, full, full, full_t, full_t, full, full, pl.BlockSpec((n, V7X_LANES), lambda i: (0, 0))],
        out_shape=[shp, shp, shp, shp_t, shp_t, shp, shp, jax.ShapeDtypeStruct((n, V7X_LANES), F32)],
        compiler_params=_cparams(("arbitrary",), big=True),
        name="even_inproj_sample",
    )(x, rows.mods, rows.mods, w_t_f32, bf_row, cos_t, sin_t)


def _own_head_lanes():
    sub = lax.broadcasted_iota(jnp.int32, (N_HEADS, HEAD_WIDTH), 0)
    lane = lax.broadcasted_iota(jnp.int32, (N_HEADS, HEAD_WIDTH), 1)
    return (lane >= sub * HEAD_DIM) & (lane < (sub + 1) * HEAD_DIM)


def _decode_fox_init(q_ref, kn_ref, vn_ref, lfn_ref, qbd_s, m_s, l_s, run_s, acc_s):
    w = HEAD_WIDTH
    q_bd = jnp.where(_own_head_lanes(), jnp.broadcast_to(q_ref[...], (N_HEADS, w)), 0.0)
    qbd_s[...] = q_bd
    m_s[...] = jnp.sum(q_bd * kn_ref[...], axis=-1, keepdims=True)
    l_s[...] = jnp.ones_like(l_s)
    page_len = acc_s.shape[1]
    lane_p = lax.broadcasted_iota(jnp.int32, (w, page_len), 1)
    acc_s[...] = jnp.where(lane_p == 0, jnp.broadcast_to(vn_ref[...], (page_len, w)).T, 0.0)
    s128 = lax.broadcasted_iota(jnp.int32, (N_HEADS, V7X_LANES), 0)
    l128 = lax.broadcasted_iota(jnp.int32, (N_HEADS, V7X_LANES), 1)
    lfn = jnp.broadcast_to(lfn_ref[...], (N_HEADS, V7X_LANES))
    run_s[...] = jnp.sum(jnp.where(s128 == l128, lfn, 0.0), axis=-1, keepdims=True)


def _decode_fox_finish(o_ref, l_s, acc_s):
    tot = jnp.sum(acc_s[...].T, axis=0, keepdims=True)
    l_b = jnp.broadcast_to(l_s[...], (N_HEADS, HEAD_WIDTH))
    o_ref[...] = tot / jnp.sum(jnp.where(_own_head_lanes(), l_b, 0.0), axis=0, keepdims=True)


def _decode_fox_pages(k_refs, v_refs, lf_refs, qbd_s, m_s, l_s, run_s, acc_s):
    pages = len(k_refs)
    page_len = k_refs[0].shape[-1]
    q_b = qbd_s[...].astype(BF16)
    order = list(range(pages - 1, -1, -1))
    s = jnp.concatenate([_dot(q_b, k_refs[r][...].astype(BF16)) for r in order], axis=1)
    lf = jnp.concatenate([lf_refs[r][...] for r in order], axis=1)
    n = pages * page_len
    lane_n = lax.broadcasted_iota(jnp.int32, (N_HEADS, n), 1)
    suf = lf
    shift = 1
    while shift < n:
        suf = suf + jnp.where(lane_n < n - shift, pltpu.roll(suf, n - shift, 1), 0.0)
        shift *= 2
    run = run_s[...]
    s = s + ((suf - lf) + run)
    m = m_s[...]
    m_new = jnp.maximum(m, jnp.max(s, axis=-1, keepdims=True))
    a = jnp.exp(m - m_new)
    p = jnp.exp(s - m_new)
    l_s[...] = a * l_s[...] + jnp.sum(p, axis=-1, keepdims=True)
    m_s[...] = m_new
    for h in range(N_HEADS):
        sl = slice(h * HEAD_DIM, (h + 1) * HEAD_DIM)
        upd = None
        for idx, r in enumerate(order):
            t = p[h:h + 1, idx * page_len:(idx + 1) * page_len] * v_refs[r][sl, :]
            upd = t if upd is None else upd + t
        acc_s[sl, :] = a[h:h + 1, :] * acc_s[sl, :] + upd
    run_s[...] = run + suf[:, 0:1]


def _decode_ret_kernel(qt_ref, kt_ref, v_ref, sg_ref, s_ref, gl_ref, y_ref, so_ref):
    qt, kt = qt_ref[...], kt_ref[...]
    g = gl_ref[...]
    for b in range(qt.shape[-1]):
        q, k = qt[:, :, b:b + 1], kt[:, :, b:b + 1]
        v = v_ref[b]
        state = s_ref[b]
        inner = jnp.sum(q * k, axis=1, keepdims=True) * v
        cross = jnp.sum(q * state, axis=1, keepdims=True) * g
        so_ref[b] = g * state + k * v
        o = inner + cross
        mu = jnp.mean(o, axis=-1, keepdims=True)
        d = o - mu
        var = jnp.mean(d * d, axis=-1, keepdims=True)
        y_ref[b] = sg_ref[b] * (d * lax.rsqrt(var + GN_EPS))


def _decode_ret_call(q_t, k_t, v_row, sg_row, state, decay):
    nseq = state.shape[0]

    def whole(arr):
        nd = arr.ndim
        return pl.BlockSpec(arr.shape, lambda i: (0,) * nd)

    return pl.pallas_call(
        _decode_ret_kernel,
        grid=(1,),
        in_specs=[whole(a) for a in (q_t, k_t, v_row, sg_row, state, decay)],
        out_specs=[whole(v_row), whole(state)],
        out_shape=[jax.ShapeDtypeStruct(v_row.shape, F32), jax.ShapeDtypeStruct(state.shape, F32)],
        compiler_params=_cparams(("arbitrary",), big=True),
        name="ret_decode",
    )(q_t, k_t, v_row, sg_row, state, decay)


def _odd_sample_kernel(x_ref, sh_ref, sc_ref, g_ref, win_ref, cw_ref, b0_ref, b1_ref, wout_ref, lng_ref, lnb_ref,
                       o_ref, u_ref, *, alpha):
    x = x_ref[...]
    d = x.shape[-1]
    h = (x * (1.0 + sc_ref[...]) + sh_ref[...]).astype(BF16)
    b_gate = _dot(h, win_ref[:, 0:d])
    u = _dot(h, win_ref[:, d:2 * d]) * _dot(h, win_ref[:, 2 * d:3 * d])
    cw = cw_ref[...]
    z = cw[0:1, :] * b0_ref[...] + cw[1:2, :] * b1_ref[...] + cw[2:3, :] * u
    u_ref[...] = u
    y = _dot((b_gate * z).astype(BF16), wout_ref[...])
    o_ref[...] = _layernorm(alpha * x + g_ref[...] * y, lng_ref[...], lnb_ref[...])


def _odd_sample_call(x, rows, ks, w_in, conv_w, buf0, buf1, w_out, ln_g, ln_b, ln_idx, alpha):
    n, d = x.shape
    full = pl.BlockSpec((n, d), lambda i: (0, 0))
    return pl.pallas_call(
        functools.partial(_odd_sample_kernel, alpha=alpha),
        grid=(1,),
        in_specs=[rows.row_spec(d), rows.mod_spec(ks[0]), rows.mod_spec(ks[1]), rows.mod_spec(ks[2]),
                  _const_spec(w_in), _const_spec(conv_w), full, full, _const_spec(w_out),
                  pl.BlockSpec((None, 1, d), lambda i: (ln_idx, 0, 0)),
                  pl.BlockSpec((None, 1, d), lambda i: (ln_idx, 0, 0))],
        out_specs=[full, full],
        out_shape=[jax.ShapeDtypeStruct((n, d), F32), jax.ShapeDtypeStruct((n, d), F32)],
        compiler_params=_cparams(("arbitrary",), big=True),
        name="odd_sample",
    )(x, rows.mods, rows.mods, rows.mods, w_in, conv_w, buf0, buf1, w_out, ln_g, ln_b)


def _rotary_tables(pos):
    half = HEAD_DIM // 2
    inv = 1.0 / (RET_ANGLE_BASE ** np.linspace(0.0, 1.0, half))
    ang = np.asarray(pos, np.float64)[:, None] * inv[None, :]
    return np.cos(ang), np.sin(ang)


def _token_major_tables(cos, sin):
    cos_h = np.concatenate([cos, cos], axis=1)
    sin_h = np.concatenate([-sin, sin], axis=1)
    return (jnp.asarray(np.tile(cos_h, (1, N_HEADS)), F32), jnp.asarray(np.tile(sin_h, (1, N_HEADS)), F32))


def kernel(x_prompt, x_sample, cache_k, cache_v, cache_logf, state_ret, state_conv, page_table, c_prompt, c_sample,
           w_ada, b_ada, w_ffn_in, w_ffn_out, ln_g, ln_b, w_in_even, b_forget, w_out_even, w_in_odd, conv_w,
           w_out_odd):
    batch, seq, d = x_prompt.shape
    nseq = x_sample.shape[0]
    depth = w_ada.shape[0]
    past_len = page_table.shape[1] * cache_k.shape[2]
    alpha = (2.0 * depth) ** 0.25
    w = HEAD_WIDTH
    tm = ROW_BLOCK
    assert seq % tm == 0 and x_sample.shape[1] == 1 and d % V7X_LANES == 0
    assert cache_k.shape[3] == N_HEADS and cache_k.shape[4] == HEAD_DIM

    n_c = batch + nseq
    pad = (-n_c) % 8
    c_all = jnp.concatenate([c_prompt, c_sample, jnp.zeros((pad, d), F32)], axis=0)
    mods = _ada_call(c_all, w_ada, b_ada)
    mods_p = mods[:, :batch].reshape(depth, batch, 1, N_MOD * d)
    mods_s = mods[:, batch:n_c]

    ln_g3 = ln_g.reshape(depth * 3, 1, d)
    ln_b3 = ln_b.reshape(depth * 3, 1, d)
    w_bf = {(0, 0): (w_ffn_in[0, 0].astype(BF16), w_ffn_out[0, 0].astype(BF16))}

    def side_cast(l, half):
        return _SideCast(w_ffn_in, w_ffn_out, (l, half), batch * seq // tm)

    xp = x_prompt.reshape(batch * seq, d)
    xs = x_sample.reshape(nseq, d)

    cos_p, sin_p = _rotary_tables(np.arange(seq))
    rot_p = _token_major_tables(cos_p, sin_p) + (jnp.asarray(cos_p.T, F32), jnp.asarray(sin_p.T, F32))
    cos_s, sin_s = (jnp.asarray(t.T, F32) for t in _rotary_tables(np.full((nseq,), past_len)))

    log_decay = jnp.log(1.0 - 2.0 ** (-5.0 - jnp.arange(N_HEADS, dtype=F32)))
    lg_pairs = jnp.repeat(log_decay.reshape(HEAD_PAIRS, 2), HEAD_DIM, axis=1)
    lg_lane = lg_pairs.reshape(HEAD_PAIRS, 1, 2 * HEAD_DIM)
    lg_row = lg_pairs.reshape(HEAD_PAIRS, 2 * HEAD_DIM, 1)
    step_decay = jnp.exp(log_decay).reshape(N_HEADS, 1, 1)

    outs_p = {"k": [], "v": [], "lf": [], "ret": [], "conv": []}
    outs_s = {"k": [], "v": [], "lf": [], "ret": [], "conv": []}

    assert depth == 2 and w_in_even.shape[0] == 1 and cache_k.shape[0] == 1
    n_hosts = 2 * depth
    assert nseq % n_hosts == 0
    assert w_in_even.shape[2] == EVEN_IN_ROWS
    w_even_t = jnp.transpose(w_in_even[0])
    w_out_even_b = w_out_even[0].astype(BF16)
    bf = b_forget[0]
    w_in_odd_b = w_in_odd[0].astype(BF16)
    w_out_odd_b = w_out_odd[0].astype(BF16)

    def rows_s(l):
        return _Rows(mods_s, l, nseq, None, d)

    def rows_p(l):
        return _Rows(mods_p, l, tm, seq // tm, d)

    def ffn(x, rows, half, l, decode=None, mixer_out=None, side=None):
        ks = (0, 1, 2) if half == 0 else (6, 7, 8)
        return _ffn_call(x, rows, ks, *w_bf[(l, half)], ln_g3, ln_b3, 3 * l + 2 * half, alpha, decode, mixer_out, side)

    xs = ffn(xs, rows_s(0), 0, 0)
    bf_row = jnp.concatenate([bf, jnp.zeros((V7X_LANES - N_HEADS,), F32)]).reshape(1, V7X_LANES)
    (qs, ks_, vs, qrs_t, krs_t, vrs, sgs, lfs) = _even_in_sample_call(xs, rows_s(0), (3, 4), w_even_t, bf_row,
                                                                       cos_s, sin_s)
    n_phys, page_len = cache_k.shape[1], cache_k.shape[2]
    kt_pages = jnp.transpose(cache_k[0], (0, 2, 3, 1)).reshape(n_phys, w, page_len)
    vt_pages = jnp.transpose(cache_v[0], (0, 2, 3, 1)).reshape(n_phys, w, page_len)
    lft_pages = jnp.transpose(cache_logf[0], (0, 2, 1))
    per_host = nseq // n_hosts
    dec_args = (page_table, qs.reshape(nseq, 1, w), ks_.reshape(nseq, 1, w), vs.reshape(nseq, 1, w),
                lfs.reshape(nseq, 1, V7X_LANES), kt_pages, vt_pages, lft_pages)
    dec_out = []

    def ffn_host(x, half, l, mixer_out=None, cast_for=None):
        side = None if cast_for is None else side_cast(*cast_for)
        res = ffn(x, rows_p(l), half, l, dec_args + (len(dec_out) * per_host, per_host), mixer_out, side)
        dec_out.append(res[1])
        if cast_for is not None:
            w_bf[cast_for] = tuple(res[2:])
        return res[0]

    xp = ffn_host(xp, 0, 0, cast_for=(1, 0))
    (q, qr, vr, sg, kt, vt, ktc, vtc, krt, lft, *w_bf[(0, 1)]) = _even_in_call(
        xp, rows_p(0), (3, 4), w_even_t, bf.reshape(N_HEADS, 1), rot_p, batch, seq, side=side_cast(0, 1))
    c_t = _cumsum_call(lft)
    of = _fox_call(q, ktc, vtc, c_t.reshape(batch, N_HEADS, seq // tm, tm), batch, seq)
    yr, s_new = _ret_call(qr, krt, vr, sg, lg_lane, lg_row, batch, seq)
    outs_p["k"].append(jnp.transpose(kt.reshape(batch, N_HEADS, HEAD_DIM, seq), (0, 3, 1, 2)))
    outs_p["v"].append(jnp.transpose(vt.reshape(batch, N_HEADS, HEAD_DIM, seq), (0, 3, 1, 2)))
    outs_p["lf"].append(jnp.transpose(lft, (0, 2, 1)))
    outs_p["ret"].append(s_new)
    xp = ffn_host(xp, 1, 0, mixer_out=(5, of, yr, w_out_even_b, 1))
    xp = ffn_host(xp, 0, 1)
    xp, conv_p, *w_bf[(1, 1)] = _odd_prompt_call(xp, rows_p(1), (3, 4, 5), w_in_odd_b, conv_w[0], w_out_odd_b,
                                                 ln_g3, ln_b3, 4, alpha, batch, seq, side=side_cast(1, 1))
    outs_p["conv"].append(conv_p)
    xp = ffn_host(xp, 1, 1)

    of_s = jnp.concatenate(dec_out, axis=0).reshape(nseq, w)
    lfs = lfs[:, :N_HEADS]
    hdn = (N_HEADS, HEAD_DIM, nseq)
    rw = (nseq, N_HEADS, 1, HEAD_DIM)
    yr_s, s_new_s = _decode_ret_call(qrs_t.reshape(hdn), krs_t.reshape(hdn), vrs.reshape(rw), sgs.reshape(rw),
                                     state_ret[0], step_decay)
    xs = _outproj_call(xs, rows_s(0), 5, of_s, yr_s.reshape(nseq, w), w_out_even_b, ln_g3, ln_b3, 1, alpha)
    outs_s["k"].append(ks_.reshape(nseq, 1, N_HEADS, HEAD_DIM))
    outs_s["v"].append(vs.reshape(nseq, 1, N_HEADS, HEAD_DIM))
    outs_s["lf"].append(lfs.reshape(nseq, 1, N_HEADS))
    outs_s["ret"].append(s_new_s)
    xs = ffn(xs, rows_s(0), 1, 0)
    xs = ffn(xs, rows_s(1), 0, 1)
    buf = state_conv[0]
    xs, u_s = _odd_sample_call(xs, rows_s(1), (3, 4, 5), w_in_odd_b, conv_w[0], buf[:, 0], buf[:, 1], w_out_odd_b,
                               ln_g3, ln_b3, 4, alpha)
    outs_s["conv"].append(jnp.stack([buf[:, 1], u_s], axis=1))
    xs = ffn(xs, rows_s(1), 1, 1)

    def stk(lst):
        return jnp.stack(lst)

    return (xp.reshape(batch, seq, d), xs.reshape(nseq, 1, d),
            stk(outs_p["k"]), stk(outs_p["v"]), stk(outs_p["lf"]), stk(outs_p["ret"]), stk(outs_p["conv"]),
            stk(outs_s["k"]), stk(outs_s["v"]), stk(outs_s["lf"]), stk(outs_s["ret"]), stk(outs_s["conv"]))
```

```python
import functools

import jax
import jax.numpy as jnp
import numpy as np
from jax import lax
from jax.experimental import pallas as pl
from jax.experimental.pallas import tpu as pltpu

F32 = jnp.float32
BF16 = jnp.bfloat16

HEAD_DIM = 64
N_HEADS = 8
HEAD_WIDTH = N_HEADS * HEAD_DIM
N_MOD = 9
CONV_WIDTH = 3
RET_ANGLE_BASE = 10000.0
LN_EPS = 1e-5
GN_EPS = 1e-6
QK_SCALE = HEAD_DIM ** -0.5
LOG2E = 1.4426950408889634

V7X_LANES = 128
V7X_BF16_SUBLANES = 16
V7X_VMEM_LIMIT_BYTES = 56 * 1024 * 1024

ROW_BLOCK = 512
FF_CHUNK = 256
FOX_BLOCKS_PER_UPDATE = 4
RET_CHUNK = 256
ADA_COL_BLOCK = 1152
HEAD_PAIRS = N_HEADS // 2


def _cparams(sem, big=False):
    return pltpu.CompilerParams(dimension_semantics=sem,
                                vmem_limit_bytes=V7X_VMEM_LIMIT_BYTES if big else None)


def _const_spec(arr):
    nd = arr.ndim
    return pl.BlockSpec(arr.shape, lambda *_: (0,) * nd, pipeline_mode=pl.Buffered(1))


def _layernorm(z, g, b):
    mu = jnp.mean(z, axis=-1, keepdims=True)
    d = z - mu
    var = jnp.mean(d * d, axis=-1, keepdims=True)
    return d * lax.rsqrt(var + LN_EPS) * g + b


def _silu(a):
    return a * jax.nn.sigmoid(a)


def _log_sigmoid(z):
    return jnp.minimum(z, 0.0) - jnp.log1p(jnp.exp(-jnp.abs(z)))


def _dot(a, b):
    return jnp.dot(a, b, preferred_element_type=F32)


def _dot_nt(a, b):
    return lax.dot_general(a, b, (((1,), (1,)), ((), ())), preferred_element_type=F32)


def _ada_kernel(c_ref, w_ref, b_ref, o_ref):
    s = _silu(c_ref[...]).astype(BF16)
    o_ref[...] = _dot(s, w_ref[...].astype(BF16)) + b_ref[...]


def _ada_call(c_all, w_ada, b_ada):
    depth, d, nm = w_ada.shape
    rows = c_all.shape[0]
    tn = ADA_COL_BLOCK
    return pl.pallas_call(
        _ada_kernel,
        grid=(depth, nm // tn),
        in_specs=[pl.BlockSpec((rows, d), lambda l, j: (0, 0)),
                  pl.BlockSpec((None, d, tn), lambda l, j: (l, 0, j)),
                  pl.BlockSpec((None, 1, tn), lambda l, j: (l, 0, j))],
        out_specs=pl.BlockSpec((None, rows, tn), lambda l, j: (l, 0, j)),
        out_shape=jax.ShapeDtypeStruct((depth, rows, nm), F32),
        compiler_params=_cparams(("arbitrary", "arbitrary"), big=True),
        name="ada_mods",
    )(c_all, w_ada, b_ada.reshape(depth, 1, nm))


class _SideCast:
    def __init__(self, w_in, w_out, which, steps):
        l, j = which
        d, ff2 = w_in.shape[2:]
        ff, d2 = w_out.shape[2:]
        rin, nout = d // steps, steps // 2
        rout = ff // nout
        assert rin * steps == d and rout * nout == ff and rin % V7X_BF16_SUBLANES == 0 and rout % V7X_BF16_SUBLANES == 0
        self.args = (w_in, w_out)
        self.in_specs = [pl.BlockSpec((None, None, rin, ff2), lambda i, *_: (l, j, i, 0)),
                         pl.BlockSpec((None, None, rout, d2), lambda i, *_: (l, j, jnp.minimum(i, nout - 1), 0))]
        self.out_specs = [pl.BlockSpec((rin, ff2), lambda i, *_: (i, 0)),
                          pl.BlockSpec((rout, d2), lambda i, *_: (jnp.minimum(i, nout - 1), 0))]
        self.out_shape = [jax.ShapeDtypeStruct((d, ff2), BF16), jax.ShapeDtypeStruct((ff, d2), BF16)]


def _with_side_cast(kernel, n_prefetch, n_in, n_out):
    def wrapped(*refs):
        a, b = n_prefetch + n_in, n_prefetch + n_in + 2 + n_out
        wi_ref, wo_ref = refs[a:a + 2]
        wib_ref, wob_ref = refs[b:b + 2]
        wib_ref[...] = wi_ref[...].astype(BF16)
        wob_ref[...] = wo_ref[...].astype(BF16)
        kernel(*refs[:a], *refs[a + 2:b], *refs[b + 2:])

    return wrapped


class _Rows:
    def __init__(self, mods, layer, tm, blocks_per_seq, d):
        self.mods, self.layer, self.tm, self.bps, self.d = mods, layer, tm, blocks_per_seq, d

    def mod_spec(self, k):
        l, d = self.layer, self.d
        if self.bps is None:
            return pl.BlockSpec((None, self.tm, d), lambda i, *_: (l, 0, k))
        bps = self.bps
        return pl.BlockSpec((None, None, 1, d), lambda i, *_: (l, i // bps, 0, k))

    def row_spec(self, width):
        return pl.BlockSpec((self.tm, width), lambda i, *_: (i, 0))


def _mixer_out_math(x, gate, a0, a1, w_ref, lng, lnb, alpha):
    half = a0.shape[-1]
    y = _dot(a0.astype(BF16), w_ref[0:half, :]) + _dot(a1.astype(BF16), w_ref[half:, :])
    return _layernorm(alpha * x + gate * y, lng, lnb)


def _ffn_rows(refs, ff, alpha, pre):
    x = refs[0][...]
    refs = refs[1:]
    if pre:
        gate, a0, a1, wo_ref, lng1, lnb1 = refs[:6]
        x = _mixer_out_math(x, gate[...], a0[...], a1[...], wo_ref, lng1[...], lnb1[...], alpha)
        refs = refs[6:]
    sh_ref, sc_ref, g_ref, win_ref, wout_ref, lng_ref, lnb_ref = refs
    h = (x * (1.0 + sc_ref[...]) + sh_ref[...]).astype(BF16)
    acc = None
    for c in range(ff // FF_CHUNK):
        lo, hi = c * FF_CHUNK, (c + 1) * FF_CHUNK
        a = _dot(h, win_ref[:, lo:hi])
        b = _dot(h, win_ref[:, ff + lo:ff + hi])
        y = _dot((_silu(a) * b).astype(BF16), wout_ref[lo:hi, :])
        acc = y if acc is None else acc + y
    z = alpha * x + (0.5 * g_ref[...]) * acc
    return _layernorm(z, lng_ref[...], lnb_ref[...])


def _ffn_kernel(*refs, ff, alpha, pre):
    refs[-1][...] = _ffn_rows(refs[:-1], ff, alpha, pre)


def _with_decode(kernel, n_in, n_out, pages, steps_per_seq, seq0):
    def wrapped(pt_ref, *refs):
        ins = refs[:n_in]
        q_ref, kn_ref, vn_ref, lfn_ref, kt_hbm, vt_hbm, lf_hbm = refs[n_in:n_in + 7]
        outs = refs[n_in + 7:n_in + 7 + n_out]
        dec_ref = refs[n_in + 7 + n_out]
        scratch = refs[n_in + 8 + n_out:]
        own, state = scratch[:-9], scratch[-9:-4]
        kbuf, vbuf, lfbuf, sem = scratch[-4:]
        npg = pt_ref.shape[1]
        i = pl.program_id(0)
        last = pl.num_programs(0) - 1
        slot = i % 2

        def page_copies(step, into):
            seq = seq0 + step // steps_per_seq
            first = npg - 1 - (step % steps_per_seq) * pages
            out = []
            for r in range(pages):
                pg = pt_ref[seq, first - r]
                out += [pltpu.make_async_copy(kt_hbm.at[pg], kbuf.at[into, r], sem.at[into, 0]),
                        pltpu.make_async_copy(vt_hbm.at[pg], vbuf.at[into, r], sem.at[into, 1]),
                        pltpu.make_async_copy(lf_hbm.at[pg], lfbuf.at[into, r], sem.at[into, 2])]
            return out

        @pl.when(i == 0)
        def _():
            for c in page_copies(0, 0):
                c.start()

        nxt = jnp.minimum(i + 1, last)
        for c in page_copies(nxt, 1 - slot):
            c.start()
        for c in page_copies(i, slot):
            c.wait()

        g = i % steps_per_seq
        pl.when(g == 0)(lambda: _decode_fox_init(q_ref, kn_ref, vn_ref, lfn_ref, *state))
        _decode_fox_pages([kbuf.at[slot, r] for r in range(pages)], [vbuf.at[slot, r] for r in range(pages)],
                          [lfbuf.at[slot, r] for r in range(pages)], *state)
        kernel(*ins, *outs, *own)
        pl.when(g == steps_per_seq - 1)(lambda: _decode_fox_finish(dec_ref, state[2], state[4]))

        @pl.when(i == last)
        def _():
            for c in page_copies(nxt, 1 - slot):
                c.wait()

    return wrapped


def _run_host(name, kernel, steps, in_specs, out_specs, out_shape, args, scratch_shapes=(), decode=None, side=None):
    in_specs, out_specs, out_shape, args = list(in_specs), list(out_specs), list(out_shape), list(args)
    scratch = list(scratch_shapes)
    page_table = None
    if decode is not None:
        page_table, q, kn, vn, lfn, kt_pages, vt_pages, lft_pages, seq0, n_seq = decode
        npg = page_table.shape[1]
        w, page_len = kt_pages.shape[1], kt_pages.shape[2]
        steps_per_seq = steps // n_seq
        pages = npg // steps_per_seq
        assert steps_per_seq * n_seq == steps and pages * steps_per_seq == npg
        kernel = _with_decode(kernel, len(in_specs), len(out_specs), pages, steps_per_seq, seq0)
        row = pl.BlockSpec((None, 1, w), lambda i, pt: (seq0 + i // steps_per_seq, 0, 0))
        hbm = pl.BlockSpec(memory_space=pl.ANY)
        in_specs += [row, row, row,
                     pl.BlockSpec((None, 1, V7X_LANES), lambda i, pt: (seq0 + i // steps_per_seq, 0, 0)), hbm, hbm, hbm]
        args += [q, kn, vn, lfn, kt_pages, vt_pages, lft_pages]
        out_specs += [pl.BlockSpec((None, 1, w), lambda i, pt: (i // steps_per_seq, 0, 0))]
        out_shape += [jax.ShapeDtypeStruct((n_seq, 1, w), F32)]
        scratch += [pltpu.VMEM((N_HEADS, w), F32), pltpu.VMEM((N_HEADS, 1), F32), pltpu.VMEM((N_HEADS, 1), F32),
                    pltpu.VMEM((N_HEADS, 1), F32), pltpu.VMEM((w, page_len), F32),
                    pltpu.VMEM((2, pages, w, page_len), F32), pltpu.VMEM((2, pages, w, page_len), F32),
                    pltpu.VMEM((2, pages, N_HEADS, page_len), F32), pltpu.SemaphoreType.DMA((2, 3))]
    n_prefetch = 0 if page_table is None else 1
    if side is not None:
        kernel = _with_side_cast(kernel, n_prefetch, len(in_specs), len(out_specs))
        in_specs += side.in_specs
        out_specs += side.out_specs
        out_shape += side.out_shape
        args += side.args
    params = _cparams(("arbitrary",), big=True)
    if page_table is None:
        return pl.pallas_call(kernel, grid=(steps,), in_specs=in_specs, out_specs=out_specs, out_shape=out_shape,
                              scratch_shapes=scratch, compiler_params=params, name=name)(*args)
    grid_spec = pltpu.PrefetchScalarGridSpec(num_scalar_prefetch=1, grid=(steps,), in_specs=in_specs,
                                             out_specs=out_specs, scratch_shapes=scratch)
    return pl.pallas_call(kernel, grid_spec=grid_spec, out_shape=out_shape, compiler_params=params,
                          name=name + "_decode")(page_table, *args)


def _ffn_call(x, rows, ks, w_in, w_out, ln_g, ln_b, ln_idx, alpha, decode=None, mixer_out=None, side=None):
    n, d = x.shape
    ff = w_out.shape[0]
    steps = n // rows.tm
    w_spec = _const_spec

    def ln_spec(idx):
        return pl.BlockSpec((None, 1, d), lambda i, *_: (idx, 0, 0))

    in_specs = [rows.row_spec(d)]
    args = [x]
    if mixer_out is not None:
        k_gate, a0, a1, w_o, ln_idx1 = mixer_out
        in_specs += [rows.mod_spec(k_gate), rows.row_spec(a0.shape[1]), rows.row_spec(a1.shape[1]), _const_spec(w_o),
                     ln_spec(ln_idx1), ln_spec(ln_idx1)]
        args += [rows.mods, a0, a1, w_o, ln_g, ln_b]
    in_specs += [rows.mod_spec(ks[0]), rows.mod_spec(ks[1]), rows.mod_spec(ks[2]), w_spec(w_in), w_spec(w_out),
                 ln_spec(ln_idx), ln_spec(ln_idx)]
    args += [rows.mods, rows.mods, rows.mods, w_in, w_out, ln_g, ln_b]
    pre = mixer_out is not None
    x_shape = jax.ShapeDtypeStruct((n, d), F32)
    if decode is None:
        assert side is None
        return pl.pallas_call(
            functools.partial(_ffn_kernel, ff=ff, alpha=alpha, pre=pre),
            grid=(steps,), in_specs=in_specs, out_specs=rows.row_spec(d), out_shape=x_shape,
            compiler_params=_cparams(("parallel",), big=True),
            name="ffn",
        )(*args)

    return _run_host("ffn", functools.partial(_ffn_kernel, ff=ff, alpha=alpha, pre=pre), steps, in_specs,
                     [rows.row_spec(d)], [x_shape], args, decode=decode, side=side)


def _outproj_kernel(x_ref, g_ref, a0_ref, a1_ref, w_ref, lng_ref, lnb_ref, o_ref, *, alpha):
    o_ref[...] = _mixer_out_math(x_ref[...], g_ref[...], a0_ref[...], a1_ref[...], w_ref, lng_ref[...], lnb_ref[...],
                                 alpha)


def _outproj_call(x, rows, k_gate, a0, a1, w_out, ln_g, ln_b, ln_idx, alpha):
    n, d = x.shape
    return pl.pallas_call(
        functools.partial(_outproj_kernel, alpha=alpha),
        grid=(n // rows.tm,),
        in_specs=[rows.row_spec(d), rows.mod_spec(k_gate), rows.row_spec(a0.shape[1]), rows.row_spec(a1.shape[1]),
                  _const_spec(w_out),
                  pl.BlockSpec((None, 1, d), lambda i: (ln_idx, 0, 0)),
                  pl.BlockSpec((None, 1, d), lambda i: (ln_idx, 0, 0))],
        out_specs=rows.row_spec(d),
        out_shape=jax.ShapeDtypeStruct((n, d), F32),
        compiler_params=_cparams(("parallel",), big=True),
        name="mixer_outproj",
    )(x, rows.mods, a0, a1, w_out, ln_g, ln_b)


def _rotate_token_major(a, cos, sin_signed):
    width = a.shape[-1]
    lane = lax.broadcasted_iota(jnp.int32, a.shape, 1)
    first_half = (lane & (HEAD_DIM // 2)) == 0
    partner = jnp.where(first_half, pltpu.roll(a, width - HEAD_DIM // 2, 1), pltpu.roll(a, HEAD_DIM // 2, 1))
    return a * cos + partner * sin_signed


_EVEN_ROWS = {}
_r0 = 0
for _name, _n in (("qf", HEAD_WIDTH), ("kf", HEAD_WIDTH), ("vf", HEAD_WIDTH), ("fg", N_HEADS),
                  ("qr", HEAD_WIDTH), ("kr", HEAD_WIDTH), ("vr", HEAD_WIDTH), ("gr", HEAD_WIDTH)):
    _EVEN_ROWS[_name] = (_r0, _r0 + _n)
    _r0 += _n
EVEN_IN_ROWS = _r0
FG_ROWS = V7X_BF16_SUBLANES


def _even_in_kernel(x_ref, sh_ref, sc_ref, w_ref, bf_ref, cos_ref, sin_ref, cost_ref, sint_ref,
                    q_o, qr_o, vr_o, sg_o, kt_o, vt_o, ktc_o, vtc_o, krt_o, lft_o, wtok_ref, wt_ref):
    w = HEAD_WIDTH

    @pl.when(pl.program_id(0) == 0)
    def _():
        for k, name in enumerate(("qf", "qr", "vr", "gr")):
            lo, hi = _EVEN_ROWS[name]
            wtok_ref[:, k * w:(k + 1) * w] = w_ref[lo:hi, :].T.astype(BF16)
        for k, name in enumerate(("kf", "vf", "kr")):
            lo, hi = _EVEN_ROWS[name]
            wt_ref[k * w:(k + 1) * w, :] = w_ref[lo:hi, :].astype(BF16)
        lo, hi = _EVEN_ROWS["fg"]
        fg_rows = jnp.concatenate([w_ref[lo:hi, :], jnp.zeros((FG_ROWS - N_HEADS, w_ref.shape[1]), F32)], axis=0)
        wt_ref[3 * w:, :] = fg_rows.astype(BF16)

    x = x_ref[...]
    h = (x * (1.0 + sc_ref[...]) + sh_ref[...]).astype(BF16)

    q_o[...] = (_dot(h, wtok_ref[:, 0:w]) * (QK_SCALE * LOG2E)).astype(BF16)
    qr = _dot(h, wtok_ref[:, w:2 * w])
    qr_o[...] = _rotate_token_major(qr, cos_ref[...], sin_ref[...]).astype(BF16)
    vr_o[...] = _dot(h, wtok_ref[:, 2 * w:3 * w]).astype(BF16)
    sg_o[...] = _silu(_dot(h, wtok_ref[:, 3 * w:4 * w])).astype(BF16)

    kt = _dot_nt(wt_ref[0:w, :], h)
    kt_o[...] = kt
    ktc_o[...] = kt.astype(BF16)
    vt = _dot_nt(wt_ref[w:2 * w, :], h)
    vt_o[...] = vt
    vtc_o[...] = vt.astype(BF16)

    krt = _dot_nt(wt_ref[2 * w:3 * w, :], h)
    cos_t, sin_t = cost_ref[...], sint_ref[...]
    half = HEAD_DIM // 2
    for hh in range(N_HEADS):
        x1 = krt[hh * HEAD_DIM:hh * HEAD_DIM + half, :]
        x2 = krt[hh * HEAD_DIM + half:(hh + 1) * HEAD_DIM, :]
        krt_o[hh * HEAD_DIM:hh * HEAD_DIM + half, :] = (x1 * cos_t - x2 * sin_t) * QK_SCALE
        krt_o[hh * HEAD_DIM + half:(hh + 1) * HEAD_DIM, :] = (x1 * sin_t + x2 * cos_t) * QK_SCALE

    fg = _dot_nt(wt_ref[3 * w:, :], h)[0:N_HEADS, :] + bf_ref[...]
    lft_o[...] = _log_sigmoid(fg)


def _even_in_call(x, rows, ks, w_t_f32, bf_col, rot, batch, seq, decode=None, side=None):
    n, d = x.shape
    tm = rows.tm
    bps = seq // tm
    w = HEAD_WIDTH
    cos_tok, sin_tok, cos_t, sin_t = rot
    tok_out = pl.BlockSpec((tm, w), lambda i, *_: (i, 0))
    t_out = pl.BlockSpec((None, w, tm), lambda i, *_: (i // bps, 0, i % bps))
    tc_out = pl.BlockSpec((None, None, w, tm), lambda i, *_: (i // bps, i % bps, 0, 0))
    tok_shape = jax.ShapeDtypeStruct((n, w), BF16)
    in_specs = [rows.row_spec(d), rows.mod_spec(ks[0]), rows.mod_spec(ks[1]),
                _const_spec(w_t_f32), _const_spec(bf_col),
                pl.BlockSpec((tm, w), lambda i, *_: (i % bps, 0)),
                pl.BlockSpec((tm, w), lambda i, *_: (i % bps, 0)),
                pl.BlockSpec((HEAD_DIM // 2, tm), lambda i, *_: (0, i % bps)),
                pl.BlockSpec((HEAD_DIM // 2, tm), lambda i, *_: (0, i % bps))]
    out_specs = [tok_out, tok_out, tok_out, tok_out, t_out, t_out, tc_out, tc_out, tc_out,
                 pl.BlockSpec((None, N_HEADS, tm), lambda i, *_: (i // bps, 0, i % bps))]
    out_shape = [tok_shape, tok_shape, tok_shape, tok_shape,
                 jax.ShapeDtypeStruct((batch, w, seq), F32),
                 jax.ShapeDtypeStruct((batch, w, seq), F32),
                 jax.ShapeDtypeStruct((batch, bps, w, tm), BF16),
                 jax.ShapeDtypeStruct((batch, bps, w, tm), BF16),
                 jax.ShapeDtypeStruct((batch, bps, w, tm), F32),
                 jax.ShapeDtypeStruct((batch, N_HEADS, seq), F32)]

    return _run_host("even_inproj", _even_in_kernel, n // tm, in_specs, out_specs, out_shape,
                     (x, rows.mods, rows.mods, w_t_f32, bf_col, cos_tok, sin_tok, cos_t, sin_t),
                     [pltpu.VMEM((d, 4 * w), BF16), pltpu.VMEM((3 * w + FG_ROWS, d), BF16)], decode, side)


def _cumsum_kernel(x_ref, o_ref):
    x = x_ref[...]
    n = x.shape[-1]
    lane = lax.broadcasted_iota(jnp.int32, x.shape, 1)
    shift = 1
    while shift < n:
        x = x + jnp.where(lane >= shift, pltpu.roll(x, shift, 1), 0.0)
        shift *= 2
    o_ref[...] = x * LOG2E


def _cumsum_call(lft):
    batch, heads, seq = lft.shape
    spec = pl.BlockSpec((None, heads, seq), lambda b: (b, 0, 0))
    return pl.pallas_call(
        _cumsum_kernel, grid=(batch,), in_specs=[spec], out_specs=spec,
        out_shape=jax.ShapeDtypeStruct(lft.shape, F32),
        compiler_params=_cparams(("parallel",)),
        name="logf_cumsum",
    )(lft)


def _split3(x):
    hi = x.astype(BF16).astype(F32)
    r = x - hi
    mid = r.astype(BF16).astype(F32)
    lo = (r - mid).astype(BF16).astype(F32)
    return hi, mid, lo


def _fox_kernel(q_ref, kt_ref, vt_ref, c_ref, o_ref):
    tq = kt_ref.shape[-1]

    def per_block(i, carry):
        rows = pl.ds(pl.multiple_of(i * tq, tq), tq)
        o_ref[rows, :] = _fox_block(i, q_ref[rows, :], kt_ref, vt_ref, c_ref)
        return carry

    lax.fori_loop(0, q_ref.shape[0] // tq, per_block, 0)


def _fox_block(i, q, kt_ref, vt_ref, c_ref):
    tq = q.shape[0]
    tk = kt_ref.shape[-1]
    pw = 2 * HEAD_DIM
    slab_rows = V7X_BF16_SUBLANES
    lane = lax.broadcasted_iota(jnp.int32, (tq, pw), 1)
    q_aug = []
    for hd in range(2):
        base = HEAD_DIM if hd == 0 else 0
        hi, mid, lo = _split3(c_ref[hd, pl.ds(i, 1), :][:, 0:1])
        aug = jnp.where(lane == base + 3, hi, jnp.where(lane == base + 4, mid, jnp.where(lane == base + 5, lo,
              jnp.where((lane >= base) & (lane < base + 3), 1.0, 0.0))))
        own = (lane < HEAD_DIM) if hd == 0 else (lane >= HEAD_DIM)
        q_aug.append(jnp.where(own, q, aug.astype(BF16)))
    r16 = lax.broadcasted_iota(jnp.int32, (slab_rows, tk), 0)
    rowv = lax.broadcasted_iota(jnp.int32, (pw, tk), 0)

    def scores(j, hd):
        kt = kt_ref[j]
        hi, mid, lo = _split3(-c_ref[hd, pl.ds(j, 1), :])
        slab = jnp.where(r16 == 0, hi, jnp.where(r16 == 1, mid, jnp.where(r16 == 2, lo,
               jnp.where(r16 < 6, 1.0, 0.0)))).astype(BF16)
        if hd == 0:
            kt_aug = jnp.concatenate([kt[0:HEAD_DIM], slab, kt[HEAD_DIM + slab_rows:]], axis=0)
        else:
            kt_aug = jnp.concatenate([slab, kt[slab_rows:]], axis=0)
        return _dot(q_aug[hd], kt_aug)

    def values(j, hd):
        vt = vt_ref[j]
        return jnp.where(rowv == (HEAD_DIM if hd == 0 else 0), jnp.ones_like(vt), vt)

    def update(s, vt_aug, m, acc):
        m_new = jnp.maximum(m, jnp.max(s, axis=-1, keepdims=True))
        return m_new, jnp.exp2(m - m_new) * acc + _dot_nt(jnp.exp2(s - m_new).astype(BF16), vt_aug)

    def diag_scores(hd):
        row = lax.broadcasted_iota(jnp.int32, (tq, tk), 0)
        col = lax.broadcasted_iota(jnp.int32, (tq, tk), 1)
        return jnp.where(row >= col, scores(i, hd), -jnp.inf)

    group = FOX_BLOCKS_PER_UPDATE

    def joint(carry, full_blocks, with_diag):
        out = []
        for hd in range(2):
            s = [scores(j, hd) for j in full_blocks] + ([diag_scores(hd)] if with_diag else [])
            v = [values(j, hd) for j in full_blocks] + ([values(i, hd)] if with_diag else [])
            out.append(update(jnp.concatenate(s, axis=1), jnp.concatenate(v, axis=1), *carry[hd]))
        return tuple(out)

    def tail(n_full):
        return lambda carry: tuple(c[1] for c in joint(carry, [i - n_full + k for k in range(n_full)], True))

    init_head = (jnp.full((tq, 1), -jnp.inf, F32), jnp.zeros((tq, pw), F32))
    carry = lax.fori_loop(0, lax.shift_right_logical(i, group.bit_length() - 1),
                          lambda t, c: joint(c, [group * t + k for k in range(group)], False),
                          (init_head, init_head))
    outs = lax.switch(i & (group - 1), [tail(n) for n in range(group)], carry)
    o0 = outs[0] / outs[0][:, HEAD_DIM:HEAD_DIM + 1]
    o1 = outs[1] / outs[1][:, 0:1]
    return jnp.where(lane < HEAD_DIM, o0, o1).astype(BF16)


def _fox_call(q, ktc, vtc, c4, batch, seq):
    n, w = q.shape
    tq = ktc.shape[-1]
    nq = seq // tq
    pw = 2 * HEAD_DIM
    return pl.pallas_call(
        _fox_kernel,
        grid=(batch, HEAD_PAIRS),
        in_specs=[pl.BlockSpec((seq, pw), lambda b, p: (b, p)),
                  pl.BlockSpec((None, nq, pw, tq), lambda b, p: (b, 0, p, 0)),
                  pl.BlockSpec((None, nq, pw, tq), lambda b, p: (b, 0, p, 0)),
                  pl.BlockSpec((None, 2, nq, tq), lambda b, p: (b, p, 0, 0))],
        out_specs=pl.BlockSpec((seq, pw), lambda b, p: (b, p)),
        out_shape=jax.ShapeDtypeStruct((n, w), BF16),
        compiler_params=_cparams(("parallel", "parallel"), big=True),
        name="fox_prompt",
    )(q, ktc, vtc, c4)


def _ret_kernel(q_ref, kt_ref, v_ref, sg_ref, lgl_ref, lgr_ref, y_ref, s_ref):
    chunk = RET_CHUNK
    pw = 2 * HEAD_DIM
    nkb, _, tb = kt_ref.shape
    lg_lane = lgl_ref[...]
    lg_row = lgr_ref[...]
    lg_a, lg_b = lg_lane[:, 0:1], lg_lane[:, HEAD_DIM:HEAD_DIM + 1]

    ri = lax.broadcasted_iota(jnp.int32, (2 * chunk, chunk), 0)
    cj = lax.broadcasted_iota(jnp.int32, (2 * chunk, chunk), 1)
    first = ri < chunk
    diff = (jnp.where(first, ri, ri - chunk) - cj).astype(F32)
    decay_mask = jnp.where(diff >= 0, jnp.exp(jnp.maximum(diff, 0.0) * jnp.where(first, lg_a, lg_b)), 0.0)
    jj = lax.broadcasted_iota(jnp.int32, (pw, chunk), 1).astype(F32)
    col_decay = jnp.exp((chunk - 1.0 - jj) * lg_row)
    ii = lax.broadcasted_iota(jnp.int32, (chunk, pw), 0).astype(F32)
    row_decay = jnp.exp((ii + 1.0) * lg_lane)
    chunk_decay = jnp.exp(float(chunk) * lg_row)
    r2 = lax.broadcasted_iota(jnp.int32, (pw, pw), 0)
    c2 = lax.broadcasted_iota(jnp.int32, (pw, pw), 1)
    same_head = (r2 < HEAD_DIM) == (c2 < HEAD_DIM)
    seg_avg = jnp.where(same_head, 1.0 / HEAD_DIM, 0.0).astype(BF16)
    lane = lax.broadcasted_iota(jnp.int32, (chunk, pw), 1)

    def seg_mean(a):
        return _dot(a.astype(BF16), seg_avg)

    def one_chunk(q, kt, v, sg, state):
        zero = jnp.zeros_like(q)
        q_stack = jnp.concatenate([jnp.where(lane < HEAD_DIM, q, zero), jnp.where(lane >= HEAD_DIM, q, zero)], axis=0)
        qk = _dot(q_stack, kt.astype(BF16)) * decay_mask
        kd = (kt * col_decay).astype(BF16)
        r = _dot(jnp.concatenate([qk.astype(BF16), kd], axis=0), v)
        inner = jnp.where(lane < HEAD_DIM, r[0:chunk], r[chunk:2 * chunk])
        update = jnp.where(same_head, r[2 * chunk:], 0.0)
        cross = _dot(q, state.astype(BF16)) * row_decay
        o = inner + cross
        mu = seg_mean(o)
        d = o - mu
        var = seg_mean(d * d)
        y = sg.astype(F32) * (d * lax.rsqrt(var + GN_EPS))
        return y.astype(BF16), chunk_decay * state + update

    def body(jb, state):
        kt_blk = kt_ref[jb]
        for sub in range(tb // chunk):
            t0 = pl.multiple_of(jb * tb + sub * chunk, chunk)
            y, state = one_chunk(q_ref[pl.ds(t0, chunk), :], kt_blk[:, sub * chunk:(sub + 1) * chunk],
                                 v_ref[pl.ds(t0, chunk), :], sg_ref[pl.ds(t0, chunk), :], state)
            y_ref[pl.ds(t0, chunk), :] = y
        return state

    state = lax.fori_loop(0, nkb, body, jnp.zeros((pw, pw), F32), unroll=8)
    s_ref[0] = state[0:HEAD_DIM, 0:HEAD_DIM]
    s_ref[1] = pltpu.roll(state, HEAD_DIM, 1)[HEAD_DIM:, 0:HEAD_DIM]


def _ret_call(qr, krt, vr, sg, lg_lane, lg_row, batch, seq):
    n, w = qr.shape
    nkb, tb = krt.shape[1], krt.shape[3]
    pw = 2 * HEAD_DIM
    seq_spec = pl.BlockSpec((seq, pw), lambda b, p: (b, p))
    return pl.pallas_call(
        _ret_kernel,
        grid=(batch, HEAD_PAIRS),
        in_specs=[seq_spec,
                  pl.BlockSpec((None, nkb, pw, tb), lambda b, p: (b, 0, p, 0)),
                  seq_spec, seq_spec,
                  pl.BlockSpec((None, 1, pw), lambda b, p: (p, 0, 0)),
                  pl.BlockSpec((None, pw, 1), lambda b, p: (p, 0, 0))],
        out_specs=[seq_spec, pl.BlockSpec((None, 2, HEAD_DIM, HEAD_DIM), lambda b, p: (b, p, 0, 0))],
        out_shape=[jax.ShapeDtypeStruct((n, w), BF16),
                   jax.ShapeDtypeStruct((batch, N_HEADS, HEAD_DIM, HEAD_DIM), F32)],
        compiler_params=_cparams(("parallel", "parallel"), big=True),
        name="ret_prompt",
    )(qr, krt, vr, sg, lg_lane, lg_row)


def _odd_prompt_kernel(x_ref, sh_ref, sc_ref, g_ref, win_ref, cw_ref, wout_ref, lng_ref, lnb_ref,
                       o_ref, st_ref, carry_ref, *, bps, alpha):
    i = pl.program_id(0)
    x = x_ref[...]
    tm, d = x.shape
    h = (x * (1.0 + sc_ref[...]) + sh_ref[...]).astype(BF16)
    b_gate = _dot(h, win_ref[:, 0:d])
    u = _dot(h, win_ref[:, d:2 * d]) * _dot(h, win_ref[:, 2 * d:3 * d])

    @pl.when(i % bps == 0)
    def _():
        carry_ref[...] = jnp.zeros_like(carry_ref)

    prev = carry_ref[...]
    p1, p2 = prev[7:8, :], prev[6:7, :]
    row = lax.broadcasted_iota(jnp.int32, (tm, d), 0)
    u1 = jnp.where(row == 0, p1, pltpu.roll(u, 1, 0))
    u2 = jnp.where(row == 0, p2, jnp.where(row == 1, p1, pltpu.roll(u, 2, 0)))
    cw = cw_ref[...]
    z = cw[0:1, :] * u2 + cw[1:2, :] * u1 + cw[2:3, :] * u
    carry_ref[...] = u[tm - 8:, :]
    st_ref[...] = u[tm - (CONV_WIDTH - 1):, :]
    y = _dot((b_gate * z).astype(BF16), wout_ref[...])
    zz = alpha * x + g_ref[...] * y
    o_ref[...] = _layernorm(zz, lng_ref[...], lnb_ref[...])


def _odd_prompt_call(x, rows, ks, w_in, conv_w, w_out, ln_g, ln_b, ln_idx, alpha, batch, seq, decode=None,
                     side=None):
    n, d = x.shape
    bps = seq // rows.tm
    in_specs = [rows.row_spec(d), rows.mod_spec(ks[0]), rows.mod_spec(ks[1]), rows.mod_spec(ks[2]),
                _const_spec(w_in), _const_spec(conv_w), _const_spec(w_out),
                pl.BlockSpec((None, 1, d), lambda i, *_: (ln_idx, 0, 0)),
                pl.BlockSpec((None, 1, d), lambda i, *_: (ln_idx, 0, 0))]
    out_specs = [rows.row_spec(d), pl.BlockSpec((None, CONV_WIDTH - 1, d), lambda i, *_: (i // bps, 0, 0))]
    out_shape = [jax.ShapeDtypeStruct((n, d), F32), jax.ShapeDtypeStruct((batch, CONV_WIDTH - 1, d), F32)]

    return _run_host("odd_prompt", functools.partial(_odd_prompt_kernel, bps=bps, alpha=alpha), n // rows.tm,
                     in_specs, out_specs, out_shape, (x, rows.mods, rows.mods, rows.mods, w_in, conv_w, w_out, ln_g, ln_b),
                     [pltpu.VMEM((8, d), F32)], decode, side)


def _even_in_sample_kernel(x_ref, sh_ref, sc_ref, w_ref, bf_ref, cost_ref, sint_ref,
                           q_o, k_o, v_o, qrt_o, krt_o, vr_o, sg_o, lf_o):
    d = w_ref.shape[1]
    h = (x_ref[...] * (1.0 + sc_ref[...]) + sh_ref[...]).astype(BF16)

    def rows(name):
        lo, hi = _EVEN_ROWS[name]
        return w_ref[lo:hi, :]

    def tok(name):
        return _dot_nt(h, rows(name).astype(BF16))

    q_o[...] = tok("qf") * QK_SCALE
    k_o[...] = tok("kf")
    v_o[...] = tok("vf")
    vr_o[...] = tok("vr")
    sg_o[...] = _silu(tok("gr"))
    fg_rows = jnp.concatenate([rows("fg"), jnp.zeros((V7X_LANES - N_HEADS, d), F32)], axis=0).astype(BF16)
    lf_o[...] = _log_sigmoid(_dot_nt(h, fg_rows) + bf_ref[...])

    cos_t, sin_t = cost_ref[...], sint_ref[...]
    half = HEAD_DIM // 2
    for name, out, scale in (("qr", qrt_o, 1.0), ("kr", krt_o, QK_SCALE)):
        t = _dot_nt(rows(name).astype(BF16), h)
        for hh in range(N_HEADS):
            x1 = t[hh * HEAD_DIM:hh * HEAD_DIM + half, :]
            x2 = t[hh * HEAD_DIM + half:(hh + 1) * HEAD_DIM, :]
            out[hh * HEAD_DIM:hh * HEAD_DIM + half, :] = (x1 * cos_t - x2 * sin_t) * scale
            out[hh * HEAD_DIM + half:(hh + 1) * HEAD_DIM, :] = (x1 * sin_t + x2 * cos_t) * scale


def _even_in_sample_call(x, rows, ks, w_t_f32, bf_row, cos_t, sin_t):
    n, d = x.shape
    w = HEAD_WIDTH
    full = pl.BlockSpec((n, w), lambda i: (0, 0))
    full_t = pl.BlockSpec((w, n), lambda i: (0, 0))
    rot = pl.BlockSpec((HEAD_DIM // 2, n), lambda i: (0, 0))
    shp = jax.ShapeDtypeStruct((n, w), F32)
    shp_t = jax.ShapeDtypeStruct((w, n), F32)
    return pl.pallas_call(
        _even_in_sample_kernel,
        grid=(1,),
        in_specs=[rows.row_spec(d), rows.mod_spec(ks[0]), rows.mod_spec(ks[1]),
                  _const_spec(w_t_f32), _const_spec(bf_row), rot, rot],
        out_specs=[full, full, full, full_t, full_t, full, full, pl.BlockSpec((n, V7X_LANES), lambda i: (0, 0))],
        out_shape=[shp, shp, shp, shp_t, shp_t, shp, shp, jax.ShapeDtypeStruct((n, V7X_LANES), F32)],
        compiler_params=_cparams(("arbitrary",), big=True),
        name="even_inproj_sample",
    )(x, rows.mods, rows.mods, w_t_f32, bf_row, cos_t, sin_t)


def _own_head_lanes():
    sub = lax.broadcasted_iota(jnp.int32, (N_HEADS, HEAD_WIDTH), 0)
    lane = lax.broadcasted_iota(jnp.int32, (N_HEADS, HEAD_WIDTH), 1)
    return (lane >= sub * HEAD_DIM) & (lane < (sub + 1) * HEAD_DIM)


def _decode_fox_init(q_ref, kn_ref, vn_ref, lfn_ref, qbd_s, m_s, l_s, run_s, acc_s):
    w = HEAD_WIDTH
    q_bd = jnp.where(_own_head_lanes(), jnp.broadcast_to(q_ref[...], (N_HEADS, w)), 0.0)
    qbd_s[...] = q_bd
    m_s[...] = jnp.sum(q_bd * kn_ref[...], axis=-1, keepdims=True)
    l_s[...] = jnp.ones_like(l_s)
    page_len = acc_s.shape[1]
    lane_p = lax.broadcasted_iota(jnp.int32, (w, page_len), 1)
    acc_s[...] = jnp.where(lane_p == 0, jnp.broadcast_to(vn_ref[...], (page_len, w)).T, 0.0)
    s128 = lax.broadcasted_iota(jnp.int32, (N_HEADS, V7X_LANES), 0)
    l128 = lax.broadcasted_iota(jnp.int32, (N_HEADS, V7X_LANES), 1)
    lfn = jnp.broadcast_to(lfn_ref[...], (N_HEADS, V7X_LANES))
    run_s[...] = jnp.sum(jnp.where(s128 == l128, lfn, 0.0), axis=-1, keepdims=True)


def _decode_fox_finish(o_ref, l_s, acc_s):
    tot = jnp.sum(acc_s[...].T, axis=0, keepdims=True)
    l_b = jnp.broadcast_to(l_s[...], (N_HEADS, HEAD_WIDTH))
    o_ref[...] = tot / jnp.sum(jnp.where(_own_head_lanes(), l_b, 0.0), axis=0, keepdims=True)


def _decode_fox_pages(k_refs, v_refs, lf_refs, qbd_s, m_s, l_s, run_s, acc_s):
    pages = len(k_refs)
    page_len = k_refs[0].shape[-1]
    q_b = qbd_s[...].astype(BF16)
    order = list(range(pages - 1, -1, -1))
    s = jnp.concatenate([_dot(q_b, k_refs[r][...].astype(BF16)) for r in order], axis=1)
    lf = jnp.concatenate([lf_refs[r][...] for r in order], axis=1)
    n = pages * page_len
    lane_n = lax.broadcasted_iota(jnp.int32, (N_HEADS, n), 1)
    suf = lf
    shift = 1
    while shift < n:
        suf = suf + jnp.where(lane_n < n - shift, pltpu.roll(suf, n - shift, 1), 0.0)
        shift *= 2
    run = run_s[...]
    s = s + ((suf - lf) + run)
    m = m_s[...]
    m_new = jnp.maximum(m, jnp.max(s, axis=-1, keepdims=True))
    a = jnp.exp(m - m_new)
    p = jnp.exp(s - m_new)
    l_s[...] = a * l_s[...] + jnp.sum(p, axis=-1, keepdims=True)
    m_s[...] = m_new
    for h in range(N_HEADS):
        sl = slice(h * HEAD_DIM, (h + 1) * HEAD_DIM)
        upd = None
        for idx, r in enumerate(order):
            t = p[h:h + 1, idx * page_len:(idx + 1) * page_len] * v_refs[r][sl, :]
            upd = t if upd is None else upd + t
        acc_s[sl, :] = a[h:h + 1, :] * acc_s[sl, :] + upd
    run_s[...] = run + suf[:, 0:1]


def _decode_ret_kernel(qt_ref, kt_ref, v_ref, sg_ref, s_ref, gl_ref, y_ref, so_ref):
    qt, kt = qt_ref[...], kt_ref[...]
    g = gl_ref[...]
    for b in range(qt.shape[-1]):
        q, k = qt[:, :, b:b + 1], kt[:, :, b:b + 1]
        v = v_ref[b]
        state = s_ref[b]
        inner = jnp.sum(q * k, axis=1, keepdims=True) * v
        cross = jnp.sum(q * state, axis=1, keepdims=True) * g
        so_ref[b] = g * state + k * v
        o = inner + cross
        mu = jnp.mean(o, axis=-1, keepdims=True)
        d = o - mu
        var = jnp.mean(d * d, axis=-1, keepdims=True)
        y_ref[b] = sg_ref[b] * (d * lax.rsqrt(var + GN_EPS))


def _decode_ret_call(q_t, k_t, v_row, sg_row, state, decay):
    nseq = state.shape[0]

    def whole(arr):
        nd = arr.ndim
        return pl.BlockSpec(arr.shape, lambda i: (0,) * nd)

    return pl.pallas_call(
        _decode_ret_kernel,
        grid=(1,),
        in_specs=[whole(a) for a in (q_t, k_t, v_row, sg_row, state, decay)],
        out_specs=[whole(v_row), whole(state)],
        out_shape=[jax.ShapeDtypeStruct(v_row.shape, F32), jax.ShapeDtypeStruct(state.shape, F32)],
        compiler_params=_cparams(("arbitrary",), big=True),
        name="ret_decode",
    )(q_t, k_t, v_row, sg_row, state, decay)


def _odd_sample_kernel(x_ref, sh_ref, sc_ref, g_ref, win_ref, cw_ref, b0_ref, b1_ref, wout_ref, lng_ref, lnb_ref,
                       o_ref, u_ref, *, alpha):
    x = x_ref[...]
    d = x.shape[-1]
    h = (x * (1.0 + sc_ref[...]) + sh_ref[...]).astype(BF16)
    b_gate = _dot(h, win_ref[:, 0:d])
    u = _dot(h, win_ref[:, d:2 * d]) * _dot(h, win_ref[:, 2 * d:3 * d])
    cw = cw_ref[...]
    z = cw[0:1, :] * b0_ref[...] + cw[1:2, :] * b1_ref[...] + cw[2:3, :] * u
    u_ref[...] = u
    y = _dot((b_gate * z).astype(BF16), wout_ref[...])
    o_ref[...] = _layernorm(alpha * x + g_ref[...] * y, lng_ref[...], lnb_ref[...])


def _odd_sample_call(x, rows, ks, w_in, conv_w, buf0, buf1, w_out, ln_g, ln_b, ln_idx, alpha):
    n, d = x.shape
    full = pl.BlockSpec((n, d), lambda i: (0, 0))
    return pl.pallas_call(
        functools.partial(_odd_sample_kernel, alpha=alpha),
        grid=(1,),
        in_specs=[rows.row_spec(d), rows.mod_spec(ks[0]), rows.mod_spec(ks[1]), rows.mod_spec(ks[2]),
                  _const_spec(w_in), _const_spec(conv_w), full, full, _const_spec(w_out),
                  pl.BlockSpec((None, 1, d), lambda i: (ln_idx, 0, 0)),
                  pl.BlockSpec((None, 1, d), lambda i: (ln_idx, 0, 0))],
        out_specs=[full, full],
        out_shape=[jax.ShapeDtypeStruct((n, d), F32), jax.ShapeDtypeStruct((n, d), F32)],
        compiler_params=_cparams(("arbitrary",), big=True),
        name="odd_sample",
    )(x, rows.mods, rows.mods, rows.mods, w_in, conv_w, buf0, buf1, w_out, ln_g, ln_b)


def _rotary_tables(pos):
    half = HEAD_DIM // 2
    inv = 1.0 / (RET_ANGLE_BASE ** np.linspace(0.0, 1.0, half))
    ang = np.asarray(pos, np.float64)[:, None] * inv[None, :]
    return np.cos(ang), np.sin(ang)


def _token_major_tables(cos, sin):
    cos_h = np.concatenate([cos, cos], axis=1)
    sin_h = np.concatenate([-sin, sin], axis=1)
    return (jnp.asarray(np.tile(cos_h, (1, N_HEADS)), F32), jnp.asarray(np.tile(sin_h, (1, N_HEADS)), F32))


def kernel(x_prompt, x_sample, cache_k, cache_v, cache_logf, state_ret, state_conv, page_table, c_prompt, c_sample,
           w_ada, b_ada, w_ffn_in, w_ffn_out, ln_g, ln_b, w_in_even, b_forget, w_out_even, w_in_odd, conv_w,
           w_out_odd):
    batch, seq, d = x_prompt.shape
    nseq = x_sample.shape[0]
    depth = w_ada.shape[0]
    past_len = page_table.shape[1] * cache_k.shape[2]
    alpha = (2.0 * depth) ** 0.25
    w = HEAD_WIDTH
    tm = ROW_BLOCK
    assert seq % tm == 0 and x_sample.shape[1] == 1 and d % V7X_LANES == 0
    assert cache_k.shape[3] == N_HEADS and cache_k.shape[4] == HEAD_DIM

    n_c = batch + nseq
    pad = (-n_c) % 8
    c_all = jnp.concatenate([c_prompt, c_sample, jnp.zeros((pad, d), F32)], axis=0)
    mods = _ada_call(c_all, w_ada, b_ada)
    mods_p = mods[:, :batch].reshape(depth, batch, 1, N_MOD * d)
    mods_s = mods[:, batch:n_c]

    ln_g3 = ln_g.reshape(depth * 3, 1, d)
    ln_b3 = ln_b.reshape(depth * 3, 1, d)
    w_bf = {(0, 0): (w_ffn_in[0, 0].astype(BF16), w_ffn_out[0, 0].astype(BF16))}

    def side_cast(l, half):
        return _SideCast(w_ffn_in, w_ffn_out, (l, half), batch * seq // tm)

    xp = x_prompt.reshape(batch * seq, d)
    xs = x_sample.reshape(nseq, d)

    cos_p, sin_p = _rotary_tables(np.arange(seq))
    rot_p = _token_major_tables(cos_p, sin_p) + (jnp.asarray(cos_p.T, F32), jnp.asarray(sin_p.T, F32))
    cos_s, sin_s = (jnp.asarray(t.T, F32) for t in _rotary_tables(np.full((nseq,), past_len)))

    log_decay = jnp.log(1.0 - 2.0 ** (-5.0 - jnp.arange(N_HEADS, dtype=F32)))
    lg_pairs = jnp.repeat(log_decay.reshape(HEAD_PAIRS, 2), HEAD_DIM, axis=1)
    lg_lane = lg_pairs.reshape(HEAD_PAIRS, 1, 2 * HEAD_DIM)
    lg_row = lg_pairs.reshape(HEAD_PAIRS, 2 * HEAD_DIM, 1)
    step_decay = jnp.exp(log_decay).reshape(N_HEADS, 1, 1)

    outs_p = {"k": [], "v": [], "lf": [], "ret": [], "conv": []}
    outs_s = {"k": [], "v": [], "lf": [], "ret": [], "conv": []}

    assert depth == 2 and w_in_even.shape[0] == 1 and cache_k.shape[0] == 1
    n_hosts = 2 * depth
    assert nseq % n_hosts == 0
    assert w_in_even.shape[2] == EVEN_IN_ROWS
    w_even_t = jnp.transpose(w_in_even[0])
    w_out_even_b = w_out_even[0].astype(BF16)
    bf = b_forget[0]
    w_in_odd_b = w_in_odd[0].astype(BF16)
    w_out_odd_b = w_out_odd[0].astype(BF16)

    def rows_s(l):
        return _Rows(mods_s, l, nseq, None, d)

    def rows_p(l):
        return _Rows(mods_p, l, tm, seq // tm, d)

    def ffn(x, rows, half, l, decode=None, mixer_out=None, side=None):
        ks = (0, 1, 2) if half == 0 else (6, 7, 8)
        return _ffn_call(x, rows, ks, *w_bf[(l, half)], ln_g3, ln_b3, 3 * l + 2 * half, alpha, decode, mixer_out, side)

    xs = ffn(xs, rows_s(0), 0, 0)
    bf_row = jnp.concatenate([bf, jnp.zeros((V7X_LANES - N_HEADS,), F32)]).reshape(1, V7X_LANES)
    (qs, ks_, vs, qrs_t, krs_t, vrs, sgs, lfs) = _even_in_sample_call(xs, rows_s(0), (3, 4), w_even_t, bf_row,
                                                                       cos_s, sin_s)
    n_phys, page_len = cache_k.shape[1], cache_k.shape[2]
    kt_pages = jnp.transpose(cache_k[0], (0, 2, 3, 1)).reshape(n_phys, w, page_len)
    vt_pages = jnp.transpose(cache_v[0], (0, 2, 3, 1)).reshape(n_phys, w, page_len)
    lft_pages = jnp.transpose(cache_logf[0], (0, 2, 1))
    per_host = nseq // n_hosts
    dec_args = (page_table, qs.reshape(nseq, 1, w), ks_.reshape(nseq, 1, w), vs.reshape(nseq, 1, w),
                lfs.reshape(nseq, 1, V7X_LANES), kt_pages, vt_pages, lft_pages)
    dec_out = []

    def ffn_host(x, half, l, mixer_out=None, cast_for=None):
        side = None if cast_for is None else side_cast(*cast_for)
        res = ffn(x, rows_p(l), half, l, dec_args + (len(dec_out) * per_host, per_host), mixer_out, side)
        dec_out.append(res[1])
        if cast_for is not None:
            w_bf[cast_for] = tuple(res[2:])
        return res[0]

    xp = ffn_host(xp, 0, 0, cast_for=(1, 0))
    (q, qr, vr, sg, kt, vt, ktc, vtc, krt, lft, *w_bf[(0, 1)]) = _even_in_call(
        xp, rows_p(0), (3, 4), w_even_t, bf.reshape(N_HEADS, 1), rot_p, batch, seq, side=side_cast(0, 1))
    c_t = _cumsum_call(lft)
    of = _fox_call(q, ktc, vtc, c_t.reshape(batch, N_HEADS, seq // tm, tm), batch, seq)
    yr, s_new = _ret_call(qr, krt, vr, sg, lg_lane, lg_row, batch, seq)
    outs_p["k"].append(jnp.transpose(kt.reshape(batch, N_HEADS, HEAD_DIM, seq), (0, 3, 1, 2)))
    outs_p["v"].append(jnp.transpose(vt.reshape(batch, N_HEADS, HEAD_DIM, seq), (0, 3, 1, 2)))
    outs_p["lf"].append(jnp.transpose(lft, (0, 2, 1)))
    outs_p["ret"].append(s_new)
    xp = ffn_host(xp, 1, 0, mixer_out=(5, of, yr, w_out_even_b, 1))
    xp = ffn_host(xp, 0, 1)
    xp, conv_p, *w_bf[(1, 1)] = _odd_prompt_call(xp, rows_p(1), (3, 4, 5), w_in_odd_b, conv_w[0], w_out_odd_b,
                                                 ln_g3, ln_b3, 4, alpha, batch, seq, side=side_cast(1, 1))
    outs_p["conv"].append(conv_p)
    xp = ffn_host(xp, 1, 1)

    of_s = jnp.concatenate(dec_out, axis=0).reshape(nseq, w)
    lfs = lfs[:, :N_HEADS]
    hdn = (N_HEADS, HEAD_DIM, nseq)
    rw = (nseq, N_HEADS, 1, HEAD_DIM)
    yr_s, s_new_s = _decode_ret_call(qrs_t.reshape(hdn), krs_t.reshape(hdn), vrs.reshape(rw), sgs.reshape(rw),
                                     state_ret[0], step_decay)
    xs = _outproj_call(xs, rows_s(0), 5, of_s, yr_s.reshape(nseq, w), w_out_even_b, ln_g3, ln_b3, 1, alpha)
    outs_s["k"].append(ks_.reshape(nseq, 1, N_HEADS, HEAD_DIM))
    outs_s["v"].append(vs.reshape(nseq, 1, N_HEADS, HEAD_DIM))
    outs_s["lf"].append(lfs.reshape(nseq, 1, N_HEADS))
    outs_s["ret"].append(s_new_s)
    xs = ffn(xs, rows_s(0), 1, 0)
    xs = ffn(xs, rows_s(1), 0, 1)
    buf = state_conv[0]
    xs, u_s = _odd_sample_call(xs, rows_s(1), (3, 4, 5), w_in_odd_b, conv_w[0], buf[:, 0], buf[:, 1], w_out_odd_b,
                               ln_g3, ln_b3, 4, alpha)
    outs_s["conv"].append(jnp.stack([buf[:, 1], u_s], axis=1))
    xs = ffn(xs, rows_s(1), 1, 1)

    def stk(lst):
        return jnp.stack(lst)

    return (xp.reshape(batch, seq, d), xs.reshape(nseq, 1, d),
            stk(outs_p["k"]), stk(outs_p["v"]), stk(outs_p["lf"]), stk(outs_p["ret"]), stk(outs_p["conv"]),
            stk(outs_s["k"]), stk(outs_s["v"]), stk(outs_s["lf"]), stk(outs_s["ret"]), stk(outs_s["conv"]))
```

```python
import functools

import jax
import jax.numpy as jnp
import numpy as np
from jax import lax
from jax.experimental import pallas as pl
from jax.experimental.pallas import tpu as pltpu

F32 = jnp.float32
BF16 = jnp.bfloat16

HEAD_DIM = 64
N_HEADS = 8
HEAD_WIDTH = N_HEADS * HEAD_DIM
N_MOD = 9
CONV_WIDTH = 3
RET_ANGLE_BASE = 10000.0
LN_EPS = 1e-5
GN_EPS = 1e-6
QK_SCALE = HEAD_DIM ** -0.5
LOG2E = 1.4426950408889634

V7X_LANES = 128
V7X_BF16_SUBLANES = 16
V7X_VMEM_LIMIT_BYTES = 56 * 1024 * 1024

ROW_BLOCK = 512
FF_CHUNK = 256
FOX_BLOCKS_PER_UPDATE = 4
RET_CHUNK = 256
ADA_COL_BLOCK = 1152
HEAD_PAIRS = N_HEADS // 2


def _cparams(sem, big=False):
    return pltpu.CompilerParams(dimension_semantics=sem,
                                vmem_limit_bytes=V7X_VMEM_LIMIT_BYTES if big else None)


def _const_spec(arr):
    nd = arr.ndim
    return pl.BlockSpec(arr.shape, lambda *_: (0,) * nd, pipeline_mode=pl.Buffered(1))


def _layernorm(z, g, b):
    mu = jnp.mean(z, axis=-1, keepdims=True)
    d = z - mu
    var = jnp.mean(d * d, axis=-1, keepdims=True)
    return d * lax.rsqrt(var + LN_EPS) * g + b


def _silu(a):
    return a * jax.nn.sigmoid(a)


def _log_sigmoid(z):
    return jnp.minimum(z, 0.0) - jnp.log1p(jnp.exp(-jnp.abs(z)))


def _dot(a, b):
    return jnp.dot(a, b, preferred_element_type=F32)


def _dot_nt(a, b):
    return lax.dot_general(a, b, (((1,), (1,)), ((), ())), preferred_element_type=F32)


def _ada_kernel(c_ref, w_ref, b_ref, o_ref):
    s = _silu(c_ref[...]).astype(BF16)
    o_ref[...] = _dot(s, w_ref[...].astype(BF16)) + b_ref[...]


def _ada_call(c_all, w_ada, b_ada):
    depth, d, nm = w_ada.shape
    rows = c_all.shape[0]
    tn = ADA_COL_BLOCK
    return pl.pallas_call(
        _ada_kernel,
        grid=(depth, nm // tn),
        in_specs=[pl.BlockSpec((rows, d), lambda l, j: (0, 0)),
                  pl.BlockSpec((None, d, tn), lambda l, j: (l, 0, j)),
                  pl.BlockSpec((None, 1, tn), lambda l, j: (l, 0, j))],
        out_specs=pl.BlockSpec((None, rows, tn), lambda l, j: (l, 0, j)),
        out_shape=jax.ShapeDtypeStruct((depth, rows, nm), F32),
        compiler_params=_cparams(("arbitrary", "arbitrary"), big=True),
        name="ada_mods",
    )(c_all, w_ada, b_ada.reshape(depth, 1, nm))


class _SideCast:
    def __init__(self, w_in, w_out, which, steps):
        l, j = which
        d, ff2 = w_in.shape[2:]
        ff, d2 = w_out.shape[2:]
        rin, nout = d // steps, steps // 2
        rout = ff // nout
        assert rin * steps == d and rout * nout == ff and rin % V7X_BF16_SUBLANES == 0 and rout % V7X_BF16_SUBLANES == 0
        self.args = (w_in, w_out)
        self.in_specs = [pl.BlockSpec((None, None, rin, ff2), lambda i, *_: (l, j, i, 0)),
                         pl.BlockSpec((None, None, rout, d2), lambda i, *_: (l, j, jnp.minimum(i, nout - 1), 0))]
        self.out_specs = [pl.BlockSpec((rin, ff2), lambda i, *_: (i, 0)),
                          pl.BlockSpec((rout, d2), lambda i, *_: (jnp.minimum(i, nout - 1), 0))]
        self.out_shape = [jax.ShapeDtypeStruct((d, ff2), BF16), jax.ShapeDtypeStruct((ff, d2), BF16)]


def _with_side_cast(kernel, n_prefetch, n_in, n_out):
    def wrapped(*refs):
        a, b = n_prefetch + n_in, n_prefetch + n_in + 2 + n_out
        wi_ref, wo_ref = refs[a:a + 2]
        wib_ref, wob_ref = refs[b:b + 2]
        wib_ref[...] = wi_ref[...].astype(BF16)
        wob_ref[...] = wo_ref[...].astype(BF16)
        kernel(*refs[:a], *refs[a + 2:b], *refs[b + 2:])

    return wrapped


class _Rows:
    def __init__(self, mods, layer, tm, blocks_per_seq, d):
        self.mods, self.layer, self.tm, self.bps, self.d = mods, layer, tm, blocks_per_seq, d

    def mod_spec(self, k):
        l, d = self.layer, self.d
        if self.bps is None:
            return pl.BlockSpec((None, self.tm, d), lambda i, *_: (l, 0, k))
        bps = self.bps
        return pl.BlockSpec((None, None, 1, d), lambda i, *_: (l, i // bps, 0, k))

    def row_spec(self, width):
        return pl.BlockSpec((self.tm, width), lambda i, *_: (i, 0))


def _mixer_out_math(x, gate, a0, a1, w_ref, lng, lnb, alpha):
    half = a0.shape[-1]
    y = _dot(a0.astype(BF16), w_ref[0:half, :]) + _dot(a1.astype(BF16), w_ref[half:, :])
    return _layernorm(alpha * x + gate * y, lng, lnb)


def _ffn_rows(refs, ff, alpha, pre, after_first_chunk=None):
    x = refs[0][...]
    refs = refs[1:]
    if pre:
        gate, a0, a1, wo_ref, lng1, lnb1 = refs[:6]
        x = _mixer_out_math(x, gate[...], a0[...], a1[...], wo_ref, lng1[...], lnb1[...], alpha)
        refs = refs[6:]
    sh_ref, sc_ref, g_ref, win_ref, wout_ref, lng_ref, lnb_ref = refs
    h = (x * (1.0 + sc_ref[...]) + sh_ref[...]).astype(BF16)
    acc = None
    for c in range(ff // FF_CHUNK):
        lo, hi = c * FF_CHUNK, (c + 1) * FF_CHUNK
        a = _dot(h, win_ref[:, lo:hi])
        b = _dot(h, win_ref[:, ff + lo:ff + hi])
        y = _dot((_silu(a) * b).astype(BF16), wout_ref[lo:hi, :])
        acc = y if acc is None else acc + y
        if c == 0 and after_first_chunk is not None:
            after_first_chunk()
    z = alpha * x + (0.5 * g_ref[...]) * acc
    return _layernorm(z, lng_ref[...], lnb_ref[...])


def _ffn_kernel(*refs, ff, alpha, pre, after_first_chunk=None):
    refs[-1][...] = _ffn_rows(refs[:-1], ff, alpha, pre, after_first_chunk)


def _with_decode(kernel, n_in, n_out, pages, steps_per_seq, seq0, host_places_issue):
    def wrapped(pt_ref, *refs):
        ins = refs[:n_in]
        q_ref, kn_ref, vn_ref, lfn_ref, kt_hbm, vt_hbm, lf_hbm = refs[n_in:n_in + 7]
        outs = refs[n_in + 7:n_in + 7 + n_out]
        dec_ref = refs[n_in + 7 + n_out]
        scratch = refs[n_in + 8 + n_out:]
        own, state = scratch[:-9], scratch[-9:-4]
        kbuf, vbuf, lfbuf, sem = scratch[-4:]
        npg = pt_ref.shape[1]
        i = pl.program_id(0)
        last = pl.num_programs(0) - 1
        slot = i % 2

        def page_copies(step, into):
            seq = seq0 + step // steps_per_seq
            first = npg - 1 - (step % steps_per_seq) * pages
            out = []
            for r in range(pages):
                pg = pt_ref[seq, first - r]
                out += [pltpu.make_async_copy(kt_hbm.at[pg], kbuf.at[into, r], sem.at[into, 0]),
                        pltpu.make_async_copy(vt_hbm.at[pg], vbuf.at[into, r], sem.at[into, 1]),
                        pltpu.make_async_copy(lf_hbm.at[pg], lfbuf.at[into, r], sem.at[into, 2])]
            return out

        @pl.when(i == 0)
        def _():
            for c in page_copies(0, 0):
                c.start()

        nxt = jnp.minimum(i + 1, last)

        def start_next():
            for c in page_copies(nxt, 1 - slot):
                c.start()

        if not host_places_issue:
            start_next()
        for c in page_copies(i, slot):
            c.wait()

        g = i % steps_per_seq
        pl.when(g == 0)(lambda: _decode_fox_init(q_ref, kn_ref, vn_ref, lfn_ref, *state))
        _decode_fox_pages([kbuf.at[slot, r] for r in range(pages)], [vbuf.at[slot, r] for r in range(pages)],
                          [lfbuf.at[slot, r] for r in range(pages)], *state)
        if host_places_issue:
            kernel(*ins, *outs, *own, after_first_chunk=start_next)
        else:
            kernel(*ins, *outs, *own)
        pl.when(g == steps_per_seq - 1)(lambda: _decode_fox_finish(dec_ref, state[2], state[4]))

        @pl.when(i == last)
        def _():
            for c in page_copies(nxt, 1 - slot):
                c.wait()

    return wrapped


def _run_host(name, kernel, steps, in_specs, out_specs, out_shape, args, scratch_shapes=(), decode=None, side=None,
              host_places_issue=False):
    in_specs, out_specs, out_shape, args = list(in_specs), list(out_specs), list(out_shape), list(args)
    scratch = list(scratch_shapes)
    page_table = None
    if decode is not None:
        page_table, q, kn, vn, lfn, kt_pages, vt_pages, lft_pages, seq0, n_seq = decode
        npg = page_table.shape[1]
        w, page_len = kt_pages.shape[1], kt_pages.shape[2]
        steps_per_seq = steps // n_seq
        pages = npg // steps_per_seq
        assert steps_per_seq * n_seq == steps and pages * steps_per_seq == npg
        kernel = _with_decode(kernel, len(in_specs), len(out_specs), pages, steps_per_seq, seq0, host_places_issue)
        row = pl.BlockSpec((None, 1, w), lambda i, pt: (seq0 + i // steps_per_seq, 0, 0))
        hbm = pl.BlockSpec(memory_space=pl.ANY)
        in_specs += [row, row, row,
                     pl.BlockSpec((None, 1, V7X_LANES), lambda i, pt: (seq0 + i // steps_per_seq, 0, 0)), hbm, hbm, hbm]
        args += [q, kn, vn, lfn, kt_pages, vt_pages, lft_pages]
        out_specs += [pl.BlockSpec((None, 1, w), lambda i, pt: (i // steps_per_seq, 0, 0))]
        out_shape += [jax.ShapeDtypeStruct((n_seq, 1, w), F32)]
        scratch += [pltpu.VMEM((N_HEADS, w), F32), pltpu.VMEM((N_HEADS, 1), F32), pltpu.VMEM((N_HEADS, 1), F32),
                    pltpu.VMEM((N_HEADS, 1), F32), pltpu.VMEM((w, page_len), F32),
                    pltpu.VMEM((2, pages, w, page_len), F32), pltpu.VMEM((2, pages, w, page_len), F32),
                    pltpu.VMEM((2, pages, N_HEADS, page_len), F32), pltpu.SemaphoreType.DMA((2, 3))]
    n_prefetch = 0 if page_table is None else 1
    if side is not None:
        kernel = _with_side_cast(kernel, n_prefetch, len(in_specs), len(out_specs))
        in_specs += side.in_specs
        out_specs += side.out_specs
        out_shape += side.out_shape
        args += side.args
    params = _cparams(("arbitrary",), big=True)
    if page_table is None:
        return pl.pallas_call(kernel, grid=(steps,), in_specs=in_specs, out_specs=out_specs, out_shape=out_shape,
                              scratch_shapes=scratch, compiler_params=params, name=name)(*args)
    grid_spec = pltpu.PrefetchScalarGridSpec(num_scalar_prefetch=1, grid=(steps,), in_specs=in_specs,
                                             out_specs=out_specs, scratch_shapes=scratch)
    return pl.pallas_call(kernel, grid_spec=grid_spec, out_shape=out_shape, compiler_params=params,
                          name=name + "_decode")(page_table, *args)


def _ffn_call(x, rows, ks, w_in, w_out, ln_g, ln_b, ln_idx, alpha, decode=None, mixer_out=None, side=None):
    n, d = x.shape
    ff = w_out.shape[0]
    steps = n // rows.tm
    w_spec = _const_spec

    def ln_spec(idx):
        return pl.BlockSpec((None, 1, d), lambda i, *_: (idx, 0, 0))

    in_specs = [rows.row_spec(d)]
    args = [x]
    if mixer_out is not None:
        k_gate, a0, a1, w_o, ln_idx1 = mixer_out
        in_specs += [rows.mod_spec(k_gate), rows.row_spec(a0.shape[1]), rows.row_spec(a1.shape[1]), _const_spec(w_o),
                     ln_spec(ln_idx1), ln_spec(ln_idx1)]
        args += [rows.mods, a0, a1, w_o, ln_g, ln_b]
    in_specs += [rows.mod_spec(ks[0]), rows.mod_spec(ks[1]), rows.mod_spec(ks[2]), w_spec(w_in), w_spec(w_out),
                 ln_spec(ln_idx), ln_spec(ln_idx)]
    args += [rows.mods, rows.mods, rows.mods, w_in, w_out, ln_g, ln_b]
    pre = mixer_out is not None
    x_shape = jax.ShapeDtypeStruct((n, d), F32)
    if decode is None:
        assert side is None
        return pl.pallas_call(
            functools.partial(_ffn_kernel, ff=ff, alpha=alpha, pre=pre),
            grid=(steps,), in_specs=in_specs, out_specs=rows.row_spec(d), out_shape=x_shape,
            compiler_params=_cparams(("parallel",), big=True),
            name="ffn",
        )(*args)

    return _run_host("ffn", functools.partial(_ffn_kernel, ff=ff, alpha=alpha, pre=pre), steps, in_specs,
                     [rows.row_spec(d)], [x_shape], args, decode=decode, side=side, host_places_issue=True)


def _outproj_kernel(x_ref, g_ref, a0_ref, a1_ref, w_ref, lng_ref, lnb_ref, o_ref, *, alpha):
    o_ref[...] = _mixer_out_math(x_ref[...], g_ref[...], a0_ref[...], a1_ref[...], w_ref, lng_ref[...], lnb_ref[...],
                                 alpha)


def _outproj_call(x, rows, k_gate, a0, a1, w_out, ln_g, ln_b, ln_idx, alpha):
    n, d = x.shape
    return pl.pallas_call(
        functools.partial(_outproj_kernel, alpha=alpha),
        grid=(n // rows.tm,),
        in_specs=[rows.row_spec(d), rows.mod_spec(k_gate), rows.row_spec(a0.shape[1]), rows.row_spec(a1.shape[1]),
                  _const_spec(w_out),
                  pl.BlockSpec((None, 1, d), lambda i: (ln_idx, 0, 0)),
                  pl.BlockSpec((None, 1, d), lambda i: (ln_idx, 0, 0))],
        out_specs=rows.row_spec(d),
        out_shape=jax.ShapeDtypeStruct((n, d), F32),
        compiler_params=_cparams(("parallel",), big=True),
        name="mixer_outproj",
    )(x, rows.mods, a0, a1, w_out, ln_g, ln_b)


def _rotate_token_major(a, cos, sin_signed):
    width = a.shape[-1]
    lane = lax.broadcasted_iota(jnp.int32, a.shape, 1)
    first_half = (lane & (HEAD_DIM // 2)) == 0
    partner = jnp.where(first_half, pltpu.roll(a, width - HEAD_DIM // 2, 1), pltpu.roll(a, HEAD_DIM // 2, 1))
    return a * cos + partner * sin_signed


_EVEN_ROWS = {}
_r0 = 0
for _name, _n in (("qf", HEAD_WIDTH), ("kf", HEAD_WIDTH), ("vf", HEAD_WIDTH), ("fg", N_HEADS),
                  ("qr", HEAD_WIDTH), ("kr", HEAD_WIDTH), ("vr", HEAD_WIDTH), ("gr", HEAD_WIDTH)):
    _EVEN_ROWS[_name] = (_r0, _r0 + _n)
    _r0 += _n
EVEN_IN_ROWS = _r0
FG_ROWS = V7X_BF16_SUBLANES


def _even_in_kernel(x_ref, sh_ref, sc_ref, w_ref, bf_ref, cos_ref, sin_ref, cost_ref, sint_ref,
                    q_o, qr_o, vr_o, sg_o, kt_o, vt_o, ktc_o, vtc_o, krt_o, lft_o, wtok_ref, wt_ref):
    w = HEAD_WIDTH

    @pl.when(pl.program_id(0) == 0)
    def _():
        for k, name in enumerate(("qf", "qr", "vr", "gr")):
            lo, hi = _EVEN_ROWS[name]
            wtok_ref[:, k * w:(k + 1) * w] = w_ref[lo:hi, :].T.astype(BF16)
        for k, name in enumerate(("kf", "vf", "kr")):
            lo, hi = _EVEN_ROWS[name]
            wt_ref[k * w:(k + 1) * w, :] = w_ref[lo:hi, :].astype(BF16)
        lo, hi = _EVEN_ROWS["fg"]
        fg_rows = jnp.concatenate([w_ref[lo:hi, :], jnp.zeros((FG_ROWS - N_HEADS, w_ref.shape[1]), F32)], axis=0)
        wt_ref[3 * w:, :] = fg_rows.astype(BF16)

    x = x_ref[...]
    h = (x * (1.0 + sc_ref[...]) + sh_ref[...]).astype(BF16)

    q_o[...] = (_dot(h, wtok_ref[:, 0:w]) * (QK_SCALE * LOG2E)).astype(BF16)
    qr = _dot(h, wtok_ref[:, w:2 * w])
    qr_o[...] = _rotate_token_major(qr, cos_ref[...], sin_ref[...]).astype(BF16)
    vr_o[...] = _dot(h, wtok_ref[:, 2 * w:3 * w]).astype(BF16)
    sg_o[...] = _silu(_dot(h, wtok_ref[:, 3 * w:4 * w])).astype(BF16)

    kt = _dot_nt(wt_ref[0:w, :], h)
    kt_o[...] = kt
    ktc_o[...] = kt.astype(BF16)
    vt = _dot_nt(wt_ref[w:2 * w, :], h)
    vt_o[...] = vt
    vtc_o[...] = vt.astype(BF16)

    krt = _dot_nt(wt_ref[2 * w:3 * w, :], h)
    cos_t, sin_t = cost_ref[...], sint_ref[...]
    half = HEAD_DIM // 2
    for hh in range(N_HEADS):
        x1 = krt[hh * HEAD_DIM:hh * HEAD_DIM + half, :]
        x2 = krt[hh * HEAD_DIM + half:(hh + 1) * HEAD_DIM, :]
        krt_o[hh * HEAD_DIM:hh * HEAD_DIM + half, :] = (x1 * cos_t - x2 * sin_t) * QK_SCALE
        krt_o[hh * HEAD_DIM + half:(hh + 1) * HEAD_DIM, :] = (x1 * sin_t + x2 * cos_t) * QK_SCALE

    fg = _dot_nt(wt_ref[3 * w:, :], h)[0:N_HEADS, :] + bf_ref[...]
    lft_o[...] = _log_sigmoid(fg)


def _even_in_call(x, rows, ks, w_t_f32, bf_col, rot, batch, seq, decode=None, side=None):
    n, d = x.shape
    tm = rows.tm
    bps = seq // tm
    w = HEAD_WIDTH
    cos_tok, sin_tok, cos_t, sin_t = rot
    tok_out = pl.BlockSpec((tm, w), lambda i, *_: (i, 0))
    t_out = pl.BlockSpec((None, w, tm), lambda i, *_: (i // bps, 0, i % bps))
    tc_out = pl.BlockSpec((None, None, w, tm), lambda i, *_: (i // bps, i % bps, 0, 0))
    tok_shape = jax.ShapeDtypeStruct((n, w), BF16)
    in_specs = [rows.row_spec(d), rows.mod_spec(ks[0]), rows.mod_spec(ks[1]),
                _const_spec(w_t_f32), _const_spec(bf_col),
                pl.BlockSpec((tm, w), lambda i, *_: (i % bps, 0)),
                pl.BlockSpec((tm, w), lambda i, *_: (i % bps, 0)),
                pl.BlockSpec((HEAD_DIM // 2, tm), lambda i, *_: (0, i % bps)),
                pl.BlockSpec((HEAD_DIM // 2, tm), lambda i, *_: (0, i % bps))]
    out_specs = [tok_out, tok_out, tok_out, tok_out, t_out, t_out, tc_out, tc_out, tc_out,
                 pl.BlockSpec((None, N_HEADS, tm), lambda i, *_: (i // bps, 0, i % bps))]
    out_shape = [tok_shape, tok_shape, tok_shape, tok_shape,
                 jax.ShapeDtypeStruct((batch, w, seq), F32),
                 jax.ShapeDtypeStruct((batch, w, seq), F32),
                 jax.ShapeDtypeStruct((batch, bps, w, tm), BF16),
                 jax.ShapeDtypeStruct((batch, bps, w, tm), BF16),
                 jax.ShapeDtypeStruct((batch, bps, w, tm), F32),
                 jax.ShapeDtypeStruct((batch, N_HEADS, seq), F32)]

    return _run_host("even_inproj", _even_in_kernel, n // tm, in_specs, out_specs, out_shape,
                     (x, rows.mods, rows.mods, w_t_f32, bf_col, cos_tok, sin_tok, cos_t, sin_t),
                     [pltpu.VMEM((d, 4 * w), BF16), pltpu.VMEM((3 * w + FG_ROWS, d), BF16)], decode, side)


def _cumsum_kernel(x_ref, o_ref):
    x = x_ref[...]
    n = x.shape[-1]
    lane = lax.broadcasted_iota(jnp.int32, x.shape, 1)
    shift = 1
    while shift < n:
        x = x + jnp.where(lane >= shift, pltpu.roll(x, shift, 1), 0.0)
        shift *= 2
    o_ref[...] = x * LOG2E


def _cumsum_call(lft):
    batch, heads, seq = lft.shape
    spec = pl.BlockSpec((None, heads, seq), lambda b: (b, 0, 0))
    return pl.pallas_call(
        _cumsum_kernel, grid=(batch,), in_specs=[spec], out_specs=spec,
        out_shape=jax.ShapeDtypeStruct(lft.shape, F32),
        compiler_params=_cparams(("parallel",)),
        name="logf_cumsum",
    )(lft)


def _split3(x):
    hi = x.astype(BF16).astype(F32)
    r = x - hi
    mid = r.astype(BF16).astype(F32)
    lo = (r - mid).astype(BF16).astype(F32)
    return hi, mid, lo


def _fox_kernel(q_ref, kt_ref, vt_ref, c_ref, o_ref):
    tq = kt_ref.shape[-1]

    def per_block(i, carry):
        rows = pl.ds(pl.multiple_of(i * tq, tq), tq)
        o_ref[rows, :] = _fox_block(i, q_ref[rows, :], kt_ref, vt_ref, c_ref)
        return carry

    lax.fori_loop(0, q_ref.shape[0] // tq, per_block, 0)


def _fox_block(i, q, kt_ref, vt_ref, c_ref):
    tq = q.shape[0]
    tk = kt_ref.shape[-1]
    pw = 2 * HEAD_DIM
    slab_rows = V7X_BF16_SUBLANES
    lane = lax.broadcasted_iota(jnp.int32, (tq, pw), 1)
    q_aug = []
    for hd in range(2):
        base = HEAD_DIM if hd == 0 else 0
        hi, mid, lo = _split3(c_ref[hd, pl.ds(i, 1), :][:, 0:1])
        aug = jnp.where(lane == base + 3, hi, jnp.where(lane == base + 4, mid, jnp.where(lane == base + 5, lo,
              jnp.where((lane >= base) & (lane < base + 3), 1.0, 0.0))))
        own = (lane < HEAD_DIM) if hd == 0 else (lane >= HEAD_DIM)
        q_aug.append(jnp.where(own, q, aug.astype(BF16)))
    r16 = lax.broadcasted_iota(jnp.int32, (slab_rows, tk), 0)
    rowv = lax.broadcasted_iota(jnp.int32, (pw, tk), 0)

    def scores(j, hd):
        kt = kt_ref[j]
        hi, mid, lo = _split3(-c_ref[hd, pl.ds(j, 1), :])
        slab = jnp.where(r16 == 0, hi, jnp.where(r16 == 1, mid, jnp.where(r16 == 2, lo,
               jnp.where(r16 < 6, 1.0, 0.0)))).astype(BF16)
        if hd == 0:
            kt_aug = jnp.concatenate([kt[0:HEAD_DIM], slab, kt[HEAD_DIM + slab_rows:]], axis=0)
        else:
            kt_aug = jnp.concatenate([slab, kt[slab_rows:]], axis=0)
        return _dot(q_aug[hd], kt_aug)

    def values(j, hd):
        vt = vt_ref[j]
        return jnp.where(rowv == (HEAD_DIM if hd == 0 else 0), jnp.ones_like(vt), vt)

    def update(s, vt_aug, m, acc):
        m_new = jnp.maximum(m, jnp.max(s, axis=-1, keepdims=True))
        return m_new, jnp.exp2(m - m_new) * acc + _dot_nt(jnp.exp2(s - m_new).astype(BF16), vt_aug)

    def diag_scores(hd):
        row = lax.broadcasted_iota(jnp.int32, (tq, tk), 0)
        col = lax.broadcasted_iota(jnp.int32, (tq, tk), 1)
        return jnp.where(row >= col, scores(i, hd), -jnp.inf)

    group = FOX_BLOCKS_PER_UPDATE

    def joint(carry, full_blocks, with_diag):
        out = []
        for hd in range(2):
            s = [scores(j, hd) for j in full_blocks] + ([diag_scores(hd)] if with_diag else [])
            v = [values(j, hd) for j in full_blocks] + ([values(i, hd)] if with_diag else [])
            out.append(update(jnp.concatenate(s, axis=1), jnp.concatenate(v, axis=1), *carry[hd]))
        return tuple(out)

    def tail(n_full):
        return lambda carry: tuple(c[1] for c in joint(carry, [i - n_full + k for k in range(n_full)], True))

    init_head = (jnp.full((tq, 1), -jnp.inf, F32), jnp.zeros((tq, pw), F32))
    carry = lax.fori_loop(0, lax.shift_right_logical(i, group.bit_length() - 1),
                          lambda t, c: joint(c, [group * t + k for k in range(group)], False),
                          (init_head, init_head))
    outs = lax.switch(i & (group - 1), [tail(n) for n in range(group)], carry)
    o0 = outs[0] / outs[0][:, HEAD_DIM:HEAD_DIM + 1]
    o1 = outs[1] / outs[1][:, 0:1]
    return jnp.where(lane < HEAD_DIM, o0, o1).astype(BF16)


def _fox_call(q, ktc, vtc, c4, batch, seq):
    n, w = q.shape
    tq = ktc.shape[-1]
    nq = seq // tq
    pw = 2 * HEAD_DIM
    return pl.pallas_call(
        _fox_kernel,
        grid=(batch, HEAD_PAIRS),
        in_specs=[pl.BlockSpec((seq, pw), lambda b, p: (b, p)),
                  pl.BlockSpec((None, nq, pw, tq), lambda b, p: (b, 0, p, 0)),
                  pl.BlockSpec((None, nq, pw, tq), lambda b, p: (b, 0, p, 0)),
                  pl.BlockSpec((None, 2, nq, tq), lambda b, p: (b, p, 0, 0))],
        out_specs=pl.BlockSpec((seq, pw), lambda b, p: (b, p)),
        out_shape=jax.ShapeDtypeStruct((n, w), BF16),
        compiler_params=_cparams(("parallel", "parallel"), big=True),
        name="fox_prompt",
    )(q, ktc, vtc, c4)


def _ret_kernel(q_ref, kt_ref, v_ref, sg_ref, lgl_ref, lgr_ref, y_ref, s_ref):
    chunk = RET_CHUNK
    pw = 2 * HEAD_DIM
    nkb, _, tb = kt_ref.shape
    lg_lane = lgl_ref[...]
    lg_row = lgr_ref[...]
    lg_a, lg_b = lg_lane[:, 0:1], lg_lane[:, HEAD_DIM:HEAD_DIM + 1]

    ri = lax.broadcasted_iota(jnp.int32, (2 * chunk, chunk), 0)
    cj = lax.broadcasted_iota(jnp.int32, (2 * chunk, chunk), 1)
    first = ri < chunk
    diff = (jnp.where(first, ri, ri - chunk) - cj).astype(F32)
    decay_mask = jnp.where(diff >= 0, jnp.exp(jnp.maximum(diff, 0.0) * jnp.where(first, lg_a, lg_b)), 0.0)
    jj = lax.broadcasted_iota(jnp.int32, (pw, chunk), 1).astype(F32)
    col_decay = jnp.exp((chunk - 1.0 - jj) * lg_row)
    ii = lax.broadcasted_iota(jnp.int32, (chunk, pw), 0).astype(F32)
    row_decay = jnp.exp((ii + 1.0) * lg_lane)
    chunk_decay = jnp.exp(float(chunk) * lg_row)
    r2 = lax.broadcasted_iota(jnp.int32, (pw, pw), 0)
    c2 = lax.broadcasted_iota(jnp.int32, (pw, pw), 1)
    same_head = (r2 < HEAD_DIM) == (c2 < HEAD_DIM)
    seg_avg = jnp.where(same_head, 1.0 / HEAD_DIM, 0.0).astype(BF16)
    lane = lax.broadcasted_iota(jnp.int32, (chunk, pw), 1)

    def seg_mean(a):
        return _dot(a.astype(BF16), seg_avg)

    def one_chunk(q, kt, v, sg, state):
        zero = jnp.zeros_like(q)
        q_stack = jnp.concatenate([jnp.where(lane < HEAD_DIM, q, zero), jnp.where(lane >= HEAD_DIM, q, zero)], axis=0)
        qk = _dot(q_stack, kt.astype(BF16)) * decay_mask
        kd = (kt * col_decay).astype(BF16)
        r = _dot(jnp.concatenate([qk.astype(BF16), kd], axis=0), v)
        inner = jnp.where(lane < HEAD_DIM, r[0:chunk], r[chunk:2 * chunk])
        update = jnp.where(same_head, r[2 * chunk:], 0.0)
        cross = _dot(q, state.astype(BF16)) * row_decay
        o = inner + cross
        mu = seg_mean(o)
        d = o - mu
        var = seg_mean(d * d)
        y = sg.astype(F32) * (d * lax.rsqrt(var + GN_EPS))
        return y.astype(BF16), chunk_decay * state + update

    def body(jb, state):
        kt_blk = kt_ref[jb]
        for sub in range(tb // chunk):
            t0 = pl.multiple_of(jb * tb + sub * chunk, chunk)
            y, state = one_chunk(q_ref[pl.ds(t0, chunk), :], kt_blk[:, sub * chunk:(sub + 1) * chunk],
                                 v_ref[pl.ds(t0, chunk), :], sg_ref[pl.ds(t0, chunk), :], state)
            y_ref[pl.ds(t0, chunk), :] = y
        return state

    state = lax.fori_loop(0, nkb, body, jnp.zeros((pw, pw), F32), unroll=8)
    s_ref[0] = state[0:HEAD_DIM, 0:HEAD_DIM]
    s_ref[1] = pltpu.roll(state, HEAD_DIM, 1)[HEAD_DIM:, 0:HEAD_DIM]


def _ret_call(qr, krt, vr, sg, lg_lane, lg_row, batch, seq):
    n, w = qr.shape
    nkb, tb = krt.shape[1], krt.shape[3]
    pw = 2 * HEAD_DIM
    seq_spec = pl.BlockSpec((seq, pw), lambda b, p: (b, p))
    return pl.pallas_call(
        _ret_kernel,
        grid=(batch, HEAD_PAIRS),
        in_specs=[seq_spec,
                  pl.BlockSpec((None, nkb, pw, tb), lambda b, p: (b, 0, p, 0)),
                  seq_spec, seq_spec,
                  pl.BlockSpec((None, 1, pw), lambda b, p: (p, 0, 0)),
                  pl.BlockSpec((None, pw, 1), lambda b, p: (p, 0, 0))],
        out_specs=[seq_spec, pl.BlockSpec((None, 2, HEAD_DIM, HEAD_DIM), lambda b, p: (b, p, 0, 0))],
        out_shape=[jax.ShapeDtypeStruct((n, w), BF16),
                   jax.ShapeDtypeStruct((batch, N_HEADS, HEAD_DIM, HEAD_DIM), F32)],
        compiler_params=_cparams(("parallel", "parallel"), big=True),
        name="ret_prompt",
    )(qr, krt, vr, sg, lg_lane, lg_row)


def _odd_prompt_kernel(x_ref, sh_ref, sc_ref, g_ref, win_ref, cw_ref, wout_ref, lng_ref, lnb_ref,
                       o_ref, st_ref, carry_ref, *, bps, alpha):
    i = pl.program_id(0)
    x = x_ref[...]
    tm, d = x.shape
    h = (x * (1.0 + sc_ref[...]) + sh_ref[...]).astype(BF16)
    b_gate = _dot(h, win_ref[:, 0:d])
    u = _dot(h, win_ref[:, d:2 * d]) * _dot(h, win_ref[:, 2 * d:3 * d])

    @pl.when(i % bps == 0)
    def _():
        carry_ref[...] = jnp.zeros_like(carry_ref)

    prev = carry_ref[...]
    p1, p2 = prev[7:8, :], prev[6:7, :]
    row = lax.broadcasted_iota(jnp.int32, (tm, d), 0)
    u1 = jnp.where(row == 0, p1, pltpu.roll(u, 1, 0))
    u2 = jnp.where(row == 0, p2, jnp.where(row == 1, p1, pltpu.roll(u, 2, 0)))
    cw = cw_ref[...]
    z = cw[0:1, :] * u2 + cw[1:2, :] * u1 + cw[2:3, :] * u
    carry_ref[...] = u[tm - 8:, :]
    st_ref[...] = u[tm - (CONV_WIDTH - 1):, :]
    y = _dot((b_gate * z).astype(BF16), wout_ref[...])
    zz = alpha * x + g_ref[...] * y
    o_ref[...] = _layernorm(zz, lng_ref[...], lnb_ref[...])


def _odd_prompt_call(x, rows, ks, w_in, conv_w, w_out, ln_g, ln_b, ln_idx, alpha, batch, seq, decode=None,
                     side=None):
    n, d = x.shape
    bps = seq // rows.tm
    in_specs = [rows.row_spec(d), rows.mod_spec(ks[0]), rows.mod_spec(ks[1]), rows.mod_spec(ks[2]),
                _const_spec(w_in), _const_spec(conv_w), _const_spec(w_out),
                pl.BlockSpec((None, 1, d), lambda i, *_: (ln_idx, 0, 0)),
                pl.BlockSpec((None, 1, d), lambda i, *_: (ln_idx, 0, 0))]
    out_specs = [rows.row_spec(d), pl.BlockSpec((None, CONV_WIDTH - 1, d), lambda i, *_: (i // bps, 0, 0))]
    out_shape = [jax.ShapeDtypeStruct((n, d), F32), jax.ShapeDtypeStruct((batch, CONV_WIDTH - 1, d), F32)]

    return _run_host("odd_prompt", functools.partial(_odd_prompt_kernel, bps=bps, alpha=alpha), n // rows.tm,
                     in_specs, out_specs, out_shape, (x, rows.mods, rows.mods, rows.mods, w_in, conv_w, w_out, ln_g, ln_b),
                     [pltpu.VMEM((8, d), F32)], decode, side)


def _even_in_sample_kernel(x_ref, sh_ref, sc_ref, w_ref, bf_ref, cost_ref, sint_ref,
                           q_o, k_o, v_o, qrt_o, krt_o, vr_o, sg_o, lf_o):
    d = w_ref.shape[1]
    h = (x_ref[...] * (1.0 + sc_ref[...]) + sh_ref[...]).astype(BF16)

    def rows(name):
        lo, hi = _EVEN_ROWS[name]
        return w_ref[lo:hi, :]

    def tok(name):
        return _dot_nt(h, rows(name).astype(BF16))

    q_o[...] = tok("qf") * QK_SCALE
    k_o[...] = tok("kf")
    v_o[...] = tok("vf")
    vr_o[...] = tok("vr")
    sg_o[...] = _silu(tok("gr"))
    fg_rows = jnp.concatenate([rows("fg"), jnp.zeros((V7X_LANES - N_HEADS, d), F32)], axis=0).astype(BF16)
    lf_o[...] = _log_sigmoid(_dot_nt(h, fg_rows) + bf_ref[...])

    cos_t, sin_t = cost_ref[...], sint_ref[...]
    half = HEAD_DIM // 2
    for name, out, scale in (("qr", qrt_o, 1.0), ("kr", krt_o, QK_SCALE)):
        t = _dot_nt(rows(name).astype(BF16), h)
        for hh in range(N_HEADS):
            x1 = t[hh * HEAD_DIM:hh * HEAD_DIM + half, :]
            x2 = t[hh * HEAD_DIM + half:(hh + 1) * HEAD_DIM, :]
            out[hh * HEAD_DIM:hh * HEAD_DIM + half, :] = (x1 * cos_t - x2 * sin_t) * scale
            out[hh * HEAD_DIM + half:(hh + 1) * HEAD_DIM, :] = (x1 * sin_t + x2 * cos_t) * scale


def _even_in_sample_call(x, rows, ks, w_t_f32, bf_row, cos_t, sin_t):
    n, d = x.shape
    w = HEAD_WIDTH
    full = pl.BlockSpec((n, w), lambda i: (0, 0))
    full_t = pl.BlockSpec((w, n), lambda i: (0, 0))
    rot = pl.BlockSpec((HEAD_DIM // 2, n), lambda i: (0, 0))
    shp = jax.ShapeDtypeStruct((n, w), F32)
    shp_t = jax.ShapeDtypeStruct((w, n), F32)
    return pl.pallas_call(
        _even_in_sample_kernel,
        grid=(1,),
        in_specs=[rows.row_spec(d), rows.mod_spec(ks[0]), rows.mod_spec(ks[1]),
                  _const_spec(w_t_f32), _const_spec(bf_row), rot, rot],
        out_specs=[full, full, full, full_t, full_t, full, full, pl.BlockSpec((n, V7X_LANES), lambda i: (0, 0))],
        out_shape=[shp, shp, shp, shp_t, shp_t, shp, shp, jax.ShapeDtypeStruct((n, V7X_LANES), F32)],
        compiler_params=_cparams(("arbitrary",), big=True),
        name="even_inproj_sample",
    )(x, rows.mods, rows.mods, w_t_f32, bf_row, cos_t, sin_t)


def _own_head_lanes():
    sub = lax.broadcasted_iota(jnp.int32, (N_HEADS, HEAD_WIDTH), 0)
    lane = lax.broadcasted_iota(jnp.int32, (N_HEADS, HEAD_WIDTH), 1)
    return (lane >= sub * HEAD_DIM) & (lane < (sub + 1) * HEAD_DIM)


def _decode_fox_init(q_ref, kn_ref, vn_ref, lfn_ref, qbd_s, m_s, l_s, run_s, acc_s):
    w = HEAD_WIDTH
    q_bd = jnp.where(_own_head_lanes(), jnp.broadcast_to(q_ref[...], (N_HEADS, w)), 0.0)
    qbd_s[...] = q_bd
    m_s[...] = jnp.sum(q_bd * kn_ref[...], axis=-1, keepdims=True)
    l_s[...] = jnp.ones_like(l_s)
    page_len = acc_s.shape[1]
    lane_p = lax.broadcasted_iota(jnp.int32, (w, page_len), 1)
    acc_s[...] = jnp.where(lane_p == 0, jnp.broadcast_to(vn_ref[...], (page_len, w)).T, 0.0)
    s128 = lax.broadcasted_iota(jnp.int32, (N_HEADS, V7X_LANES), 0)
    l128 = lax.broadcasted_iota(jnp.int32, (N_HEADS, V7X_LANES), 1)
    lfn = jnp.broadcast_to(lfn_ref[...], (N_HEADS, V7X_LANES))
    run_s[...] = jnp.sum(jnp.where(s128 == l128, lfn, 0.0), axis=-1, keepdims=True)


def _decode_fox_finish(o_ref, l_s, acc_s):
    tot = jnp.sum(acc_s[...].T, axis=0, keepdims=True)
    l_b = jnp.broadcast_to(l_s[...], (N_HEADS, HEAD_WIDTH))
    o_ref[...] = tot / jnp.sum(jnp.where(_own_head_lanes(), l_b, 0.0), axis=0, keepdims=True)


def _decode_fox_pages(k_refs, v_refs, lf_refs, qbd_s, m_s, l_s, run_s, acc_s):
    pages = len(k_refs)
    page_len = k_refs[0].shape[-1]
    q_b = qbd_s[...].astype(BF16)
    order = list(range(pages - 1, -1, -1))
    s = jnp.concatenate([_dot(q_b, k_refs[r][...].astype(BF16)) for r in order], axis=1)
    lf = jnp.concatenate([lf_refs[r][...] for r in order], axis=1)
    n = pages * page_len
    lane_n = lax.broadcasted_iota(jnp.int32, (N_HEADS, n), 1)
    suf = lf
    shift = 1
    while shift < n:
        suf = suf + jnp.where(lane_n < n - shift, pltpu.roll(suf, n - shift, 1), 0.0)
        shift *= 2
    run = run_s[...]
    s = s + ((suf - lf) + run)
    m = m_s[...]
    m_new = jnp.maximum(m, jnp.max(s, axis=-1, keepdims=True))
    a = jnp.exp(m - m_new)
    p = jnp.exp(s - m_new)
    l_s[...] = a * l_s[...] + jnp.sum(p, axis=-1, keepdims=True)
    m_s[...] = m_new
    for h in range(N_HEADS):
        sl = slice(h * HEAD_DIM, (h + 1) * HEAD_DIM)
        upd = None
        for idx, r in enumerate(order):
            t = p[h:h + 1, idx * page_len:(idx + 1) * page_len] * v_refs[r][sl, :]
            upd = t if upd is None else upd + t
        acc_s[sl, :] = a[h:h + 1, :] * acc_s[sl, :] + upd
    run_s[...] = run + suf[:, 0:1]


def _decode_ret_kernel(qt_ref, kt_ref, v_ref, sg_ref, s_ref, gl_ref, y_ref, so_ref):
    qt, kt = qt_ref[...], kt_ref[...]
    g = gl_ref[...]
    for b in range(qt.shape[-1]):
        q, k = qt[:, :, b:b + 1], kt[:, :, b:b + 1]
        v = v_ref[b]
        state = s_ref[b]
        inner = jnp.sum(q * k, axis=1, keepdims=True) * v
        cross = jnp.sum(q * state, axis=1, keepdims=True) * g
        so_ref[b] = g * state + k * v
        o = inner + cross
        mu = jnp.mean(o, axis=-1, keepdims=True)
        d = o - mu
        var = jnp.mean(d * d, axis=-1, keepdims=True)
        y_ref[b] = sg_ref[b] * (d * lax.rsqrt(var + GN_EPS))


def _decode_ret_call(q_t, k_t, v_row, sg_row, state, decay):
    nseq = state.shape[0]

    def whole(arr):
        nd = arr.ndim
        return pl.BlockSpec(arr.shape, lambda i: (0,) * nd)

    return pl.pallas_call(
        _decode_ret_kernel,
        grid=(1,),
        in_specs=[whole(a) for a in (q_t, k_t, v_row, sg_row, state, decay)],
        out_specs=[whole(v_row), whole(state)],
        out_shape=[jax.ShapeDtypeStruct(v_row.shape, F32), jax.ShapeDtypeStruct(state.shape, F32)],
        compiler_params=_cparams(("arbitrary",), big=True),
        name="ret_decode",
    )(q_t, k_t, v_row, sg_row, state, decay)


def _odd_sample_kernel(x_ref, sh_ref, sc_ref, g_ref, win_ref, cw_ref, b0_ref, b1_ref, wout_ref, lng_ref, lnb_ref,
                       o_ref, u_ref, *, alpha):
    x = x_ref[...]
    d = x.shape[-1]
    h = (x * (1.0 + sc_ref[...]) + sh_ref[...]).astype(BF16)
    b_gate = _dot(h, win_ref[:, 0:d])
    u = _dot(h, win_ref[:, d:2 * d]) * _dot(h, win_ref[:, 2 * d:3 * d])
    cw = cw_ref[...]
    z = cw[0:1, :] * b0_ref[...] + cw[1:2, :] * b1_ref[...] + cw[2:3, :] * u
    u_ref[...] = u
    y = _dot((b_gate * z).astype(BF16), wout_ref[...])
    o_ref[...] = _layernorm(alpha * x + g_ref[...] * y, lng_ref[...], lnb_ref[...])


def _odd_sample_call(x, rows, ks, w_in, conv_w, buf0, buf1, w_out, ln_g, ln_b, ln_idx, alpha):
    n, d = x.shape
    full = pl.BlockSpec((n, d), lambda i: (0, 0))
    return pl.pallas_call(
        functools.partial(_odd_sample_kernel, alpha=alpha),
        grid=(1,),
        in_specs=[rows.row_spec(d), rows.mod_spec(ks[0]), rows.mod_spec(ks[1]), rows.mod_spec(ks[2]),
                  _const_spec(w_in), _const_spec(conv_w), full, full, _const_spec(w_out),
                  pl.BlockSpec((None, 1, d), lambda i: (ln_idx, 0, 0)),
                  pl.BlockSpec((None, 1, d), lambda i: (ln_idx, 0, 0))],
        out_specs=[full, full],
        out_shape=[jax.ShapeDtypeStruct((n, d), F32), jax.ShapeDtypeStruct((n, d), F32)],
        compiler_params=_cparams(("arbitrary",), big=True),
        name="odd_sample",
    )(x, rows.mods, rows.mods, rows.mods, w_in, conv_w, buf0, buf1, w_out, ln_g, ln_b)


def _rotary_tables(pos):
    half = HEAD_DIM // 2
    inv = 1.0 / (RET_ANGLE_BASE ** np.linspace(0.0, 1.0, half))
    ang = np.asarray(pos, np.float64)[:, None] * inv[None, :]
    return np.cos(ang), np.sin(ang)


def _token_major_tables(cos, sin):
    cos_h = np.concatenate([cos, cos], axis=1)
    sin_h = np.concatenate([-sin, sin], axis=1)
    return (jnp.asarray(np.tile(cos_h, (1, N_HEADS)), F32), jnp.asarray(np.tile(sin_h, (1, N_HEADS)), F32))


def kernel(x_prompt, x_sample, cache_k, cache_v, cache_logf, state_ret, state_conv, page_table, c_prompt, c_sample,
           w_ada, b_ada, w_ffn_in, w_ffn_out, ln_g, ln_b, w_in_even, b_forget, w_out_even, w_in_odd, conv_w,
           w_out_odd):
    batch, seq, d = x_prompt.shape
    nseq = x_sample.shape[0]
    depth = w_ada.shape[0]
    past_len = page_table.shape[1] * cache_k.shape[2]
    alpha = (2.0 * depth) ** 0.25
    w = HEAD_WIDTH
    tm = ROW_BLOCK
    assert seq % tm == 0 and x_sample.shape[1] == 1 and d % V7X_LANES == 0
    assert cache_k.shape[3] == N_HEADS and cache_k.shape[4] == HEAD_DIM

    n_c = batch + nseq
    pad = (-n_c) % 8
    c_all = jnp.concatenate([c_prompt, c_sample, jnp.zeros((pad, d), F32)], axis=0)
    mods = _ada_call(c_all, w_ada, b_ada)
    mods_p = mods[:, :batch].reshape(depth, batch, 1, N_MOD * d)
    mods_s = mods[:, batch:n_c]

    ln_g3 = ln_g.reshape(depth * 3, 1, d)
    ln_b3 = ln_b.reshape(depth * 3, 1, d)
    w_bf = {(0, 0): (w_ffn_in[0, 0].astype(BF16), w_ffn_out[0, 0].astype(BF16))}

    def side_cast(l, half):
        return _SideCast(w_ffn_in, w_ffn_out, (l, half), batch * seq // tm)

    xp = x_prompt.reshape(batch * seq, d)
    xs = x_sample.reshape(nseq, d)

    cos_p, sin_p = _rotary_tables(np.arange(seq))
    rot_p = _token_major_tables(cos_p, sin_p) + (jnp.asarray(cos_p.T, F32), jnp.asarray(sin_p.T, F32))
    cos_s, sin_s = (jnp.asarray(t.T, F32) for t in _rotary_tables(np.full((nseq,), past_len)))

    log_decay = jnp.log(1.0 - 2.0 ** (-5.0 - jnp.arange(N_HEADS, dtype=F32)))
    lg_pairs = jnp.repeat(log_decay.reshape(HEAD_PAIRS, 2), HEAD_DIM, axis=1)
    lg_lane = lg_pairs.reshape(HEAD_PAIRS, 1, 2 * HEAD_DIM)
    lg_row = lg_pairs.reshape(HEAD_PAIRS, 2 * HEAD_DIM, 1)
    step_decay = jnp.exp(log_decay).reshape(N_HEADS, 1, 1)

    outs_p = {"k": [], "v": [], "lf": [], "ret": [], "conv": []}
    outs_s = {"k": [], "v": [], "lf": [], "ret": [], "conv": []}

    assert depth == 2 and w_in_even.shape[0] == 1 and cache_k.shape[0] == 1
    n_hosts = 2 * depth
    assert nseq % n_hosts == 0
    assert w_in_even.shape[2] == EVEN_IN_ROWS
    w_even_t = jnp.transpose(w_in_even[0])
    w_out_even_b = w_out_even[0].astype(BF16)
    bf = b_forget[0]
    w_in_odd_b = w_in_odd[0].astype(BF16)
    w_out_odd_b = w_out_odd[0].astype(BF16)

    def rows_s(l):
        return _Rows(mods_s, l, nseq, None, d)

    def rows_p(l):
        return _Rows(mods_p, l, tm, seq // tm, d)

    def ffn(x, rows, half, l, decode=None, mixer_out=None, side=None):
        ks = (0, 1, 2) if half == 0 else (6, 7, 8)
        return _ffn_call(x, rows, ks, *w_bf[(l, half)], ln_g3, ln_b3, 3 * l + 2 * half, alpha, decode, mixer_out, side)

    xs = ffn(xs, rows_s(0), 0, 0)
    bf_row = jnp.concatenate([bf, jnp.zeros((V7X_LANES - N_HEADS,), F32)]).reshape(1, V7X_LANES)
    (qs, ks_, vs, qrs_t, krs_t, vrs, sgs, lfs) = _even_in_sample_call(xs, rows_s(0), (3, 4), w_even_t, bf_row,
                                                                       cos_s, sin_s)
    n_phys, page_len = cache_k.shape[1], cache_k.shape[2]
    kt_pages = jnp.transpose(cache_k[0], (0, 2, 3, 1)).reshape(n_phys, w, page_len)
    vt_pages = jnp.transpose(cache_v[0], (0, 2, 3, 1)).reshape(n_phys, w, page_len)
    lft_pages = jnp.transpose(cache_logf[0], (0, 2, 1))
    per_host = nseq // n_hosts
    dec_args = (page_table, qs.reshape(nseq, 1, w), ks_.reshape(nseq, 1, w), vs.reshape(nseq, 1, w),
                lfs.reshape(nseq, 1, V7X_LANES), kt_pages, vt_pages, lft_pages)
    dec_out = []

    def ffn_host(x, half, l, mixer_out=None, cast_for=None):
        side = None if cast_for is None else side_cast(*cast_for)
        res = ffn(x, rows_p(l), half, l, dec_args + (len(dec_out) * per_host, per_host), mixer_out, side)
        dec_out.append(res[1])
        if cast_for is not None:
            w_bf[cast_for] = tuple(res[2:])
        return res[0]

    xp = ffn_host(xp, 0, 0, cast_for=(1, 0))
    (q, qr, vr, sg, kt, vt, ktc, vtc, krt, lft, *w_bf[(0, 1)]) = _even_in_call(
        xp, rows_p(0), (3, 4), w_even_t, bf.reshape(N_HEADS, 1), rot_p, batch, seq, side=side_cast(0, 1))
    c_t = _cumsum_call(lft)
    of = _fox_call(q, ktc, vtc, c_t.reshape(batch, N_HEADS, seq // tm, tm), batch, seq)
    yr, s_new = _ret_call(qr, krt, vr, sg, lg_lane, lg_row, batch, seq)
    outs_p["k"].append(jnp.transpose(kt.reshape(batch, N_HEADS, HEAD_DIM, seq), (0, 3, 1, 2)))
    outs_p["v"].append(jnp.transpose(vt.reshape(batch, N_HEADS, HEAD_DIM, seq), (0, 3, 1, 2)))
    outs_p["lf"].append(jnp.transpose(lft, (0, 2, 1)))
    outs_p["ret"].append(s_new)
    xp = ffn_host(xp, 1, 0, mixer_out=(5, of, yr, w_out_even_b, 1))
    xp = ffn_host(xp, 0, 1)
    xp, conv_p, *w_bf[(1, 1)] = _odd_prompt_call(xp, rows_p(1), (3, 4, 5), w_in_odd_b, conv_w[0], w_out_odd_b,
                                                 ln_g3, ln_b3, 4, alpha, batch, seq, side=side_cast(1, 1))
    outs_p["conv"].append(conv_p)
    xp = ffn_host(xp, 1, 1)

    of_s = jnp.concatenate(dec_out, axis=0).reshape(nseq, w)
    lfs = lfs[:, :N_HEADS]
    hdn = (N_HEADS, HEAD_DIM, nseq)
    rw = (nseq, N_HEADS, 1, HEAD_DIM)
    yr_s, s_new_s = _decode_ret_call(qrs_t.reshape(hdn), krs_t.reshape(hdn), vrs.reshape(rw), sgs.reshape(rw),
                                     state_ret[0], step_decay)
    xs = _outproj_call(xs, rows_s(0), 5, of_s, yr_s.reshape(nseq, w), w_out_even_b, ln_g3, ln_b3, 1, alpha)
    outs_s["k"].append(ks_.reshape(nseq, 1, N_HEADS, HEAD_DIM))
    outs_s["v"].append(vs.reshape(nseq, 1, N_HEADS, HEAD_DIM))
    outs_s["lf"].append(lfs.reshape(nseq, 1, N_HEADS))
    outs_s["ret"].append(s_new_s)
    xs = ffn(xs, rows_s(0), 1, 0)
    xs = ffn(xs, rows_s(1), 0, 1)
    buf = state_conv[0]
    xs, u_s = _odd_sample_call(xs, rows_s(1), (3, 4, 5), w_in_odd_b, conv_w[0], buf[:, 0], buf[:, 1], w_out_odd_b,
                               ln_g3, ln_b3, 4, alpha)
    outs_s["conv"].append(jnp.stack([buf[:, 1], u_s], axis=1))
    xs = ffn(xs, rows_s(1), 1, 1)

    def stk(lst):
        return jnp.stack(lst)

    return (xp.reshape(batch, seq, d), xs.reshape(nseq, 1, d),
            stk(outs_p["k"]), stk(outs_p["v"]), stk(outs_p["lf"]), stk(outs_p["ret"]), stk(outs_p["conv"]),
            stk(outs_s["k"]), stk(outs_s["v"]), stk(outs_s["lf"]), stk(outs_s["ret"]), stk(outs_s["conv"]))
```

```python
import functools

import jax
import jax.numpy as jnp
import numpy as np
from jax import lax
from jax.experimental import pallas as pl
from jax.experimental.pallas import tpu as pltpu

F32 = jnp.float32
BF16 = jnp.bfloat16

HEAD_DIM = 64
N_HEADS = 8
HEAD_WIDTH = N_HEADS * HEAD_DIM
N_MOD = 9
CONV_WIDTH = 3
RET_ANGLE_BASE = 10000.0
LN_EPS = 1e-5
GN_EPS = 1e-6
QK_SCALE = HEAD_DIM ** -0.5
LOG2E = 1.4426950408889634

V7X_LANES = 128
V7X_BF16_SUBLANES = 16
V7X_VMEM_LIMIT_BYTES = 56 * 1024 * 1024

ROW_BLOCK = 512
FF_CHUNK = 256
FOX_BLOCKS_PER_UPDATE = 4
RET_CHUNK = 256
ADA_COL_BLOCK = 1152
HEAD_PAIRS = N_HEADS // 2


def _cparams(sem, big=False):
    return pltpu.CompilerParams(dimension_semantics=sem,
                                vmem_limit_bytes=V7X_VMEM_LIMIT_BYTES if big else None)


def _const_spec(arr):
    nd = arr.ndim
    return pl.BlockSpec(arr.shape, lambda *_: (0,) * nd, pipeline_mode=pl.Buffered(1))


def _layernorm(z, g, b):
    mu = jnp.mean(z, axis=-1, keepdims=True)
    d = z - mu
    var = jnp.mean(d * d, axis=-1, keepdims=True)
    return d * lax.rsqrt(var + LN_EPS) * g + b


def _silu(a):
    return a * jax.nn.sigmoid(a)


def _log_sigmoid(z):
    return jnp.minimum(z, 0.0) - jnp.log1p(jnp.exp(-jnp.abs(z)))


def _dot(a, b):
    return jnp.dot(a, b, preferred_element_type=F32)


def _dot_nt(a, b):
    return lax.dot_general(a, b, (((1,), (1,)), ((), ())), preferred_element_type=F32)


def _ada_kernel(c_ref, w_ref, b_ref, o_ref):
    s = _silu(c_ref[...]).astype(BF16)
    o_ref[...] = _dot(s, w_ref[...].astype(BF16)) + b_ref[...]


def _ada_call(c_all, w_ada, b_ada):
    depth, d, nm = w_ada.shape
    rows = c_all.shape[0]
    tn = ADA_COL_BLOCK
    return pl.pallas_call(
        _ada_kernel,
        grid=(depth, nm // tn),
        in_specs=[pl.BlockSpec((rows, d), lambda l, j: (0, 0)),
                  pl.BlockSpec((None, d, tn), lambda l, j: (l, 0, j)),
                  pl.BlockSpec((None, 1, tn), lambda l, j: (l, 0, j))],
        out_specs=pl.BlockSpec((None, rows, tn), lambda l, j: (l, 0, j)),
        out_shape=jax.ShapeDtypeStruct((depth, rows, nm), F32),
        compiler_params=_cparams(("arbitrary", "arbitrary"), big=True),
        name="ada_mods",
    )(c_all, w_ada, b_ada.reshape(depth, 1, nm))


class _SideCast:
    def __init__(self, w_in, w_out, which, steps):
        l, j = which
        d, ff2 = w_in.shape[2:]
        ff, d2 = w_out.shape[2:]
        rin, nout = d // steps, steps // 2
        rout = ff // nout
        assert rin * steps == d and rout * nout == ff and rin % V7X_BF16_SUBLANES == 0 and rout % V7X_BF16_SUBLANES == 0
        self.args = (w_in, w_out)
        self.in_specs = [pl.BlockSpec((None, None, rin, ff2), lambda i, *_: (l, j, i, 0)),
                         pl.BlockSpec((None, None, rout, d2), lambda i, *_: (l, j, jnp.minimum(i, nout - 1), 0))]
        self.out_specs = [pl.BlockSpec((rin, ff2), lambda i, *_: (i, 0)),
                          pl.BlockSpec((rout, d2), lambda i, *_: (jnp.minimum(i, nout - 1), 0))]
        self.out_shape = [jax.ShapeDtypeStruct((d, ff2), BF16), jax.ShapeDtypeStruct((ff, d2), BF16)]


def _with_side_cast(kernel, n_prefetch, n_in, n_out):
    def wrapped(*refs):
        a, b = n_prefetch + n_in, n_prefetch + n_in + 2 + n_out
        wi_ref, wo_ref = refs[a:a + 2]
        wib_ref, wob_ref = refs[b:b + 2]
        wib_ref[...] = wi_ref[...].astype(BF16)
        wob_ref[...] = wo_ref[...].astype(BF16)
        kernel(*refs[:a], *refs[a + 2:b], *refs[b + 2:])

    return wrapped


class _Rows:
    def __init__(self, mods, layer, tm, blocks_per_seq, d):
        self.mods, self.layer, self.tm, self.bps, self.d = mods, layer, tm, blocks_per_seq, d

    def mod_spec(self, k):
        l, d = self.layer, self.d
        if self.bps is None:
            return pl.BlockSpec((None, self.tm, d), lambda i, *_: (l, 0, k))
        bps = self.bps
        return pl.BlockSpec((None, None, 1, d), lambda i, *_: (l, i // bps, 0, k))

    def row_spec(self, width):
        return pl.BlockSpec((self.tm, width), lambda i, *_: (i, 0))


def _mixer_out_math(x, gate, a0, a1, w_ref, lng, lnb, alpha):
    half = a0.shape[-1]
    y = _dot(a0.astype(BF16), w_ref[0:half, :]) + _dot(a1.astype(BF16), w_ref[half:, :])
    return _layernorm(alpha * x + gate * y, lng, lnb)


def _ffn_rows(refs, ff, alpha, pre):
    x = refs[0][...]
    refs = refs[1:]
    if pre:
        gate, a0, a1, wo_ref, lng1, lnb1 = refs[:6]
        x = _mixer_out_math(x, gate[...], a0[...], a1[...], wo_ref, lng1[...], lnb1[...], alpha)
        refs = refs[6:]
    sh_ref, sc_ref, g_ref, win_ref, wout_ref, lng_ref, lnb_ref = refs
    h = (x * (1.0 + sc_ref[...]) + sh_ref[...]).astype(BF16)
    acc = None
    for c in range(ff // FF_CHUNK):
        lo, hi = c * FF_CHUNK, (c + 1) * FF_CHUNK
        a = _dot(h, win_ref[:, lo:hi])
        b = _dot(h, win_ref[:, ff + lo:ff + hi])
        y = _dot((_silu(a) * b).astype(BF16), wout_ref[lo:hi, :])
        acc = y if acc is None else acc + y
    z = alpha * x + (0.5 * g_ref[...]) * acc
    return _layernorm(z, lng_ref[...], lnb_ref[...])


def _ffn_kernel(*refs, ff, alpha, pre):
    refs[-1][...] = _ffn_rows(refs[:-1], ff, alpha, pre)


def _with_decode(kernel, n_in, n_out, pages, steps_per_seq, seq0):
    def wrapped(pt_ref, *refs):
        ins = refs[:n_in]
        q_ref, kn_ref, vn_ref, lfn_ref, kt_hbm, vt_hbm, lf_hbm = refs[n_in:n_in + 7]
        outs = refs[n_in + 7:n_in + 7 + n_out]
        dec_ref = refs[n_in + 7 + n_out]
        scratch = refs[n_in + 8 + n_out:]
        own, state = scratch[:-9], scratch[-9:-4]
        kbuf, vbuf, lfbuf, sem = scratch[-4:]
        npg = pt_ref.shape[1]
        i = pl.program_id(0)
        last = pl.num_programs(0) - 1
        slot = i % 2

        def page_copies(step, into):
            seq = seq0 + step // steps_per_seq
            first = npg - 1 - (step % steps_per_seq) * pages
            out = []
            for r in range(pages):
                pg = pt_ref[seq, first - r]
                out += [pltpu.make_async_copy(kt_hbm.at[pg], kbuf.at[into, r], sem.at[into, 0]),
                        pltpu.make_async_copy(vt_hbm.at[pg], vbuf.at[into, r], sem.at[into, 1]),
                        pltpu.make_async_copy(lf_hbm.at[pg], lfbuf.at[into, r], sem.at[into, 2])]
            return out

        @pl.when(i == 0)
        def _():
            for c in page_copies(0, 0):
                c.start()

        nxt = jnp.minimum(i + 1, last)
        for n, c in enumerate(page_copies(nxt, 1 - slot)):
            c.start(priority=n % 2)
        for c in page_copies(i, slot):
            c.wait()

        g = i % steps_per_seq
        pl.when(g == 0)(lambda: _decode_fox_init(q_ref, kn_ref, vn_ref, lfn_ref, *state))
        _decode_fox_pages([kbuf.at[slot, r] for r in range(pages)], [vbuf.at[slot, r] for r in range(pages)],
                          [lfbuf.at[slot, r] for r in range(pages)], *state)
        kernel(*ins, *outs, *own)
        pl.when(g == steps_per_seq - 1)(lambda: _decode_fox_finish(dec_ref, state[2], state[4]))

        @pl.when(i == last)
        def _():
            for c in page_copies(nxt, 1 - slot):
                c.wait()

    return wrapped


def _run_host(name, kernel, steps, in_specs, out_specs, out_shape, args, scratch_shapes=(), decode=None, side=None):
    in_specs, out_specs, out_shape, args = list(in_specs), list(out_specs), list(out_shape), list(args)
    scratch = list(scratch_shapes)
    page_table = None
    if decode is not None:
        page_table, q, kn, vn, lfn, kt_pages, vt_pages, lft_pages, seq0, n_seq = decode
        npg = page_table.shape[1]
        w, page_len = kt_pages.shape[1], kt_pages.shape[2]
        steps_per_seq = steps // n_seq
        pages = npg // steps_per_seq
        assert steps_per_seq * n_seq == steps and pages * steps_per_seq == npg
        kernel = _with_decode(kernel, len(in_specs), len(out_specs), pages, steps_per_seq, seq0)
        row = pl.BlockSpec((None, 1, w), lambda i, pt: (seq0 + i // steps_per_seq, 0, 0))
        hbm = pl.BlockSpec(memory_space=pl.ANY)
        in_specs += [row, row, row,
                     pl.BlockSpec((None, 1, V7X_LANES), lambda i, pt: (seq0 + i // steps_per_seq, 0, 0)), hbm, hbm, hbm]
        args += [q, kn, vn, lfn, kt_pages, vt_pages, lft_pages]
        out_specs += [pl.BlockSpec((None, 1, w), lambda i, pt: (i // steps_per_seq, 0, 0))]
        out_shape += [jax.ShapeDtypeStruct((n_seq, 1, w), F32)]
        scratch += [pltpu.VMEM((N_HEADS, w), F32), pltpu.VMEM((N_HEADS, 1), F32), pltpu.VMEM((N_HEADS, 1), F32),
                    pltpu.VMEM((N_HEADS, 1), F32), pltpu.VMEM((w, page_len), F32),
                    pltpu.VMEM((2, pages, w, page_len), F32), pltpu.VMEM((2, pages, w, page_len), F32),
                    pltpu.VMEM((2, pages, N_HEADS, page_len), F32), pltpu.SemaphoreType.DMA((2, 3))]
    n_prefetch = 0 if page_table is None else 1
    if side is not None:
        kernel = _with_side_cast(kernel, n_prefetch, len(in_specs), len(out_specs))
        in_specs += side.in_specs
        out_specs += side.out_specs
        out_shape += side.out_shape
        args += side.args
    params = _cparams(("arbitrary",), big=True)
    if page_table is None:
        return pl.pallas_call(kernel, grid=(steps,), in_specs=in_specs, out_specs=out_specs, out_shape=out_shape,
                              scratch_shapes=scratch, compiler_params=params, name=name)(*args)
    grid_spec = pltpu.PrefetchScalarGridSpec(num_scalar_prefetch=1, grid=(steps,), in_specs=in_specs,
                                             out_specs=out_specs, scratch_shapes=scratch)
    return pl.pallas_call(kernel, grid_spec=grid_spec, out_shape=out_shape, compiler_params=params,
                          name=name + "_decode")(page_table, *args)


def _ffn_call(x, rows, ks, w_in, w_out, ln_g, ln_b, ln_idx, alpha, decode=None, mixer_out=None, side=None):
    n, d = x.shape
    ff = w_out.shape[0]
    steps = n // rows.tm
    w_spec = _const_spec

    def ln_spec(idx):
        return pl.BlockSpec((None, 1, d), lambda i, *_: (idx, 0, 0))

    in_specs = [rows.row_spec(d)]
    args = [x]
    if mixer_out is not None:
        k_gate, a0, a1, w_o, ln_idx1 = mixer_out
        in_specs += [rows.mod_spec(k_gate), rows.row_spec(a0.shape[1]), rows.row_spec(a1.shape[1]), _const_spec(w_o),
                     ln_spec(ln_idx1), ln_spec(ln_idx1)]
        args += [rows.mods, a0, a1, w_o, ln_g, ln_b]
    in_specs += [rows.mod_spec(ks[0]), rows.mod_spec(ks[1]), rows.mod_spec(ks[2]), w_spec(w_in), w_spec(w_out),
                 ln_spec(ln_idx), ln_spec(ln_idx)]
    args += [rows.mods, rows.mods, rows.mods, w_in, w_out, ln_g, ln_b]
    pre = mixer_out is not None
    x_shape = jax.ShapeDtypeStruct((n, d), F32)
    if decode is None:
        assert side is None
        return pl.pallas_call(
            functools.partial(_ffn_kernel, ff=ff, alpha=alpha, pre=pre),
            grid=(steps,), in_specs=in_specs, out_specs=rows.row_spec(d), out_shape=x_shape,
            compiler_params=_cparams(("parallel",), big=True),
            name="ffn",
        )(*args)

    return _run_host("ffn", functools.partial(_ffn_kernel, ff=ff, alpha=alpha, pre=pre), steps, in_specs,
                     [rows.row_spec(d)], [x_shape], args, decode=decode, side=side)


def _outproj_kernel(x_ref, g_ref, a0_ref, a1_ref, w_ref, lng_ref, lnb_ref, o_ref, *, alpha):
    o_ref[...] = _mixer_out_math(x_ref[...], g_ref[...], a0_ref[...], a1_ref[...], w_ref, lng_ref[...], lnb_ref[...],
                                 alpha)


def _outproj_call(x, rows, k_gate, a0, a1, w_out, ln_g, ln_b, ln_idx, alpha):
    n, d = x.shape
    return pl.pallas_call(
        functools.partial(_outproj_kernel, alpha=alpha),
        grid=(n // rows.tm,),
        in_specs=[rows.row_spec(d), rows.mod_spec(k_gate), rows.row_spec(a0.shape[1]), rows.row_spec(a1.shape[1]),
                  _const_spec(w_out),
                  pl.BlockSpec((None, 1, d), lambda i: (ln_idx, 0, 0)),
                  pl.BlockSpec((None, 1, d), lambda i: (ln_idx, 0, 0))],
        out_specs=rows.row_spec(d),
        out_shape=jax.ShapeDtypeStruct((n, d), F32),
        compiler_params=_cparams(("parallel",), big=True),
        name="mixer_outproj",
    )(x, rows.mods, a0, a1, w_out, ln_g, ln_b)


def _rotate_token_major(a, cos, sin_signed):
    width = a.shape[-1]
    lane = lax.broadcasted_iota(jnp.int32, a.shape, 1)
    first_half = (lane & (HEAD_DIM // 2)) == 0
    partner = jnp.where(first_half, pltpu.roll(a, width - HEAD_DIM // 2, 1), pltpu.roll(a, HEAD_DIM // 2, 1))
    return a * cos + partner * sin_signed


_EVEN_ROWS = {}
_r0 = 0
for _name, _n in (("qf", HEAD_WIDTH), ("kf", HEAD_WIDTH), ("vf", HEAD_WIDTH), ("fg", N_HEADS),
                  ("qr", HEAD_WIDTH), ("kr", HEAD_WIDTH), ("vr", HEAD_WIDTH), ("gr", HEAD_WIDTH)):
    _EVEN_ROWS[_name] = (_r0, _r0 + _n)
    _r0 += _n
EVEN_IN_ROWS = _r0
FG_ROWS = V7X_BF16_SUBLANES


def _even_in_kernel(x_ref, sh_ref, sc_ref, w_ref, bf_ref, cos_ref, sin_ref, cost_ref, sint_ref,
                    q_o, qr_o, vr_o, sg_o, kt_o, vt_o, ktc_o, vtc_o, krt_o, lft_o, wtok_ref, wt_ref):
    w = HEAD_WIDTH

    @pl.when(pl.program_id(0) == 0)
    def _():
        for k, name in enumerate(("qf", "qr", "vr", "gr")):
            lo, hi = _EVEN_ROWS[name]
            wtok_ref[:, k * w:(k + 1) * w] = w_ref[lo:hi, :].T.astype(BF16)
        for k, name in enumerate(("kf", "vf", "kr")):
            lo, hi = _EVEN_ROWS[name]
            wt_ref[k * w:(k + 1) * w, :] = w_ref[lo:hi, :].astype(BF16)
        lo, hi = _EVEN_ROWS["fg"]
        fg_rows = jnp.concatenate([w_ref[lo:hi, :], jnp.zeros((FG_ROWS - N_HEADS, w_ref.shape[1]), F32)], axis=0)
        wt_ref[3 * w:, :] = fg_rows.astype(BF16)

    x = x_ref[...]
    h = (x * (1.0 + sc_ref[...]) + sh_ref[...]).astype(BF16)

    q_o[...] = (_dot(h, wtok_ref[:, 0:w]) * (QK_SCALE * LOG2E)).astype(BF16)
    qr = _dot(h, wtok_ref[:, w:2 * w])
    qr_o[...] = _rotate_token_major(qr, cos_ref[...], sin_ref[...]).astype(BF16)
    vr_o[...] = _dot(h, wtok_ref[:, 2 * w:3 * w]).astype(BF16)
    sg_o[...] = _silu(_dot(h, wtok_ref[:, 3 * w:4 * w])).astype(BF16)

    kt = _dot_nt(wt_ref[0:w, :], h)
    kt_o[...] = kt
    ktc_o[...] = kt.astype(BF16)
    vt = _dot_nt(wt_ref[w:2 * w, :], h)
    vt_o[...] = vt
    vtc_o[...] = vt.astype(BF16)

    krt = _dot_nt(wt_ref[2 * w:3 * w, :], h)
    cos_t, sin_t = cost_ref[...], sint_ref[...]
    half = HEAD_DIM // 2
    for hh in range(N_HEADS):
        x1 = krt[hh * HEAD_DIM:hh * HEAD_DIM + half, :]
        x2 = krt[hh * HEAD_DIM + half:(hh + 1) * HEAD_DIM, :]
        krt_o[hh * HEAD_DIM:hh * HEAD_DIM + half, :] = (x1 * cos_t - x2 * sin_t) * QK_SCALE
        krt_o[hh * HEAD_DIM + half:(hh + 1) * HEAD_DIM, :] = (x1 * sin_t + x2 * cos_t) * QK_SCALE

    fg = _dot_nt(wt_ref[3 * w:, :], h)[0:N_HEADS, :] + bf_ref[...]
    lft_o[...] = _log_sigmoid(fg)


def _even_in_call(x, rows, ks, w_t_f32, bf_col, rot, batch, seq, decode=None, side=None):
    n, d = x.shape
    tm = rows.tm
    bps = seq // tm
    w = HEAD_WIDTH
    cos_tok, sin_tok, cos_t, sin_t = rot
    tok_out = pl.BlockSpec((tm, w), lambda i, *_: (i, 0))
    t_out = pl.BlockSpec((None, w, tm), lambda i, *_: (i // bps, 0, i % bps))
    tc_out = pl.BlockSpec((None, None, w, tm), lambda i, *_: (i // bps, i % bps, 0, 0))
    tok_shape = jax.ShapeDtypeStruct((n, w), BF16)
    in_specs = [rows.row_spec(d), rows.mod_spec(ks[0]), rows.mod_spec(ks[1]),
                _const_spec(w_t_f32), _const_spec(bf_col),
                pl.BlockSpec((tm, w), lambda i, *_: (i % bps, 0)),
                pl.BlockSpec((tm, w), lambda i, *_: (i % bps, 0)),
                pl.BlockSpec((HEAD_DIM // 2, tm), lambda i, *_: (0, i % bps)),
                pl.BlockSpec((HEAD_DIM // 2, tm), lambda i, *_: (0, i % bps))]
    out_specs = [tok_out, tok_out, tok_out, tok_out, t_out, t_out, tc_out, tc_out, tc_out,
                 pl.BlockSpec((None, N_HEADS, tm), lambda i, *_: (i // bps, 0, i % bps))]
    out_shape = [tok_shape, tok_shape, tok_shape, tok_shape,
                 jax.ShapeDtypeStruct((batch, w, seq), F32),
                 jax.ShapeDtypeStruct((batch, w, seq), F32),
                 jax.ShapeDtypeStruct((batch, bps, w, tm), BF16),
                 jax.ShapeDtypeStruct((batch, bps, w, tm), BF16),
                 jax.ShapeDtypeStruct((batch, bps, w, tm), F32),
                 jax.ShapeDtypeStruct((batch, N_HEADS, seq), F32)]

    return _run_host("even_inproj", _even_in_kernel, n // tm, in_specs, out_specs, out_shape,
                     (x, rows.mods, rows.mods, w_t_f32, bf_col, cos_tok, sin_tok, cos_t, sin_t),
                     [pltpu.VMEM((d, 4 * w), BF16), pltpu.VMEM((3 * w + FG_ROWS, d), BF16)], decode, side)


def _cumsum_kernel(x_ref, o_ref):
    x = x_ref[...]
    n = x.shape[-1]
    lane = lax.broadcasted_iota(jnp.int32, x.shape, 1)
    shift = 1
    while shift < n:
        x = x + jnp.where(lane >= shift, pltpu.roll(x, shift, 1), 0.0)
        shift *= 2
    o_ref[...] = x * LOG2E


def _cumsum_call(lft):
    batch, heads, seq = lft.shape
    spec = pl.BlockSpec((None, heads, seq), lambda b: (b, 0, 0))
    return pl.pallas_call(
        _cumsum_kernel, grid=(batch,), in_specs=[spec], out_specs=spec,
        out_shape=jax.ShapeDtypeStruct(lft.shape, F32),
        compiler_params=_cparams(("parallel",)),
        name="logf_cumsum",
    )(lft)


def _split3(x):
    hi = x.astype(BF16).astype(F32)
    r = x - hi
    mid = r.astype(BF16).astype(F32)
    lo = (r - mid).astype(BF16).astype(F32)
    return hi, mid, lo


def _fox_kernel(q_ref, kt_ref, vt_ref, c_ref, o_ref):
    tq = kt_ref.shape[-1]

    def per_block(i, carry):
        rows = pl.ds(pl.multiple_of(i * tq, tq), tq)
        o_ref[rows, :] = _fox_block(i, q_ref[rows, :], kt_ref, vt_ref, c_ref)
        return carry

    lax.fori_loop(0, q_ref.shape[0] // tq, per_block, 0)


def _fox_block(i, q, kt_ref, vt_ref, c_ref):
    tq = q.shape[0]
    tk = kt_ref.shape[-1]
    pw = 2 * HEAD_DIM
    slab_rows = V7X_BF16_SUBLANES
    lane = lax.broadcasted_iota(jnp.int32, (tq, pw), 1)
    q_aug = []
    for hd in range(2):
        base = HEAD_DIM if hd == 0 else 0
        hi, mid, lo = _split3(c_ref[hd, pl.ds(i, 1), :][:, 0:1])
        aug = jnp.where(lane == base + 3, hi, jnp.where(lane == base + 4, mid, jnp.where(lane == base + 5, lo,
              jnp.where((lane >= base) & (lane < base + 3), 1.0, 0.0))))
        own = (lane < HEAD_DIM) if hd == 0 else (lane >= HEAD_DIM)
        q_aug.append(jnp.where(own, q, aug.astype(BF16)))
    r16 = lax.broadcasted_iota(jnp.int32, (slab_rows, tk), 0)
    rowv = lax.broadcasted_iota(jnp.int32, (pw, tk), 0)

    def scores(j, hd):
        kt = kt_ref[j]
        hi, mid, lo = _split3(-c_ref[hd, pl.ds(j, 1), :])
        slab = jnp.where(r16 == 0, hi, jnp.where(r16 == 1, mid, jnp.where(r16 == 2, lo,
               jnp.where(r16 < 6, 1.0, 0.0)))).astype(BF16)
        if hd == 0:
            kt_aug = jnp.concatenate([kt[0:HEAD_DIM], slab, kt[HEAD_DIM + slab_rows:]], axis=0)
        else:
            kt_aug = jnp.concatenate([slab, kt[slab_rows:]], axis=0)
        return _dot(q_aug[hd], kt_aug)

    def values(j, hd):
        vt = vt_ref[j]
        return jnp.where(rowv == (HEAD_DIM if hd == 0 else 0), jnp.ones_like(vt), vt)

    def update(s, vt_aug, m, acc):
        m_new = jnp.maximum(m, jnp.max(s, axis=-1, keepdims=True))
        return m_new, jnp.exp2(m - m_new) * acc + _dot_nt(jnp.exp2(s - m_new).astype(BF16), vt_aug)

    def diag_scores(hd):
        row = lax.broadcasted_iota(jnp.int32, (tq, tk), 0)
        col = lax.broadcasted_iota(jnp.int32, (tq, tk), 1)
        return jnp.where(row >= col, scores(i, hd), -jnp.inf)

    group = FOX_BLOCKS_PER_UPDATE

    def joint(carry, full_blocks, with_diag):
        out = []
        for hd in range(2):
            s = [scores(j, hd) for j in full_blocks] + ([diag_scores(hd)] if with_diag else [])
            v = [values(j, hd) for j in full_blocks] + ([values(i, hd)] if with_diag else [])
            out.append(update(jnp.concatenate(s, axis=1), jnp.concatenate(v, axis=1), *carry[hd]))
        return tuple(out)

    def tail(n_full):
        return lambda carry: tuple(c[1] for c in joint(carry, [i - n_full + k for k in range(n_full)], True))

    init_head = (jnp.full((tq, 1), -jnp.inf, F32), jnp.zeros((tq, pw), F32))
    carry = lax.fori_loop(0, lax.shift_right_logical(i, group.bit_length() - 1),
                          lambda t, c: joint(c, [group * t + k for k in range(group)], False),
                          (init_head, init_head))
    outs = lax.switch(i & (group - 1), [tail(n) for n in range(group)], carry)
    o0 = outs[0] / outs[0][:, HEAD_DIM:HEAD_DIM + 1]
    o1 = outs[1] / outs[1][:, 0:1]
    return jnp.where(lane < HEAD_DIM, o0, o1).astype(BF16)


def _fox_call(q, ktc, vtc, c4, batch, seq):
    n, w = q.shape
    tq = ktc.shape[-1]
    nq = seq // tq
    pw = 2 * HEAD_DIM
    return pl.pallas_call(
        _fox_kernel,
        grid=(batch, HEAD_PAIRS),
        in_specs=[pl.BlockSpec((seq, pw), lambda b, p: (b, p)),
                  pl.BlockSpec((None, nq, pw, tq), lambda b, p: (b, 0, p, 0)),
                  pl.BlockSpec((None, nq, pw, tq), lambda b, p: (b, 0, p, 0)),
                  pl.BlockSpec((None, 2, nq, tq), lambda b, p: (b, p, 0, 0))],
        out_specs=pl.BlockSpec((seq, pw), lambda b, p: (b, p)),
        out_shape=jax.ShapeDtypeStruct((n, w), BF16),
        compiler_params=_cparams(("parallel", "parallel"), big=True),
        name="fox_prompt",
    )(q, ktc, vtc, c4)


def _ret_kernel(q_ref, kt_ref, v_ref, sg_ref, lgl_ref, lgr_ref, y_ref, s_ref):
    chunk = RET_CHUNK
    pw = 2 * HEAD_DIM
    nkb, _, tb = kt_ref.shape
    lg_lane = lgl_ref[...]
    lg_row = lgr_ref[...]
    lg_a, lg_b = lg_lane[:, 0:1], lg_lane[:, HEAD_DIM:HEAD_DIM + 1]

    ri = lax.broadcasted_iota(jnp.int32, (2 * chunk, chunk), 0)
    cj = lax.broadcasted_iota(jnp.int32, (2 * chunk, chunk), 1)
    first = ri < chunk
    diff = (jnp.where(first, ri, ri - chunk) - cj).astype(F32)
    decay_mask = jnp.where(diff >= 0, jnp.exp(jnp.maximum(diff, 0.0) * jnp.where(first, lg_a, lg_b)), 0.0)
    jj = lax.broadcasted_iota(jnp.int32, (pw, chunk), 1).astype(F32)
    col_decay = jnp.exp((chunk - 1.0 - jj) * lg_row)
    ii = lax.broadcasted_iota(jnp.int32, (chunk, pw), 0).astype(F32)
    row_decay = jnp.exp((ii + 1.0) * lg_lane)
    chunk_decay = jnp.exp(float(chunk) * lg_row)
    r2 = lax.broadcasted_iota(jnp.int32, (pw, pw), 0)
    c2 = lax.broadcasted_iota(jnp.int32, (pw, pw), 1)
    same_head = (r2 < HEAD_DIM) == (c2 < HEAD_DIM)
    seg_avg = jnp.where(same_head, 1.0 / HEAD_DIM, 0.0).astype(BF16)
    lane = lax.broadcasted_iota(jnp.int32, (chunk, pw), 1)

    def seg_mean(a):
        return _dot(a.astype(BF16), seg_avg)

    def one_chunk(q, kt, v, sg, state):
        zero = jnp.zeros_like(q)
        q_stack = jnp.concatenate([jnp.where(lane < HEAD_DIM, q, zero), jnp.where(lane >= HEAD_DIM, q, zero)], axis=0)
        qk = _dot(q_stack, kt.astype(BF16)) * decay_mask
        kd = (kt * col_decay).astype(BF16)
        r = _dot(jnp.concatenate([qk.astype(BF16), kd], axis=0), v)
        inner = jnp.where(lane < HEAD_DIM, r[0:chunk], r[chunk:2 * chunk])
        update = jnp.where(same_head, r[2 * chunk:], 0.0)
        cross = _dot(q, state.astype(BF16)) * row_decay
        o = inner + cross
        mu = seg_mean(o)
        d = o - mu
        var = seg_mean(d * d)
        y = sg.astype(F32) * (d * lax.rsqrt(var + GN_EPS))
        return y.astype(BF16), chunk_decay * state + update

    def body(jb, state):
        kt_blk = kt_ref[jb]
        for sub in range(tb // chunk):
            t0 = pl.multiple_of(jb * tb + sub * chunk, chunk)
            y, state = one_chunk(q_ref[pl.ds(t0, chunk), :], kt_blk[:, sub * chunk:(sub + 1) * chunk],
                                 v_ref[pl.ds(t0, chunk), :], sg_ref[pl.ds(t0, chunk), :], state)
            y_ref[pl.ds(t0, chunk), :] = y
        return state

    state = lax.fori_loop(0, nkb, body, jnp.zeros((pw, pw), F32), unroll=8)
    s_ref[0] = state[0:HEAD_DIM, 0:HEAD_DIM]
    s_ref[1] = pltpu.roll(state, HEAD_DIM, 1)[HEAD_DIM:, 0:HEAD_DIM]


def _ret_call(qr, krt, vr, sg, lg_lane, lg_row, batch, seq):
    n, w = qr.shape
    nkb, tb = krt.shape[1], krt.shape[3]
    pw = 2 * HEAD_DIM
    seq_spec = pl.BlockSpec((seq, pw), lambda b, p: (b, p))
    return pl.pallas_call(
        _ret_kernel,
        grid=(batch, HEAD_PAIRS),
        in_specs=[seq_spec,
                  pl.BlockSpec((None, nkb, pw, tb), lambda b, p: (b, 0, p, 0)),
                  seq_spec, seq_spec,
                  pl.BlockSpec((None, 1, pw), lambda b, p: (p, 0, 0)),
                  pl.BlockSpec((None, pw, 1), lambda b, p: (p, 0, 0))],
        out_specs=[seq_spec, pl.BlockSpec((None, 2, HEAD_DIM, HEAD_DIM), lambda b, p: (b, p, 0, 0))],
        out_shape=[jax.ShapeDtypeStruct((n, w), BF16),
                   jax.ShapeDtypeStruct((batch, N_HEADS, HEAD_DIM, HEAD_DIM), F32)],
        compiler_params=_cparams(("parallel", "parallel"), big=True),
        name="ret_prompt",
    )(qr, krt, vr, sg, lg_lane, lg_row)


def _odd_prompt_kernel(x_ref, sh_ref, sc_ref, g_ref, win_ref, cw_ref, wout_ref, lng_ref, lnb_ref,
                       o_ref, st_ref, carry_ref, *, bps, alpha):
    i = pl.program_id(0)
    x = x_ref[...]
    tm, d = x.shape
    h = (x * (1.0 + sc_ref[...]) + sh_ref[...]).astype(BF16)
    b_gate = _dot(h, win_ref[:, 0:d])
    u = _dot(h, win_ref[:, d:2 * d]) * _dot(h, win_ref[:, 2 * d:3 * d])

    @pl.when(i % bps == 0)
    def _():
        carry_ref[...] = jnp.zeros_like(carry_ref)

    prev = carry_ref[...]
    p1, p2 = prev[7:8, :], prev[6:7, :]
    row = lax.broadcasted_iota(jnp.int32, (tm, d), 0)
    u1 = jnp.where(row == 0, p1, pltpu.roll(u, 1, 0))
    u2 = jnp.where(row == 0, p2, jnp.where(row == 1, p1, pltpu.roll(u, 2, 0)))
    cw = cw_ref[...]
    z = cw[0:1, :] * u2 + cw[1:2, :] * u1 + cw[2:3, :] * u
    carry_ref[...] = u[tm - 8:, :]
    st_ref[...] = u[tm - (CONV_WIDTH - 1):, :]
    y = _dot((b_gate * z).astype(BF16), wout_ref[...])
    zz = alpha * x + g_ref[...] * y
    o_ref[...] = _layernorm(zz, lng_ref[...], lnb_ref[...])


def _odd_prompt_call(x, rows, ks, w_in, conv_w, w_out, ln_g, ln_b, ln_idx, alpha, batch, seq, decode=None,
                     side=None):
    n, d = x.shape
    bps = seq // rows.tm
    in_specs = [rows.row_spec(d), rows.mod_spec(ks[0]), rows.mod_spec(ks[1]), rows.mod_spec(ks[2]),
                _const_spec(w_in), _const_spec(conv_w), _const_spec(w_out),
                pl.BlockSpec((None, 1, d), lambda i, *_: (ln_idx, 0, 0)),
                pl.BlockSpec((None, 1, d), lambda i, *_: (ln_idx, 0, 0))]
    out_specs = [rows.row_spec(d), pl.BlockSpec((None, CONV_WIDTH - 1, d), lambda i, *_: (i // bps, 0, 0))]
    out_shape = [jax.ShapeDtypeStruct((n, d), F32), jax.ShapeDtypeStruct((batch, CONV_WIDTH - 1, d), F32)]

    return _run_host("odd_prompt", functools.partial(_odd_prompt_kernel, bps=bps, alpha=alpha), n // rows.tm,
                     in_specs, out_specs, out_shape, (x, rows.mods, rows.mods, rows.mods, w_in, conv_w, w_out, ln_g, ln_b),
                     [pltpu.VMEM((8, d), F32)], decode, side)


def _even_in_sample_kernel(x_ref, sh_ref, sc_ref, w_ref, bf_ref, cost_ref, sint_ref,
                           q_o, k_o, v_o, qrt_o, krt_o, vr_o, sg_o, lf_o):
    d = w_ref.shape[1]
    h = (x_ref[...] * (1.0 + sc_ref[...]) + sh_ref[...]).astype(BF16)

    def rows(name):
        lo, hi = _EVEN_ROWS[name]
        return w_ref[lo:hi, :]

    def tok(name):
        return _dot_nt(h, rows(name).astype(BF16))

    q_o[...] = tok("qf") * QK_SCALE
    k_o[...] = tok("kf")
    v_o[...] = tok("vf")
    vr_o[...] = tok("vr")
    sg_o[...] = _silu(tok("gr"))
    fg_rows = jnp.concatenate([rows("fg"), jnp.zeros((V7X_LANES - N_HEADS, d), F32)], axis=0).astype(BF16)
    lf_o[...] = _log_sigmoid(_dot_nt(h, fg_rows) + bf_ref[...])

    cos_t, sin_t = cost_ref[...], sint_ref[...]
    half = HEAD_DIM // 2
    for name, out, scale in (("qr", qrt_o, 1.0), ("kr", krt_o, QK_SCALE)):
        t = _dot_nt(rows(name).astype(BF16), h)
        for hh in range(N_HEADS):
            x1 = t[hh * HEAD_DIM:hh * HEAD_DIM + half, :]
            x2 = t[hh * HEAD_DIM + half:(hh + 1) * HEAD_DIM, :]
            out[hh * HEAD_DIM:hh * HEAD_DIM + half, :] = (x1 * cos_t - x2 * sin_t) * scale
            out[hh * HEAD_DIM + half:(hh + 1) * HEAD_DIM, :] = (x1 * sin_t + x2 * cos_t) * scale


def _even_in_sample_call(x, rows, ks, w_t_f32, bf_row, cos_t, sin_t):
    n, d = x.shape
    w = HEAD_WIDTH
    full = pl.BlockSpec((n, w), lambda i: (0, 0))
    full_t = pl.BlockSpec((w, n), lambda i: (0, 0))
    rot = pl.BlockSpec((HEAD_DIM // 2, n), lambda i: (0, 0))
    shp = jax.ShapeDtypeStruct((n, w), F32)
    shp_t = jax.ShapeDtypeStruct((w, n), F32)
    return pl.pallas_call(
        _even_in_sample_kernel,
        grid=(1,),
        in_specs=[rows.row_spec(d), rows.mod_spec(ks[0]), rows.mod_spec(ks[1]),
                  _const_spec(w_t_f32), _const_spec(bf_row), rot, rot],
        out_specs=[full, full, full, full_t, full_t, full, full, pl.BlockSpec((n, V7X_LANES), lambda i: (0, 0))],
        out_shape=[shp, shp, shp, shp_t, shp_t, shp, shp, jax.ShapeDtypeStruct((n, V7X_LANES), F32)],
        compiler_params=_cparams(("arbitrary",), big=True),
        name="even_inproj_sample",
    )(x, rows.mods, rows.mods, w_t_f32, bf_row, cos_t, sin_t)


def _own_head_lanes():
    sub = lax.broadcasted_iota(jnp.int32, (N_HEADS, HEAD_WIDTH), 0)
    lane = lax.broadcasted_iota(jnp.int32, (N_HEADS, HEAD_WIDTH), 1)
    return (lane >= sub * HEAD_DIM) & (lane < (sub + 1) * HEAD_DIM)


def _decode_fox_init(q_ref, kn_ref, vn_ref, lfn_ref, qbd_s, m_s, l_s, run_s, acc_s):
    w = HEAD_WIDTH
    q_bd = jnp.where(_own_head_lanes(), jnp.broadcast_to(q_ref[...], (N_HEADS, w)), 0.0)
    qbd_s[...] = q_bd
    m_s[...] = jnp.sum(q_bd * kn_ref[...], axis=-1, keepdims=True)
    l_s[...] = jnp.ones_like(l_s)
    page_len = acc_s.shape[1]
    lane_p = lax.broadcasted_iota(jnp.int32, (w, page_len), 1)
    acc_s[...] = jnp.where(lane_p == 0, jnp.broadcast_to(vn_ref[...], (page_len, w)).T, 0.0)
    s128 = lax.broadcasted_iota(jnp.int32, (N_HEADS, V7X_LANES), 0)
    l128 = lax.broadcasted_iota(jnp.int32, (N_HEADS, V7X_LANES), 1)
    lfn = jnp.broadcast_to(lfn_ref[...], (N_HEADS, V7X_LANES))
    run_s[...] = jnp.sum(jnp.where(s128 == l128, lfn, 0.0), axis=-1, keepdims=True)


def _decode_fox_finish(o_ref, l_s, acc_s):
    tot = jnp.sum(acc_s[...].T, axis=0, keepdims=True)
    l_b = jnp.broadcast_to(l_s[...], (N_HEADS, HEAD_WIDTH))
    o_ref[...] = tot / jnp.sum(jnp.where(_own_head_lanes(), l_b, 0.0), axis=0, keepdims=True)


def _decode_fox_pages(k_refs, v_refs, lf_refs, qbd_s, m_s, l_s, run_s, acc_s):
    pages = len(k_refs)
    page_len = k_refs[0].shape[-1]
    q_b = qbd_s[...].astype(BF16)
    order = list(range(pages - 1, -1, -1))
    s = jnp.concatenate([_dot(q_b, k_refs[r][...].astype(BF16)) for r in order], axis=1)
    lf = jnp.concatenate([lf_refs[r][...] for r in order], axis=1)
    n = pages * page_len
    lane_n = lax.broadcasted_iota(jnp.int32, (N_HEADS, n), 1)
    suf = lf
    shift = 1
    while shift < n:
        suf = suf + jnp.where(lane_n < n - shift, pltpu.roll(suf, n - shift, 1), 0.0)
        shift *= 2
    run = run_s[...]
    s = s + ((suf - lf) + run)
    m = m_s[...]
    m_new = jnp.maximum(m, jnp.max(s, axis=-1, keepdims=True))
    a = jnp.exp(m - m_new)
    p = jnp.exp(s - m_new)
    l_s[...] = a * l_s[...] + jnp.sum(p, axis=-1, keepdims=True)
    m_s[...] = m_new
    for h in range(N_HEADS):
        sl = slice(h * HEAD_DIM, (h + 1) * HEAD_DIM)
        upd = None
        for idx, r in enumerate(order):
            t = p[h:h + 1, idx * page_len:(idx + 1) * page_len] * v_refs[r][sl, :]
            upd = t if upd is None else upd + t
        acc_s[sl, :] = a[h:h + 1, :] * acc_s[sl, :] + upd
    run_s[...] = run + suf[:, 0:1]


def _decode_ret_kernel(qt_ref, kt_ref, v_ref, sg_ref, s_ref, gl_ref, y_ref, so_ref):
    qt, kt = qt_ref[...], kt_ref[...]
    g = gl_ref[...]
    for b in range(qt.shape[-1]):
        q, k = qt[:, :, b:b + 1], kt[:, :, b:b + 1]
        v = v_ref[b]
        state = s_ref[b]
        inner = jnp.sum(q * k, axis=1, keepdims=True) * v
        cross = jnp.sum(q * state, axis=1, keepdims=True) * g
        so_ref[b] = g * state + k * v
        o = inner + cross
        mu = jnp.mean(o, axis=-1, keepdims=True)
        d = o - mu
        var = jnp.mean(d * d, axis=-1, keepdims=True)
        y_ref[b] = sg_ref[b] * (d * lax.rsqrt(var + GN_EPS))


def _decode_ret_call(q_t, k_t, v_row, sg_row, state, decay):
    nseq = state.shape[0]

    def whole(arr):
        nd = arr.ndim
        return pl.BlockSpec(arr.shape, lambda i: (0,) * nd)

    return pl.pallas_call(
        _decode_ret_kernel,
        grid=(1,),
        in_specs=[whole(a) for a in (q_t, k_t, v_row, sg_row, state, decay)],
        out_specs=[whole(v_row), whole(state)],
        out_shape=[jax.ShapeDtypeStruct(v_row.shape, F32), jax.ShapeDtypeStruct(state.shape, F32)],
        compiler_params=_cparams(("arbitrary",), big=True),
        name="ret_decode",
    )(q_t, k_t, v_row, sg_row, state, decay)


def _odd_sample_kernel(x_ref, sh_ref, sc_ref, g_ref, win_ref, cw_ref, b0_ref, b1_ref, wout_ref, lng_ref, lnb_ref,
                       o_ref, u_ref, *, alpha):
    x = x_ref[...]
    d = x.shape[-1]
    h = (x * (1.0 + sc_ref[...]) + sh_ref[...]).astype(BF16)
    b_gate = _dot(h, win_ref[:, 0:d])
    u = _dot(h, win_ref[:, d:2 * d]) * _dot(h, win_ref[:, 2 * d:3 * d])
    cw = cw_ref[...]
    z = cw[0:1, :] * b0_ref[...] + cw[1:2, :] * b1_ref[...] + cw[2:3, :] * u
    u_ref[...] = u
    y = _dot((b_gate * z).astype(BF16), wout_ref[...])
    o_ref[...] = _layernorm(alpha * x + g_ref[...] * y, lng_ref[...], lnb_ref[...])


def _odd_sample_call(x, rows, ks, w_in, conv_w, buf0, buf1, w_out, ln_g, ln_b, ln_idx, alpha):
    n, d = x.shape
    full = pl.BlockSpec((n, d), lambda i: (0, 0))
    return pl.pallas_call(
        functools.partial(_odd_sample_kernel, alpha=alpha),
        grid=(1,),
        in_specs=[rows.row_spec(d), rows.mod_spec(ks[0]), rows.mod_spec(ks[1]), rows.mod_spec(ks[2]),
                  _const_spec(w_in), _const_spec(conv_w), full, full, _const_spec(w_out),
                  pl.BlockSpec((None, 1, d), lambda i: (ln_idx, 0, 0)),
                  pl.BlockSpec((None, 1, d), lambda i: (ln_idx, 0, 0))],
        out_specs=[full, full],
        out_shape=[jax.ShapeDtypeStruct((n, d), F32), jax.ShapeDtypeStruct((n, d), F32)],
        compiler_params=_cparams(("arbitrary",), big=True),
        name="odd_sample",
    )(x, rows.mods, rows.mods, rows.mods, w_in, conv_w, buf0, buf1, w_out, ln_g, ln_b)


def _rotary_tables(pos):
    half = HEAD_DIM // 2
    inv = 1.0 / (RET_ANGLE_BASE ** np.linspace(0.0, 1.0, half))
    ang = np.asarray(pos, np.float64)[:, None] * inv[None, :]
    return np.cos(ang), np.sin(ang)


def _token_major_tables(cos, sin):
    cos_h = np.concatenate([cos, cos], axis=1)
    sin_h = np.concatenate([-sin, sin], axis=1)
    return (jnp.asarray(np.tile(cos_h, (1, N_HEADS)), F32), jnp.asarray(np.tile(sin_h, (1, N_HEADS)), F32))


def kernel(x_prompt, x_sample, cache_k, cache_v, cache_logf, state_ret, state_conv, page_table, c_prompt, c_sample,
           w_ada, b_ada, w_ffn_in, w_ffn_out, ln_g, ln_b, w_in_even, b_forget, w_out_even, w_in_odd, conv_w,
           w_out_odd):
    batch, seq, d = x_prompt.shape
    nseq = x_sample.shape[0]
    depth = w_ada.shape[0]
    past_len = page_table.shape[1] * cache_k.shape[2]
    alpha = (2.0 * depth) ** 0.25
    w = HEAD_WIDTH
    tm = ROW_BLOCK
    assert seq % tm == 0 and x_sample.shape[1] == 1 and d % V7X_LANES == 0
    assert cache_k.shape[3] == N_HEADS and cache_k.shape[4] == HEAD_DIM

    n_c = batch + nseq
    pad = (-n_c) % 8
    c_all = jnp.concatenate([c_prompt, c_sample, jnp.zeros((pad, d), F32)], axis=0)
    mods = _ada_call(c_all, w_ada, b_ada)
    mods_p = mods[:, :batch].reshape(depth, batch, 1, N_MOD * d)
    mods_s = mods[:, batch:n_c]

    ln_g3 = ln_g.reshape(depth * 3, 1, d)
    ln_b3 = ln_b.reshape(depth * 3, 1, d)
    w_bf = {(0, 0): (w_ffn_in[0, 0].astype(BF16), w_ffn_out[0, 0].astype(BF16))}

    def side_cast(l, half):
        return _SideCast(w_ffn_in, w_ffn_out, (l, half), batch * seq // tm)

    xp = x_prompt.reshape(batch * seq, d)
    xs = x_sample.reshape(nseq, d)

    cos_p, sin_p = _rotary_tables(np.arange(seq))
    rot_p = _token_major_tables(cos_p, sin_p) + (jnp.asarray(cos_p.T, F32), jnp.asarray(sin_p.T, F32))
    cos_s, sin_s = (jnp.asarray(t.T, F32) for t in _rotary_tables(np.full((nseq,), past_len)))

    log_decay = jnp.log(1.0 - 2.0 ** (-5.0 - jnp.arange(N_HEADS, dtype=F32)))
    lg_pairs = jnp.repeat(log_decay.reshape(HEAD_PAIRS, 2), HEAD_DIM, axis=1)
    lg_lane = lg_pairs.reshape(HEAD_PAIRS, 1, 2 * HEAD_DIM)
    lg_row = lg_pairs.reshape(HEAD_PAIRS, 2 * HEAD_DIM, 1)
    step_decay = jnp.exp(log_decay).reshape(N_HEADS, 1, 1)

    outs_p = {"k": [], "v": [], "lf": [], "ret": [], "conv": []}
    outs_s = {"k": [], "v": [], "lf": [], "ret": [], "conv": []}

    assert depth == 2 and w_in_even.shape[0] == 1 and cache_k.shape[0] == 1
    n_hosts = 2 * depth
    assert nseq % n_hosts == 0
    assert w_in_even.shape[2] == EVEN_IN_ROWS
    w_even_t = jnp.transpose(w_in_even[0])
    w_out_even_b = w_out_even[0].astype(BF16)
    bf = b_forget[0]
    w_in_odd_b = w_in_odd[0].astype(BF16)
    w_out_odd_b = w_out_odd[0].astype(BF16)

    def rows_s(l):
        return _Rows(mods_s, l, nseq, None, d)

    def rows_p(l):
        return _Rows(mods_p, l, tm, seq // tm, d)

    def ffn(x, rows, half, l, decode=None, mixer_out=None, side=None):
        ks = (0, 1, 2) if half == 0 else (6, 7, 8)
        return _ffn_call(x, rows, ks, *w_bf[(l, half)], ln_g3, ln_b3, 3 * l + 2 * half, alpha, decode, mixer_out, side)

    xs = ffn(xs, rows_s(0), 0, 0)
    bf_row = jnp.concatenate([bf, jnp.zeros((V7X_LANES - N_HEADS,), F32)]).reshape(1, V7X_LANES)
    (qs, ks_, vs, qrs_t, krs_t, vrs, sgs, lfs) = _even_in_sample_call(xs, rows_s(0), (3, 4), w_even_t, bf_row,
                                                                       cos_s, sin_s)
    n_phys, page_len = cache_k.shape[1], cache_k.shape[2]
    kt_pages = jnp.transpose(cache_k[0], (0, 2, 3, 1)).reshape(n_phys, w, page_len)
    vt_pages = jnp.transpose(cache_v[0], (0, 2, 3, 1)).reshape(n_phys, w, page_len)
    lft_pages = jnp.transpose(cache_logf[0], (0, 2, 1))
    per_host = nseq // n_hosts
    dec_args = (page_table, qs.reshape(nseq, 1, w), ks_.reshape(nseq, 1, w), vs.reshape(nseq, 1, w),
                lfs.reshape(nseq, 1, V7X_LANES), kt_pages, vt_pages, lft_pages)
    dec_out = []

    def ffn_host(x, half, l, mixer_out=None, cast_for=None):
        side = None if cast_for is None else side_cast(*cast_for)
        res = ffn(x, rows_p(l), half, l, dec_args + (len(dec_out) * per_host, per_host), mixer_out, side)
        dec_out.append(res[1])
        if cast_for is not None:
            w_bf[cast_for] = tuple(res[2:])
        return res[0]

    xp = ffn_host(xp, 0, 0, cast_for=(1, 0))
    (q, qr, vr, sg, kt, vt, ktc, vtc, krt, lft, *w_bf[(0, 1)]) = _even_in_call(
        xp, rows_p(0), (3, 4), w_even_t, bf.reshape(N_HEADS, 1), rot_p, batch, seq, side=side_cast(0, 1))
    c_t = _cumsum_call(lft)
    of = _fox_call(q, ktc, vtc, c_t.reshape(batch, N_HEADS, seq // tm, tm), batch, seq)
    yr, s_new = _ret_call(qr, krt, vr, sg, lg_lane, lg_row, batch, seq)
    outs_p["k"].append(jnp.transpose(kt.reshape(batch, N_HEADS, HEAD_DIM, seq), (0, 3, 1, 2)))
    outs_p["v"].append(jnp.transpose(vt.reshape(batch, N_HEADS, HEAD_DIM, seq), (0, 3, 1, 2)))
    outs_p["lf"].append(jnp.transpose(lft, (0, 2, 1)))
    outs_p["ret"].append(s_new)
    xp = ffn_host(xp, 1, 0, mixer_out=(5, of, yr, w_out_even_b, 1))
    xp = ffn_host(xp, 0, 1)
    xp, conv_p, *w_bf[(1, 1)] = _odd_prompt_call(xp, rows_p(1), (3, 4, 5), w_in_odd_b, conv_w[0], w_out_odd_b,
                                                 ln_g3, ln_b3, 4, alpha, batch, seq, side=side_cast(1, 1))
    outs_p["conv"].append(conv_p)
    xp = ffn_host(xp, 1, 1)

    of_s = jnp.concatenate(dec_out, axis=0).reshape(nseq, w)
    lfs = lfs[:, :N_HEADS]
    hdn = (N_HEADS, HEAD_DIM, nseq)
    rw = (nseq, N_HEADS, 1, HEAD_DIM)
    yr_s, s_new_s = _decode_ret_call(qrs_t.reshape(hdn), krs_t.reshape(hdn), vrs.reshape(rw), sgs.reshape(rw),
                                     state_ret[0], step_decay)
    xs = _outproj_call(xs, rows_s(0), 5, of_s, yr_s.reshape(nseq, w), w_out_even_b, ln_g3, ln_b3, 1, alpha)
    outs_s["k"].append(ks_.reshape(nseq, 1, N_HEADS, HEAD_DIM))
    outs_s["v"].append(vs.reshape(nseq, 1, N_HEADS, HEAD_DIM))
    outs_s["lf"].append(lfs.reshape(nseq, 1, N_HEADS))
    outs_s["ret"].append(s_new_s)
    xs = ffn(xs, rows_s(0), 1, 0)
    xs = ffn(xs, rows_s(1), 0, 1)
    buf = state_conv[0]
    xs, u_s = _odd_sample_call(xs, rows_s(1), (3, 4, 5), w_in_odd_b, conv_w[0], buf[:, 0], buf[:, 1], w_out_odd_b,
                               ln_g3, ln_b3, 4, alpha)
    outs_s["conv"].append(jnp.stack([buf[:, 1], u_s], axis=1))
    xs = ffn(xs, rows_s(1), 1, 1)

    def stk(lst):
        return jnp.stack(lst)

    return (xp.reshape(batch, seq, d), xs.reshape(nseq, 1, d),
            stk(outs_p["k"]), stk(outs_p["v"]), stk(outs_p["lf"]), stk(outs_p["ret"]), stk(outs_p["conv"]),
            stk(outs_s["k"]), stk(outs_s["v"]), stk(outs_s["lf"]), stk(outs_s["ret"]), stk(outs_s["conv"]))
```
